```python
import jax, jax.numpy as jnp
from jax import lax
import numpy as np

D_MODEL = 1024
BATCH = 32
SEQ = 2048
DEPTH = 1

CHUNK = 64
Q_BLOCK = 128
A_HEADS = 8
A_HEAD_DIM = 64
A_WIDTH = A_HEADS * A_HEAD_DIM
KV_LATENT = 128
IDX_HEADS = 8
IDX_DIM = 64
TOPK_MAX = 256
B_HEADS = 8
B_HEAD_DIM = 64
B_WIDTH = B_HEADS * B_HEAD_DIM
W_LORA = 64
A_LORA = 64
G_LORA = 128
D_FF = 4 * D_MODEL
RMS_EPS = 1e-6
GN_EPS = 64e-5
N_IN_A = A_WIDTH + KV_LATENT + IDX_HEADS * IDX_DIM + IDX_DIM + IDX_HEADS
N_IN_B = 3 * B_WIDTH + W_LORA + A_LORA + G_LORA
N_IN = N_IN_A + N_IN_B
MIX_WIDTH = A_WIDTH + B_WIDTH

kernel_name = 'hybrid_dsa_rwkv7_adaln_block'


def rms_norm(x, g):
    xf = x.astype(jnp.float32)
    y = xf * lax.rsqrt(jnp.mean(xf * xf, axis=-1, keepdims=True) + RMS_EPS)
    return (y * g).astype(x.dtype)


def split_cols(p, sizes):
    outs, start = [], 0
    for s in sizes:
        outs.append(p[..., start:start + s])
        start += s
    return outs


def token_shift(p, mu):
    prev = jnp.pad(p, ((0, 0), (1, 0), (0, 0)))[:, :-1]
    return p + mu * (prev - p)


def dsa_attention(q, c_kv, q_idx, k_idx, w_idx, g_q, g_k, w_uk, w_uv):
    B_, S, H, HD = q.shape
    f32 = jnp.float32
    topk = min(TOPK_MAX, S // 4)
    k_full = jnp.einsum('bsl,lhd->bshd', c_kv, w_uk).astype(f32)
    inv_rms_k = lax.rsqrt(jnp.mean(k_full * k_full, axis=-1) + RMS_EPS)
    q_abs = jnp.einsum('bshd,lhd->bshl', rms_norm(q, g_q) * g_k, w_uk) * (HD ** -0.5)
    w_idx = w_idx * (IDX_HEADS ** -0.5 * IDX_DIM ** -0.5)
    slopes = jnp.exp2(-8.0 * jnp.arange(1, H + 1, dtype=f32) / H)
    key_pos = jnp.arange(S)
    gather = jax.vmap(lambda table, idx: table[idx])

    def block(i):
        t0 = i * Q_BLOCK
        tpos = t0 + jnp.arange(Q_BLOCK)
        limit = (tpos // CHUNK + 1) * CHUNK
        qi = lax.dynamic_slice_in_dim(q_idx, t0, Q_BLOCK, axis=1)
        wi = lax.dynamic_slice_in_dim(w_idx, t0, Q_BLOCK, axis=1)
        qa = lax.dynamic_slice_in_dim(q_abs, t0, Q_BLOCK, axis=1)
        rel = jax.nn.relu(jnp.einsum('bthd,bsd->bths', qi, k_idx).astype(f32))
        score_idx = jnp.einsum('bth,bths->bts', wi.astype(f32), rel)
        score_idx = jnp.where(key_pos[None, :] < limit[:, None], score_idx, -jnp.inf)
        _, sel = lax.top_k(score_idx, topk)
        valid = sel < limit[None, :, None]
        c_sel = gather(c_kv, sel)
        r_sel = gather(inv_rms_k, sel)
        logits = jnp.einsum('bthl,btkl->bthk', qa, c_sel).astype(f32) * jnp.swapaxes(r_sel, 2, 3)
        dist = jnp.abs(tpos[None, :, None] - sel).astype(f32)
        logits = logits - slopes[None, None, :, None] * dist[:, :, None, :]
        logits = jnp.where(valid[:, :, None, :], logits, -jnp.inf)
        probs = jax.nn.softmax(logits, axis=-1).astype(c_kv.dtype)
        return jnp.einsum('bthk,btkl->bthl', probs, c_sel)

    o_lat = lax.map(block, jnp.arange(S // Q_BLOCK))
    o_lat = jnp.moveaxis(o_lat, 0, 1).reshape(B_, S, H, KV_LATENT)
    return jnp.einsum('bshl,lhd->bshd', o_lat, w_uv)


def rwkv7_time_mix(p, w0, w2, a0, a2, g2, k_k, k_a, r_k, ln_w, ln_b):
    B_, S, _ = p.shape
    H, N = B_HEADS, B_HEAD_DIM
    f32 = jnp.float32
    r, k, v, xw, xa, xg = split_cols(p, (B_WIDTH, B_WIDTH, B_WIDTH, W_LORA, A_LORA, G_LORA))
    w_log = -jax.nn.softplus(-(w0 + jnp.tanh(xw) @ w2)) - 0.5
    decay = jnp.exp(-jnp.exp(w_log.astype(f32)))
    a = jax.nn.sigmoid(a0 + xa @ a2)
    g = jax.nn.sigmoid(xg) @ g2
    heads = lambda t: t.astype(f32).reshape(B_, S, H, N)
    kk = heads(k * k_k)
    kk = kk / jnp.maximum(jnp.sqrt(jnp.sum(kk * kk, axis=-1, keepdims=True)), 1e-12)
    k = k * (1 + (a - 1) * k_a)
    r_h, k_h, v_h, a_h, w_h = heads(r), heads(k), heads(v), heads(a), heads(decay)

    def step(state, inp):
        r_t, w_t, k_t, v_t, kk_t, a_t = inp
        sa = jnp.einsum('bhij,bhj->bhi', state, -kk_t)
        state = (state * w_t[:, :, None, :] + sa[..., None] * (kk_t * a_t)[:, :, None, :]
                 + v_t[..., None] * k_t[:, :, None, :])
        return state, jnp.einsum('bhij,bhj->bhi', state, r_t)

    xs = tuple(jnp.moveaxis(t, 1, 0) for t in (r_h, w_h, k_h, v_h, kk, a_h))
    state0 = jnp.zeros((B_, H, N, N), f32)
    _, out = lax.scan(step, state0, xs)
    out = jnp.moveaxis(out, 0, 1)
    mean = jnp.mean(out, axis=-1, keepdims=True)
    var = jnp.mean((out - mean) ** 2, axis=-1, keepdims=True)
    out = (out - mean) * lax.rsqrt(var + GN_EPS) * ln_w.reshape(H, N) + ln_b.reshape(H, N)
    out = out + jnp.sum(r_h * k_h * r_k, axis=-1, keepdims=True) * v_h
    return (out.reshape(B_, S, B_WIDTH) * g.astype(f32)).astype(p.dtype)


def setup_inputs(seed: int = 0) -> dict:
    key = jax.random.key(seed)
    ks = jax.random.split(key, 32)
    f32 = jnp.float32
    nrm = lambda k, shape, scale: jax.random.normal(k, shape, f32) * scale
    L = DEPTH
    return {
        'x': nrm(ks[0], (BATCH, SEQ, D_MODEL), 1.0),
        'c': nrm(ks[1], (BATCH, D_MODEL), 1.0),
        'w_ada': nrm(ks[2], (L, D_MODEL, 6 * D_MODEL), 0.2 * D_MODEL ** -0.5),
        'b_ada': nrm(ks[3], (L, 6 * D_MODEL), 0.01),
        'g_mix': 1.0 + nrm(ks[4], (L, D_MODEL), 0.02),
        'g_ffn': 1.0 + nrm(ks[5], (L, D_MODEL), 0.02),
        'w_in': nrm(ks[6], (L, D_MODEL, N_IN), D_MODEL ** -0.5),
        'g_q': 1.0 + nrm(ks[7], (L, A_HEAD_DIM), 0.02),
        'g_k': 1.0 + nrm(ks[8], (L, A_HEAD_DIM), 0.02),
        'g_kv': 1.0 + nrm(ks[9], (L, KV_LATENT), 0.02),
        'w_uk': nrm(ks[10], (L, KV_LATENT, A_HEADS, A_HEAD_DIM), KV_LATENT ** -0.5),
        'w_uv': nrm(ks[11], (L, KV_LATENT, A_HEADS, A_HEAD_DIM), KV_LATENT ** -0.5),
        'mu_shift': jax.random.uniform(ks[12], (L, N_IN_B), f32),
        'w0': jax.random.uniform(ks[13], (L, B_WIDTH), f32, -5.0, -1.0),
        'w2': nrm(ks[14], (L, W_LORA, B_WIDTH), 0.1 * W_LORA ** -0.5),
        'a0': nrm(ks[15], (L, B_WIDTH), 0.1),
        'a2': nrm(ks[16], (L, A_LORA, B_WIDTH), 0.1 * A_LORA ** -0.5),
        'g2': nrm(ks[17], (L, G_LORA, B_WIDTH), G_LORA ** -0.5),
        'k_k': 0.85 + nrm(ks[18], (L, B_WIDTH), 0.02),
        'k_a': 1.0 + nrm(ks[19], (L, B_WIDTH), 0.02),
        'r_k': nrm(ks[20], (L, B_HEADS, B_HEAD_DIM), 0.1),
        'ln_w': 1.0 + nrm(ks[21], (L, B_WIDTH), 0.02),
        'ln_b': nrm(ks[22], (L, B_WIDTH), 0.01),
        'w_out': nrm(ks[23], (L, MIX_WIDTH, D_MODEL), MIX_WIDTH ** -0.5),
        'w_ff1': nrm(ks[24], (L, D_MODEL, D_FF), D_MODEL ** -0.5),
        'w_ff2': nrm(ks[25], (L, D_FF, D_MODEL), D_FF ** -0.5),
    }


def reference(x, c, w_ada, b_ada, g_mix, g_ffn, w_in, g_q, g_k, g_kv, w_uk, w_uv,
              mu_shift, w0, w2, a0, a2, g2, k_k, k_a, r_k, ln_w, ln_b,
              w_out, w_ff1, w_ff2):
    B_, S, D = x.shape
    for l in range(DEPTH):
        mod = jnp.einsum('bd,de->be', jax.nn.silu(c), w_ada[l]) + b_ada[l]
        sh1, sc1, gt1, sh2, sc2, gt2 = [m[:, None, :] for m in jnp.split(mod, 6, axis=-1)]
        h = rms_norm(x, g_mix[l]) * (1 + sc1) + sh1
        proj = jnp.einsum('bsd,de->bse', h, w_in[l])
        p_a, p_b = proj[..., :N_IN_A], proj[..., N_IN_A:]
        q, c_lat, q_idx, k_idx, w_idx = split_cols(
            p_a, (A_WIDTH, KV_LATENT, IDX_HEADS * IDX_DIM, IDX_DIM, IDX_HEADS))
        o_a = dsa_attention(q.reshape(B_, S, A_HEADS, A_HEAD_DIM), rms_norm(c_lat, g_kv[l]),
                            q_idx.reshape(B_, S, IDX_HEADS, IDX_DIM), k_idx, w_idx,
                            g_q[l], g_k[l], w_uk[l], w_uv[l])
        o_b = rwkv7_time_mix(token_shift(p_b, mu_shift[l]), w0[l], w2[l], a0[l], a2[l], g2[l],
                             k_k[l], k_a[l], r_k[l], ln_w[l], ln_b[l])
        mixed = jnp.concatenate([o_a.reshape(B_, S, A_WIDTH).astype(x.dtype), o_b], axis=-1)
        x = x + gt1 * jnp.einsum('bse,ed->bsd', mixed, w_out[l])
        h = rms_norm(x, g_ffn[l]) * (1 + sc2) + sh2
        u = jax.nn.relu(jnp.einsum('bsd,df->bsf', h, w_ff1[l]))
        x = x + gt2 * jnp.einsum('bsf,fd->bsd', u * u, w_ff2[l])
    return x
```

```python
import functools

import jax
import jax.numpy as jnp
from jax import lax
from jax.experimental import pallas as pl
from jax.experimental.pallas import tpu as pltpu

F32 = jnp.float32
BF16 = jnp.bfloat16

CHUNK = 64
A_HEADS = 8
A_HEAD_DIM = 64
A_WIDTH = A_HEADS * A_HEAD_DIM
KV_LATENT = 128
IDX_HEADS = 8
IDX_DIM = 64
TOPK_MAX = 256
B_HEADS = 8
B_HEAD_DIM = 64
B_WIDTH = B_HEADS * B_HEAD_DIM
W_LORA = 64
A_LORA = 64
G_LORA = 128
RMS_EPS = 1e-6
GN_EPS = 64e-5
N_IN_A = A_WIDTH + KV_LATENT + IDX_HEADS * IDX_DIM + IDX_DIM + IDX_HEADS
N_IN_B = 3 * B_WIDTH + W_LORA + A_LORA + G_LORA
N_A_PAD = 1280

LANES = 128
TOK_TILE = 256
Q_TILE = 256
K_TILE = 256
NEG_BIG = -1e30
BISECT_ITERS = 32
VMEM_LIMIT = 56 * 1024 * 1024


def _dot(a, b):
    return jnp.dot(a, b, preferred_element_type=F32)


def _dot_nt(a, b):
    return lax.dot_general(a, b, (((1,), (1,)), ((), ())), preferred_element_type=F32)


def _dot_tn(a, b):
    return lax.dot_general(a, b, (((0,), (0,)), ((), ())), preferred_element_type=F32)


def _split(x):
    hi = x.astype(BF16)
    lo = (x - hi.astype(F32)).astype(BF16)
    return hi, lo


def _dot_hl(x, e):
    hi, lo = _split(x)
    return _dot(hi, e) + _dot(lo, e)


def _params(sem):
    return pltpu.CompilerParams(dimension_semantics=sem, vmem_limit_bytes=VMEM_LIMIT)


def _mod_kernel(c_ref, w_ref, b_ref, o_ref):
    c = c_ref[...]
    s = c * jax.nn.sigmoid(c)
    s_hi, s_lo = _split(s)
    w_hi, w_lo = _split(w_ref[...])
    o_ref[...] = _dot(s_hi, w_hi) + _dot(s_hi, w_lo) + _dot(s_lo, w_hi) + b_ref[...]


def _mod_call(c, w_ada, b_ada):
    bsz, d = c.shape
    n = w_ada.shape[1]
    tn = 1024
    return pl.pallas_call(
        _mod_kernel,
        grid=(n // tn,),
        in_specs=[pl.BlockSpec((bsz, d), lambda j: (0, 0)),
                  pl.BlockSpec((d, tn), lambda j: (0, j)),
                  pl.BlockSpec((1, tn), lambda j: (0, j))],
        out_specs=pl.BlockSpec((bsz, tn), lambda j: (0, j)),
        out_shape=jax.ShapeDtypeStruct((bsz, n), F32),
        compiler_params=_params(("arbitrary",)),
    )(c, w_ada, b_ada.reshape(1, n))


def _proj_kernel(x_ref, sh_ref, sc_ref, g_ref, w_ref, mu_ref, pa_ref, pb_ref, carry_ref):
    j = pl.program_id(1)

    @pl.when(j == 0)
    def _():
        carry_ref[...] = jnp.zeros_like(carry_ref)

    x = x_ref[0]
    y = x * lax.rsqrt(jnp.mean(x * x, axis=-1, keepdims=True) + RMS_EPS) * g_ref[...]
    h = y * (1.0 + sc_ref[0, 0]) + sh_ref[0, 0]
    p = _dot(h.astype(BF16), w_ref[...])
    pa_ref[0] = p[:, :N_A_PAD]
    pb = p[:, N_A_PAD:]
    tm = pb.shape[0]
    row = lax.broadcasted_iota(jnp.int32, (tm, 1), 0)
    prev = jnp.where(row == 0, carry_ref[...], pltpu.roll(pb, 1, axis=0))
    carry_ref[...] = pb[tm - 1:tm, :]
    pb_ref[0] = pb + mu_ref[...] * (prev - pb)


def _proj_call(x, mod4, g_mix, w_in_p, mu):
    bsz, s, d = x.shape
    n = w_in_p.shape[1]
    nb = n - N_A_PAD
    tm = TOK_TILE
    return pl.pallas_call(
        _proj_kernel,
        grid=(bsz, s // tm),
        in_specs=[pl.BlockSpec((1, tm, d), lambda b, j: (b, j, 0)),
                  pl.BlockSpec((1, 1, 1, d), lambda b, j: (b, 0, 0, 0)),
                  pl.BlockSpec((1, 1, 1, d), lambda b, j: (b, 1, 0, 0)),
                  pl.BlockSpec((1, d), lambda b, j: (0, 0)),
                  pl.BlockSpec((d, n), lambda b, j: (0, 0)),
                  pl.BlockSpec((1, nb), lambda b, j: (0, 0))],
        out_specs=[pl.BlockSpec((1, tm, N_A_PAD), lambda b, j: (b, j, 0)),
                   pl.BlockSpec((1, tm, nb), lambda b, j: (b, j, 0))],
        out_shape=[jax.ShapeDtypeStruct((bsz, s, N_A_PAD), F32),
                   jax.ShapeDtypeStruct((bsz, s, nb), F32)],
        scratch_shapes=[pltpu.VMEM((1, nb), F32)],
        compiler_params=_params(("arbitrary", "arbitrary")),
    )(x, mod4, mod4, g_mix, w_in_p, mu)


def _prep_a_kernel(pa_ref, gkv_ref, gqk_ref, wuk_ref, wukbd_ref, eb_ref, et_ref,
                   ckv_ref, invr_ref, qabs_ref, qidx_ref, kidx_ref, widx_ref):
    pa = pa_ref[0]
    q = pa[:, :A_WIDTH]
    cl = pa[:, A_WIDTH:A_WIDTH + KV_LATENT]
    o_qi = A_WIDTH + KV_LATENT
    qi = pa[:, o_qi:o_qi + IDX_HEADS * IDX_DIM]
    o_kw = o_qi + IDX_HEADS * IDX_DIM
    kw = pa[:, o_kw:o_kw + LANES]

    ckv = cl * lax.rsqrt(jnp.mean(cl * cl, axis=-1, keepdims=True) + RMS_EPS) * gkv_ref[...]
    ckv_b = ckv.astype(BF16)
    ckv_ref[0] = ckv_b
    kf = _dot(ckv_b, wuk_ref[...])
    kf2_hi, kf2_lo = _split(kf * kf)
    ss_t = _dot_nt(et_ref[...], kf2_hi) + _dot_nt(et_ref[...], kf2_lo)
    invr_ref[0, 0] = lax.rsqrt(ss_t * (1.0 / A_HEAD_DIM) + RMS_EPS)

    ssq = _dot_hl(q * q, eb_ref[...])
    qh = q * lax.rsqrt(ssq * (1.0 / A_HEAD_DIM) + RMS_EPS) * gqk_ref[...]
    qabs = _dot(qh.astype(BF16), wukbd_ref[...]) * (A_HEAD_DIM ** -0.5)
    qabs_ref[0] = qabs.astype(BF16)
    qidx_ref[0] = qi.astype(BF16)
    kidx_ref[0] = kw[:, :IDX_DIM].astype(BF16)
    widx_ref[0] = kw[:, IDX_DIM:IDX_DIM + IDX_HEADS] * (IDX_HEADS ** -0.5 * IDX_DIM ** -0.5)


def _prep_a_call(pa, gkv, gqk, wuk_flat, wuk_bd, eb, et):
    bsz, s, _ = pa.shape
    tm = TOK_TILE
    full = lambda shape: pl.BlockSpec(shape, lambda b, j: (0,) * len(shape))
    tok = lambda w: pl.BlockSpec((1, tm, w), lambda b, j: (b, j, 0))
    return pl.pallas_call(
        _prep_a_kernel,
        grid=(bsz, s // tm),
        in_specs=[tok(N_A_PAD), full((1, KV_LATENT)), full((1, A_WIDTH)),
                  full((KV_LATENT, A_WIDTH)), full((A_WIDTH, A_HEADS * KV_LATENT)),
                  full((A_WIDTH, A_WIDTH)), full((A_HEADS, A_WIDTH))],
        out_specs=[tok(KV_LATENT),
                   pl.BlockSpec((1, 1, A_HEADS, tm), lambda b, j: (b, j, 0, 0)),
                   tok(A_HEADS * KV_LATENT), tok(IDX_HEADS * IDX_DIM), tok(IDX_DIM), tok(IDX_HEADS)],
        out_shape=[jax.ShapeDtypeStruct((bsz, s, KV_LATENT), BF16),
                   jax.ShapeDtypeStruct((bsz, s // tm, A_HEADS, tm), F32),
                   jax.ShapeDtypeStruct((bsz, s, A_HEADS * KV_LATENT), BF16),
                   jax.ShapeDtypeStruct((bsz, s, IDX_HEADS * IDX_DIM), BF16),
                   jax.ShapeDtypeStruct((bsz, s, IDX_DIM), BF16),
                   jax.ShapeDtypeStruct((bsz, s, IDX_HEADS), F32)],
        compiler_params=_params(("arbitrary", "arbitrary")),
    )(pa, gkv, gqk, wuk_flat, wuk_bd, eb, et)


def _fold(x):
    return x[:, :LANES] + x[:, LANES:]


def _dsa_kernel(topk, qabs_ref, qidx_ref, widx_ref, ckv_ref, kidx_ref, invr_ref, wuv_ref, ustrict_ref,
                o_ref, score_ref, bias_ref, m_ref, acc_ref):
    i = pl.program_id(1)
    nkc = i + 1
    t0 = i * Q_TILE
    row = lax.broadcasted_iota(jnp.int32, (Q_TILE, 1), 0)
    col = lax.broadcasted_iota(jnp.int32, (1, K_TILE), 1)
    limit = ((t0 + row) // CHUNK + 1) * CHUNK
    kp = jnp.minimum(limit, topk).astype(F32)

    w = widx_ref[0]

    def p1(kc, carry):
        rmin, rmax = carry
        k = kidx_ref[0, pl.ds(pl.multiple_of(kc * K_TILE, K_TILE), K_TILE), :]
        acc = jnp.zeros((Q_TILE, K_TILE), F32)
        for h in range(IDX_HEADS):
            s = _dot_nt(qidx_ref[0, :, h * IDX_DIM:(h + 1) * IDX_DIM], k)
            acc = acc + w[:, h:h + 1] * jnp.maximum(s, 0.0)
        adm = (kc * K_TILE + col) < limit
        score_ref[kc] = jnp.where(adm, acc, -jnp.inf)
        lo_c = jnp.where(adm, acc, jnp.inf)
        hi_c = jnp.where(adm, acc, -jnp.inf)
        rmin = jnp.minimum(rmin, jnp.minimum(lo_c[:, :LANES], lo_c[:, LANES:]))
        rmax = jnp.maximum(rmax, jnp.maximum(hi_c[:, :LANES], hi_c[:, LANES:]))
        return rmin, rmax

    rmin, rmax = lax.fori_loop(
        0, nkc, p1, (jnp.full((Q_TILE, LANES), jnp.inf, F32), jnp.full((Q_TILE, LANES), -jnp.inf, F32)))
    lo = jnp.min(rmin, axis=1, keepdims=True)
    hi = jnp.max(rmax, axis=1, keepdims=True)

    def count_ge(v):
        def body(kc, acc):
            return acc + _fold(jnp.where(score_ref[kc] >= v, 1.0, 0.0))
        acc = lax.fori_loop(0, nkc, body, jnp.zeros((Q_TILE, LANES), F32))
        return jnp.sum(acc, axis=1, keepdims=True)

    def bis(_, carry):
        lo, hi = carry
        mid = lo + 0.5 * (hi - lo)
        ge = count_ge(mid) >= kp
        return jnp.where(ge, mid, lo), jnp.where(ge, hi, mid)

    lo, hi = lax.fori_loop(0, BISECT_ITERS, bis, (lo, hi))

    def thr_body(kc, acc):
        sc = score_ref[kc]
        c = jnp.where(sc >= lo, sc, jnp.inf)
        return jnp.minimum(acc, jnp.minimum(c[:, :LANES], c[:, LANES:]))

    thr = jnp.min(lax.fori_loop(0, nkc, thr_body, jnp.full((Q_TILE, LANES), jnp.inf, F32)),
                  axis=1, keepdims=True)

    def gt_body(kc, acc):
        return acc + _fold(jnp.where(score_ref[kc] > thr, 1.0, 0.0))

    cnt_gt = jnp.sum(lax.fori_loop(0, nkc, gt_body, jnp.zeros((Q_TILE, LANES), F32)), axis=1, keepdims=True)
    need = kp - cnt_gt

    def sel_body(kc, run):
        sc = score_ref[kc]
        eq = sc == thr
        eq_b = jnp.where(eq, 1.0, 0.0).astype(BF16)
        ext = _dot(eq_b, ustrict_ref[...])
        pre = run + ext[:, :K_TILE]
        keep = (sc > thr) | (eq & (pre < need))
        bias_ref[kc] = jnp.where(keep, 0.0, NEG_BIG)
        return run + ext[:, K_TILE:K_TILE + 1]

    lax.fori_loop(0, nkc, sel_body, jnp.zeros((Q_TILE, 1), F32))

    rel = (row - col).astype(F32)
    ones_v = jnp.ones((K_TILE, LANES), BF16)
    for pair in range(A_HEADS // 2):
        o_pair = []
        for hh in range(2):
            h = 2 * pair + hh
            slope = 2.0 ** (-8.0 * (h + 1) / A_HEADS)
            m_ref[...] = jnp.full(m_ref.shape, NEG_BIG, F32)
            acc_ref[...] = jnp.zeros(acc_ref.shape, F32)
            qa = qabs_ref[0, :, h * KV_LATENT:(h + 1) * KV_LATENT]

            def att(kc, _, h=h, slope=slope, qa=qa):
                c = ckv_ref[0, pl.ds(pl.multiple_of(kc * K_TILE, K_TILE), K_TILE), :]
                logit = _dot_nt(qa, c) * invr_ref[0, kc, h:h + 1, :]
                dist = jnp.abs(rel + (t0 - kc * K_TILE).astype(F32))
                logit = logit - slope * dist + bias_ref[kc]
                m_old = m_ref[...]
                m_new = jnp.maximum(m_old, jnp.max(logit, axis=1, keepdims=True))
                alpha = jnp.exp(m_old - m_new)
                p = jnp.exp(logit - m_new[:, :1])
                cv = jnp.concatenate([c, ones_v], axis=1)
                acc_ref[...] = acc_ref[...] * jnp.concatenate([alpha, alpha], axis=1) + _dot(p.astype(BF16), cv)
                m_ref[...] = m_new
                return 0

            lax.fori_loop(0, nkc, att, 0)
            a = acc_ref[...]
            o_pair.append((a[:, :KV_LATENT] / a[:, KV_LATENT:]).astype(BF16))
        o_lat = jnp.concatenate(o_pair, axis=1)
        o_ref[0, :, pair * LANES:(pair + 1) * LANES] = _dot(o_lat, wuv_ref[pair]).astype(o_ref.dtype)


def _dsa_call(topk, qabs, qidx, widx, ckv, kidx, invr, wuv_pair, ustrict):
    bsz, s, _ = qabs.shape
    nq = s // Q_TILE
    nk = s // K_TILE
    qt = lambda w: pl.BlockSpec((1, Q_TILE, w), lambda b, i: (b, i, 0))
    return pl.pallas_call(
        functools.partial(_dsa_kernel, topk),
        grid=(bsz, nq),
        in_specs=[qt(A_HEADS * KV_LATENT), qt(IDX_HEADS * IDX_DIM), qt(IDX_HEADS),
                  pl.BlockSpec((1, s, KV_LATENT), lambda b, i: (b, 0, 0)),
                  pl.BlockSpec((1, s, IDX_DIM), lambda b, i: (b, 0, 0)),
                  pl.BlockSpec((1, nk, A_HEADS, K_TILE), lambda b, i: (b, 0, 0, 0)),
                  pl.BlockSpec((A_HEADS // 2, 2 * KV_LATENT, LANES), lambda b, i: (0, 0, 0)),
                  pl.BlockSpec((K_TILE, K_TILE + LANES), lambda b, i: (0, 0))],
        out_specs=qt(A_WIDTH),
        out_shape=jax.ShapeDtypeStruct((bsz, s, A_WIDTH), BF16),
        scratch_shapes=[pltpu.VMEM((nk, Q_TILE, K_TILE), F32),
                        pltpu.VMEM((nk, Q_TILE, K_TILE), F32),
                        pltpu.VMEM((Q_TILE, LANES), F32),
                        pltpu.VMEM((Q_TILE, 2 * KV_LATENT), F32)],
        compiler_params=_params(("arbitrary", "arbitrary")),
    )(qabs, qidx, widx, ckv, kidx, invr, wuv_pair, ustrict)


def _prep_b_kernel(pb_ref, w0_ref, w2_ref, a0_ref, a2_ref, g2_ref, kk_ref, ka_ref, rk_ref, eb_ref, tri_ref,
                   rt_ref, kt_ref, bt_ref, kl_ref, v_ref, g_ref, bv_ref, pc_ref):
    pb = pb_ref[0]
    r = pb[:, :B_WIDTH]
    k = pb[:, B_WIDTH:2 * B_WIDTH]
    v = pb[:, 2 * B_WIDTH:3 * B_WIDTH]
    o = 3 * B_WIDTH
    xw = pb[:, o:o + W_LORA]
    xa = pb[:, o + W_LORA:o + W_LORA + A_LORA]
    xg = pb[:, o + W_LORA + A_LORA:o + W_LORA + A_LORA + G_LORA]

    z = w0_ref[...] + _dot(jnp.tanh(xw).astype(BF16), w2_ref[...])
    nz = -z
    softplus = jnp.maximum(nz, 0.0) + jnp.log(1.0 + jnp.exp(-jnp.abs(nz)))
    lw = -jnp.exp(-softplus - 0.5)
    a = jax.nn.sigmoid(a0_ref[...] + _dot(xa.astype(BF16), a2_ref[...]))
    g = _dot(jax.nn.sigmoid(xg).astype(BF16), g2_ref[...])
    kk = k * kk_ref[...]
    kkn = kk / jnp.maximum(jnp.sqrt(_dot_hl(kk * kk, eb_ref[...])), 1e-12)
    kp = k * (1.0 + (a - 1.0) * ka_ref[...])
    bonus = _dot_hl(r * kp * rk_ref[...], eb_ref[...])

    lw_hi, lw_lo = _split(lw)
    cum = _dot(tri_ref[...], lw_hi) + _dot(tri_ref[...], lw_lo)
    e_pos = jnp.exp(cum)
    e_neg = jnp.exp(-cum)
    rt_ref[0] = (r * e_pos).astype(BF16)
    kt_ref[0] = (kkn * jnp.exp(cum - lw)).astype(BF16)
    bt_ref[0] = (kkn * a * e_neg).astype(BF16)
    kl_ref[0] = (kp * e_neg).astype(BF16)
    v_ref[0] = v.astype(BF16)
    g_ref[0] = g
    bv_ref[0] = bonus * v
    for c in range(pb.shape[0] // CHUNK):
        pc_ref[0, 0, c:c + 1, :] = e_pos[(c + 1) * CHUNK - 1:(c + 1) * CHUNK, :]


def _prep_b_call(pb, w0, w2, a0, a2, g2, k_k, k_a, r_k, eb, tri):
    bsz, s, nb = pb.shape
    tm = TOK_TILE
    full = lambda shape: pl.BlockSpec(shape, lambda b, j: (0,) * len(shape))
    tok = lambda w: pl.BlockSpec((1, tm, w), lambda b, j: (b, j, 0))
    row = full((1, B_WIDTH))
    bf = jax.ShapeDtypeStruct((bsz, s, B_WIDTH), BF16)
    ff = jax.ShapeDtypeStruct((bsz, s, B_WIDTH), F32)
    return pl.pallas_call(
        _prep_b_kernel,
        grid=(bsz, s // tm),
        in_specs=[tok(nb), row, full((W_LORA, B_WIDTH)), row, full((A_LORA, B_WIDTH)),
                  full((G_LORA, B_WIDTH)), row, row, row, full((B_WIDTH, B_WIDTH)), full((tm, tm))],
        out_specs=[tok(B_WIDTH)] * 7 + [pl.BlockSpec((1, 1, tm // CHUNK, B_WIDTH), lambda b, j: (b, j, 0, 0))],
        out_shape=[bf, bf, bf, bf, bf, ff, ff,
                   jax.ShapeDtypeStruct((bsz, s // tm, tm // CHUNK, B_WIDTH), F32)],
        compiler_params=_params(("arbitrary", "arbitrary")),
    )(pb, w0, w2, a0, a2, g2, k_k, k_a, r_k, eb, tri)


def _rwkv_kernel(rt_ref, kt_ref, bt_ref, kl_ref, v_ref, g_ref, bv_ref, pc_ref, lnw_ref, lnb_ref, eb_ref,
                 o_ref, h_ref):
    j = pl.program_id(1)

    @pl.when(j == 0)
    def _():
        h_ref[...] = jnp.zeros_like(h_ref)

    tm = rt_ref.shape[1]
    nch = tm // CHUNK
    ri = lax.broadcasted_iota(jnp.int32, (tm, tm), 0)
    ci = lax.broadcasted_iota(jnp.int32, (tm, tm), 1)
    same = (ri // CHUNK) == (ci // CHUNK)
    strict = same & (ri > ci)
    incl = same & (ri >= ci)
    eye_t = jnp.where(ri == ci, 1.0, 0.0)
    r2 = lax.broadcasted_iota(jnp.int32, (LANES, LANES), 0)
    c2 = lax.broadcasted_iota(jnp.int32, (LANES, LANES), 1)
    blk = (r2 // B_HEAD_DIM) == (c2 // B_HEAD_DIM)
    diag = r2 == c2
    lane = lax.broadcasted_iota(jnp.int32, (1, LANES), 1)
    zero_b = jnp.zeros((), BF16)

    outs = []
    for p in range(B_HEADS // 2):
        sl = slice(p * LANES, (p + 1) * LANES)
        rt = rt_ref[0, :, sl]
        kt = kt_ref[0, :, sl]
        bt = bt_ref[0, :, sl]
        kl = kl_ref[0, :, sl]
        v = v_ref[0, :, sl]
        w_pair = u_pair = q_pair = ol_pair = None
        for hh in range(2):
            hm = (lane // B_HEAD_DIM) == hh
            kt_m = jnp.where(hm, kt, zero_b)
            rt_m = jnp.where(hm, rt, zero_b)
            a_ab = jnp.where(strict, _dot_nt(kt_m, bt), 0.0)
            a_ak = jnp.where(strict, _dot_nt(kt_m, kl), 0.0)
            m_rb = jnp.where(incl, _dot_nt(rt_m, bt), 0.0).astype(BF16)
            m_rk = jnp.where(incl, _dot_nt(rt_m, kl), 0.0).astype(BF16)
            t_inv = eye_t - a_ab
            a_pow = a_ab.astype(BF16)
            for _ in range(5):
                a_sq = _dot(a_pow, a_pow)
                a_pow = a_sq.astype(BF16)
                t_inv = _dot(t_inv.astype(BF16), (eye_t + a_sq).astype(BF16))
            t_inv = t_inv.astype(BF16)
            av = _dot(a_ak.astype(BF16), v)
            w_h = -_dot(t_inv, kt)
            u_h = -_dot(t_inv, av.astype(BF16))
            q_h = rt.astype(F32) + _dot(m_rb, w_h.astype(BF16))
            ol_h = _dot(m_rb, u_h.astype(BF16)) + _dot(m_rk, v)
            if hh == 0:
                w_pair, u_pair, q_pair, ol_pair = w_h, u_h, q_h, ol_h
            else:
                w_pair = jnp.where(hm, w_h, w_pair)
                u_pair = jnp.where(hm, u_h, u_pair)
                q_pair = jnp.where(hm, q_h, q_pair)
                ol_pair = jnp.where(hm, ol_h, ol_pair)
        w_b = w_pair.astype(BF16)
        u_b = u_pair.astype(BF16)
        q_b = q_pair.astype(BF16)
        o_chunks = []
        h = h_ref[p]
        for c in range(nch):
            rows = slice(c * CHUNK, (c + 1) * CHUNK)
            pc = pc_ref[0, 0, c:c + 1, sl]
            bh = (bt[rows].astype(F32) * pc).astype(BF16)
            kh = (kl[rows].astype(F32) * pc).astype(BF16)
            h_b = h.astype(BF16)
            o_chunks.append(_dot(q_b[rows], h_b) + ol_pair[rows])
            g_mat = jnp.where(diag, pc, 0.0) + jnp.where(blk, _dot_tn(bh, w_b[rows]), 0.0)
            f_mat = jnp.where(blk, _dot_tn(bh, u_b[rows]) + _dot_tn(kh, v[rows]), 0.0)
            h = _dot(g_mat.astype(BF16), h_b) + f_mat
        h_ref[p] = h
        outs.append(jnp.concatenate(o_chunks, axis=0))
    out = jnp.concatenate(outs, axis=1)

    eb = eb_ref[...]
    mean = _dot_hl(out, eb) * (1.0 / B_HEAD_DIM)
    d = out - mean
    var = _dot_hl(d * d, eb) * (1.0 / B_HEAD_DIM)
    y = d * lax.rsqrt(var + GN_EPS) * lnw_ref[...] + lnb_ref[...] + bv_ref[0]
    o_ref[0] = (y * g_ref[0]).astype(o_ref.dtype)


def _rwkv_call(rt, kt, bt, kl, v, g, bv, pc, ln_w, ln_b, eb):
    bsz, s, _ = rt.shape
    tm = TOK_TILE
    tok = pl.BlockSpec((1, tm, B_WIDTH), lambda b, j: (b, j, 0))
    row = pl.BlockSpec((1, B_WIDTH), lambda b, j: (0, 0))
    return pl.pallas_call(
        _rwkv_kernel,
        grid=(bsz, s // tm),
        in_specs=[tok] * 7 + [pl.BlockSpec((1, 1, tm // CHUNK, B_WIDTH), lambda b, j: (b, j, 0, 0)),
                              row, row, pl.BlockSpec((B_WIDTH, B_WIDTH), lambda b, j: (0, 0))],
        out_specs=tok,
        out_shape=jax.ShapeDtypeStruct((bsz, s, B_WIDTH), BF16),
        scratch_shapes=[pltpu.VMEM((B_HEADS // 2, LANES, LANES), F32)],
        compiler_params=_params(("arbitrary", "arbitrary")),
    )(rt, kt, bt, kl, v, g, bv, pc, ln_w, ln_b, eb)


def _ffn_kernel(x_ref, oa_ref, ob_ref, gt1_ref, sh2_ref, sc2_ref, gt2_ref, gf_ref, woa_ref, wob_ref,
                w1_ref, w2_ref, o_ref, x1_ref, h2_ref, acc_ref):
    f = pl.program_id(2)

    @pl.when(f == 0)
    def _():
        mix = _dot(oa_ref[0], woa_ref[...]) + _dot(ob_ref[0], wob_ref[...])
        x1 = x_ref[0] + gt1_ref[0, 0] * mix
        x1_ref[...] = x1
        y = x1 * lax.rsqrt(jnp.mean(x1 * x1, axis=-1, keepdims=True) + RMS_EPS) * gf_ref[...]
        h2_ref[...] = (y * (1.0 + sc2_ref[0, 0]) + sh2_ref[0, 0]).astype(BF16)
        acc_ref[...] = jnp.zeros_like(acc_ref)

    u = jnp.maximum(_dot(h2_ref[...], w1_ref[...]), 0.0)
    acc_ref[...] += _dot((u * u).astype(BF16), w2_ref[...])

    @pl.when(f == pl.num_programs(2) - 1)
    def _():
        o_ref[0] = x1_ref[...] + gt2_ref[0, 0] * acc_ref[...]


def _ffn_call(x, oa, ob, mod4, g_ffn, w_out_a, w_out_b, w1, w2):
    bsz, s, d = x.shape
    dff = w1.shape[1]
    tm = 512 if s % 512 == 0 else TOK_TILE
    tf = 1024
    tok = lambda w: pl.BlockSpec((1, tm, w), lambda b, j, f: (b, j, 0))
    modk = lambda k: pl.BlockSpec((1, 1, 1, d), lambda b, j, f, k=k: (b, k, 0, 0))
    return pl.pallas_call(
        _ffn_kernel,
        grid=(bsz, s // tm, dff // tf),
        in_specs=[tok(d), tok(A_WIDTH), tok(B_WIDTH), modk(2), modk(3), modk(4), modk(5),
                  pl.BlockSpec((1, d), lambda b, j, f: (0, 0)),
                  pl.BlockSpec((A_WIDTH, d), lambda b, j, f: (0, 0)),
                  pl.BlockSpec((B_WIDTH, d), lambda b, j, f: (0, 0)),
                  pl.BlockSpec((d, tf), lambda b, j, f: (0, f)),
                  pl.BlockSpec((tf, d), lambda b, j, f: (f, 0))],
        out_specs=tok(d),
        out_shape=jax.ShapeDtypeStruct((bsz, s, d), F32),
        scratch_shapes=[pltpu.VMEM((tm, d), F32), pltpu.VMEM((tm, d), BF16), pltpu.VMEM((tm, d), F32)],
        compiler_params=_params(("arbitrary", "arbitrary", "arbitrary")),
    )(x, oa, ob, mod4, mod4, mod4, mod4, g_ffn, w_out_a, w_out_b, w1, w2)


def _block_ones(n, blk, dtype=BF16):
    i = jnp.arange(n)
    return ((i[:, None] // blk) == (i[None, :] // blk)).astype(dtype)


def kernel(x, c, w_ada, b_ada, g_mix, g_ffn, w_in, g_q, g_k, g_kv, w_uk, w_uv, mu_shift, w0, w2, a0, a2, g2,
           k_k, k_a, r_k, ln_w, ln_b, w_out, w_ff1, w_ff2):
    bsz, s, d = x.shape
    depth = w_ada.shape[0]
    assert s % Q_TILE == 0 and s % TOK_TILE == 0
    topk = min(TOPK_MAX, s // 4)

    eb = _block_ones(B_WIDTH, B_HEAD_DIM)
    et = (jnp.arange(A_HEADS)[:, None] == (jnp.arange(A_WIDTH)[None, :] // A_HEAD_DIM)).astype(BF16)
    ti = jnp.arange(TOK_TILE)
    tri = (((ti[:, None] // CHUNK) == (ti[None, :] // CHUNK)) & (ti[:, None] >= ti[None, :])).astype(BF16)
    ki = jnp.arange(K_TILE)
    ustrict = jnp.concatenate([(ki[:, None] < ki[None, :]).astype(BF16), jnp.ones((K_TILE, LANES), BF16)], axis=1)

    for l in range(depth):
        w_a = jnp.pad(w_in[l][:, :N_IN_A], ((0, 0), (0, N_A_PAD - N_IN_A)))
        w_in_p = jnp.concatenate([w_a, w_in[l][:, N_IN_A:]], axis=1).astype(BF16)
        wuk_flat = w_uk[l].reshape(KV_LATENT, A_WIDTH).astype(BF16)
        wuk_t = jnp.transpose(w_uk[l], (1, 2, 0))
        wuk_bd = (jnp.eye(A_HEADS, dtype=F32)[:, None, :, None] * wuk_t[:, :, None, :]).reshape(
            A_WIDTH, A_HEADS * KV_LATENT).astype(BF16)
        wuv_t = jnp.transpose(w_uv[l], (1, 0, 2)).reshape(A_HEADS // 2, 2, KV_LATENT, A_HEAD_DIM)
        wuv_pair = (jnp.eye(2, dtype=F32)[None, :, None, :, None] * wuv_t[:, :, :, None, :]).reshape(
            A_HEADS // 2, 2 * KV_LATENT, 2 * A_HEAD_DIM).astype(BF16)
        gqk = jnp.tile(g_q[l] * g_k[l], A_HEADS).reshape(1, A_WIDTH)
        r1 = lambda t: t.reshape(1, -1)

        mod = _mod_call(c, w_ada[l], b_ada[l])
        mod4 = mod.reshape(bsz, 6, 1, d)
        pa, pb = _proj_call(x, mod4, r1(g_mix[l]), w_in_p, r1(mu_shift[l]))
        ckv, invr, qabs, qidx, kidx, widx = _prep_a_call(pa, r1(g_kv[l]), gqk, wuk_flat, wuk_bd, eb, et)
        o_a = _dsa_call(topk, qabs, qidx, widx, ckv, kidx, invr, wuv_pair, ustrict)
        rt, kt, bt, kl, v, g, bv, pc = _prep_b_call(
            pb, r1(w0[l]), w2[l].astype(BF16), r1(a0[l]), a2[l].astype(BF16), g2[l].astype(BF16),
            r1(k_k[l]), r1(k_a[l]), r1(r_k[l]), eb, tri)
        o_b = _rwkv_call(rt, kt, bt, kl, v, g, bv, pc, r1(ln_w[l]), r1(ln_b[l]), eb)
        x = _ffn_call(x, o_a, o_b, mod4, r1(g_ffn[l]), w_out[l][:A_WIDTH].astype(BF16),
                      w_out[l][A_WIDTH:].astype(BF16), w_ff1[l].astype(BF16), w_ff2[l].astype(BF16))
    return x
```

```python
import functools

import jax
import jax.numpy as jnp
from jax import lax
from jax.experimental import pallas as pl
from jax.experimental.pallas import tpu as pltpu

F32 = jnp.float32
BF16 = jnp.bfloat16

CHUNK = 64
A_HEADS = 8
A_HEAD_DIM = 64
A_WIDTH = A_HEADS * A_HEAD_DIM
KV_LATENT = 128
IDX_HEADS = 8
IDX_DIM = 64
TOPK_MAX = 256
B_HEADS = 8
B_HEAD_DIM = 64
B_WIDTH = B_HEADS * B_HEAD_DIM
W_LORA = 64
A_LORA = 64
G_LORA = 128
RMS_EPS = 1e-6
GN_EPS = 64e-5
N_IN_A = A_WIDTH + KV_LATENT + IDX_HEADS * IDX_DIM + IDX_DIM + IDX_HEADS
N_IN_B = 3 * B_WIDTH + W_LORA + A_LORA + G_LORA
N_A_PAD = 1280

LANES = 128
TOK_TILE = 256
Q_TILE = 256
K_TILE = 256
DIST_BIG = 1e30
ONES_ROWS = 16
LOG2E = 1.4426950408889634
BISECT_FIRST = 12
BISECT_STEP = 4
BISECT_MAX = 40
VMEM_LIMIT = 56 * 1024 * 1024


def _dot(a, b):
    return jnp.dot(a, b, preferred_element_type=F32)


def _dot_nt(a, b):
    return lax.dot_general(a, b, (((1,), (1,)), ((), ())), preferred_element_type=F32)


def _dot_tn(a, b):
    return lax.dot_general(a, b, (((0,), (0,)), ((), ())), preferred_element_type=F32)


def _split(x):
    hi = x.astype(BF16)
    lo = (x - hi.astype(F32)).astype(BF16)
    return hi, lo


def _dot_hl(x, e):
    hi, lo = _split(x)
    return _dot(hi, e) + _dot(lo, e)


def _params(sem):
    return pltpu.CompilerParams(dimension_semantics=sem, vmem_limit_bytes=VMEM_LIMIT)


def _mod_kernel(c_ref, w_ref, b_ref, o_ref):
    c = c_ref[...]
    s = c * jax.nn.sigmoid(c)
    s_hi, s_lo = _split(s)
    w_hi, w_lo = _split(w_ref[...])
    o_ref[...] = _dot(s_hi, w_hi) + _dot(s_hi, w_lo) + _dot(s_lo, w_hi) + b_ref[...]


def _mod_call(c, w_ada, b_ada):
    bsz, d = c.shape
    n = w_ada.shape[1]
    tn = 1024
    return pl.pallas_call(
        _mod_kernel,
        grid=(n // tn,),
        in_specs=[pl.BlockSpec((bsz, d), lambda j: (0, 0)),
                  pl.BlockSpec((d, tn), lambda j: (0, j)),
                  pl.BlockSpec((1, tn), lambda j: (0, j))],
        out_specs=pl.BlockSpec((bsz, tn), lambda j: (0, j)),
        out_shape=jax.ShapeDtypeStruct((bsz, n), F32),
        compiler_params=_params(("arbitrary",)),
    )(c, w_ada, b_ada.reshape(1, n))


def _proj_kernel(x_ref, sh_ref, sc_ref, g_ref, w_ref, mu_ref, pa_ref, pb_ref, carry_ref):
    j = pl.program_id(1)

    @pl.when(j == 0)
    def _():
        carry_ref[...] = jnp.zeros_like(carry_ref)

    x = x_ref[0]
    y = x * lax.rsqrt(jnp.mean(x * x, axis=-1, keepdims=True) + RMS_EPS) * g_ref[...]
    h = y * (1.0 + sc_ref[0, 0]) + sh_ref[0, 0]
    p = _dot(h.astype(BF16), w_ref[...])
    pa_ref[0] = p[:, :N_A_PAD]
    pb = p[:, N_A_PAD:]
    tm = pb.shape[0]
    row = lax.broadcasted_iota(jnp.int32, (tm, 1), 0)
    prev = jnp.where(row == 0, carry_ref[...], pltpu.roll(pb, 1, axis=0))
    carry_ref[...] = pb[tm - 1:tm, :]
    pb_ref[0] = pb + mu_ref[...] * (prev - pb)


def _proj_call(x, mod4, g_mix, w_in_p, mu):
    bsz, s, d = x.shape
    n = w_in_p.shape[1]
    nb = n - N_A_PAD
    tm = TOK_TILE
    return pl.pallas_call(
        _proj_kernel,
        grid=(bsz, s // tm),
        in_specs=[pl.BlockSpec((1, tm, d), lambda b, j: (b, j, 0)),
                  pl.BlockSpec((1, 1, 1, d), lambda b, j: (b, 0, 0, 0)),
                  pl.BlockSpec((1, 1, 1, d), lambda b, j: (b, 1, 0, 0)),
                  pl.BlockSpec((1, d), lambda b, j: (0, 0)),
                  pl.BlockSpec((d, n), lambda b, j: (0, 0)),
                  pl.BlockSpec((1, nb), lambda b, j: (0, 0))],
        out_specs=[pl.BlockSpec((1, tm, N_A_PAD), lambda b, j: (b, j, 0)),
                   pl.BlockSpec((1, tm, nb), lambda b, j: (b, j, 0))],
        out_shape=[jax.ShapeDtypeStruct((bsz, s, N_A_PAD), F32),
                   jax.ShapeDtypeStruct((bsz, s, nb), F32)],
        scratch_shapes=[pltpu.VMEM((1, nb), F32)],
        compiler_params=_params(("arbitrary", "arbitrary")),
    )(x, mod4, mod4, g_mix, w_in_p, mu)


def _prep_a_kernel(pa_ref, gkv_ref, gqk_ref, wuk_ref, wukbd_ref, eb_ref, ex_ref, sel_ref, eye_ref,
                   ckr_ref, cvt_ref, qabs_ref, qidx_ref, kidx_ref, widx_ref):
    pa = pa_ref[0]
    tm = pa.shape[0]
    q = pa[:, :A_WIDTH]
    cl = pa[:, A_WIDTH:A_WIDTH + KV_LATENT]
    o_qi = A_WIDTH + KV_LATENT
    qi = pa[:, o_qi:o_qi + IDX_HEADS * IDX_DIM]
    o_kw = o_qi + IDX_HEADS * IDX_DIM
    kw = pa[:, o_kw:o_kw + LANES]

    ckv = cl * lax.rsqrt(jnp.mean(cl * cl, axis=-1, keepdims=True) + RMS_EPS) * gkv_ref[...]
    ckv_b = ckv.astype(BF16)
    cvt_ref[0, 0, :KV_LATENT, :] = _dot_nt(eye_ref[...], ckv_b).astype(BF16)
    cvt_ref[0, 0, KV_LATENT:, :] = jnp.ones((ONES_ROWS, tm), BF16)
    kf = _dot(ckv_b, wuk_ref[...])
    ss = _dot_hl(kf * kf, ex_ref[...])
    inv_rms = lax.rsqrt(ss * (1.0 / A_HEAD_DIM) + RMS_EPS)
    ckr_ref[0] = (jnp.concatenate([ckv] * A_HEADS, axis=1) * inv_rms).astype(BF16)

    ssq = _dot_hl(q * q, eb_ref[...])
    qh = q * lax.rsqrt(ssq * (1.0 / A_HEAD_DIM) + RMS_EPS) * gqk_ref[...]
    qabs = _dot(qh.astype(BF16), wukbd_ref[...]) * (A_HEAD_DIM ** -0.5 * LOG2E)
    qabs_ref[0] = qabs.astype(BF16)
    for h in range(IDX_HEADS):
        qidx_ref[0, h] = qi[:, h * IDX_DIM:(h + 1) * IDX_DIM].astype(BF16)
    kidx_ref[0] = kw[:, :IDX_DIM].astype(BF16)
    kw_hi, kw_lo = _split(kw)
    w_t = _dot_nt(sel_ref[...], kw_hi) + _dot_nt(sel_ref[...], kw_lo)
    widx_ref[0, 0] = w_t * (IDX_HEADS ** -0.5 * IDX_DIM ** -0.5)


def _prep_a_call(pa, gkv, gqk, wuk_flat, wuk_bd, eb, ex, sel, eye):
    bsz, s, _ = pa.shape
    tm = TOK_TILE
    full = lambda shape: pl.BlockSpec(shape, lambda b, j: (0,) * len(shape))
    tok = lambda w: pl.BlockSpec((1, tm, w), lambda b, j: (b, j, 0))
    return pl.pallas_call(
        _prep_a_kernel,
        grid=(bsz, s // tm),
        in_specs=[tok(N_A_PAD), full((1, KV_LATENT)), full((1, A_WIDTH)),
                  full((KV_LATENT, A_WIDTH)), full((A_WIDTH, A_HEADS * KV_LATENT)),
                  full((A_WIDTH, A_WIDTH)), full((A_WIDTH, A_HEADS * KV_LATENT)),
                  full((IDX_HEADS, LANES)), full((KV_LATENT, KV_LATENT))],
        out_specs=[tok(A_HEADS * KV_LATENT),
                   pl.BlockSpec((1, 1, KV_LATENT + ONES_ROWS, tm), lambda b, j: (b, j, 0, 0)),
                   tok(A_HEADS * KV_LATENT),
                   pl.BlockSpec((1, IDX_HEADS, tm, IDX_DIM), lambda b, j: (b, 0, j, 0)),
                   tok(IDX_DIM),
                   pl.BlockSpec((1, 1, IDX_HEADS, tm), lambda b, j: (b, j, 0, 0))],
        out_shape=[jax.ShapeDtypeStruct((bsz, s, A_HEADS * KV_LATENT), BF16),
                   jax.ShapeDtypeStruct((bsz, s // tm, KV_LATENT + ONES_ROWS, tm), BF16),
                   jax.ShapeDtypeStruct((bsz, s, A_HEADS * KV_LATENT), BF16),
                   jax.ShapeDtypeStruct((bsz, IDX_HEADS, s, IDX_DIM), BF16),
                   jax.ShapeDtypeStruct((bsz, s, IDX_DIM), BF16),
                   jax.ShapeDtypeStruct((bsz, s // tm, IDX_HEADS, tm), F32)],
        compiler_params=_params(("arbitrary", "arbitrary")),
    )(pa, gkv, gqk, wuk_flat, wuk_bd, eb, ex, sel, eye)


def _colsum8(x):
    y = x.reshape(4, K_TILE // 32, 8, Q_TILE)
    return jnp.sum(jnp.sum(y, axis=1), axis=0)


def _colmin8(x):
    y = x.reshape(4, K_TILE // 32, 8, Q_TILE)
    return jnp.min(jnp.min(y, axis=1), axis=0)


def _colmax8(x):
    y = x.reshape(4, K_TILE // 32, 8, Q_TILE)
    return jnp.max(jnp.max(y, axis=1), axis=0)


def _dsa_kernel(topk, qabs_ref, qidx_ref, widx_ref, ckr_ref, cvt_ref, kidx_ref, wuv_ref, lstrict_ref,
                o_ref, score_ref, dist_ref, logit_ref, m_ref, acc_ref):
    i = pl.program_id(1)
    nkc = i + 1
    t0 = i * Q_TILE
    krow = lax.broadcasted_iota(jnp.int32, (K_TILE, 1), 0)
    qcol = lax.broadcasted_iota(jnp.int32, (1, Q_TILE), 1)
    limit = ((t0 + qcol) // CHUNK + 1) * CHUNK
    kp = jnp.minimum(limit, topk).astype(F32)
    rel = (qcol - krow).astype(F32)

    def p1(kc, carry):
        rmin, rmax = carry
        k = kidx_ref[0, pl.ds(pl.multiple_of(kc * K_TILE, K_TILE), K_TILE), :]
        acc = jnp.zeros((K_TILE, Q_TILE), F32)
        for h in range(IDX_HEADS):
            s = _dot_nt(k, qidx_ref[0, h])
            acc = acc + widx_ref[0, 0, h:h + 1, :] * jnp.maximum(s, 0.0)
        adm = (kc * K_TILE + krow) < limit
        score_ref[kc] = jnp.where(adm, acc, -jnp.inf)
        rmin = jnp.minimum(rmin, _colmin8(jnp.where(adm, acc, jnp.inf)))
        rmax = jnp.maximum(rmax, _colmax8(jnp.where(adm, acc, -jnp.inf)))
        return rmin, rmax

    rmin, rmax = lax.fori_loop(
        0, nkc, p1, (jnp.full((8, Q_TILE), jnp.inf, F32), jnp.full((8, Q_TILE), -jnp.inf, F32)))
    lo = jnp.min(rmin, axis=0, keepdims=True)
    hi = jnp.max(rmax, axis=0, keepdims=True)

    def count(pred):
        def body(kc, acc):
            return acc + _colsum8(jnp.where(pred(score_ref[kc]), 1.0, 0.0))
        return jnp.sum(lax.fori_loop(0, nkc, body, jnp.zeros((8, Q_TILE), F32)), axis=0, keepdims=True)

    def bis_block(n, lo, hi, cnt_lo):
        def body(_, c):
            lo, hi, cnt_lo = c
            mid = lo + 0.5 * (hi - lo)
            cnt = count(lambda sc: sc >= mid)
            ge = cnt >= kp
            return jnp.where(ge, mid, lo), jnp.where(ge, hi, mid), jnp.where(ge, cnt, cnt_lo)
        return lax.fori_loop(0, n, body, (lo, hi, cnt_lo))

    cnt_lo = count(lambda sc: sc >= lo)
    lo, hi, cnt_lo = bis_block(BISECT_FIRST, lo, hi, cnt_lo)

    def unsettled(cnt_lo):
        return jnp.max(jnp.where(cnt_lo == kp, 0.0, 1.0)) > 0.0

    def w_cond(c):
        it, _, _, cnt_lo = c
        return jnp.logical_and(it < BISECT_MAX, unsettled(cnt_lo))

    def w_body(c):
        it, lo, hi, cnt_lo = c
        lo, hi, cnt_lo = bis_block(BISECT_STEP, lo, hi, cnt_lo)
        return it + BISECT_STEP, lo, hi, cnt_lo

    _, lo, hi, cnt_lo = lax.while_loop(w_cond, w_body, (jnp.int32(BISECT_FIRST), lo, hi, cnt_lo))

    def dist_tile(kc):
        return jnp.abs(rel + (t0 - kc * K_TILE).astype(F32))

    def sel_plain():
        def body(kc, _):
            dist_ref[kc] = jnp.where(score_ref[kc] >= lo, dist_tile(kc), DIST_BIG)
            return 0
        lax.fori_loop(0, nkc, body, 0)

    def sel_ties():
        def thr_body(kc, acc):
            sc = score_ref[kc]
            return jnp.minimum(acc, _colmin8(jnp.where(sc >= lo, sc, jnp.inf)))
        thr = jnp.min(lax.fori_loop(0, nkc, thr_body, jnp.full((8, Q_TILE), jnp.inf, F32)), axis=0, keepdims=True)
        need = kp - count(lambda sc: sc > thr)

        def body(kc, run):
            sc = score_ref[kc]
            eq = sc == thr
            eq_f = jnp.where(eq, 1.0, 0.0)
            pre = run + _dot(lstrict_ref[...], eq_f.astype(BF16))
            keep = (sc > thr) | (eq & (pre < need))
            dist_ref[kc] = jnp.where(keep, dist_tile(kc), DIST_BIG)
            return run + jnp.sum(_colsum8(eq_f), axis=0, keepdims=True)
        lax.fori_loop(0, nkc, body, jnp.zeros((1, Q_TILE), F32))

    lax.cond(unsettled(cnt_lo), sel_ties, sel_plain)

    m_ref[...] = jnp.full(m_ref.shape, -jnp.inf, F32)
    acc_ref[...] = jnp.zeros(acc_ref.shape, F32)

    def att(kc, _):
        dist = dist_ref[kc]
        m_new = []
        for h in range(A_HEADS):
            slope = 2.0 ** (-8.0 * (h + 1) / A_HEADS) * LOG2E
            ck = ckr_ref[0, pl.ds(pl.multiple_of(kc * K_TILE, K_TILE), K_TILE), h * KV_LATENT:(h + 1) * KV_LATENT]
            logit = _dot_nt(ck, qabs_ref[0, :, h * KV_LATENT:(h + 1) * KV_LATENT]) - slope * dist
            logit_ref[h] = logit
            m_new.append(jnp.maximum(m_ref[h], jnp.max(_colmax8(logit), axis=0, keepdims=True)))
        cv = cvt_ref[0, kc]
        for h in range(A_HEADS):
            p = jnp.exp2(logit_ref[h] - m_new[h])
            acc_ref[h] = acc_ref[h] * jnp.exp2(m_ref[h] - m_new[h]) + _dot(cv, p.astype(BF16))
            m_ref[h] = m_new[h]
        return 0

    lax.fori_loop(0, nkc, att, 0)

    for pair in range(A_HEADS // 2):
        o_pair = []
        for hh in range(2):
            a = acc_ref[2 * pair + hh]
            o_t = a[:KV_LATENT] * (1.0 / a[KV_LATENT:KV_LATENT + 1])
            o_pair.append(o_t.T.astype(BF16))
        o_lat = jnp.concatenate(o_pair, axis=1)
        o_ref[0, :, pair * LANES:(pair + 1) * LANES] = _dot(o_lat, wuv_ref[pair]).astype(o_ref.dtype)


def _dsa_call(topk, qabs, qidx, widx, ckr, cvt, kidx, wuv_pair, lstrict):
    bsz, s, _ = qabs.shape
    nq = s // Q_TILE
    nk = s // K_TILE
    qt = lambda w: pl.BlockSpec((1, Q_TILE, w), lambda b, i: (b, i, 0))
    return pl.pallas_call(
        functools.partial(_dsa_kernel, topk),
        grid=(bsz, nq),
        in_specs=[qt(A_HEADS * KV_LATENT),
                  pl.BlockSpec((1, IDX_HEADS, Q_TILE, IDX_DIM), lambda b, i: (b, 0, i, 0)),
                  pl.BlockSpec((1, 1, IDX_HEADS, Q_TILE), lambda b, i: (b, i, 0, 0)),
                  pl.BlockSpec((1, s, A_HEADS * KV_LATENT), lambda b, i: (b, 0, 0)),
                  pl.BlockSpec((1, nk, KV_LATENT + ONES_ROWS, K_TILE), lambda b, i: (b, 0, 0, 0)),
                  pl.BlockSpec((1, s, IDX_DIM), lambda b, i: (b, 0, 0)),
                  pl.BlockSpec((A_HEADS // 2, 2 * KV_LATENT, LANES), lambda b, i: (0, 0, 0)),
                  pl.BlockSpec((K_TILE, K_TILE), lambda b, i: (0, 0))],
        out_specs=qt(A_WIDTH),
        out_shape=jax.ShapeDtypeStruct((bsz, s, A_WIDTH), BF16),
        scratch_shapes=[pltpu.VMEM((nk, K_TILE, Q_TILE), F32),
                        pltpu.VMEM((nk, K_TILE, Q_TILE), F32),
                        pltpu.VMEM((A_HEADS, K_TILE, Q_TILE), F32),
                        pltpu.VMEM((A_HEADS, 1, Q_TILE), F32),
                        pltpu.VMEM((A_HEADS, KV_LATENT + ONES_ROWS, Q_TILE), F32)],
        compiler_params=_params(("arbitrary", "arbitrary")),
    )(qabs, qidx, widx, ckr, cvt, kidx, wuv_pair, lstrict)


def _prep_b_kernel(pb_ref, w0_ref, w2_ref, a0_ref, a2_ref, g2_ref, kk_ref, ka_ref, rk_ref, eb_ref, tri_ref,
                   rt_ref, kt_ref, bt_ref, kl_ref, v_ref, g_ref, bv_ref, pc_ref):
    pb = pb_ref[0]
    r = pb[:, :B_WIDTH]
    k = pb[:, B_WIDTH:2 * B_WIDTH]
    v = pb[:, 2 * B_WIDTH:3 * B_WIDTH]
    o = 3 * B_WIDTH
    xw = pb[:, o:o + W_LORA]
    xa = pb[:, o + W_LORA:o + W_LORA + A_LORA]
    xg = pb[:, o + W_LORA + A_LORA:o + W_LORA + A_LORA + G_LORA]

    z = w0_ref[...] + _dot(jnp.tanh(xw).astype(BF16), w2_ref[...])
    nz = -z
    softplus = jnp.maximum(nz, 0.0) + jnp.log(1.0 + jnp.exp(-jnp.abs(nz)))
    lw = -jnp.exp(-softplus - 0.5)
    a = jax.nn.sigmoid(a0_ref[...] + _dot(xa.astype(BF16), a2_ref[...]))
    g = _dot(jax.nn.sigmoid(xg).astype(BF16), g2_ref[...])
    kk = k * kk_ref[...]
    kkn = kk / jnp.maximum(jnp.sqrt(_dot_hl(kk * kk, eb_ref[...])), 1e-12)
    kp = k * (1.0 + (a - 1.0) * ka_ref[...])
    bonus = _dot_hl(r * kp * rk_ref[...], eb_ref[...])

    lw_hi, lw_lo = _split(lw)
    cum = _dot(tri_ref[...], lw_hi) + _dot(tri_ref[...], lw_lo)
    e_pos = jnp.exp(cum)
    e_neg = jnp.exp(-cum)
    rt_ref[0] = (r * e_pos).astype(BF16)
    kt_ref[0] = (kkn * jnp.exp(cum - lw)).astype(BF16)
    bt_ref[0] = (kkn * a * e_neg).astype(BF16)
    kl_ref[0] = (kp * e_neg).astype(BF16)
    v_ref[0] = v.astype(BF16)
    g_ref[0] = g
    bv_ref[0] = bonus * v
    for c in range(pb.shape[0] // CHUNK):
        pc_ref[0, 0, c:c + 1, :] = e_pos[(c + 1) * CHUNK - 1:(c + 1) * CHUNK, :]


def _prep_b_call(pb, w0, w2, a0, a2, g2, k_k, k_a, r_k, eb, tri):
    bsz, s, nb = pb.shape
    tm = TOK_TILE
    full = lambda shape: pl.BlockSpec(shape, lambda b, j: (0,) * len(shape))
    tok = lambda w: pl.BlockSpec((1, tm, w), lambda b, j: (b, j, 0))
    row = full((1, B_WIDTH))
    bf = jax.ShapeDtypeStruct((bsz, s, B_WIDTH), BF16)
    ff = jax.ShapeDtypeStruct((bsz, s, B_WIDTH), F32)
    return pl.pallas_call(
        _prep_b_kernel,
        grid=(bsz, s // tm),
        in_specs=[tok(nb), row, full((W_LORA, B_WIDTH)), row, full((A_LORA, B_WIDTH)),
                  full((G_LORA, B_WIDTH)), row, row, row, full((B_WIDTH, B_WIDTH)), full((tm, tm))],
        out_specs=[tok(B_WIDTH)] * 7 + [pl.BlockSpec((1, 1, tm // CHUNK, B_WIDTH), lambda b, j: (b, j, 0, 0))],
        out_shape=[bf, bf, bf, bf, bf, ff, ff,
                   jax.ShapeDtypeStruct((bsz, s // tm, tm // CHUNK, B_WIDTH), F32)],
        compiler_params=_params(("arbitrary", "arbitrary")),
    )(pb, w0, w2, a0, a2, g2, k_k, k_a, r_k, eb, tri)


def _rwkv_kernel(rt_ref, kt_ref, bt_ref, kl_ref, v_ref, g_ref, bv_ref, pc_ref, lnw_ref, lnb_ref, eb_ref,
                 o_ref, h_ref):
    j = pl.program_id(1)

    @pl.when(j == 0)
    def _():
        h_ref[...] = jnp.zeros_like(h_ref)

    tm = rt_ref.shape[1]
    nch = tm // CHUNK
    ri = lax.broadcasted_iota(jnp.int32, (tm, tm), 0)
    ci = lax.broadcasted_iota(jnp.int32, (tm, tm), 1)
    same = (ri // CHUNK) == (ci // CHUNK)
    strict = same & (ri > ci)
    incl = same & (ri >= ci)
    eye_t = jnp.where(ri == ci, 1.0, 0.0)
    r2 = lax.broadcasted_iota(jnp.int32, (LANES, LANES), 0)
    c2 = lax.broadcasted_iota(jnp.int32, (LANES, LANES), 1)
    blk = (r2 // B_HEAD_DIM) == (c2 // B_HEAD_DIM)
    diag = r2 == c2
    lane = lax.broadcasted_iota(jnp.int32, (1, LANES), 1)
    zero_b = jnp.zeros((), BF16)

    outs = []
    for p in range(B_HEADS // 2):
        sl = slice(p * LANES, (p + 1) * LANES)
        rt = rt_ref[0, :, sl]
        kt = kt_ref[0, :, sl]
        bt = bt_ref[0, :, sl]
        kl = kl_ref[0, :, sl]
        v = v_ref[0, :, sl]
        w_pair = u_pair = q_pair = ol_pair = None
        for hh in range(2):
            hm = (lane // B_HEAD_DIM) == hh
            kt_m = jnp.where(hm, kt, zero_b)
            rt_m = jnp.where(hm, rt, zero_b)
            a_ab = jnp.where(strict, _dot_nt(kt_m, bt), 0.0)
            a_ak = jnp.where(strict, _dot_nt(kt_m, kl), 0.0)
            m_rb = jnp.where(incl, _dot_nt(rt_m, bt), 0.0).astype(BF16)
            m_rk = jnp.where(incl, _dot_nt(rt_m, kl), 0.0).astype(BF16)
            t_inv = eye_t - a_ab
            a_pow = a_ab.astype(BF16)
            for _ in range(5):
                a_sq = _dot(a_pow, a_pow)
                a_pow = a_sq.astype(BF16)
                t_inv = _dot(t_inv.astype(BF16), (eye_t + a_sq).astype(BF16))
            t_inv = t_inv.astype(BF16)
            av = _dot(a_ak.astype(BF16), v)
            w_h = -_dot(t_inv, kt)
            u_h = -_dot(t_inv, av.astype(BF16))
            q_h = rt.astype(F32) + _dot(m_rb, w_h.astype(BF16))
            ol_h = _dot(m_rb, u_h.astype(BF16)) + _dot(m_rk, v)
            if hh == 0:
                w_pair, u_pair, q_pair, ol_pair = w_h, u_h, q_h, ol_h
            else:
                w_pair = jnp.where(hm, w_h, w_pair)
                u_pair = jnp.where(hm, u_h, u_pair)
                q_pair = jnp.where(hm, q_h, q_pair)
                ol_pair = jnp.where(hm, ol_h, ol_pair)
        w_b = w_pair.astype(BF16)
        u_b = u_pair.astype(BF16)
        q_b = q_pair.astype(BF16)
        o_chunks = []
        h = h_ref[p]
        for c in range(nch):
            rows = slice(c * CHUNK, (c + 1) * CHUNK)
            pc = pc_ref[0, 0, c:c + 1, sl]
            bh = (bt[rows].astype(F32) * pc).astype(BF16)
            kh = (kl[rows].astype(F32) * pc).astype(BF16)
            h_b = h.astype(BF16)
            o_chunks.append(_dot(q_b[rows], h_b) + ol_pair[rows])
            g_mat = jnp.where(diag, pc, 0.0) + jnp.where(blk, _dot_tn(bh, w_b[rows]), 0.0)
            f_mat = jnp.where(blk, _dot_tn(bh, u_b[rows]) + _dot_tn(kh, v[rows]), 0.0)
            h = _dot(g_mat.astype(BF16), h_b) + f_mat
        h_ref[p] = h
        outs.append(jnp.concatenate(o_chunks, axis=0))
    out = jnp.concatenate(outs, axis=1)

    eb = eb_ref[...]
    mean = _dot_hl(out, eb) * (1.0 / B_HEAD_DIM)
    d = out - mean
    var = _dot_hl(d * d, eb) * (1.0 / B_HEAD_DIM)
    y = d * lax.rsqrt(var + GN_EPS) * lnw_ref[...] + lnb_ref[...] + bv_ref[0]
    o_ref[0] = (y * g_ref[0]).astype(o_ref.dtype)


def _rwkv_call(rt, kt, bt, kl, v, g, bv, pc, ln_w, ln_b, eb):
    bsz, s, _ = rt.shape
    tm = TOK_TILE
    tok = pl.BlockSpec((1, tm, B_WIDTH), lambda b, j: (b, j, 0))
    row = pl.BlockSpec((1, B_WIDTH), lambda b, j: (0, 0))
    return pl.pallas_call(
        _rwkv_kernel,
        grid=(bsz, s // tm),
        in_specs=[tok] * 7 + [pl.BlockSpec((1, 1, tm // CHUNK, B_WIDTH), lambda b, j: (b, j, 0, 0)),
                              row, row, pl.BlockSpec((B_WIDTH, B_WIDTH), lambda b, j: (0, 0))],
        out_specs=tok,
        out_shape=jax.ShapeDtypeStruct((bsz, s, B_WIDTH), BF16),
        scratch_shapes=[pltpu.VMEM((B_HEADS // 2, LANES, LANES), F32)],
        compiler_params=_params(("arbitrary", "arbitrary")),
    )(rt, kt, bt, kl, v, g, bv, pc, ln_w, ln_b, eb)


def _ffn_kernel(x_ref, oa_ref, ob_ref, gt1_ref, sh2_ref, sc2_ref, gt2_ref, gf_ref, woa_ref, wob_ref,
                w1_ref, w2_ref, o_ref, x1_ref, h2_ref, acc_ref):
    f = pl.program_id(2)

    @pl.when(f == 0)
    def _():
        mix = _dot(oa_ref[0], woa_ref[...]) + _dot(ob_ref[0], wob_ref[...])
        x1 = x_ref[0] + gt1_ref[0, 0] * mix
        x1_ref[...] = x1
        y = x1 * lax.rsqrt(jnp.mean(x1 * x1, axis=-1, keepdims=True) + RMS_EPS) * gf_ref[...]
        h2_ref[...] = (y * (1.0 + sc2_ref[0, 0]) + sh2_ref[0, 0]).astype(BF16)
        acc_ref[...] = jnp.zeros_like(acc_ref)

    u = jnp.maximum(_dot(h2_ref[...], w1_ref[...]), 0.0)
    acc_ref[...] += _dot((u * u).astype(BF16), w2_ref[...])

    @pl.when(f == pl.num_programs(2) - 1)
    def _():
        o_ref[0] = x1_ref[...] + gt2_ref[0, 0] * acc_ref[...]


def _ffn_call(x, oa, ob, mod4, g_ffn, w_out_a, w_out_b, w1, w2):
    bsz, s, d = x.shape
    dff = w1.shape[1]
    tm = 512 if s % 512 == 0 else TOK_TILE
    tf = 1024
    tok = lambda w: pl.BlockSpec((1, tm, w), lambda b, j, f: (b, j, 0))
    modk = lambda k: pl.BlockSpec((1, 1, 1, d), lambda b, j, f, k=k: (b, k, 0, 0))
    return pl.pallas_call(
        _ffn_kernel,
        grid=(bsz, s // tm, dff // tf),
        in_specs=[tok(d), tok(A_WIDTH), tok(B_WIDTH), modk(2), modk(3), modk(4), modk(5),
                  pl.BlockSpec((1, d), lambda b, j, f: (0, 0)),
                  pl.BlockSpec((A_WIDTH, d), lambda b, j, f: (0, 0)),
                  pl.BlockSpec((B_WIDTH, d), lambda b, j, f: (0, 0)),
                  pl.BlockSpec((d, tf), lambda b, j, f: (0, f)),
                  pl.BlockSpec((tf, d), lambda b, j, f: (f, 0))],
        out_specs=tok(d),
        out_shape=jax.ShapeDtypeStruct((bsz, s, d), F32),
        scratch_shapes=[pltpu.VMEM((tm, d), F32), pltpu.VMEM((tm, d), BF16), pltpu.VMEM((tm, d), F32)],
        compiler_params=_params(("arbitrary", "arbitrary", "arbitrary")),
    )(x, oa, ob, mod4, mod4, mod4, mod4, g_ffn, w_out_a, w_out_b, w1, w2)


def _block_ones(n, blk, dtype=BF16):
    i = jnp.arange(n)
    return ((i[:, None] // blk) == (i[None, :] // blk)).astype(dtype)


def kernel(x, c, w_ada, b_ada, g_mix, g_ffn, w_in, g_q, g_k, g_kv, w_uk, w_uv, mu_shift, w0, w2, a0, a2, g2,
           k_k, k_a, r_k, ln_w, ln_b, w_out, w_ff1, w_ff2):
    bsz, s, d = x.shape
    depth = w_ada.shape[0]
    assert s % Q_TILE == 0 and s % TOK_TILE == 0
    topk = min(TOPK_MAX, s // 4)

    eb = _block_ones(B_WIDTH, B_HEAD_DIM)
    ex = (jnp.arange(A_WIDTH)[:, None] // A_HEAD_DIM == jnp.arange(A_HEADS * KV_LATENT)[None, :] // KV_LATENT
          ).astype(BF16)
    sel = (jnp.arange(LANES)[None, :] == IDX_DIM + jnp.arange(IDX_HEADS)[:, None]).astype(BF16)
    eye_l = jnp.eye(KV_LATENT, dtype=BF16)
    ti = jnp.arange(TOK_TILE)
    tri = (((ti[:, None] // CHUNK) == (ti[None, :] // CHUNK)) & (ti[:, None] >= ti[None, :])).astype(BF16)
    ki = jnp.arange(K_TILE)
    lstrict = (ki[None, :] < ki[:, None]).astype(BF16)

    for l in range(depth):
        w_a = jnp.pad(w_in[l][:, :N_IN_A], ((0, 0), (0, N_A_PAD - N_IN_A)))
        w_in_p = jnp.concatenate([w_a, w_in[l][:, N_IN_A:]], axis=1).astype(BF16)
        wuk_flat = w_uk[l].reshape(KV_LATENT, A_WIDTH).astype(BF16)
        wuk_t = jnp.transpose(w_uk[l], (1, 2, 0))
        wuk_bd = (jnp.eye(A_HEADS, dtype=F32)[:, None, :, None] * wuk_t[:, :, None, :]).reshape(
            A_WIDTH, A_HEADS * KV_LATENT).astype(BF16)
        wuv_t = jnp.transpose(w_uv[l], (1, 0, 2)).reshape(A_HEADS // 2, 2, KV_LATENT, A_HEAD_DIM)
        wuv_pair = (jnp.eye(2, dtype=F32)[None, :, None, :, None] * wuv_t[:, :, :, None, :]).reshape(
            A_HEADS // 2, 2 * KV_LATENT, 2 * A_HEAD_DIM).astype(BF16)
        gqk = jnp.tile(g_q[l] * g_k[l], A_HEADS).reshape(1, A_WIDTH)
        r1 = lambda t: t.reshape(1, -1)

        mod = _mod_call(c, w_ada[l], b_ada[l])
        mod4 = mod.reshape(bsz, 6, 1, d)
        pa, pb = _proj_call(x, mod4, r1(g_mix[l]), w_in_p, r1(mu_shift[l]))
        ckr, cvt, qabs, qidx, kidx, widx = _prep_a_call(pa, r1(g_kv[l]), gqk, wuk_flat, wuk_bd, eb, ex, sel, eye_l)
        o_a = _dsa_call(topk, qabs, qidx, widx, ckr, cvt, kidx, wuv_pair, lstrict)
        rt, kt, bt, kl, v, g, bv, pc = _prep_b_call(
            pb, r1(w0[l]), w2[l].astype(BF16), r1(a0[l]), a2[l].astype(BF16), g2[l].astype(BF16),
            r1(k_k[l]), r1(k_a[l]), r1(r_k[l]), eb, tri)
        o_b = _rwkv_call(rt, kt, bt, kl, v, g, bv, pc, r1(ln_w[l]), r1(ln_b[l]), eb)
        x = _ffn_call(x, o_a, o_b, mod4, r1(g_ffn[l]), w_out[l][:A_WIDTH].astype(BF16),
                      w_out[l][A_WIDTH:].astype(BF16), w_ff1[l].astype(BF16), w_ff2[l].astype(BF16))
    return x
```

```python
import functools

import jax
import jax.numpy as jnp
from jax import lax
from jax.experimental import pallas as pl
from jax.experimental.pallas import tpu as pltpu

F32 = jnp.float32
BF16 = jnp.bfloat16

CHUNK = 64
A_HEADS = 8
A_HEAD_DIM = 64
A_WIDTH = A_HEADS * A_HEAD_DIM
KV_LATENT = 128
IDX_HEADS = 8
IDX_DIM = 64
TOPK_MAX = 256
B_HEADS = 8
B_HEAD_DIM = 64
B_WIDTH = B_HEADS * B_HEAD_DIM
W_LORA = 64
A_LORA = 64
G_LORA = 128
RMS_EPS = 1e-6
GN_EPS = 64e-5
N_IN_A = A_WIDTH + KV_LATENT + IDX_HEADS * IDX_DIM + IDX_DIM + IDX_HEADS
N_IN_B = 3 * B_WIDTH + W_LORA + A_LORA + G_LORA
N_A_PAD = 1280

LANES = 128
TOK_TILE = 256
Q_TILE = 256
K_TILE = 256
DIST_BIG = 1e30
ONES_ROWS = 16
LOG2E = 1.4426950408889634
BISECT_FIRST = 12
BISECT_STEP = 4
BISECT_MAX = 40
VMEM_LIMIT = 56 * 1024 * 1024


def _dot(a, b):
    return jnp.dot(a, b, preferred_element_type=F32)


def _dot_nt(a, b):
    return lax.dot_general(a, b, (((1,), (1,)), ((), ())), preferred_element_type=F32)


def _dot_tn(a, b):
    return lax.dot_general(a, b, (((0,), (0,)), ((), ())), preferred_element_type=F32)


def _split(x):
    hi = x.astype(BF16)
    lo = (x - hi.astype(F32)).astype(BF16)
    return hi, lo


def _dot_hl(x, e):
    hi, lo = _split(x)
    return _dot(hi, e) + _dot(lo, e)


def _params(sem):
    return pltpu.CompilerParams(dimension_semantics=sem, vmem_limit_bytes=VMEM_LIMIT)


def _mod_kernel(c_ref, w_ref, b_ref, o_ref):
    c = c_ref[...]
    s = c * jax.nn.sigmoid(c)
    s_hi, s_lo = _split(s)
    w_hi, w_lo = _split(w_ref[...])
    o_ref[...] = _dot(s_hi, w_hi) + _dot(s_hi, w_lo) + _dot(s_lo, w_hi) + b_ref[...]


def _mod_call(c, w_ada, b_ada):
    bsz, d = c.shape
    n = w_ada.shape[1]
    tn = 1024
    return pl.pallas_call(
        _mod_kernel,
        grid=(n // tn,),
        in_specs=[pl.BlockSpec((bsz, d), lambda j: (0, 0)),
                  pl.BlockSpec((d, tn), lambda j: (0, j)),
                  pl.BlockSpec((1, tn), lambda j: (0, j))],
        out_specs=pl.BlockSpec((bsz, tn), lambda j: (0, j)),
        out_shape=jax.ShapeDtypeStruct((bsz, n), F32),
        compiler_params=_params(("arbitrary",)),
    )(c, w_ada, b_ada.reshape(1, n))


def _proj_kernel(x_ref, sh_ref, sc_ref, g_ref, w_ref, mu_ref, pa_ref, pb_ref, carry_ref):
    j = pl.program_id(1)

    @pl.when(j == 0)
    def _():
        carry_ref[...] = jnp.zeros_like(carry_ref)

    x = x_ref[0]
    y = x * lax.rsqrt(jnp.mean(x * x, axis=-1, keepdims=True) + RMS_EPS) * g_ref[...]
    h = y * (1.0 + sc_ref[0, 0]) + sh_ref[0, 0]
    p = _dot(h.astype(BF16), w_ref[...])
    pa_ref[0] = p[:, :N_A_PAD]
    pb = p[:, N_A_PAD:]
    tm = pb.shape[0]
    row = lax.broadcasted_iota(jnp.int32, (tm, 1), 0)
    prev = jnp.where(row == 0, carry_ref[...], pltpu.roll(pb, 1, axis=0))
    carry_ref[...] = pb[tm - 1:tm, :]
    pb_ref[0] = pb + mu_ref[...] * (prev - pb)


def _proj_call(x, mod4, g_mix, w_in_p, mu):
    bsz, s, d = x.shape
    n = w_in_p.shape[1]
    nb = n - N_A_PAD
    tm = TOK_TILE
    return pl.pallas_call(
        _proj_kernel,
        grid=(bsz, s // tm),
        in_specs=[pl.BlockSpec((1, tm, d), lambda b, j: (b, j, 0)),
                  pl.BlockSpec((1, 1, 1, d), lambda b, j: (b, 0, 0, 0)),
                  pl.BlockSpec((1, 1, 1, d), lambda b, j: (b, 1, 0, 0)),
                  pl.BlockSpec((1, d), lambda b, j: (0, 0)),
                  pl.BlockSpec((d, n), lambda b, j: (0, 0)),
                  pl.BlockSpec((1, nb), lambda b, j: (0, 0))],
        out_specs=[pl.BlockSpec((1, tm, N_A_PAD), lambda b, j: (b, j, 0)),
                   pl.BlockSpec((1, tm, nb), lambda b, j: (b, j, 0))],
        out_shape=[jax.ShapeDtypeStruct((bsz, s, N_A_PAD), F32),
                   jax.ShapeDtypeStruct((bsz, s, nb), F32)],
        scratch_shapes=[pltpu.VMEM((1, nb), F32)],
        compiler_params=_params(("arbitrary", "arbitrary")),
    )(x, mod4, mod4, g_mix, w_in_p, mu)


def _prep_a_kernel(pa_ref, gkv_ref, gqk_ref, wuk_ref, wukbd_ref, eb_ref, ex_ref, sel_ref, eye_ref,
                   ckr_ref, cvt_ref, qabs_ref, qidx_ref, kidx_ref, widx_ref):
    pa = pa_ref[0]
    tm = pa.shape[0]
    q = pa[:, :A_WIDTH]
    cl = pa[:, A_WIDTH:A_WIDTH + KV_LATENT]
    o_qi = A_WIDTH + KV_LATENT
    qi = pa[:, o_qi:o_qi + IDX_HEADS * IDX_DIM]
    o_kw = o_qi + IDX_HEADS * IDX_DIM
    kw = pa[:, o_kw:o_kw + LANES]

    ckv = cl * lax.rsqrt(jnp.mean(cl * cl, axis=-1, keepdims=True) + RMS_EPS) * gkv_ref[...]
    ckv_b = ckv.astype(BF16)
    cvt_ref[0, 0, :KV_LATENT, :] = _dot_nt(eye_ref[...], ckv_b).astype(BF16)
    cvt_ref[0, 0, KV_LATENT:, :] = jnp.ones((ONES_ROWS, tm), BF16)
    kf = _dot(ckv_b, wuk_ref[...])
    ss = _dot_hl(kf * kf, ex_ref[...])
    inv_rms = lax.rsqrt(ss * (1.0 / A_HEAD_DIM) + RMS_EPS)
    ckr_ref[0] = (jnp.concatenate([ckv] * A_HEADS, axis=1) * inv_rms).astype(BF16)

    ssq = _dot_hl(q * q, eb_ref[...])
    qh = q * lax.rsqrt(ssq * (1.0 / A_HEAD_DIM) + RMS_EPS) * gqk_ref[...]
    qabs = _dot(qh.astype(BF16), wukbd_ref[...]) * (A_HEAD_DIM ** -0.5 * LOG2E)
    qabs_ref[0] = qabs.astype(BF16)
    for h in range(IDX_HEADS):
        qidx_ref[0, h] = qi[:, h * IDX_DIM:(h + 1) * IDX_DIM].astype(BF16)
    kidx_ref[0] = kw[:, :IDX_DIM].astype(BF16)
    kw_hi, kw_lo = _split(kw)
    w_t = _dot_nt(sel_ref[...], kw_hi) + _dot_nt(sel_ref[...], kw_lo)
    widx_ref[0, 0] = w_t * (IDX_HEADS ** -0.5 * IDX_DIM ** -0.5)


def _prep_a_call(pa, gkv, gqk, wuk_flat, wuk_bd, eb, ex, sel, eye):
    bsz, s, _ = pa.shape
    tm = TOK_TILE
    full = lambda shape: pl.BlockSpec(shape, lambda b, j: (0,) * len(shape))
    tok = lambda w: pl.BlockSpec((1, tm, w), lambda b, j: (b, j, 0))
    return pl.pallas_call(
        _prep_a_kernel,
        grid=(bsz, s // tm),
        in_specs=[tok(N_A_PAD), full((1, KV_LATENT)), full((1, A_WIDTH)),
                  full((KV_LATENT, A_WIDTH)), full((A_WIDTH, A_HEADS * KV_LATENT)),
                  full((A_WIDTH, A_WIDTH)), full((A_WIDTH, A_HEADS * KV_LATENT)),
                  full((IDX_HEADS, LANES)), full((KV_LATENT, KV_LATENT))],
        out_specs=[tok(A_HEADS * KV_LATENT),
                   pl.BlockSpec((1, 1, KV_LATENT + ONES_ROWS, tm), lambda b, j: (b, j, 0, 0)),
                   tok(A_HEADS * KV_LATENT),
                   pl.BlockSpec((1, IDX_HEADS, tm, IDX_DIM), lambda b, j: (b, 0, j, 0)),
                   tok(IDX_DIM),
                   pl.BlockSpec((1, 1, IDX_HEADS, tm), lambda b, j: (b, j, 0, 0))],
        out_shape=[jax.ShapeDtypeStruct((bsz, s, A_HEADS * KV_LATENT), BF16),
                   jax.ShapeDtypeStruct((bsz, s // tm, KV_LATENT + ONES_ROWS, tm), BF16),
                   jax.ShapeDtypeStruct((bsz, s, A_HEADS * KV_LATENT), BF16),
                   jax.ShapeDtypeStruct((bsz, IDX_HEADS, s, IDX_DIM), BF16),
                   jax.ShapeDtypeStruct((bsz, s, IDX_DIM), BF16),
                   jax.ShapeDtypeStruct((bsz, s // tm, IDX_HEADS, tm), F32)],
        compiler_params=_params(("arbitrary", "arbitrary")),
    )(pa, gkv, gqk, wuk_flat, wuk_bd, eb, ex, sel, eye)


def _colsum8(x):
    y = x.reshape(4, K_TILE // 32, 8, Q_TILE)
    return jnp.sum(jnp.sum(y, axis=1), axis=0)


def _colmin8(x):
    y = x.reshape(4, K_TILE // 32, 8, Q_TILE)
    return jnp.min(jnp.min(y, axis=1), axis=0)


def _colmax8(x):
    y = x.reshape(4, K_TILE // 32, 8, Q_TILE)
    return jnp.max(jnp.max(y, axis=1), axis=0)


def _dsa_kernel(topk, qabs_ref, qidx_ref, widx_ref, ckr_ref, cvt_ref, kidx_ref, wuv_ref, lstrict_ref,
                o_ref, score_ref, dist_ref, logit_ref, m_ref, acc_ref):
    i = pl.program_id(1)
    nkc = i + 1
    t0 = i * Q_TILE
    krow = lax.broadcasted_iota(jnp.int32, (K_TILE, 1), 0)
    qcol = lax.broadcasted_iota(jnp.int32, (1, Q_TILE), 1)
    limit = ((t0 + qcol) // CHUNK + 1) * CHUNK
    kp = jnp.minimum(limit, topk).astype(F32)
    rel = (qcol - krow).astype(F32)

    def p1(kc, carry):
        rmin, rmax = carry
        k = kidx_ref[0, pl.ds(pl.multiple_of(kc * K_TILE, K_TILE), K_TILE), :]
        acc = jnp.zeros((K_TILE, Q_TILE), F32)
        for h in range(IDX_HEADS):
            s = _dot_nt(k, qidx_ref[0, h])
            acc = acc + widx_ref[0, 0, h:h + 1, :] * jnp.maximum(s, 0.0)
        adm = (kc * K_TILE + krow) < limit
        score_ref[kc] = jnp.where(adm, acc, -jnp.inf)
        rmin = jnp.minimum(rmin, _colmin8(jnp.where(adm, acc, jnp.inf)))
        rmax = jnp.maximum(rmax, _colmax8(jnp.where(adm, acc, -jnp.inf)))
        return rmin, rmax

    rmin, rmax = lax.fori_loop(
        0, nkc, p1, (jnp.full((8, Q_TILE), jnp.inf, F32), jnp.full((8, Q_TILE), -jnp.inf, F32)))
    lo = jnp.min(rmin, axis=0, keepdims=True)
    hi = jnp.max(rmax, axis=0, keepdims=True)

    def count(pred):
        def body(kc, acc):
            return acc + _colsum8(jnp.where(pred(score_ref[kc]), 1.0, 0.0))
        return jnp.sum(lax.fori_loop(0, nkc, body, jnp.zeros((8, Q_TILE), F32)), axis=0, keepdims=True)

    def bis_block(n, lo, hi, cnt_lo):
        def body(_, c):
            lo, hi, cnt_lo = c
            mid = lo + 0.5 * (hi - lo)
            cnt = count(lambda sc: sc >= mid)
            ge = cnt >= kp
            return jnp.where(ge, mid, lo), jnp.where(ge, hi, mid), jnp.where(ge, cnt, cnt_lo)
        return lax.fori_loop(0, n, body, (lo, hi, cnt_lo))

    cnt_lo = count(lambda sc: sc >= lo)
    lo, hi, cnt_lo = bis_block(BISECT_FIRST, lo, hi, cnt_lo)

    def unsettled(cnt_lo):
        return jnp.max(jnp.where(cnt_lo == kp, 0.0, 1.0)) > 0.0

    def w_cond(c):
        it, _, _, cnt_lo = c
        return jnp.logical_and(it < BISECT_MAX, unsettled(cnt_lo))

    def w_body(c):
        it, lo, hi, cnt_lo = c
        lo, hi, cnt_lo = bis_block(BISECT_STEP, lo, hi, cnt_lo)
        return it + BISECT_STEP, lo, hi, cnt_lo

    _, lo, hi, cnt_lo = lax.while_loop(w_cond, w_body, (jnp.int32(BISECT_FIRST), lo, hi, cnt_lo))

    def dist_tile(kc):
        return jnp.abs(rel + (t0 - kc * K_TILE).astype(F32))

    def sel_plain():
        def body(kc, _):
            dist_ref[kc] = jnp.where(score_ref[kc] >= lo, dist_tile(kc), DIST_BIG)
            return 0
        lax.fori_loop(0, nkc, body, 0)

    def sel_ties():
        def thr_body(kc, acc):
            sc = score_ref[kc]
            return jnp.minimum(acc, _colmin8(jnp.where(sc >= lo, sc, jnp.inf)))
        thr = jnp.min(lax.fori_loop(0, nkc, thr_body, jnp.full((8, Q_TILE), jnp.inf, F32)), axis=0, keepdims=True)
        need = kp - count(lambda sc: sc > thr)

        def body(kc, run):
            sc = score_ref[kc]
            eq = sc == thr
            eq_f = jnp.where(eq, 1.0, 0.0)
            pre = run + _dot(lstrict_ref[...], eq_f.astype(BF16))
            keep = (sc > thr) | (eq & (pre < need))
            dist_ref[kc] = jnp.where(keep, dist_tile(kc), DIST_BIG)
            return run + jnp.sum(_colsum8(eq_f), axis=0, keepdims=True)
        lax.fori_loop(0, nkc, body, jnp.zeros((1, Q_TILE), F32))

    lax.cond(unsettled(cnt_lo), sel_ties, sel_plain)

    m_ref[...] = jnp.full(m_ref.shape, -jnp.inf, F32)
    acc_ref[...] = jnp.zeros(acc_ref.shape, F32)

    def att(kc, _):
        dist = dist_ref[kc]
        m_new = []
        for h in range(A_HEADS):
            slope = 2.0 ** (-8.0 * (h + 1) / A_HEADS) * LOG2E
            ck = ckr_ref[0, pl.ds(pl.multiple_of(kc * K_TILE, K_TILE), K_TILE), h * KV_LATENT:(h + 1) * KV_LATENT]
            logit = _dot_nt(ck, qabs_ref[0, :, h * KV_LATENT:(h + 1) * KV_LATENT]) - slope * dist
            logit_ref[h] = logit
            m_new.append(jnp.maximum(m_ref[h], jnp.max(_colmax8(logit), axis=0, keepdims=True)))
        cv = cvt_ref[0, kc]
        for h in range(A_HEADS):
            p = jnp.exp2(logit_ref[h] - m_new[h])
            acc_ref[h] = acc_ref[h] * jnp.exp2(m_ref[h] - m_new[h]) + _dot(cv, p.astype(BF16))
            m_ref[h] = m_new[h]
        return 0

    lax.fori_loop(0, nkc, att, 0)

    for pair in range(A_HEADS // 2):
        o_pair = []
        for hh in range(2):
            a = acc_ref[2 * pair + hh]
            o_t = a[:KV_LATENT] * (1.0 / a[KV_LATENT:KV_LATENT + 1])
            o_pair.append(o_t.T.astype(BF16))
        o_lat = jnp.concatenate(o_pair, axis=1)
        o_ref[0, :, pair * LANES:(pair + 1) * LANES] = _dot(o_lat, wuv_ref[pair]).astype(o_ref.dtype)


def _dsa_call(topk, qabs, qidx, widx, ckr, cvt, kidx, wuv_pair, lstrict):
    bsz, s, _ = qabs.shape
    nq = s // Q_TILE
    nk = s // K_TILE
    qt = lambda w: pl.BlockSpec((1, Q_TILE, w), lambda b, i: (b, i, 0))
    return pl.pallas_call(
        functools.partial(_dsa_kernel, topk),
        grid=(bsz, nq),
        in_specs=[qt(A_HEADS * KV_LATENT),
                  pl.BlockSpec((1, IDX_HEADS, Q_TILE, IDX_DIM), lambda b, i: (b, 0, i, 0)),
                  pl.BlockSpec((1, 1, IDX_HEADS, Q_TILE), lambda b, i: (b, i, 0, 0)),
                  pl.BlockSpec((1, s, A_HEADS * KV_LATENT), lambda b, i: (b, 0, 0)),
                  pl.BlockSpec((1, nk, KV_LATENT + ONES_ROWS, K_TILE), lambda b, i: (b, 0, 0, 0)),
                  pl.BlockSpec((1, s, IDX_DIM), lambda b, i: (b, 0, 0)),
                  pl.BlockSpec((A_HEADS // 2, 2 * KV_LATENT, LANES), lambda b, i: (0, 0, 0)),
                  pl.BlockSpec((K_TILE, K_TILE), lambda b, i: (0, 0))],
        out_specs=qt(A_WIDTH),
        out_shape=jax.ShapeDtypeStruct((bsz, s, A_WIDTH), BF16),
        scratch_shapes=[pltpu.VMEM((nk, K_TILE, Q_TILE), F32),
                        pltpu.VMEM((nk, K_TILE, Q_TILE), F32),
                        pltpu.VMEM((A_HEADS, K_TILE, Q_TILE), F32),
                        pltpu.VMEM((A_HEADS, 1, Q_TILE), F32),
                        pltpu.VMEM((A_HEADS, KV_LATENT + ONES_ROWS, Q_TILE), F32)],
        compiler_params=_params(("arbitrary", "arbitrary")),
    )(qabs, qidx, widx, ckr, cvt, kidx, wuv_pair, lstrict)


def _prep_b_kernel(pb_ref, w0_ref, w2_ref, a0_ref, a2_ref, g2_ref, kk_ref, ka_ref, rk_ref, eb_ref, tri_ref,
                   rt_ref, kt_ref, bt_ref, kl_ref, v_ref, g_ref, bv_ref, pc_ref):
    pb = pb_ref[0]
    r = pb[:, :B_WIDTH]
    k = pb[:, B_WIDTH:2 * B_WIDTH]
    v = pb[:, 2 * B_WIDTH:3 * B_WIDTH]
    o = 3 * B_WIDTH
    xw = pb[:, o:o + W_LORA]
    xa = pb[:, o + W_LORA:o + W_LORA + A_LORA]
    xg = pb[:, o + W_LORA + A_LORA:o + W_LORA + A_LORA + G_LORA]

    z = w0_ref[...] + _dot(jnp.tanh(xw).astype(BF16), w2_ref[...])
    nz = -z
    softplus = jnp.maximum(nz, 0.0) + jnp.log(1.0 + jnp.exp(-jnp.abs(nz)))
    lw = -jnp.exp(-softplus - 0.5)
    a = jax.nn.sigmoid(a0_ref[...] + _dot(xa.astype(BF16), a2_ref[...]))
    g = _dot(jax.nn.sigmoid(xg).astype(BF16), g2_ref[...])
    kk = k * kk_ref[...]
    kkn = kk / jnp.maximum(jnp.sqrt(_dot_hl(kk * kk, eb_ref[...])), 1e-12)
    kp = k * (1.0 + (a - 1.0) * ka_ref[...])
    bonus = _dot_hl(r * kp * rk_ref[...], eb_ref[...])

    lw_hi, lw_lo = _split(lw)
    cum = _dot(tri_ref[...], lw_hi) + _dot(tri_ref[...], lw_lo)
    e_pos = jnp.exp(cum)
    e_neg = jnp.exp(-cum)
    rt_ref[0] = (r * e_pos).astype(BF16)
    kt_ref[0] = (kkn * jnp.exp(cum - lw)).astype(BF16)
    bt_ref[0] = (kkn * a * e_neg).astype(BF16)
    kl_ref[0] = (kp * e_neg).astype(BF16)
    v_ref[0] = v.astype(BF16)
    g_ref[0] = g
    bv_ref[0] = bonus * v
    for c in range(pb.shape[0] // CHUNK):
        pc_ref[0, 0, c:c + 1, :] = e_pos[(c + 1) * CHUNK - 1:(c + 1) * CHUNK, :]


def _prep_b_call(pb, w0, w2, a0, a2, g2, k_k, k_a, r_k, eb, tri):
    bsz, s, nb = pb.shape
    tm = TOK_TILE
    full = lambda shape: pl.BlockSpec(shape, lambda b, j: (0,) * len(shape))
    tok = lambda w: pl.BlockSpec((1, tm, w), lambda b, j: (b, j, 0))
    row = full((1, B_WIDTH))
    bf = jax.ShapeDtypeStruct((bsz, s, B_WIDTH), BF16)
    ff = jax.ShapeDtypeStruct((bsz, s, B_WIDTH), F32)
    return pl.pallas_call(
        _prep_b_kernel,
        grid=(bsz, s // tm),
        in_specs=[tok(nb), row, full((W_LORA, B_WIDTH)), row, full((A_LORA, B_WIDTH)),
                  full((G_LORA, B_WIDTH)), row, row, row, full((B_WIDTH, B_WIDTH)), full((tm, tm))],
        out_specs=[tok(B_WIDTH)] * 7 + [pl.BlockSpec((1, 1, tm // CHUNK, B_WIDTH), lambda b, j: (b, j, 0, 0))],
        out_shape=[bf, bf, bf, bf, bf, ff, ff,
                   jax.ShapeDtypeStruct((bsz, s // tm, tm // CHUNK, B_WIDTH), F32)],
        compiler_params=_params(("arbitrary", "arbitrary")),
    )(pb, w0, w2, a0, a2, g2, k_k, k_a, r_k, eb, tri)


def _rwkv_kernel(rt_ref, kt_ref, bt_ref, kl_ref, v_ref, g_ref, bv_ref, pc_ref, lnw_ref, lnb_ref, eb_ref,
                 o_ref, h_ref):
    j = pl.program_id(1)

    @pl.when(j == 0)
    def _():
        h_ref[...] = jnp.zeros_like(h_ref)

    tm = rt_ref.shape[1]
    nch = tm // CHUNK
    ri = lax.broadcasted_iota(jnp.int32, (tm, tm), 0)
    ci = lax.broadcasted_iota(jnp.int32, (tm, tm), 1)
    same = (ri // CHUNK) == (ci // CHUNK)
    strict = same & (ri > ci)
    incl = same & (ri >= ci)
    eye_t = jnp.where(ri == ci, 1.0, 0.0)
    r2 = lax.broadcasted_iota(jnp.int32, (LANES, LANES), 0)
    c2 = lax.broadcasted_iota(jnp.int32, (LANES, LANES), 1)
    blk = (r2 // B_HEAD_DIM) == (c2 // B_HEAD_DIM)
    diag = r2 == c2
    lane = lax.broadcasted_iota(jnp.int32, (1, LANES), 1)
    zero_b = jnp.zeros((), BF16)

    npair = B_HEADS // 2
    heads = [(p, hh) for p in range(npair) for hh in range(2)]
    head0 = (lane // B_HEAD_DIM) == 0
    rt, kt, bt, kl, v = [], [], [], [], []
    a_ab, a_ak, m_rb, m_rk = [], [], [], []
    for p in range(npair):
        sl = slice(p * LANES, (p + 1) * LANES)
        rt.append(rt_ref[0, :, sl])
        kt.append(kt_ref[0, :, sl])
        bt.append(bt_ref[0, :, sl])
        kl.append(kl_ref[0, :, sl])
        v.append(v_ref[0, :, sl])
        lhs = jnp.concatenate([jnp.where(head0, kt[p], zero_b), jnp.where(head0, zero_b, kt[p]),
                               jnp.where(head0, rt[p], zero_b), jnp.where(head0, zero_b, rt[p])], axis=0)
        prod = _dot_nt(lhs, jnp.concatenate([bt[p], kl[p]], axis=0))
        for hh in range(2):
            a_ab.append(jnp.where(strict, prod[hh * tm:(hh + 1) * tm, :tm], 0.0))
            a_ak.append(jnp.where(strict, prod[hh * tm:(hh + 1) * tm, tm:], 0.0).astype(BF16))
            m_rb.append(jnp.where(incl, prod[(2 + hh) * tm:(3 + hh) * tm, :tm], 0.0).astype(BF16))
            m_rk.append(jnp.where(incl, prod[(2 + hh) * tm:(3 + hh) * tm, tm:], 0.0).astype(BF16))

    t_inv = [(eye_t - a).astype(BF16) for a in a_ab]
    a_pow = [a.astype(BF16) for a in a_ab]
    for _ in range(5):
        a_sq = [_dot(a, a) for a in a_pow]
        a_pow = [a.astype(BF16) for a in a_sq]
        t_inv = [_dot(t, (eye_t + a).astype(BF16)).astype(BF16) for t, a in zip(t_inv, a_sq)]

    avm = [_dot(jnp.concatenate([a_ak[i], m_rk[i]], axis=0), v[p]) for i, (p, _) in enumerate(heads)]
    x = [_dot(t_inv[i], jnp.concatenate([kt[p], avm[i][:tm].astype(BF16)], axis=1))
         for i, (p, _) in enumerate(heads)]
    y = [_dot(m_rb[i], x[i].astype(BF16)) for i in range(len(heads))]

    wu_b, q_b, ol = [], [], []
    for p in range(npair):
        i0, i1 = 2 * p, 2 * p + 1
        head0_2 = jnp.concatenate([head0, head0], axis=1)
        wu_b.append((-jnp.where(head0_2, x[i0], x[i1])).astype(BF16))
        yy = jnp.where(head0_2, y[i0], y[i1])
        q_b.append((rt[p].astype(F32) - yy[:, :LANES]).astype(BF16))
        ol.append(jnp.where(head0, avm[i0][tm:], avm[i1][tm:]) - yy[:, LANES:])

    g_mat, f_mat = [], []
    zeros_b = jnp.zeros((CHUNK, LANES), BF16)
    for p in range(npair):
        sl = slice(p * LANES, (p + 1) * LANES)
        gp, fp = [], []
        for c in range(nch):
            rows = slice(c * CHUNK, (c + 1) * CHUNK)
            pc = pc_ref[0, 0, c:c + 1, sl]
            bh = (bt[p][rows].astype(F32) * pc).astype(BF16)
            kh = (kl[p][rows].astype(F32) * pc).astype(BF16)
            rhs = jnp.concatenate([wu_b[p][rows], jnp.concatenate([zeros_b, v[p][rows]], axis=1)], axis=0)
            bw = _dot_tn(jnp.concatenate([bh, kh], axis=0), rhs)
            gp.append((jnp.where(diag, pc, 0.0) + jnp.where(blk, bw[:, :LANES], 0.0)).astype(BF16))
            fp.append(jnp.where(blk, bw[:, LANES:], 0.0))
        g_mat.append(gp)
        f_mat.append(fp)

    h = [h_ref[p] for p in range(npair)]
    o_chunks = [[] for _ in range(npair)]
    for c in range(nch):
        rows = slice(c * CHUNK, (c + 1) * CHUNK)
        for p in range(npair):
            h_b = h[p].astype(BF16)
            o_chunks[p].append(_dot(q_b[p][rows], h_b) + ol[p][rows])
            h[p] = _dot(g_mat[p][c], h_b) + f_mat[p][c]
    for p in range(npair):
        h_ref[p] = h[p]
    out = jnp.concatenate([jnp.concatenate(oc, axis=0) for oc in o_chunks], axis=1)

    eb = eb_ref[...]
    mean = _dot_hl(out, eb) * (1.0 / B_HEAD_DIM)
    d = out - mean
    var = _dot_hl(d * d, eb) * (1.0 / B_HEAD_DIM)
    y = d * lax.rsqrt(var + GN_EPS) * lnw_ref[...] + lnb_ref[...] + bv_ref[0]
    o_ref[0] = (y * g_ref[0]).astype(o_ref.dtype)


def _rwkv_call(rt, kt, bt, kl, v, g, bv, pc, ln_w, ln_b, eb):
    bsz, s, _ = rt.shape
    tm = TOK_TILE
    tok = pl.BlockSpec((1, tm, B_WIDTH), lambda b, j: (b, j, 0))
    row = pl.BlockSpec((1, B_WIDTH), lambda b, j: (0, 0))
    return pl.pallas_call(
        _rwkv_kernel,
        grid=(bsz, s // tm),
        in_specs=[tok] * 7 + [pl.BlockSpec((1, 1, tm // CHUNK, B_WIDTH), lambda b, j: (b, j, 0, 0)),
                              row, row, pl.BlockSpec((B_WIDTH, B_WIDTH), lambda b, j: (0, 0))],
        out_specs=tok,
        out_shape=jax.ShapeDtypeStruct((bsz, s, B_WIDTH), BF16),
        scratch_shapes=[pltpu.VMEM((B_HEADS // 2, LANES, LANES), F32)],
        compiler_params=_params(("arbitrary", "arbitrary")),
    )(rt, kt, bt, kl, v, g, bv, pc, ln_w, ln_b, eb)


def _ffn_kernel(x_ref, oa_ref, ob_ref, gt1_ref, sh2_ref, sc2_ref, gt2_ref, gf_ref, woa_ref, wob_ref,
                w1_ref, w2_ref, o_ref, x1_ref, h2_ref, acc_ref):
    f = pl.program_id(2)

    @pl.when(f == 0)
    def _():
        mix = _dot(oa_ref[0], woa_ref[...]) + _dot(ob_ref[0], wob_ref[...])
        x1 = x_ref[0] + gt1_ref[0, 0] * mix
        x1_ref[...] = x1
        y = x1 * lax.rsqrt(jnp.mean(x1 * x1, axis=-1, keepdims=True) + RMS_EPS) * gf_ref[...]
        h2_ref[...] = (y * (1.0 + sc2_ref[0, 0]) + sh2_ref[0, 0]).astype(BF16)
        acc_ref[...] = jnp.zeros_like(acc_ref)

    u = jnp.maximum(_dot(h2_ref[...], w1_ref[...]), 0.0)
    acc_ref[...] += _dot((u * u).astype(BF16), w2_ref[...])

    @pl.when(f == pl.num_programs(2) - 1)
    def _():
        o_ref[0] = x1_ref[...] + gt2_ref[0, 0] * acc_ref[...]


def _ffn_call(x, oa, ob, mod4, g_ffn, w_out_a, w_out_b, w1, w2):
    bsz, s, d = x.shape
    dff = w1.shape[1]
    tm = 512 if s % 512 == 0 else TOK_TILE
    tf = 1024
    tok = lambda w: pl.BlockSpec((1, tm, w), lambda b, j, f: (b, j, 0))
    modk = lambda k: pl.BlockSpec((1, 1, 1, d), lambda b, j, f, k=k: (b, k, 0, 0))
    return pl.pallas_call(
        _ffn_kernel,
        grid=(bsz, s // tm, dff // tf),
        in_specs=[tok(d), tok(A_WIDTH), tok(B_WIDTH), modk(2), modk(3), modk(4), modk(5),
                  pl.BlockSpec((1, d), lambda b, j, f: (0, 0)),
                  pl.BlockSpec((A_WIDTH, d), lambda b, j, f: (0, 0)),
                  pl.BlockSpec((B_WIDTH, d), lambda b, j, f: (0, 0)),
                  pl.BlockSpec((d, tf), lambda b, j, f: (0, f)),
                  pl.BlockSpec((tf, d), lambda b, j, f: (f, 0))],
        out_specs=tok(d),
        out_shape=jax.ShapeDtypeStruct((bsz, s, d), F32),
        scratch_shapes=[pltpu.VMEM((tm, d), F32), pltpu.VMEM((tm, d), BF16), pltpu.VMEM((tm, d), F32)],
        compiler_params=_params(("arbitrary", "arbitrary", "arbitrary")),
    )(x, oa, ob, mod4, mod4, mod4, mod4, g_ffn, w_out_a, w_out_b, w1, w2)


def _block_ones(n, blk, dtype=BF16):
    i = jnp.arange(n)
    return ((i[:, None] // blk) == (i[None, :] // blk)).astype(dtype)


def kernel(x, c, w_ada, b_ada, g_mix, g_ffn, w_in, g_q, g_k, g_kv, w_uk, w_uv, mu_shift, w0, w2, a0, a2, g2,
           k_k, k_a, r_k, ln_w, ln_b, w_out, w_ff1, w_ff2):
    bsz, s, d = x.shape
    depth = w_ada.shape[0]
    assert s % Q_TILE == 0 and s % TOK_TILE == 0
    topk = min(TOPK_MAX, s // 4)

    eb = _block_ones(B_WIDTH, B_HEAD_DIM)
    ex = (jnp.arange(A_WIDTH)[:, None] // A_HEAD_DIM == jnp.arange(A_HEADS * KV_LATENT)[None, :] // KV_LATENT
          ).astype(BF16)
    sel = (jnp.arange(LANES)[None, :] == IDX_DIM + jnp.arange(IDX_HEADS)[:, None]).astype(BF16)
    eye_l = jnp.eye(KV_LATENT, dtype=BF16)
    ti = jnp.arange(TOK_TILE)
    tri = (((ti[:, None] // CHUNK) == (ti[None, :] // CHUNK)) & (ti[:, None] >= ti[None, :])).astype(BF16)
    ki = jnp.arange(K_TILE)
    lstrict = (ki[None, :] < ki[:, None]).astype(BF16)

    for l in range(depth):
        w_a = jnp.pad(w_in[l][:, :N_IN_A], ((0, 0), (0, N_A_PAD - N_IN_A)))
        w_in_p = jnp.concatenate([w_a, w_in[l][:, N_IN_A:]], axis=1).astype(BF16)
        wuk_flat = w_uk[l].reshape(KV_LATENT, A_WIDTH).astype(BF16)
        wuk_t = jnp.transpose(w_uk[l], (1, 2, 0))
        wuk_bd = (jnp.eye(A_HEADS, dtype=F32)[:, None, :, None] * wuk_t[:, :, None, :]).reshape(
            A_WIDTH, A_HEADS * KV_LATENT).astype(BF16)
        wuv_t = jnp.transpose(w_uv[l], (1, 0, 2)).reshape(A_HEADS // 2, 2, KV_LATENT, A_HEAD_DIM)
        wuv_pair = (jnp.eye(2, dtype=F32)[None, :, None, :, None] * wuv_t[:, :, :, None, :]).reshape(
            A_HEADS // 2, 2 * KV_LATENT, 2 * A_HEAD_DIM).astype(BF16)
        gqk = jnp.tile(g_q[l] * g_k[l], A_HEADS).reshape(1, A_WIDTH)
        r1 = lambda t: t.reshape(1, -1)

        mod = _mod_call(c, w_ada[l], b_ada[l])
        mod4 = mod.reshape(bsz, 6, 1, d)
        pa, pb = _proj_call(x, mod4, r1(g_mix[l]), w_in_p, r1(mu_shift[l]))
        ckr, cvt, qabs, qidx, kidx, widx = _prep_a_call(pa, r1(g_kv[l]), gqk, wuk_flat, wuk_bd, eb, ex, sel, eye_l)
        o_a = _dsa_call(topk, qabs, qidx, widx, ckr, cvt, kidx, wuv_pair, lstrict)
        rt, kt, bt, kl, v, g, bv, pc = _prep_b_call(
            pb, r1(w0[l]), w2[l].astype(BF16), r1(a0[l]), a2[l].astype(BF16), g2[l].astype(BF16),
            r1(k_k[l]), r1(k_a[l]), r1(r_k[l]), eb, tri)
        o_b = _rwkv_call(rt, kt, bt, kl, v, g, bv, pc, r1(ln_w[l]), r1(ln_b[l]), eb)
        x = _ffn_call(x, o_a, o_b, mod4, r1(g_ffn[l]), w_out[l][:A_WIDTH].astype(BF16),
                      w_out[l][A_WIDTH:].astype(BF16), w_ff1[l].astype(BF16), w_ff2[l].astype(BF16))
    return x
```

```python
import functools

import jax
import jax.numpy as jnp
from jax import lax
from jax.experimental import pallas as pl
from jax.experimental.pallas import tpu as pltpu

F32 = jnp.float32
BF16 = jnp.bfloat16

CHUNK = 64
A_HEADS = 8
A_HEAD_DIM = 64
A_WIDTH = A_HEADS * A_HEAD_DIM
KV_LATENT = 128
IDX_HEADS = 8
IDX_DIM = 64
TOPK_MAX = 256
B_HEADS = 8
B_HEAD_DIM = 64
B_WIDTH = B_HEADS * B_HEAD_DIM
W_LORA = 64
A_LORA = 64
G_LORA = 128
RMS_EPS = 1e-6
GN_EPS = 64e-5
N_IN_A = A_WIDTH + KV_LATENT + IDX_HEADS * IDX_DIM + IDX_DIM + IDX_HEADS
N_IN_B = 3 * B_WIDTH + W_LORA + A_LORA + G_LORA
N_A_PAD = 1280

LANES = 128
TOK_TILE = 256
Q_TILE = 256
K_TILE = 256
DIST_BIG = 1e30
ONES_ROWS = 16
LOG2E = 1.4426950408889634
SEARCH_TRIPS = 10
SEARCH_TRIPS_MAX = 64
VMEM_LIMIT = 56 * 1024 * 1024


def _dot(a, b):
    return jnp.dot(a, b, preferred_element_type=F32)


def _dot_nt(a, b):
    return lax.dot_general(a, b, (((1,), (1,)), ((), ())), preferred_element_type=F32)


def _dot_tn(a, b):
    return lax.dot_general(a, b, (((0,), (0,)), ((), ())), preferred_element_type=F32)


def _split(x):
    hi = x.astype(BF16)
    lo = (x - hi.astype(F32)).astype(BF16)
    return hi, lo


def _dot_hl(x, e):
    hi, lo = _split(x)
    return _dot(hi, e) + _dot(lo, e)


def _params(sem):
    return pltpu.CompilerParams(dimension_semantics=sem, vmem_limit_bytes=VMEM_LIMIT)


def _mod_kernel(c_ref, w_ref, b_ref, o_ref):
    c = c_ref[...]
    s = c * jax.nn.sigmoid(c)
    s_hi, s_lo = _split(s)
    w_hi, w_lo = _split(w_ref[...])
    o_ref[...] = _dot(s_hi, w_hi) + _dot(s_hi, w_lo) + _dot(s_lo, w_hi) + b_ref[...]


def _mod_call(c, w_ada, b_ada):
    bsz, d = c.shape
    n = w_ada.shape[1]
    tn = 1024
    return pl.pallas_call(
        _mod_kernel,
        grid=(n // tn,),
        in_specs=[pl.BlockSpec((bsz, d), lambda j: (0, 0)),
                  pl.BlockSpec((d, tn), lambda j: (0, j)),
                  pl.BlockSpec((1, tn), lambda j: (0, j))],
        out_specs=pl.BlockSpec((bsz, tn), lambda j: (0, j)),
        out_shape=jax.ShapeDtypeStruct((bsz, n), F32),
        compiler_params=_params(("arbitrary",)),
    )(c, w_ada, b_ada.reshape(1, n))


def _proj_kernel(x_ref, sh_ref, sc_ref, g_ref, w_ref, mu_ref, pa_ref, pb_ref, carry_ref):
    j = pl.program_id(1)

    @pl.when(j == 0)
    def _():
        carry_ref[...] = jnp.zeros_like(carry_ref)

    x = x_ref[0]
    y = x * lax.rsqrt(jnp.mean(x * x, axis=-1, keepdims=True) + RMS_EPS) * g_ref[...]
    h = y * (1.0 + sc_ref[0, 0]) + sh_ref[0, 0]
    p = _dot(h.astype(BF16), w_ref[...])
    pa_ref[0] = p[:, :N_A_PAD]
    pb = p[:, N_A_PAD:]
    tm = pb.shape[0]
    row = lax.broadcasted_iota(jnp.int32, (tm, 1), 0)
    prev = jnp.where(row == 0, carry_ref[...], pltpu.roll(pb, 1, axis=0))
    carry_ref[...] = pb[tm - 1:tm, :]
    pb_ref[0] = pb + mu_ref[...] * (prev - pb)


def _proj_call(x, mod4, g_mix, w_in_p, mu):
    bsz, s, d = x.shape
    n = w_in_p.shape[1]
    nb = n - N_A_PAD
    tm = TOK_TILE
    return pl.pallas_call(
        _proj_kernel,
        grid=(bsz, s // tm),
        in_specs=[pl.BlockSpec((1, tm, d), lambda b, j: (b, j, 0)),
                  pl.BlockSpec((1, 1, 1, d), lambda b, j: (b, 0, 0, 0)),
                  pl.BlockSpec((1, 1, 1, d), lambda b, j: (b, 1, 0, 0)),
                  pl.BlockSpec((1, d), lambda b, j: (0, 0)),
                  pl.BlockSpec((d, n), lambda b, j: (0, 0)),
                  pl.BlockSpec((1, nb), lambda b, j: (0, 0))],
        out_specs=[pl.BlockSpec((1, tm, N_A_PAD), lambda b, j: (b, j, 0)),
                   pl.BlockSpec((1, tm, nb), lambda b, j: (b, j, 0))],
        out_shape=[jax.ShapeDtypeStruct((bsz, s, N_A_PAD), F32),
                   jax.ShapeDtypeStruct((bsz, s, nb), F32)],
        scratch_shapes=[pltpu.VMEM((1, nb), F32)],
        compiler_params=_params(("arbitrary", "arbitrary")),
    )(x, mod4, mod4, g_mix, w_in_p, mu)


def _prep_a_kernel(pa_ref, gkv_ref, gqk_ref, wuk_ref, wukbd_ref, eb_ref, ex_ref, sel_ref, eye_ref,
                   ckr_ref, cvt_ref, qabs_ref, qidx_ref, kidx_ref, widx_ref):
    pa = pa_ref[0]
    tm = pa.shape[0]
    q = pa[:, :A_WIDTH]
    cl = pa[:, A_WIDTH:A_WIDTH + KV_LATENT]
    o_qi = A_WIDTH + KV_LATENT
    qi = pa[:, o_qi:o_qi + IDX_HEADS * IDX_DIM]
    o_kw = o_qi + IDX_HEADS * IDX_DIM
    kw = pa[:, o_kw:o_kw + LANES]

    ckv = cl * lax.rsqrt(jnp.mean(cl * cl, axis=-1, keepdims=True) + RMS_EPS) * gkv_ref[...]
    ckv_b = ckv.astype(BF16)
    cvt_ref[0, 0, :KV_LATENT, :] = _dot_nt(eye_ref[...], ckv_b).astype(BF16)
    cvt_ref[0, 0, KV_LATENT:, :] = jnp.ones((ONES_ROWS, tm), BF16)
    kf = _dot(ckv_b, wuk_ref[...])
    ss = _dot_hl(kf * kf, ex_ref[...])
    inv_rms = lax.rsqrt(ss * (1.0 / A_HEAD_DIM) + RMS_EPS)
    ckr_ref[0] = (jnp.concatenate([ckv] * A_HEADS, axis=1) * inv_rms).astype(BF16)

    ssq = _dot_hl(q * q, eb_ref[...])
    qh = q * lax.rsqrt(ssq * (1.0 / A_HEAD_DIM) + RMS_EPS) * gqk_ref[...]
    qabs = _dot(qh.astype(BF16), wukbd_ref[...]) * (A_HEAD_DIM ** -0.5 * LOG2E)
    qabs_ref[0] = qabs.astype(BF16)
    for h in range(IDX_HEADS):
        qidx_ref[0, h] = qi[:, h * IDX_DIM:(h + 1) * IDX_DIM].astype(BF16)
    kidx_ref[0] = kw[:, :IDX_DIM].astype(BF16)
    kw_hi, kw_lo = _split(kw)
    w_t = _dot_nt(sel_ref[...], kw_hi) + _dot_nt(sel_ref[...], kw_lo)
    widx_ref[0, 0] = w_t * (IDX_HEADS ** -0.5 * IDX_DIM ** -0.5)


def _prep_a_call(pa, gkv, gqk, wuk_flat, wuk_bd, eb, ex, sel, eye):
    bsz, s, _ = pa.shape
    tm = TOK_TILE
    full = lambda shape: pl.BlockSpec(shape, lambda b, j: (0,) * len(shape))
    tok = lambda w: pl.BlockSpec((1, tm, w), lambda b, j: (b, j, 0))
    return pl.pallas_call(
        _prep_a_kernel,
        grid=(bsz, s // tm),
        in_specs=[tok(N_A_PAD), full((1, KV_LATENT)), full((1, A_WIDTH)),
                  full((KV_LATENT, A_WIDTH)), full((A_WIDTH, A_HEADS * KV_LATENT)),
                  full((A_WIDTH, A_WIDTH)), full((A_WIDTH, A_HEADS * KV_LATENT)),
                  full((IDX_HEADS, LANES)), full((KV_LATENT, KV_LATENT))],
        out_specs=[tok(A_HEADS * KV_LATENT),
                   pl.BlockSpec((1, 1, KV_LATENT + ONES_ROWS, tm), lambda b, j: (b, j, 0, 0)),
                   tok(A_HEADS * KV_LATENT),
                   pl.BlockSpec((1, IDX_HEADS, tm, IDX_DIM), lambda b, j: (b, 0, j, 0)),
                   tok(IDX_DIM),
                   pl.BlockSpec((1, 1, IDX_HEADS, tm), lambda b, j: (b, j, 0, 0))],
        out_shape=[jax.ShapeDtypeStruct((bsz, s, A_HEADS * KV_LATENT), BF16),
                   jax.ShapeDtypeStruct((bsz, s // tm, KV_LATENT + ONES_ROWS, tm), BF16),
                   jax.ShapeDtypeStruct((bsz, s, A_HEADS * KV_LATENT), BF16),
                   jax.ShapeDtypeStruct((bsz, IDX_HEADS, s, IDX_DIM), BF16),
                   jax.ShapeDtypeStruct((bsz, s, IDX_DIM), BF16),
                   jax.ShapeDtypeStruct((bsz, s // tm, IDX_HEADS, tm), F32)],
        compiler_params=_params(("arbitrary", "arbitrary")),
    )(pa, gkv, gqk, wuk_flat, wuk_bd, eb, ex, sel, eye)


def _colsum8(x):
    y = x.reshape(4, K_TILE // 32, 8, Q_TILE)
    return jnp.sum(jnp.sum(y, axis=1), axis=0)


def _colmin8(x):
    y = x.reshape(4, K_TILE // 32, 8, Q_TILE)
    return jnp.min(jnp.min(y, axis=1), axis=0)


def _colmax8(x):
    y = x.reshape(4, K_TILE // 32, 8, Q_TILE)
    return jnp.max(jnp.max(y, axis=1), axis=0)


def _dsa_kernel(topk, qabs_ref, qidx_ref, widx_ref, ckr_ref, cvt_ref, kidx_ref, wuv_ref, lstrict_ref,
                o_ref, score_ref, dist_ref, logit_ref, m_ref, acc_ref):
    i = pl.program_id(1)
    nkc = i + 1
    t0 = i * Q_TILE
    krow = lax.broadcasted_iota(jnp.int32, (K_TILE, 1), 0)
    qcol = lax.broadcasted_iota(jnp.int32, (1, Q_TILE), 1)
    limit = ((t0 + qcol) // CHUNK + 1) * CHUNK
    kp = jnp.minimum(limit, topk).astype(F32)
    rel = (qcol - krow).astype(F32)

    def p1(kc, carry):
        rmin, rmax = carry
        k = kidx_ref[0, pl.ds(pl.multiple_of(kc * K_TILE, K_TILE), K_TILE), :]
        acc = jnp.zeros((K_TILE, Q_TILE), F32)
        for h in range(IDX_HEADS):
            s = _dot_nt(k, qidx_ref[0, h])
            acc = acc + widx_ref[0, 0, h:h + 1, :] * jnp.maximum(s, 0.0)
        adm = (kc * K_TILE + krow) < limit
        score_ref[kc] = jnp.where(adm, acc, -jnp.inf)
        rmin = jnp.minimum(rmin, _colmin8(jnp.where(adm, acc, jnp.inf)))
        rmax = jnp.maximum(rmax, _colmax8(jnp.where(adm, acc, -jnp.inf)))
        return rmin, rmax

    rmin, rmax = lax.fori_loop(
        0, nkc, p1, (jnp.full((8, Q_TILE), jnp.inf, F32), jnp.full((8, Q_TILE), -jnp.inf, F32)))
    lo = jnp.min(rmin, axis=0, keepdims=True)
    hi = jnp.max(rmax, axis=0, keepdims=True)

    def count(pred):
        def body(kc, acc):
            return acc + _colsum8(jnp.where(pred(score_ref[kc]), 1.0, 0.0))
        return jnp.sum(lax.fori_loop(0, nkc, body, jnp.zeros((8, Q_TILE), F32)), axis=0, keepdims=True)

    def probe(c):
        lo, hi, cnt_lo = c
        mid = lo + 0.5 * (hi - lo)
        cnt = count(lambda sc: sc >= mid)
        ge = cnt >= kp
        return jnp.where(ge, mid, lo), jnp.where(ge, hi, mid), jnp.where(ge, cnt, cnt_lo)

    def any_true(x):
        return jnp.max(jnp.where(x, 1.0, 0.0)) > 0.0

    search = lax.while_loop(
        lambda c: jnp.logical_and(c[0] < SEARCH_TRIPS, any_true(c[3] != kp)),
        lambda c: (c[0] + 1,) + probe(probe(c[1:])),
        (jnp.int32(0), lo, hi, limit.astype(F32)))
    lo = search[1]

    def dist_tile(kc):
        return jnp.abs(rel + (t0 - kc * K_TILE).astype(F32))

    def sel_plain():
        def body(kc, _):
            dist_ref[kc] = jnp.where(score_ref[kc] >= lo, dist_tile(kc), DIST_BIG)
            return 0
        lax.fori_loop(0, nkc, body, 0)

    def sel_ties():
        def thr_of(lo):
            def thr_body(kc, acc):
                sc = score_ref[kc]
                return jnp.minimum(acc, _colmin8(jnp.where(sc >= lo, sc, jnp.inf)))
            thr = jnp.min(lax.fori_loop(0, nkc, thr_body, jnp.full((8, Q_TILE), jnp.inf, F32)),
                          axis=0, keepdims=True)
            return thr, count(lambda sc: sc > thr)

        def t_body(c):
            bracket = probe(probe(c[1:4]))
            return (c[0] + 1,) + bracket + thr_of(bracket[0])

        tie = lax.while_loop(
            lambda c: jnp.logical_and(c[0] < SEARCH_TRIPS_MAX, any_true(c[5] >= kp)),
            t_body, search + thr_of(lo))
        thr, need = tie[4], kp - tie[5]

        def body(kc, run):
            sc = score_ref[kc]
            eq = sc == thr
            eq_f = jnp.where(eq, 1.0, 0.0)
            pre = run + _dot(lstrict_ref[...], eq_f.astype(BF16))
            keep = (sc > thr) | (eq & (pre < need))
            dist_ref[kc] = jnp.where(keep, dist_tile(kc), DIST_BIG)
            return run + jnp.sum(_colsum8(eq_f), axis=0, keepdims=True)
        lax.fori_loop(0, nkc, body, jnp.zeros((1, Q_TILE), F32))

    lax.cond(any_true(search[3] != kp), sel_ties, sel_plain)

    m_ref[...] = jnp.full(m_ref.shape, -jnp.inf, F32)
    acc_ref[...] = jnp.zeros(acc_ref.shape, F32)

    def att(kc, _):
        dist = dist_ref[kc]
        m_new = []
        for h in range(A_HEADS):
            slope = 2.0 ** (-8.0 * (h + 1) / A_HEADS) * LOG2E
            ck = ckr_ref[0, pl.ds(pl.multiple_of(kc * K_TILE, K_TILE), K_TILE), h * KV_LATENT:(h + 1) * KV_LATENT]
            logit = _dot_nt(ck, qabs_ref[0, :, h * KV_LATENT:(h + 1) * KV_LATENT]) - slope * dist
            logit_ref[h] = logit
            m_new.append(jnp.maximum(m_ref[h], jnp.max(_colmax8(logit), axis=0, keepdims=True)))
        cv = cvt_ref[0, kc]
        for h in range(A_HEADS):
            p = jnp.exp2(logit_ref[h] - m_new[h])
            acc_ref[h] = acc_ref[h] * jnp.exp2(m_ref[h] - m_new[h]) + _dot(cv, p.astype(BF16))
            m_ref[h] = m_new[h]
        return 0

    lax.fori_loop(0, nkc, att, 0)

    for pair in range(A_HEADS // 2):
        o_pair = []
        for hh in range(2):
            a = acc_ref[2 * pair + hh]
            o_t = a[:KV_LATENT] * (1.0 / a[KV_LATENT:KV_LATENT + 1])
            o_pair.append(o_t.T.astype(BF16))
        o_lat = jnp.concatenate(o_pair, axis=1)
        o_ref[0, :, pair * LANES:(pair + 1) * LANES] = _dot(o_lat, wuv_ref[pair]).astype(o_ref.dtype)


def _dsa_call(topk, qabs, qidx, widx, ckr, cvt, kidx, wuv_pair, lstrict):
    bsz, s, _ = qabs.shape
    nq = s // Q_TILE
    nk = s // K_TILE
    qt = lambda w: pl.BlockSpec((1, Q_TILE, w), lambda b, i: (b, i, 0))
    return pl.pallas_call(
        functools.partial(_dsa_kernel, topk),
        grid=(bsz, nq),
        in_specs=[qt(A_HEADS * KV_LATENT),
                  pl.BlockSpec((1, IDX_HEADS, Q_TILE, IDX_DIM), lambda b, i: (b, 0, i, 0)),
                  pl.BlockSpec((1, 1, IDX_HEADS, Q_TILE), lambda b, i: (b, i, 0, 0)),
                  pl.BlockSpec((1, s, A_HEADS * KV_LATENT), lambda b, i: (b, 0, 0)),
                  pl.BlockSpec((1, nk, KV_LATENT + ONES_ROWS, K_TILE), lambda b, i: (b, 0, 0, 0)),
                  pl.BlockSpec((1, s, IDX_DIM), lambda b, i: (b, 0, 0)),
                  pl.BlockSpec((A_HEADS // 2, 2 * KV_LATENT, LANES), lambda b, i: (0, 0, 0)),
                  pl.BlockSpec((K_TILE, K_TILE), lambda b, i: (0, 0))],
        out_specs=qt(A_WIDTH),
        out_shape=jax.ShapeDtypeStruct((bsz, s, A_WIDTH), BF16),
        scratch_shapes=[pltpu.VMEM((nk, K_TILE, Q_TILE), F32),
                        pltpu.VMEM((nk, K_TILE, Q_TILE), F32),
                        pltpu.VMEM((A_HEADS, K_TILE, Q_TILE), F32),
                        pltpu.VMEM((A_HEADS, 1, Q_TILE), F32),
                        pltpu.VMEM((A_HEADS, KV_LATENT + ONES_ROWS, Q_TILE), F32)],
        compiler_params=_params(("arbitrary", "arbitrary")),
    )(qabs, qidx, widx, ckr, cvt, kidx, wuv_pair, lstrict)


def _prep_b_kernel(pb_ref, w0_ref, w2_ref, a0_ref, a2_ref, g2_ref, kk_ref, ka_ref, rk_ref, eb_ref, tri_ref,
                   rt_ref, kt_ref, bt_ref, kl_ref, v_ref, g_ref, bv_ref, pc_ref):
    pb = pb_ref[0]
    r = pb[:, :B_WIDTH]
    k = pb[:, B_WIDTH:2 * B_WIDTH]
    v = pb[:, 2 * B_WIDTH:3 * B_WIDTH]
    o = 3 * B_WIDTH
    xw = pb[:, o:o + W_LORA]
    xa = pb[:, o + W_LORA:o + W_LORA + A_LORA]
    xg = pb[:, o + W_LORA + A_LORA:o + W_LORA + A_LORA + G_LORA]

    z = w0_ref[...] + _dot(jnp.tanh(xw).astype(BF16), w2_ref[...])
    nz = -z
    softplus = jnp.maximum(nz, 0.0) + jnp.log(1.0 + jnp.exp(-jnp.abs(nz)))
    lw = -jnp.exp(-softplus - 0.5)
    a = jax.nn.sigmoid(a0_ref[...] + _dot(xa.astype(BF16), a2_ref[...]))
    g = _dot(jax.nn.sigmoid(xg).astype(BF16), g2_ref[...])
    kk = k * kk_ref[...]
    kkn = kk / jnp.maximum(jnp.sqrt(_dot_hl(kk * kk, eb_ref[...])), 1e-12)
    kp = k * (1.0 + (a - 1.0) * ka_ref[...])
    bonus = _dot_hl(r * kp * rk_ref[...], eb_ref[...])

    lw_hi, lw_lo = _split(lw)
    cum = _dot(tri_ref[...], lw_hi) + _dot(tri_ref[...], lw_lo)
    e_pos = jnp.exp(cum)
    e_neg = jnp.exp(-cum)
    rt_ref[0] = (r * e_pos).astype(BF16)
    kt_ref[0] = (kkn * jnp.exp(cum - lw)).astype(BF16)
    bt_ref[0] = (kkn * a * e_neg).astype(BF16)
    kl_ref[0] = (kp * e_neg).astype(BF16)
    v_ref[0] = v.astype(BF16)
    g_ref[0] = g
    bv_ref[0] = bonus * v
    for c in range(pb.shape[0] // CHUNK):
        pc_ref[0, 0, c:c + 1, :] = e_pos[(c + 1) * CHUNK - 1:(c + 1) * CHUNK, :]


def _prep_b_call(pb, w0, w2, a0, a2, g2, k_k, k_a, r_k, eb, tri):
    bsz, s, nb = pb.shape
    tm = TOK_TILE
    full = lambda shape: pl.BlockSpec(shape, lambda b, j: (0,) * len(shape))
    tok = lambda w: pl.BlockSpec((1, tm, w), lambda b, j: (b, j, 0))
    row = full((1, B_WIDTH))
    bf = jax.ShapeDtypeStruct((bsz, s, B_WIDTH), BF16)
    ff = jax.ShapeDtypeStruct((bsz, s, B_WIDTH), F32)
    return pl.pallas_call(
        _prep_b_kernel,
        grid=(bsz, s // tm),
        in_specs=[tok(nb), row, full((W_LORA, B_WIDTH)), row, full((A_LORA, B_WIDTH)),
                  full((G_LORA, B_WIDTH)), row, row, row, full((B_WIDTH, B_WIDTH)), full((tm, tm))],
        out_specs=[tok(B_WIDTH)] * 7 + [pl.BlockSpec((1, 1, tm // CHUNK, B_WIDTH), lambda b, j: (b, j, 0, 0))],
        out_shape=[bf, bf, bf, bf, bf, ff, ff,
                   jax.ShapeDtypeStruct((bsz, s // tm, tm // CHUNK, B_WIDTH), F32)],
        compiler_params=_params(("arbitrary", "arbitrary")),
    )(pb, w0, w2, a0, a2, g2, k_k, k_a, r_k, eb, tri)


def _rwkv_kernel(rt_ref, kt_ref, bt_ref, kl_ref, v_ref, g_ref, bv_ref, pc_ref, lnw_ref, lnb_ref, eb_ref,
                 o_ref, h_ref):
    j = pl.program_id(1)

    @pl.when(j == 0)
    def _():
        h_ref[...] = jnp.zeros_like(h_ref)

    tm = rt_ref.shape[1]
    nch = tm // CHUNK
    ri = lax.broadcasted_iota(jnp.int32, (tm, tm), 0)
    ci = lax.broadcasted_iota(jnp.int32, (tm, tm), 1)
    same = (ri // CHUNK) == (ci // CHUNK)
    strict = same & (ri > ci)
    incl = same & (ri >= ci)
    eye_t = jnp.where(ri == ci, 1.0, 0.0)
    r2 = lax.broadcasted_iota(jnp.int32, (LANES, LANES), 0)
    c2 = lax.broadcasted_iota(jnp.int32, (LANES, LANES), 1)
    blk = (r2 // B_HEAD_DIM) == (c2 // B_HEAD_DIM)
    diag = r2 == c2
    lane = lax.broadcasted_iota(jnp.int32, (1, LANES), 1)
    zero_b = jnp.zeros((), BF16)

    npair = B_HEADS // 2
    heads = [(p, hh) for p in range(npair) for hh in range(2)]
    head0 = (lane // B_HEAD_DIM) == 0
    rt, kt, bt, kl, v = [], [], [], [], []
    a_ab, a_ak, m_rb, m_rk = [], [], [], []
    for p in range(npair):
        sl = slice(p * LANES, (p + 1) * LANES)
        rt.append(rt_ref[0, :, sl])
        kt.append(kt_ref[0, :, sl])
        bt.append(bt_ref[0, :, sl])
        kl.append(kl_ref[0, :, sl])
        v.append(v_ref[0, :, sl])
        lhs = jnp.concatenate([jnp.where(head0, kt[p], zero_b), jnp.where(head0, zero_b, kt[p]),
                               jnp.where(head0, rt[p], zero_b), jnp.where(head0, zero_b, rt[p])], axis=0)
        prod = _dot_nt(lhs, jnp.concatenate([bt[p], kl[p]], axis=0))
        for hh in range(2):
            a_ab.append(jnp.where(strict, prod[hh * tm:(hh + 1) * tm, :tm], 0.0))
            a_ak.append(jnp.where(strict, prod[hh * tm:(hh + 1) * tm, tm:], 0.0).astype(BF16))
            m_rb.append(jnp.where(incl, prod[(2 + hh) * tm:(3 + hh) * tm, :tm], 0.0).astype(BF16))
            m_rk.append(jnp.where(incl, prod[(2 + hh) * tm:(3 + hh) * tm, tm:], 0.0).astype(BF16))

    t_inv = [(eye_t - a).astype(BF16) for a in a_ab]
    a_pow = [a.astype(BF16) for a in a_ab]
    for _ in range(5):
        a_sq = [_dot(a, a) for a in a_pow]
        a_pow = [a.astype(BF16) for a in a_sq]
        t_inv = [_dot(t, (eye_t + a).astype(BF16)).astype(BF16) for t, a in zip(t_inv, a_sq)]

    avm = [_dot(jnp.concatenate([a_ak[i], m_rk[i]], axis=0), v[p]) for i, (p, _) in enumerate(heads)]
    x = [_dot(t_inv[i], jnp.concatenate([kt[p], avm[i][:tm].astype(BF16)], axis=1))
         for i, (p, _) in enumerate(heads)]
    y = [_dot(m_rb[i], x[i].astype(BF16)) for i in range(len(heads))]

    wu_b, q_b, ol = [], [], []
    for p in range(npair):
        i0, i1 = 2 * p, 2 * p + 1
        head0_2 = jnp.concatenate([head0, head0], axis=1)
        wu_b.append((-jnp.where(head0_2, x[i0], x[i1])).astype(BF16))
        yy = jnp.where(head0_2, y[i0], y[i1])
        q_b.append((rt[p].astype(F32) - yy[:, :LANES]).astype(BF16))
        ol.append(jnp.where(head0, avm[i0][tm:], avm[i1][tm:]) - yy[:, LANES:])

    g_mat, f_mat = [], []
    zeros_b = jnp.zeros((CHUNK, LANES), BF16)
    for p in range(npair):
        sl = slice(p * LANES, (p + 1) * LANES)
        gp, fp = [], []
        for c in range(nch):
            rows = slice(c * CHUNK, (c + 1) * CHUNK)
            pc = pc_ref[0, 0, c:c + 1, sl]
            bh = (bt[p][rows].astype(F32) * pc).astype(BF16)
            kh = (kl[p][rows].astype(F32) * pc).astype(BF16)
            rhs = jnp.concatenate([wu_b[p][rows], jnp.concatenate([zeros_b, v[p][rows]], axis=1)], axis=0)
            bw = _dot_tn(jnp.concatenate([bh, kh], axis=0), rhs)
            gp.append((jnp.where(diag, pc, 0.0) + jnp.where(blk, bw[:, :LANES], 0.0)).astype(BF16))
            fp.append(jnp.where(blk, bw[:, LANES:], 0.0))
        g_mat.append(gp)
        f_mat.append(fp)

    h = [h_ref[p] for p in range(npair)]
    o_chunks = [[] for _ in range(npair)]
    for c in range(nch):
        rows = slice(c * CHUNK, (c + 1) * CHUNK)
        for p in range(npair):
            h_b = h[p].astype(BF16)
            o_chunks[p].append(_dot(q_b[p][rows], h_b) + ol[p][rows])
            h[p] = _dot(g_mat[p][c], h_b) + f_mat[p][c]
    for p in range(npair):
        h_ref[p] = h[p]
    out = jnp.concatenate([jnp.concatenate(oc, axis=0) for oc in o_chunks], axis=1)

    eb = eb_ref[...]
    mean = _dot_hl(out, eb) * (1.0 / B_HEAD_DIM)
    d = out - mean
    var = _dot_hl(d * d, eb) * (1.0 / B_HEAD_DIM)
    y = d * lax.rsqrt(var + GN_EPS) * lnw_ref[...] + lnb_ref[...] + bv_ref[0]
    o_ref[0] = (y * g_ref[0]).astype(o_ref.dtype)


def _rwkv_call(rt, kt, bt, kl, v, g, bv, pc, ln_w, ln_b, eb):
    bsz, s, _ = rt.shape
    tm = TOK_TILE
    tok = pl.BlockSpec((1, tm, B_WIDTH), lambda b, j: (b, j, 0))
    row = pl.BlockSpec((1, B_WIDTH), lambda b, j: (0, 0))
    return pl.pallas_call(
        _rwkv_kernel,
        grid=(bsz, s // tm),
        in_specs=[tok] * 7 + [pl.BlockSpec((1, 1, tm // CHUNK, B_WIDTH), lambda b, j: (b, j, 0, 0)),
                              row, row, pl.BlockSpec((B_WIDTH, B_WIDTH), lambda b, j: (0, 0))],
        out_specs=tok,
        out_shape=jax.ShapeDtypeStruct((bsz, s, B_WIDTH), BF16),
        scratch_shapes=[pltpu.VMEM((B_HEADS // 2, LANES, LANES), F32)],
        compiler_params=_params(("arbitrary", "arbitrary")),
    )(rt, kt, bt, kl, v, g, bv, pc, ln_w, ln_b, eb)


def _ffn_kernel(x_ref, oa_ref, ob_ref, gt1_ref, sh2_ref, sc2_ref, gt2_ref, gf_ref, woa_ref, wob_ref,
                w1_ref, w2_ref, o_ref, x1_ref, h2_ref, acc_ref):
    f = pl.program_id(2)

    @pl.when(f == 0)
    def _():
        mix = _dot(oa_ref[0], woa_ref[...]) + _dot(ob_ref[0], wob_ref[...])
        x1 = x_ref[0] + gt1_ref[0, 0] * mix
        x1_ref[...] = x1
        y = x1 * lax.rsqrt(jnp.mean(x1 * x1, axis=-1, keepdims=True) + RMS_EPS) * gf_ref[...]
        h2_ref[...] = (y * (1.0 + sc2_ref[0, 0]) + sh2_ref[0, 0]).astype(BF16)
        acc_ref[...] = jnp.zeros_like(acc_ref)

    u = jnp.maximum(_dot(h2_ref[...], w1_ref[...]), 0.0)
    acc_ref[...] += _dot((u * u).astype(BF16), w2_ref[...])

    @pl.when(f == pl.num_programs(2) - 1)
    def _():
        o_ref[0] = x1_ref[...] + gt2_ref[0, 0] * acc_ref[...]


def _ffn_call(x, oa, ob, mod4, g_ffn, w_out_a, w_out_b, w1, w2):
    bsz, s, d = x.shape
    dff = w1.shape[1]
    tm = 512 if s % 512 == 0 else TOK_TILE
    tf = 1024
    tok = lambda w: pl.BlockSpec((1, tm, w), lambda b, j, f: (b, j, 0))
    modk = lambda k: pl.BlockSpec((1, 1, 1, d), lambda b, j, f, k=k: (b, k, 0, 0))
    return pl.pallas_call(
        _ffn_kernel,
        grid=(bsz, s // tm, dff // tf),
        in_specs=[tok(d), tok(A_WIDTH), tok(B_WIDTH), modk(2), modk(3), modk(4), modk(5),
                  pl.BlockSpec((1, d), lambda b, j, f: (0, 0)),
                  pl.BlockSpec((A_WIDTH, d), lambda b, j, f: (0, 0)),
                  pl.BlockSpec((B_WIDTH, d), lambda b, j, f: (0, 0)),
                  pl.BlockSpec((d, tf), lambda b, j, f: (0, f)),
                  pl.BlockSpec((tf, d), lambda b, j, f: (f, 0))],
        out_specs=tok(d),
        out_shape=jax.ShapeDtypeStruct((bsz, s, d), F32),
        scratch_shapes=[pltpu.VMEM((tm, d), F32), pltpu.VMEM((tm, d), BF16), pltpu.VMEM((tm, d), F32)],
        compiler_params=_params(("arbitrary", "arbitrary", "arbitrary")),
    )(x, oa, ob, mod4, mod4, mod4, mod4, g_ffn, w_out_a, w_out_b, w1, w2)


def _block_ones(n, blk, dtype=BF16):
    i = jnp.arange(n)
    return ((i[:, None] // blk) == (i[None, :] // blk)).astype(dtype)


def kernel(x, c, w_ada, b_ada, g_mix, g_ffn, w_in, g_q, g_k, g_kv, w_uk, w_uv, mu_shift, w0, w2, a0, a2, g2,
           k_k, k_a, r_k, ln_w, ln_b, w_out, w_ff1, w_ff2):
    bsz, s, d = x.shape
    depth = w_ada.shape[0]
    assert s % Q_TILE == 0 and s % TOK_TILE == 0
    topk = min(TOPK_MAX, s // 4)

    eb = _block_ones(B_WIDTH, B_HEAD_DIM)
    ex = (jnp.arange(A_WIDTH)[:, None] // A_HEAD_DIM == jnp.arange(A_HEADS * KV_LATENT)[None, :] // KV_LATENT
          ).astype(BF16)
    sel = (jnp.arange(LANES)[None, :] == IDX_DIM + jnp.arange(IDX_HEADS)[:, None]).astype(BF16)
    eye_l = jnp.eye(KV_LATENT, dtype=BF16)
    ti = jnp.arange(TOK_TILE)
    tri = (((ti[:, None] // CHUNK) == (ti[None, :] // CHUNK)) & (ti[:, None] >= ti[None, :])).astype(BF16)
    ki = jnp.arange(K_TILE)
    lstrict = (ki[None, :] < ki[:, None]).astype(BF16)

    for l in range(depth):
        w_a = jnp.pad(w_in[l][:, :N_IN_A], ((0, 0), (0, N_A_PAD - N_IN_A)))
        w_in_p = jnp.concatenate([w_a, w_in[l][:, N_IN_A:]], axis=1).astype(BF16)
        wuk_flat = w_uk[l].reshape(KV_LATENT, A_WIDTH).astype(BF16)
        wuk_t = jnp.transpose(w_uk[l], (1, 2, 0))
        wuk_bd = (jnp.eye(A_HEADS, dtype=F32)[:, None, :, None] * wuk_t[:, :, None, :]).reshape(
            A_WIDTH, A_HEADS * KV_LATENT).astype(BF16)
        wuv_t = jnp.transpose(w_uv[l], (1, 0, 2)).reshape(A_HEADS // 2, 2, KV_LATENT, A_HEAD_DIM)
        wuv_pair = (jnp.eye(2, dtype=F32)[None, :, None, :, None] * wuv_t[:, :, :, None, :]).reshape(
            A_HEADS // 2, 2 * KV_LATENT, 2 * A_HEAD_DIM).astype(BF16)
        gqk = jnp.tile(g_q[l] * g_k[l], A_HEADS).reshape(1, A_WIDTH)
        r1 = lambda t: t.reshape(1, -1)

        mod = _mod_call(c, w_ada[l], b_ada[l])
        mod4 = mod.reshape(bsz, 6, 1, d)
        pa, pb = _proj_call(x, mod4, r1(g_mix[l]), w_in_p, r1(mu_shift[l]))
        ckr, cvt, qabs, qidx, kidx, widx = _prep_a_call(pa, r1(g_kv[l]), gqk, wuk_flat, wuk_bd, eb, ex, sel, eye_l)
        o_a = _dsa_call(topk, qabs, qidx, widx, ckr, cvt, kidx, wuv_pair, lstrict)
        rt, kt, bt, kl, v, g, bv, pc = _prep_b_call(
            pb, r1(w0[l]), w2[l].astype(BF16), r1(a0[l]), a2[l].astype(BF16), g2[l].astype(BF16),
            r1(k_k[l]), r1(k_a[l]), r1(r_k[l]), eb, tri)
        o_b = _rwkv_call(rt, kt, bt, kl, v, g, bv, pc, r1(ln_w[l]), r1(ln_b[l]), eb)
        x = _ffn_call(x, o_a, o_b, mod4, r1(g_ffn[l]), w_out[l][:A_WIDTH].astype(BF16),
                      w_out[l][A_WIDTH:].astype(BF16), w_ff1[l].astype(BF16), w_ff2[l].astype(BF16))
    return x
```

```python
import functools

import jax
import jax.numpy as jnp
from jax import lax
from jax.experimental import pallas as pl
from jax.experimental.pallas import tpu as pltpu

F32 = jnp.float32
BF16 = jnp.bfloat16

CHUNK = 64
A_HEADS = 8
A_HEAD_DIM = 64
A_WIDTH = A_HEADS * A_HEAD_DIM
KV_LATENT = 128
IDX_HEADS = 8
IDX_DIM = 64
TOPK_MAX = 256
B_HEADS = 8
B_HEAD_DIM = 64
B_WIDTH = B_HEADS * B_HEAD_DIM
W_LORA = 64
A_LORA = 64
G_LORA = 128
RMS_EPS = 1e-6
GN_EPS = 64e-5
N_IN_A = A_WIDTH + KV_LATENT + IDX_HEADS * IDX_DIM + IDX_DIM + IDX_HEADS
N_IN_B = 3 * B_WIDTH + W_LORA + A_LORA + G_LORA
N_A_PAD = 1280

LANES = 128
SEG_K = 256
TOK_TILE = 256
Q_TILE = 256
K_TILE = 256
DIST_BIG = 1e30
ONES_ROWS = 16
LOG2E = 1.4426950408889634
SEARCH_PROBES = 20
SEARCH_TRIPS_MAX = 64
VMEM_LIMIT = 56 * 1024 * 1024


def _dot(a, b):
    return jnp.dot(a, b, preferred_element_type=F32)


def _dot_nt(a, b):
    return lax.dot_general(a, b, (((1,), (1,)), ((), ())), preferred_element_type=F32)


def _dot_tn(a, b):
    return lax.dot_general(a, b, (((0,), (0,)), ((), ())), preferred_element_type=F32)


def _split(x):
    hi = x.astype(BF16)
    lo = (x - hi.astype(F32)).astype(BF16)
    return hi, lo


def _dot_hl(x, e):
    hi, lo = _split(x)
    return _dot(hi, e) + _dot(lo, e)


def _seg_dot_hl(x, e):
    k = e.shape[0]
    return jnp.concatenate([_dot_hl(x[:, j:j + k], e) for j in range(0, x.shape[1], k)], axis=1)


def _params(sem):
    return pltpu.CompilerParams(dimension_semantics=sem, vmem_limit_bytes=VMEM_LIMIT)


def _mod_kernel(c_ref, w_ref, b_ref, o_ref):
    c = c_ref[...]
    s = c * jax.nn.sigmoid(c)
    s_hi, s_lo = _split(s)
    w_hi, w_lo = _split(w_ref[...])
    o_ref[...] = _dot(s_hi, w_hi) + _dot(s_hi, w_lo) + _dot(s_lo, w_hi) + b_ref[...]


def _mod_call(c, w_ada, b_ada):
    bsz, d = c.shape
    n = w_ada.shape[1]
    tn = 1024
    return pl.pallas_call(
        _mod_kernel,
        grid=(n // tn,),
        in_specs=[pl.BlockSpec((bsz, d), lambda j: (0, 0)),
                  pl.BlockSpec((d, tn), lambda j: (0, j)),
                  pl.BlockSpec((1, tn), lambda j: (0, j))],
        out_specs=pl.BlockSpec((bsz, tn), lambda j: (0, j)),
        out_shape=jax.ShapeDtypeStruct((bsz, n), F32),
        compiler_params=_params(("arbitrary",)),
    )(c, w_ada, b_ada.reshape(1, n))


def _proj_kernel(x_ref, sh_ref, sc_ref, g_ref, w_ref, mu_ref, pa_ref, pb_ref, carry_ref):
    j = pl.program_id(1)

    @pl.when(j == 0)
    def _():
        carry_ref[...] = jnp.zeros_like(carry_ref)

    x = x_ref[0]
    y = x * lax.rsqrt(jnp.mean(x * x, axis=-1, keepdims=True) + RMS_EPS) * g_ref[...]
    h = y * (1.0 + sc_ref[0, 0]) + sh_ref[0, 0]
    p = _dot(h.astype(BF16), w_ref[...])
    pa_ref[0] = p[:, :N_A_PAD]
    pb = p[:, N_A_PAD:]
    tm = pb.shape[0]
    row = lax.broadcasted_iota(jnp.int32, (tm, 1), 0)
    prev = jnp.where(row == 0, carry_ref[...], pltpu.roll(pb, 1, axis=0))
    carry_ref[...] = pb[tm - 1:tm, :]
    pb_ref[0] = pb + mu_ref[...] * (prev - pb)


def _proj_call(x, mod4, g_mix, w_in_p, mu):
    bsz, s, d = x.shape
    n = w_in_p.shape[1]
    nb = n - N_A_PAD
    tm = TOK_TILE
    return pl.pallas_call(
        _proj_kernel,
        grid=(bsz, s // tm),
        in_specs=[pl.BlockSpec((1, tm, d), lambda b, j: (b, j, 0)),
                  pl.BlockSpec((1, 1, 1, d), lambda b, j: (b, 0, 0, 0)),
                  pl.BlockSpec((1, 1, 1, d), lambda b, j: (b, 1, 0, 0)),
                  pl.BlockSpec((1, d), lambda b, j: (0, 0)),
                  pl.BlockSpec((d, n), lambda b, j: (0, 0)),
                  pl.BlockSpec((1, nb), lambda b, j: (0, 0))],
        out_specs=[pl.BlockSpec((1, tm, N_A_PAD), lambda b, j: (b, j, 0)),
                   pl.BlockSpec((1, tm, nb), lambda b, j: (b, j, 0))],
        out_shape=[jax.ShapeDtypeStruct((bsz, s, N_A_PAD), F32),
                   jax.ShapeDtypeStruct((bsz, s, nb), F32)],
        scratch_shapes=[pltpu.VMEM((1, nb), F32)],
        compiler_params=_params(("arbitrary", "arbitrary")),
    )(x, mod4, mod4, g_mix, w_in_p, mu)


def _prep_a_kernel(pa_ref, gkv_ref, gqk_ref, wuk_ref, wukbd_ref, eb_ref, ex_ref, sel_ref, eye_ref,
                   ckr_ref, cvt_ref, qabs_ref, qidx_ref, kidx_ref, widx_ref):
    pa = pa_ref[0]
    tm = pa.shape[0]
    q = pa[:, :A_WIDTH]
    cl = pa[:, A_WIDTH:A_WIDTH + KV_LATENT]
    o_qi = A_WIDTH + KV_LATENT
    qi = pa[:, o_qi:o_qi + IDX_HEADS * IDX_DIM]
    o_kw = o_qi + IDX_HEADS * IDX_DIM
    kw = pa[:, o_kw:o_kw + LANES]

    ckv = cl * lax.rsqrt(jnp.mean(cl * cl, axis=-1, keepdims=True) + RMS_EPS) * gkv_ref[...]
    ckv_b = ckv.astype(BF16)
    cvt_ref[0, 0, :KV_LATENT, :] = _dot_nt(eye_ref[...], ckv_b).astype(BF16)
    cvt_ref[0, 0, KV_LATENT:, :] = jnp.ones((ONES_ROWS, tm), BF16)
    kf = _dot(ckv_b, wuk_ref[...])
    ss = _seg_dot_hl(kf * kf, ex_ref[...])
    inv_rms = lax.rsqrt(ss * (1.0 / A_HEAD_DIM) + RMS_EPS)
    ckr_ref[0] = (jnp.concatenate([ckv] * A_HEADS, axis=1) * inv_rms).astype(BF16)

    ssq = _seg_dot_hl(q * q, eb_ref[...])
    qh = q * lax.rsqrt(ssq * (1.0 / A_HEAD_DIM) + RMS_EPS) * gqk_ref[...]
    qh_b = qh.astype(BF16)
    for j in range(A_HEADS // 2):
        qabs = _dot(qh_b[:, j * LANES:(j + 1) * LANES], wukbd_ref[j]) * (A_HEAD_DIM ** -0.5 * LOG2E)
        qabs_ref[0, :, 2 * j * KV_LATENT:2 * (j + 1) * KV_LATENT] = qabs.astype(BF16)
    for h in range(IDX_HEADS):
        qidx_ref[0, h] = qi[:, h * IDX_DIM:(h + 1) * IDX_DIM].astype(BF16)
    kidx_ref[0] = kw[:, :IDX_DIM].astype(BF16)
    kw_hi, kw_lo = _split(kw)
    w_t = _dot_nt(sel_ref[...], kw_hi) + _dot_nt(sel_ref[...], kw_lo)
    widx_ref[0, 0] = w_t * (IDX_HEADS ** -0.5 * IDX_DIM ** -0.5)


def _prep_a_call(pa, gkv, gqk, wuk_flat, wuk_bd, eb, ex, sel, eye):
    bsz, s, _ = pa.shape
    tm = TOK_TILE
    full = lambda shape: pl.BlockSpec(shape, lambda b, j: (0,) * len(shape))
    tok = lambda w: pl.BlockSpec((1, tm, w), lambda b, j: (b, j, 0))
    return pl.pallas_call(
        _prep_a_kernel,
        grid=(bsz, s // tm),
        in_specs=[tok(N_A_PAD), full((1, KV_LATENT)), full((1, A_WIDTH)),
                  full((KV_LATENT, A_WIDTH)), full((A_HEADS // 2, 2 * A_HEAD_DIM, 2 * KV_LATENT)),
                  full((SEG_K, SEG_K)), full((2 * A_HEAD_DIM, 2 * KV_LATENT)),
                  full((IDX_HEADS, LANES)), full((KV_LATENT, KV_LATENT))],
        out_specs=[tok(A_HEADS * KV_LATENT),
                   pl.BlockSpec((1, 1, KV_LATENT + ONES_ROWS, tm), lambda b, j: (b, j, 0, 0)),
                   tok(A_HEADS * KV_LATENT),
                   pl.BlockSpec((1, IDX_HEADS, tm, IDX_DIM), lambda b, j: (b, 0, j, 0)),
                   tok(IDX_DIM),
                   pl.BlockSpec((1, 1, IDX_HEADS, tm), lambda b, j: (b, j, 0, 0))],
        out_shape=[jax.ShapeDtypeStruct((bsz, s, A_HEADS * KV_LATENT), BF16),
                   jax.ShapeDtypeStruct((bsz, s // tm, KV_LATENT + ONES_ROWS, tm), BF16),
                   jax.ShapeDtypeStruct((bsz, s, A_HEADS * KV_LATENT), BF16),
                   jax.ShapeDtypeStruct((bsz, IDX_HEADS, s, IDX_DIM), BF16),
                   jax.ShapeDtypeStruct((bsz, s, IDX_DIM), BF16),
                   jax.ShapeDtypeStruct((bsz, s // tm, IDX_HEADS, tm), F32)],
        compiler_params=_params(("arbitrary", "arbitrary")),
    )(pa, gkv, gqk, wuk_flat, wuk_bd, eb, ex, sel, eye)


def _colsum8(x):
    y = x.reshape(4, K_TILE // 32, 8, Q_TILE)
    return jnp.sum(jnp.sum(y, axis=1), axis=0)


def _colmin8(x):
    y = x.reshape(4, K_TILE // 32, 8, Q_TILE)
    return jnp.min(jnp.min(y, axis=1), axis=0)


def _colmax8(x):
    y = x.reshape(4, K_TILE // 32, 8, Q_TILE)
    return jnp.max(jnp.max(y, axis=1), axis=0)


def _dsa_kernel(topk, qabs_ref, qidx_ref, widx_ref, ckr_ref, cvt_ref, kidx_ref, wuv_ref, lstrict_ref,
                o_ref, score_ref, dist_ref, logit_ref, m_ref, acc_ref):
    i = pl.program_id(1)
    nkc = i + 1
    t0 = i * Q_TILE
    krow = lax.broadcasted_iota(jnp.int32, (K_TILE, 1), 0)
    qcol = lax.broadcasted_iota(jnp.int32, (1, Q_TILE), 1)
    limit = ((t0 + qcol) // CHUNK + 1) * CHUNK
    kp = jnp.minimum(limit, topk).astype(F32)
    rel = (qcol - krow).astype(F32)

    def p1(kc, carry):
        rmin, rmax = carry
        k = kidx_ref[0, pl.ds(pl.multiple_of(kc * K_TILE, K_TILE), K_TILE), :]
        acc = jnp.zeros((K_TILE, Q_TILE), F32)
        for h in range(IDX_HEADS):
            s = _dot_nt(k, qidx_ref[0, h])
            acc = acc + widx_ref[0, 0, h:h + 1, :] * jnp.maximum(s, 0.0)
        adm = (kc * K_TILE + krow) < limit
        score_ref[kc] = jnp.where(adm, acc, -jnp.inf)
        rmin = jnp.minimum(rmin, _colmin8(jnp.where(adm, acc, jnp.inf)))
        rmax = jnp.maximum(rmax, _colmax8(jnp.where(adm, acc, -jnp.inf)))
        return rmin, rmax

    rmin, rmax = lax.fori_loop(
        0, nkc, p1, (jnp.full((8, Q_TILE), jnp.inf, F32), jnp.full((8, Q_TILE), -jnp.inf, F32)))
    lo = jnp.min(rmin, axis=0, keepdims=True)
    hi = jnp.max(rmax, axis=0, keepdims=True)

    def count(pred):
        def body(kc, acc):
            return acc + _colsum8(jnp.where(pred(score_ref[kc]), 1.0, 0.0))
        return jnp.sum(lax.fori_loop(0, nkc, body, jnp.zeros((8, Q_TILE), F32)), axis=0, keepdims=True)

    def probe(c):
        lo, hi, cnt_lo = c
        mid = lo + 0.5 * (hi - lo)
        cnt = count(lambda sc: sc >= mid)
        ge = cnt >= kp
        return jnp.where(ge, mid, lo), jnp.where(ge, hi, mid), jnp.where(ge, cnt, cnt_lo)

    def thr_of(lo):
        def thr_body(kc, acc):
            sc = score_ref[kc]
            return jnp.minimum(acc, _colmin8(jnp.where(sc >= lo, sc, jnp.inf)))
        thr = jnp.min(lax.fori_loop(0, nkc, thr_body, jnp.full((8, Q_TILE), jnp.inf, F32)), axis=0, keepdims=True)
        return thr, count(lambda sc: sc > thr)

    def any_true(x):
        return jnp.max(jnp.where(x, 1.0, 0.0)) > 0.0

    bracket = lax.fori_loop(0, SEARCH_PROBES, lambda _, c: probe(c), (lo, hi, limit.astype(F32)))
    search = lax.while_loop(
        lambda c: jnp.logical_and(c[0] < SEARCH_TRIPS_MAX, any_true(c[5] >= kp)),
        lambda c: (c[0] + 1,) + (lambda br: br + thr_of(br[0]))(probe(probe(c[1:4]))),
        (jnp.int32(0),) + bracket + thr_of(bracket[0]))
    lo, cnt_lo, thr, need = search[1], search[3], search[4], kp - search[5]

    def dist_tile(kc):
        return jnp.abs(rel + (t0 - kc * K_TILE).astype(F32))

    def sel_plain():
        def body(kc, _):
            dist_ref[kc] = jnp.where(score_ref[kc] >= lo, dist_tile(kc), DIST_BIG)
            return 0
        lax.fori_loop(0, nkc, body, 0)

    def sel_ties():
        def body(kc, run):
            sc = score_ref[kc]
            eq = sc == thr
            eq_f = jnp.where(eq, 1.0, 0.0)
            pre = run + _dot(lstrict_ref[...], eq_f.astype(BF16))
            keep = (sc > thr) | (eq & (pre < need))
            dist_ref[kc] = jnp.where(keep, dist_tile(kc), DIST_BIG)
            return run + jnp.sum(_colsum8(eq_f), axis=0, keepdims=True)
        lax.fori_loop(0, nkc, body, jnp.zeros((1, Q_TILE), F32))

    lax.cond(any_true(cnt_lo != kp), sel_ties, sel_plain)

    m_ref[...] = jnp.full(m_ref.shape, -jnp.inf, F32)
    acc_ref[...] = jnp.zeros(acc_ref.shape, F32)

    def att(kc, _):
        dist = dist_ref[kc]
        m_new = []
        for h in range(A_HEADS):
            slope = 2.0 ** (-8.0 * (h + 1) / A_HEADS) * LOG2E
            ck = ckr_ref[0, pl.ds(pl.multiple_of(kc * K_TILE, K_TILE), K_TILE), h * KV_LATENT:(h + 1) * KV_LATENT]
            logit = _dot_nt(ck, qabs_ref[0, :, h * KV_LATENT:(h + 1) * KV_LATENT]) - slope * dist
            logit_ref[h] = logit
            m_new.append(jnp.maximum(m_ref[h], jnp.max(_colmax8(logit), axis=0, keepdims=True)))
        cv = cvt_ref[0, kc]
        for h in range(A_HEADS):
            p = jnp.exp2(logit_ref[h] - m_new[h])
            acc_ref[h] = acc_ref[h] * jnp.exp2(m_ref[h] - m_new[h]) + _dot(cv, p.astype(BF16))
            m_ref[h] = m_new[h]
        return 0

    lax.fori_loop(0, nkc, att, 0)

    for pair in range(A_HEADS // 2):
        o_pair = []
        for hh in range(2):
            a = acc_ref[2 * pair + hh]
            o_t = a[:KV_LATENT] * (1.0 / a[KV_LATENT:KV_LATENT + 1])
            o_pair.append(o_t.T.astype(BF16))
        o_lat = jnp.concatenate(o_pair, axis=1)
        o_ref[0, :, pair * LANES:(pair + 1) * LANES] = _dot(o_lat, wuv_ref[pair]).astype(o_ref.dtype)


def _dsa_call(topk, qabs, qidx, widx, ckr, cvt, kidx, wuv_pair, lstrict):
    bsz, s, _ = qabs.shape
    nq = s // Q_TILE
    nk = s // K_TILE
    qt = lambda w: pl.BlockSpec((1, Q_TILE, w), lambda b, i: (b, i, 0))
    return pl.pallas_call(
        functools.partial(_dsa_kernel, topk),
        grid=(bsz, nq),
        in_specs=[qt(A_HEADS * KV_LATENT),
                  pl.BlockSpec((1, IDX_HEADS, Q_TILE, IDX_DIM), lambda b, i: (b, 0, i, 0)),
                  pl.BlockSpec((1, 1, IDX_HEADS, Q_TILE), lambda b, i: (b, i, 0, 0)),
                  pl.BlockSpec((1, s, A_HEADS * KV_LATENT), lambda b, i: (b, 0, 0)),
                  pl.BlockSpec((1, nk, KV_LATENT + ONES_ROWS, K_TILE), lambda b, i: (b, 0, 0, 0)),
                  pl.BlockSpec((1, s, IDX_DIM), lambda b, i: (b, 0, 0)),
                  pl.BlockSpec((A_HEADS // 2, 2 * KV_LATENT, LANES), lambda b, i: (0, 0, 0)),
                  pl.BlockSpec((K_TILE, K_TILE), lambda b, i: (0, 0))],
        out_specs=qt(A_WIDTH),
        out_shape=jax.ShapeDtypeStruct((bsz, s, A_WIDTH), BF16),
        scratch_shapes=[pltpu.VMEM((nk, K_TILE, Q_TILE), F32),
                        pltpu.VMEM((nk, K_TILE, Q_TILE), F32),
                        pltpu.VMEM((A_HEADS, K_TILE, Q_TILE), F32),
                        pltpu.VMEM((A_HEADS, 1, Q_TILE), F32),
                        pltpu.VMEM((A_HEADS, KV_LATENT + ONES_ROWS, Q_TILE), F32)],
        compiler_params=_params(("arbitrary", "arbitrary")),
    )(qabs, qidx, widx, ckr, cvt, kidx, wuv_pair, lstrict)


def _prep_b_kernel(pb_ref, w0_ref, w2_ref, a0_ref, a2_ref, g2_ref, kk_ref, ka_ref, rk_ref, eb_ref, tri_ref,
                   rt_ref, kt_ref, bt_ref, kl_ref, v_ref, g_ref, bv_ref, pc_ref):
    pb = pb_ref[0]
    r = pb[:, :B_WIDTH]
    k = pb[:, B_WIDTH:2 * B_WIDTH]
    v = pb[:, 2 * B_WIDTH:3 * B_WIDTH]
    o = 3 * B_WIDTH
    xw = pb[:, o:o + W_LORA]
    xa = pb[:, o + W_LORA:o + W_LORA + A_LORA]
    xg = pb[:, o + W_LORA + A_LORA:o + W_LORA + A_LORA + G_LORA]

    z = w0_ref[...] + _dot(jnp.tanh(xw).astype(BF16), w2_ref[...])
    nz = -z
    softplus = jnp.maximum(nz, 0.0) + jnp.log(1.0 + jnp.exp(-jnp.abs(nz)))
    lw = -jnp.exp(-softplus - 0.5)
    a = jax.nn.sigmoid(a0_ref[...] + _dot(xa.astype(BF16), a2_ref[...]))
    g = _dot(jax.nn.sigmoid(xg).astype(BF16), g2_ref[...])
    kk = k * kk_ref[...]
    kkn = kk / jnp.maximum(jnp.sqrt(_seg_dot_hl(kk * kk, eb_ref[...])), 1e-12)
    kp = k * (1.0 + (a - 1.0) * ka_ref[...])
    bonus = _seg_dot_hl(r * kp * rk_ref[...], eb_ref[...])

    lw_hi, lw_lo = _split(lw)
    cum = _dot(tri_ref[...], lw_hi) + _dot(tri_ref[...], lw_lo)
    e_pos = jnp.exp(cum)
    e_neg = jnp.exp(-cum)
    rt_ref[0] = (r * e_pos).astype(BF16)
    kt_ref[0] = (kkn * jnp.exp(cum - lw)).astype(BF16)
    bt_ref[0] = (kkn * a * e_neg).astype(BF16)
    kl_ref[0] = (kp * e_neg).astype(BF16)
    v_ref[0] = v.astype(BF16)
    g_ref[0] = g
    bv_ref[0] = bonus * v
    for c in range(pb.shape[0] // CHUNK):
        pc_ref[0, 0, c:c + 1, :] = e_pos[(c + 1) * CHUNK - 1:(c + 1) * CHUNK, :]


def _prep_b_call(pb, w0, w2, a0, a2, g2, k_k, k_a, r_k, eb, tri):
    bsz, s, nb = pb.shape
    tm = TOK_TILE
    full = lambda shape: pl.BlockSpec(shape, lambda b, j: (0,) * len(shape))
    tok = lambda w: pl.BlockSpec((1, tm, w), lambda b, j: (b, j, 0))
    row = full((1, B_WIDTH))
    bf = jax.ShapeDtypeStruct((bsz, s, B_WIDTH), BF16)
    ff = jax.ShapeDtypeStruct((bsz, s, B_WIDTH), F32)
    return pl.pallas_call(
        _prep_b_kernel,
        grid=(bsz, s // tm),
        in_specs=[tok(nb), row, full((W_LORA, B_WIDTH)), row, full((A_LORA, B_WIDTH)),
                  full((G_LORA, B_WIDTH)), row, row, row, full((SEG_K, SEG_K)), full((tm, tm))],
        out_specs=[tok(B_WIDTH)] * 7 + [pl.BlockSpec((1, 1, tm // CHUNK, B_WIDTH), lambda b, j: (b, j, 0, 0))],
        out_shape=[bf, bf, bf, bf, bf, ff, ff,
                   jax.ShapeDtypeStruct((bsz, s // tm, tm // CHUNK, B_WIDTH), F32)],
        compiler_params=_params(("arbitrary", "arbitrary")),
    )(pb, w0, w2, a0, a2, g2, k_k, k_a, r_k, eb, tri)


def _rwkv_kernel(rt_ref, kt_ref, bt_ref, kl_ref, v_ref, g_ref, bv_ref, pc_ref, lnw_ref, lnb_ref, eb_ref,
                 o_ref, h_ref):
    j = pl.program_id(1)

    @pl.when(j == 0)
    def _():
        h_ref[...] = jnp.zeros_like(h_ref)

    tm = rt_ref.shape[1]
    nch = tm // CHUNK
    ri = lax.broadcasted_iota(jnp.int32, (tm, tm), 0)
    ci = lax.broadcasted_iota(jnp.int32, (tm, tm), 1)
    same = (ri // CHUNK) == (ci // CHUNK)
    strict = same & (ri > ci)
    incl = same & (ri >= ci)
    eye_t = jnp.where(ri == ci, 1.0, 0.0)
    r2 = lax.broadcasted_iota(jnp.int32, (LANES, LANES), 0)
    c2 = lax.broadcasted_iota(jnp.int32, (LANES, LANES), 1)
    blk = (r2 // B_HEAD_DIM) == (c2 // B_HEAD_DIM)
    diag = r2 == c2
    lane = lax.broadcasted_iota(jnp.int32, (1, LANES), 1)
    zero_b = jnp.zeros((), BF16)

    npair = B_HEADS // 2
    heads = [(p, hh) for p in range(npair) for hh in range(2)]
    head0 = (lane // B_HEAD_DIM) == 0
    rt, kt, bt, kl, v = [], [], [], [], []
    a_ab, a_ak, m_rb, m_rk = [], [], [], []
    for p in range(npair):
        sl = slice(p * LANES, (p + 1) * LANES)
        rt.append(rt_ref[0, :, sl])
        kt.append(kt_ref[0, :, sl])
        bt.append(bt_ref[0, :, sl])
        kl.append(kl_ref[0, :, sl])
        v.append(v_ref[0, :, sl])
        lhs = jnp.concatenate([jnp.where(head0, kt[p], zero_b), jnp.where(head0, zero_b, kt[p]),
                               jnp.where(head0, rt[p], zero_b), jnp.where(head0, zero_b, rt[p])], axis=0)
        prod = _dot_nt(lhs, jnp.concatenate([bt[p], kl[p]], axis=0))
        for hh in range(2):
            a_ab.append(jnp.where(strict, prod[hh * tm:(hh + 1) * tm, :tm], 0.0))
            a_ak.append(jnp.where(strict, prod[hh * tm:(hh + 1) * tm, tm:], 0.0).astype(BF16))
            m_rb.append(jnp.where(incl, prod[(2 + hh) * tm:(3 + hh) * tm, :tm], 0.0).astype(BF16))
            m_rk.append(jnp.where(incl, prod[(2 + hh) * tm:(3 + hh) * tm, tm:], 0.0).astype(BF16))

    t_inv = [(eye_t - a).astype(BF16) for a in a_ab]
    a_pow = [a.astype(BF16) for a in a_ab]
    for _ in range(5):
        a_sq = [_dot(a, a) for a in a_pow]
        a_pow = [a.astype(BF16) for a in a_sq]
        t_inv = [_dot(t, (eye_t + a).astype(BF16)).astype(BF16) for t, a in zip(t_inv, a_sq)]

    avm = [_dot(jnp.concatenate([a_ak[i], m_rk[i]], axis=0), v[p]) for i, (p, _) in enumerate(heads)]
    x = [_dot(t_inv[i], jnp.concatenate([kt[p], avm[i][:tm].astype(BF16)], axis=1))
         for i, (p, _) in enumerate(heads)]
    y = [_dot(m_rb[i], x[i].astype(BF16)) for i in range(len(heads))]

    wu_b, q_b, ol = [], [], []
    for p in range(npair):
        i0, i1 = 2 * p, 2 * p + 1
        head0_2 = jnp.concatenate([head0, head0], axis=1)
        wu_b.append((-jnp.where(head0_2, x[i0], x[i1])).astype(BF16))
        yy = jnp.where(head0_2, y[i0], y[i1])
        q_b.append((rt[p].astype(F32) - yy[:, :LANES]).astype(BF16))
        ol.append(jnp.where(head0, avm[i0][tm:], avm[i1][tm:]) - yy[:, LANES:])

    g_mat, f_mat = [], []
    zeros_b = jnp.zeros((CHUNK, LANES), BF16)
    for p in range(npair):
        sl = slice(p * LANES, (p + 1) * LANES)
        gp, fp = [], []
        for c in range(nch):
            rows = slice(c * CHUNK, (c + 1) * CHUNK)
            pc = pc_ref[0, 0, c:c + 1, sl]
            bh = (bt[p][rows].astype(F32) * pc).astype(BF16)
            kh = (kl[p][rows].astype(F32) * pc).astype(BF16)
            rhs = jnp.concatenate([wu_b[p][rows], jnp.concatenate([zeros_b, v[p][rows]], axis=1)], axis=0)
            bw = _dot_tn(jnp.concatenate([bh, kh], axis=0), rhs)
            gp.append((jnp.where(diag, pc, 0.0) + jnp.where(blk, bw[:, :LANES], 0.0)).astype(BF16))
            fp.append(jnp.where(blk, bw[:, LANES:], 0.0))
        g_mat.append(gp)
        f_mat.append(fp)

    h = [h_ref[p] for p in range(npair)]
    o_chunks = [[] for _ in range(npair)]
    for c in range(nch):
        rows = slice(c * CHUNK, (c + 1) * CHUNK)
        for p in range(npair):
            h_b = h[p].astype(BF16)
            o_chunks[p].append(_dot(q_b[p][rows], h_b) + ol[p][rows])
            h[p] = _dot(g_mat[p][c], h_b) + f_mat[p][c]
    for p in range(npair):
        h_ref[p] = h[p]
    out = jnp.concatenate([jnp.concatenate(oc, axis=0) for oc in o_chunks], axis=1)

    eb = eb_ref[...]
    mean = _seg_dot_hl(out, eb) * (1.0 / B_HEAD_DIM)
    d = out - mean
    var = _seg_dot_hl(d * d, eb) * (1.0 / B_HEAD_DIM)
    y = d * lax.rsqrt(var + GN_EPS) * lnw_ref[...] + lnb_ref[...] + bv_ref[0]
    o_ref[0] = (y * g_ref[0]).astype(o_ref.dtype)


def _rwkv_call(rt, kt, bt, kl, v, g, bv, pc, ln_w, ln_b, eb):
    bsz, s, _ = rt.shape
    tm = TOK_TILE
    tok = pl.BlockSpec((1, tm, B_WIDTH), lambda b, j: (b, j, 0))
    row = pl.BlockSpec((1, B_WIDTH), lambda b, j: (0, 0))
    return pl.pallas_call(
        _rwkv_kernel,
        grid=(bsz, s // tm),
        in_specs=[tok] * 7 + [pl.BlockSpec((1, 1, tm // CHUNK, B_WIDTH), lambda b, j: (b, j, 0, 0)),
                              row, row, pl.BlockSpec((SEG_K, SEG_K), lambda b, j: (0, 0))],
        out_specs=tok,
        out_shape=jax.ShapeDtypeStruct((bsz, s, B_WIDTH), BF16),
        scratch_shapes=[pltpu.VMEM((B_HEADS // 2, LANES, LANES), F32)],
        compiler_params=_params(("arbitrary", "arbitrary")),
    )(rt, kt, bt, kl, v, g, bv, pc, ln_w, ln_b, eb)


def _ffn_kernel(x_ref, oa_ref, ob_ref, gt1_ref, sh2_ref, sc2_ref, gt2_ref, gf_ref, woa_ref, wob_ref,
                w1_ref, w2_ref, o_ref, x1_ref, h2_ref, acc_ref):
    f = pl.program_id(2)

    @pl.when(f == 0)
    def _():
        mix = _dot(oa_ref[0], woa_ref[...]) + _dot(ob_ref[0], wob_ref[...])
        x1 = x_ref[0] + gt1_ref[0, 0] * mix
        x1_ref[...] = x1
        y = x1 * lax.rsqrt(jnp.mean(x1 * x1, axis=-1, keepdims=True) + RMS_EPS) * gf_ref[...]
        h2_ref[...] = (y * (1.0 + sc2_ref[0, 0]) + sh2_ref[0, 0]).astype(BF16)
        acc_ref[...] = jnp.zeros_like(acc_ref)

    u = jnp.maximum(_dot(h2_ref[...], w1_ref[...]), 0.0)
    acc_ref[...] += _dot((u * u).astype(BF16), w2_ref[...])

    @pl.when(f == pl.num_programs(2) - 1)
    def _():
        o_ref[0] = x1_ref[...] + gt2_ref[0, 0] * acc_ref[...]


def _ffn_call(x, oa, ob, mod4, g_ffn, w_out_a, w_out_b, w1, w2):
    bsz, s, d = x.shape
    dff = w1.shape[1]
    tm = 512 if s % 512 == 0 else TOK_TILE
    tf = 1024
    tok = lambda w: pl.BlockSpec((1, tm, w), lambda b, j, f: (b, j, 0))
    modk = lambda k: pl.BlockSpec((1, 1, 1, d), lambda b, j, f, k=k: (b, k, 0, 0))
    return pl.pallas_call(
        _ffn_kernel,
        grid=(bsz, s // tm, dff // tf),
        in_specs=[tok(d), tok(A_WIDTH), tok(B_WIDTH), modk(2), modk(3), modk(4), modk(5),
                  pl.BlockSpec((1, d), lambda b, j, f: (0, 0)),
                  pl.BlockSpec((A_WIDTH, d), lambda b, j, f: (0, 0)),
                  pl.BlockSpec((B_WIDTH, d), lambda b, j, f: (0, 0)),
                  pl.BlockSpec((d, tf), lambda b, j, f: (0, f)),
                  pl.BlockSpec((tf, d), lambda b, j, f: (f, 0))],
        out_specs=tok(d),
        out_shape=jax.ShapeDtypeStruct((bsz, s, d), F32),
        scratch_shapes=[pltpu.VMEM((tm, d), F32), pltpu.VMEM((tm, d), BF16), pltpu.VMEM((tm, d), F32)],
        compiler_params=_params(("arbitrary", "arbitrary", "arbitrary")),
    )(x, oa, ob, mod4, mod4, mod4, mod4, g_ffn, w_out_a, w_out_b, w1, w2)


def _block_ones(n, blk, dtype=BF16):
    i = jnp.arange(n)
    return ((i[:, None] // blk) == (i[None, :] // blk)).astype(dtype)


def kernel(x, c, w_ada, b_ada, g_mix, g_ffn, w_in, g_q, g_k, g_kv, w_uk, w_uv, mu_shift, w0, w2, a0, a2, g2,
           k_k, k_a, r_k, ln_w, ln_b, w_out, w_ff1, w_ff2):
    bsz, s, d = x.shape
    depth = w_ada.shape[0]
    assert s % Q_TILE == 0 and s % TOK_TILE == 0
    topk = min(TOPK_MAX, s // 4)

    eb = _block_ones(SEG_K, B_HEAD_DIM)
    ex = (jnp.arange(2 * A_HEAD_DIM)[:, None] // A_HEAD_DIM == jnp.arange(2 * KV_LATENT)[None, :] // KV_LATENT
          ).astype(BF16)
    sel = (jnp.arange(LANES)[None, :] == IDX_DIM + jnp.arange(IDX_HEADS)[:, None]).astype(BF16)
    eye_l = jnp.eye(KV_LATENT, dtype=BF16)
    ti = jnp.arange(TOK_TILE)
    tri = (((ti[:, None] // CHUNK) == (ti[None, :] // CHUNK)) & (ti[:, None] >= ti[None, :])).astype(BF16)
    ki = jnp.arange(K_TILE)
    lstrict = (ki[None, :] < ki[:, None]).astype(BF16)

    for l in range(depth):
        w_a = jnp.pad(w_in[l][:, :N_IN_A], ((0, 0), (0, N_A_PAD - N_IN_A)))
        w_in_p = jnp.concatenate([w_a, w_in[l][:, N_IN_A:]], axis=1).astype(BF16)
        wuk_flat = w_uk[l].reshape(KV_LATENT, A_WIDTH).astype(BF16)
        wuk_t = jnp.transpose(w_uk[l], (1, 2, 0)).reshape(A_HEADS // 2, 2, A_HEAD_DIM, KV_LATENT)
        wuk_bd = (jnp.eye(2, dtype=F32)[None, :, None, :, None] * wuk_t[:, :, :, None, :]).reshape(
            A_HEADS // 2, 2 * A_HEAD_DIM, 2 * KV_LATENT).astype(BF16)
        wuv_t = jnp.transpose(w_uv[l], (1, 0, 2)).reshape(A_HEADS // 2, 2, KV_LATENT, A_HEAD_DIM)
        wuv_pair = (jnp.eye(2, dtype=F32)[None, :, None, :, None] * wuv_t[:, :, :, None, :]).reshape(
            A_HEADS // 2, 2 * KV_LATENT, 2 * A_HEAD_DIM).astype(BF16)
        gqk = jnp.tile(g_q[l] * g_k[l], A_HEADS).reshape(1, A_WIDTH)
        r1 = lambda t: t.reshape(1, -1)

        mod = _mod_call(c, w_ada[l], b_ada[l])
        mod4 = mod.reshape(bsz, 6, 1, d)
        pa, pb = _proj_call(x, mod4, r1(g_mix[l]), w_in_p, r1(mu_shift[l]))
        ckr, cvt, qabs, qidx, kidx, widx = _prep_a_call(pa, r1(g_kv[l]), gqk, wuk_flat, wuk_bd, eb, ex, sel, eye_l)
        o_a = _dsa_call(topk, qabs, qidx, widx, ckr, cvt, kidx, wuv_pair, lstrict)
        rt, kt, bt, kl, v, g, bv, pc = _prep_b_call(
            pb, r1(w0[l]), w2[l].astype(BF16), r1(a0[l]), a2[l].astype(BF16), g2[l].astype(BF16),
            r1(k_k[l]), r1(k_a[l]), r1(r_k[l]), eb, tri)
        o_b = _rwkv_call(rt, kt, bt, kl, v, g, bv, pc, r1(ln_w[l]), r1(ln_b[l]), eb)
        x = _ffn_call(x, o_a, o_b, mod4, r1(g_ffn[l]), w_out[l][:A_WIDTH].astype(BF16),
                      w_out[l][A_WIDTH:].astype(BF16), w_ff1[l].astype(BF16), w_ff2[l].astype(BF16))
    return x
```

```python
import functools

import jax
import jax.numpy as jnp
from jax import lax
from jax.experimental import pallas as pl
from jax.experimental.pallas import tpu as pltpu

F32 = jnp.float32
BF16 = jnp.bfloat16

CHUNK = 64
A_HEADS = 8
A_HEAD_DIM = 64
A_WIDTH = A_HEADS * A_HEAD_DIM
KV_LATENT = 128
IDX_HEADS = 8
IDX_DIM = 64
TOPK_MAX = 256
B_HEADS = 8
B_HEAD_DIM = 64
B_WIDTH = B_HEADS * B_HEAD_DIM
W_LORA = 64
A_LORA = 64
G_LORA = 128
RMS_EPS = 1e-6
GN_EPS = 64e-5
N_IN_A = A_WIDTH + KV_LATENT + IDX_HEADS * IDX_DIM + IDX_DIM + IDX_HEADS
N_IN_B = 3 * B_WIDTH + W_LORA + A_LORA + G_LORA
N_A_PAD = 1280

LANES = 128
SEG_K = 256
TOK_TILE = 256
Q_TILE = 256
FFN_TILE = 512
FFN_ROWS = 256
K_TILE = 256
DIST_BIG = 1e30
ONES_ROWS = 16
LOG2E = 1.4426950408889634
SEARCH_PROBES = 20
SEARCH_TRIPS_MAX = 64
VMEM_LIMIT = 56 * 1024 * 1024


def _dot(a, b):
    return jnp.dot(a, b, preferred_element_type=F32)


def _dot_nt(a, b):
    return lax.dot_general(a, b, (((1,), (1,)), ((), ())), preferred_element_type=F32)


def _dot_tn(a, b):
    return lax.dot_general(a, b, (((0,), (0,)), ((), ())), preferred_element_type=F32)


def _split(x):
    hi = x.astype(BF16)
    lo = (x - hi.astype(F32)).astype(BF16)
    return hi, lo


def _dot_hl(x, e):
    hi, lo = _split(x)
    return _dot(hi, e) + _dot(lo, e)


def _seg_dot_hl(x, e):
    k = e.shape[0]
    return jnp.concatenate([_dot_hl(x[:, j:j + k], e) for j in range(0, x.shape[1], k)], axis=1)


def _params(sem):
    return pltpu.CompilerParams(dimension_semantics=sem, vmem_limit_bytes=VMEM_LIMIT)


def _mod_kernel(c_ref, w_ref, b_ref, o_ref):
    c = c_ref[...]
    s = c * jax.nn.sigmoid(c)
    s_hi, s_lo = _split(s)
    w_hi, w_lo = _split(w_ref[...])
    o_ref[...] = _dot(s_hi, w_hi) + _dot(s_hi, w_lo) + _dot(s_lo, w_hi) + b_ref[...]


def _mod_call(c, w_ada, b_ada):
    bsz, d = c.shape
    n = w_ada.shape[1]
    tn = 1024
    return pl.pallas_call(
        _mod_kernel,
        grid=(n // tn,),
        in_specs=[pl.BlockSpec((bsz, d), lambda j: (0, 0)),
                  pl.BlockSpec((d, tn), lambda j: (0, j)),
                  pl.BlockSpec((1, tn), lambda j: (0, j))],
        out_specs=pl.BlockSpec((bsz, tn), lambda j: (0, j)),
        out_shape=jax.ShapeDtypeStruct((bsz, n), F32),
        compiler_params=_params(("arbitrary",)),
    )(c, w_ada, b_ada.reshape(1, n))


def _proj_kernel(x_ref, sh_ref, sc_ref, g_ref, w_ref, mu_ref, pa_ref, pb_ref, carry_ref):
    j = pl.program_id(1)

    @pl.when(j == 0)
    def _():
        carry_ref[...] = jnp.zeros_like(carry_ref)

    x = x_ref[0]
    y = x * lax.rsqrt(jnp.mean(x * x, axis=-1, keepdims=True) + RMS_EPS) * g_ref[...]
    h = y * (1.0 + sc_ref[0, 0]) + sh_ref[0, 0]
    p = _dot(h.astype(BF16), w_ref[...])
    pa_ref[0] = p[:, :N_A_PAD]
    pb = p[:, N_A_PAD:]
    tm = pb.shape[0]
    row = lax.broadcasted_iota(jnp.int32, (tm, 1), 0)
    prev = jnp.where(row == 0, carry_ref[...], pltpu.roll(pb, 1, axis=0))
    carry_ref[...] = pb[tm - 1:tm, :]
    pb_ref[0] = pb + mu_ref[...] * (prev - pb)


def _proj_call(x, mod4, g_mix, w_in_p, mu):
    bsz, s, d = x.shape
    n = w_in_p.shape[1]
    nb = n - N_A_PAD
    tm = TOK_TILE
    return pl.pallas_call(
        _proj_kernel,
        grid=(bsz, s // tm),
        in_specs=[pl.BlockSpec((1, tm, d), lambda b, j: (b, j, 0)),
                  pl.BlockSpec((1, 1, 1, d), lambda b, j: (b, 0, 0, 0)),
                  pl.BlockSpec((1, 1, 1, d), lambda b, j: (b, 1, 0, 0)),
                  pl.BlockSpec((1, d), lambda b, j: (0, 0)),
                  pl.BlockSpec((d, n), lambda b, j: (0, 0)),
                  pl.BlockSpec((1, nb), lambda b, j: (0, 0))],
        out_specs=[pl.BlockSpec((1, tm, N_A_PAD), lambda b, j: (b, j, 0)),
                   pl.BlockSpec((1, tm, nb), lambda b, j: (b, j, 0))],
        out_shape=[jax.ShapeDtypeStruct((bsz, s, N_A_PAD), F32),
                   jax.ShapeDtypeStruct((bsz, s, nb), F32)],
        scratch_shapes=[pltpu.VMEM((1, nb), F32)],
        compiler_params=_params(("arbitrary", "arbitrary")),
    )(x, mod4, mod4, g_mix, w_in_p, mu)


def _prep_a_kernel(pa_ref, gkv_ref, gqk_ref, wuk_ref, wukbd_ref, eb_ref, ex_ref, sel_ref, eye_ref,
                   ckr_ref, cvt_ref, qabs_ref, qidx_ref, kidx_ref, widx_ref):
    pa = pa_ref[0]
    tm = pa.shape[0]
    q = pa[:, :A_WIDTH]
    cl = pa[:, A_WIDTH:A_WIDTH + KV_LATENT]
    o_qi = A_WIDTH + KV_LATENT
    qi = pa[:, o_qi:o_qi + IDX_HEADS * IDX_DIM]
    o_kw = o_qi + IDX_HEADS * IDX_DIM
    kw = pa[:, o_kw:o_kw + LANES]

    ckv = cl * lax.rsqrt(jnp.mean(cl * cl, axis=-1, keepdims=True) + RMS_EPS) * gkv_ref[...]
    ckv_b = ckv.astype(BF16)
    cvt_ref[0, 0, :KV_LATENT, :] = _dot_nt(eye_ref[...], ckv_b).astype(BF16)
    cvt_ref[0, 0, KV_LATENT:, :] = jnp.ones((ONES_ROWS, tm), BF16)
    kf = _dot(ckv_b, wuk_ref[...])
    ss = _seg_dot_hl(kf * kf, ex_ref[...])
    inv_rms = lax.rsqrt(ss * (1.0 / A_HEAD_DIM) + RMS_EPS)
    ckr_ref[0] = (jnp.concatenate([ckv] * A_HEADS, axis=1) * inv_rms).astype(BF16)

    ssq = _seg_dot_hl(q * q, eb_ref[...])
    qh = q * lax.rsqrt(ssq * (1.0 / A_HEAD_DIM) + RMS_EPS) * gqk_ref[...]
    qh_b = qh.astype(BF16)
    for j in range(A_HEADS // 2):
        qabs = _dot(qh_b[:, j * LANES:(j + 1) * LANES], wukbd_ref[j]) * (A_HEAD_DIM ** -0.5 * LOG2E)
        qabs_ref[0, :, 2 * j * KV_LATENT:2 * (j + 1) * KV_LATENT] = qabs.astype(BF16)
    for h in range(IDX_HEADS):
        qidx_ref[0, h] = qi[:, h * IDX_DIM:(h + 1) * IDX_DIM].astype(BF16)
    kidx_ref[0] = kw[:, :IDX_DIM].astype(BF16)
    kw_hi, kw_lo = _split(kw)
    w_t = _dot_nt(sel_ref[...], kw_hi) + _dot_nt(sel_ref[...], kw_lo)
    widx_ref[0, 0] = w_t * (IDX_HEADS ** -0.5 * IDX_DIM ** -0.5)


def _prep_a_call(pa, gkv, gqk, wuk_flat, wuk_bd, eb, ex, sel, eye):
    bsz, s, _ = pa.shape
    tm = TOK_TILE
    full = lambda shape: pl.BlockSpec(shape, lambda b, j: (0,) * len(shape))
    tok = lambda w: pl.BlockSpec((1, tm, w), lambda b, j: (b, j, 0))
    return pl.pallas_call(
        _prep_a_kernel,
        grid=(bsz, s // tm),
        in_specs=[tok(N_A_PAD), full((1, KV_LATENT)), full((1, A_WIDTH)),
                  full((KV_LATENT, A_WIDTH)), full((A_HEADS // 2, 2 * A_HEAD_DIM, 2 * KV_LATENT)),
                  full((SEG_K, SEG_K)), full((2 * A_HEAD_DIM, 2 * KV_LATENT)),
                  full((IDX_HEADS, LANES)), full((KV_LATENT, KV_LATENT))],
        out_specs=[tok(A_HEADS * KV_LATENT),
                   pl.BlockSpec((1, 1, KV_LATENT + ONES_ROWS, tm), lambda b, j: (b, j, 0, 0)),
                   tok(A_HEADS * KV_LATENT),
                   pl.BlockSpec((1, IDX_HEADS, tm, IDX_DIM), lambda b, j: (b, 0, j, 0)),
                   tok(IDX_DIM),
                   pl.BlockSpec((1, 1, IDX_HEADS, tm), lambda b, j: (b, j, 0, 0))],
        out_shape=[jax.ShapeDtypeStruct((bsz, s, A_HEADS * KV_LATENT), BF16),
                   jax.ShapeDtypeStruct((bsz, s // tm, KV_LATENT + ONES_ROWS, tm), BF16),
                   jax.ShapeDtypeStruct((bsz, s, A_HEADS * KV_LATENT), BF16),
                   jax.ShapeDtypeStruct((bsz, IDX_HEADS, s, IDX_DIM), BF16),
                   jax.ShapeDtypeStruct((bsz, s, IDX_DIM), BF16),
                   jax.ShapeDtypeStruct((bsz, s // tm, IDX_HEADS, tm), F32)],
        compiler_params=_params(("arbitrary", "arbitrary")),
    )(pa, gkv, gqk, wuk_flat, wuk_bd, eb, ex, sel, eye)


def _colsum8(x):
    y = x.reshape(4, K_TILE // 32, 8, Q_TILE)
    return jnp.sum(jnp.sum(y, axis=1), axis=0)


def _colmin8(x):
    y = x.reshape(4, K_TILE // 32, 8, Q_TILE)
    return jnp.min(jnp.min(y, axis=1), axis=0)


def _colmax8(x):
    y = x.reshape(4, K_TILE // 32, 8, Q_TILE)
    return jnp.max(jnp.max(y, axis=1), axis=0)


def _dsa_kernel(topk, qabs_ref, qidx_ref, widx_ref, ckr_ref, cvt_ref, kidx_ref, wuv_ref, lstrict_ref,
                o_ref, score_ref, dist_ref, logit_ref, m_ref, acc_ref):
    i = pl.program_id(1)
    nkc = i + 1
    t0 = i * Q_TILE
    krow = lax.broadcasted_iota(jnp.int32, (K_TILE, 1), 0)
    qcol = lax.broadcasted_iota(jnp.int32, (1, Q_TILE), 1)
    limit = ((t0 + qcol) // CHUNK + 1) * CHUNK
    kp = jnp.minimum(limit, topk).astype(F32)
    rel = (qcol - krow).astype(F32)

    def p1(kc, carry):
        rmin, rmax = carry
        k = kidx_ref[0, pl.ds(pl.multiple_of(kc * K_TILE, K_TILE), K_TILE), :]
        acc = jnp.zeros((K_TILE, Q_TILE), F32)
        for h in range(IDX_HEADS):
            s = _dot_nt(k, qidx_ref[0, h])
            acc = acc + widx_ref[0, 0, h:h + 1, :] * jnp.maximum(s, 0.0)
        adm = (kc * K_TILE + krow) < limit
        score_ref[kc] = jnp.where(adm, acc, -jnp.inf)
        rmin = jnp.minimum(rmin, _colmin8(jnp.where(adm, acc, jnp.inf)))
        rmax = jnp.maximum(rmax, _colmax8(jnp.where(adm, acc, -jnp.inf)))
        return rmin, rmax

    rmin, rmax = lax.fori_loop(
        0, nkc, p1, (jnp.full((8, Q_TILE), jnp.inf, F32), jnp.full((8, Q_TILE), -jnp.inf, F32)))
    lo = jnp.min(rmin, axis=0, keepdims=True)
    hi = jnp.max(rmax, axis=0, keepdims=True)

    def count(pred):
        def body(kc, acc):
            return acc + _colsum8(jnp.where(pred(score_ref[kc]), 1.0, 0.0))
        return jnp.sum(lax.fori_loop(0, nkc, body, jnp.zeros((8, Q_TILE), F32)), axis=0, keepdims=True)

    def probe(c):
        lo, hi, cnt_lo = c
        mid = lo + 0.5 * (hi - lo)
        cnt = count(lambda sc: sc >= mid)
        ge = cnt >= kp
        return jnp.where(ge, mid, lo), jnp.where(ge, hi, mid), jnp.where(ge, cnt, cnt_lo)

    def thr_of(lo):
        def thr_body(kc, acc):
            sc = score_ref[kc]
            return jnp.minimum(acc, _colmin8(jnp.where(sc >= lo, sc, jnp.inf)))
        thr = jnp.min(lax.fori_loop(0, nkc, thr_body, jnp.full((8, Q_TILE), jnp.inf, F32)), axis=0, keepdims=True)
        return thr, count(lambda sc: sc > thr)

    def any_true(x):
        return jnp.max(jnp.where(x, 1.0, 0.0)) > 0.0

    bracket = lax.fori_loop(0, SEARCH_PROBES, lambda _, c: probe(c), (lo, hi, limit.astype(F32)))
    search = lax.while_loop(
        lambda c: jnp.logical_and(c[0] < SEARCH_TRIPS_MAX, any_true(c[5] >= kp)),
        lambda c: (c[0] + 1,) + (lambda br: br + thr_of(br[0]))(probe(probe(c[1:4]))),
        (jnp.int32(0),) + bracket + thr_of(bracket[0]))
    lo, cnt_lo, thr, need = search[1], search[3], search[4], kp - search[5]

    def dist_tile(kc):
        return jnp.abs(rel + (t0 - kc * K_TILE).astype(F32))

    def sel_plain():
        def body(kc, _):
            dist_ref[kc] = jnp.where(score_ref[kc] >= lo, dist_tile(kc), DIST_BIG)
            return 0
        lax.fori_loop(0, nkc, body, 0)

    def sel_ties():
        def body(kc, run):
            sc = score_ref[kc]
            eq = sc == thr
            eq_f = jnp.where(eq, 1.0, 0.0)
            pre = run + _dot(lstrict_ref[...], eq_f.astype(BF16))
            keep = (sc > thr) | (eq & (pre < need))
            dist_ref[kc] = jnp.where(keep, dist_tile(kc), DIST_BIG)
            return run + jnp.sum(_colsum8(eq_f), axis=0, keepdims=True)
        lax.fori_loop(0, nkc, body, jnp.zeros((1, Q_TILE), F32))

    lax.cond(any_true(cnt_lo != kp), sel_ties, sel_plain)

    m_ref[...] = jnp.full(m_ref.shape, -jnp.inf, F32)
    acc_ref[...] = jnp.zeros(acc_ref.shape, F32)

    def att(kc, _):
        dist = dist_ref[kc]
        m_new = []
        for h in range(A_HEADS):
            slope = 2.0 ** (-8.0 * (h + 1) / A_HEADS) * LOG2E
            ck = ckr_ref[0, pl.ds(pl.multiple_of(kc * K_TILE, K_TILE), K_TILE), h * KV_LATENT:(h + 1) * KV_LATENT]
            logit = _dot_nt(ck, qabs_ref[0, :, h * KV_LATENT:(h + 1) * KV_LATENT]) - slope * dist
            logit_ref[h] = logit
            m_new.append(jnp.maximum(m_ref[h], jnp.max(_colmax8(logit), axis=0, keepdims=True)))
        cv = cvt_ref[0, kc]
        for h in range(A_HEADS):
            p = jnp.exp2(logit_ref[h] - m_new[h])
            acc_ref[h] = acc_ref[h] * jnp.exp2(m_ref[h] - m_new[h]) + _dot(cv, p.astype(BF16))
            m_ref[h] = m_new[h]
        return 0

    lax.fori_loop(0, nkc, att, 0)

    for pair in range(A_HEADS // 2):
        o_pair = []
        for hh in range(2):
            a = acc_ref[2 * pair + hh]
            o_t = a[:KV_LATENT] * (1.0 / a[KV_LATENT:KV_LATENT + 1])
            o_pair.append(o_t.T.astype(BF16))
        o_lat = jnp.concatenate(o_pair, axis=1)
        o_ref[0, :, pair * LANES:(pair + 1) * LANES] = _dot(o_lat, wuv_ref[pair]).astype(o_ref.dtype)


def _dsa_call(topk, qabs, qidx, widx, ckr, cvt, kidx, wuv_pair, lstrict):
    bsz, s, _ = qabs.shape
    nq = s // Q_TILE
    nk = s // K_TILE
    qt = lambda w: pl.BlockSpec((1, Q_TILE, w), lambda b, i: (b, i, 0))
    return pl.pallas_call(
        functools.partial(_dsa_kernel, topk),
        grid=(bsz, nq),
        in_specs=[qt(A_HEADS * KV_LATENT),
                  pl.BlockSpec((1, IDX_HEADS, Q_TILE, IDX_DIM), lambda b, i: (b, 0, i, 0)),
                  pl.BlockSpec((1, 1, IDX_HEADS, Q_TILE), lambda b, i: (b, i, 0, 0)),
                  pl.BlockSpec((1, s, A_HEADS * KV_LATENT), lambda b, i: (b, 0, 0)),
                  pl.BlockSpec((1, nk, KV_LATENT + ONES_ROWS, K_TILE), lambda b, i: (b, 0, 0, 0)),
                  pl.BlockSpec((1, s, IDX_DIM), lambda b, i: (b, 0, 0)),
                  pl.BlockSpec((A_HEADS // 2, 2 * KV_LATENT, LANES), lambda b, i: (0, 0, 0)),
                  pl.BlockSpec((K_TILE, K_TILE), lambda b, i: (0, 0))],
        out_specs=qt(A_WIDTH),
        out_shape=jax.ShapeDtypeStruct((bsz, s, A_WIDTH), BF16),
        scratch_shapes=[pltpu.VMEM((nk, K_TILE, Q_TILE), F32),
                        pltpu.VMEM((nk, K_TILE, Q_TILE), F32),
                        pltpu.VMEM((A_HEADS, K_TILE, Q_TILE), F32),
                        pltpu.VMEM((A_HEADS, 1, Q_TILE), F32),
                        pltpu.VMEM((A_HEADS, KV_LATENT + ONES_ROWS, Q_TILE), F32)],
        compiler_params=_params(("arbitrary", "arbitrary")),
    )(qabs, qidx, widx, ckr, cvt, kidx, wuv_pair, lstrict)


def _prep_b_kernel(pb_ref, w0_ref, w2_ref, a0_ref, a2_ref, g2_ref, kk_ref, ka_ref, rk_ref, eb_ref, tri_ref,
                   rt_ref, kt_ref, bt_ref, kl_ref, v_ref, g_ref, bv_ref, pc_ref):
    pb = pb_ref[0]
    r = pb[:, :B_WIDTH]
    k = pb[:, B_WIDTH:2 * B_WIDTH]
    v = pb[:, 2 * B_WIDTH:3 * B_WIDTH]
    o = 3 * B_WIDTH
    xw = pb[:, o:o + W_LORA]
    xa = pb[:, o + W_LORA:o + W_LORA + A_LORA]
    xg = pb[:, o + W_LORA + A_LORA:o + W_LORA + A_LORA + G_LORA]

    z = w0_ref[...] + _dot(jnp.tanh(xw).astype(BF16), w2_ref[...])
    nz = -z
    softplus = jnp.maximum(nz, 0.0) + jnp.log(1.0 + jnp.exp(-jnp.abs(nz)))
    lw = -jnp.exp(-softplus - 0.5)
    a = jax.nn.sigmoid(a0_ref[...] + _dot(xa.astype(BF16), a2_ref[...]))
    g = _dot(jax.nn.sigmoid(xg).astype(BF16), g2_ref[...])
    kk = k * kk_ref[...]
    kkn = kk / jnp.maximum(jnp.sqrt(_seg_dot_hl(kk * kk, eb_ref[...])), 1e-12)
    kp = k * (1.0 + (a - 1.0) * ka_ref[...])
    bonus = _seg_dot_hl(r * kp * rk_ref[...], eb_ref[...])

    lw_hi, lw_lo = _split(lw)
    cum = _dot(tri_ref[...], lw_hi) + _dot(tri_ref[...], lw_lo)
    e_pos = jnp.exp(cum)
    e_neg = jnp.exp(-cum)
    rt_ref[0] = (r * e_pos).astype(BF16)
    kt_ref[0] = (kkn * jnp.exp(cum - lw)).astype(BF16)
    bt_ref[0] = (kkn * a * e_neg).astype(BF16)
    kl_ref[0] = (kp * e_neg).astype(BF16)
    v_ref[0] = v.astype(BF16)
    g_ref[0] = g
    bv_ref[0] = bonus * v
    for c in range(pb.shape[0] // CHUNK):
        pc_ref[0, 0, c:c + 1, :] = e_pos[(c + 1) * CHUNK - 1:(c + 1) * CHUNK, :]


def _prep_b_call(pb, w0, w2, a0, a2, g2, k_k, k_a, r_k, eb, tri):
    bsz, s, nb = pb.shape
    tm = TOK_TILE
    full = lambda shape: pl.BlockSpec(shape, lambda b, j: (0,) * len(shape))
    tok = lambda w: pl.BlockSpec((1, tm, w), lambda b, j: (b, j, 0))
    row = full((1, B_WIDTH))
    bf = jax.ShapeDtypeStruct((bsz, s, B_WIDTH), BF16)
    ff = jax.ShapeDtypeStruct((bsz, s, B_WIDTH), F32)
    return pl.pallas_call(
        _prep_b_kernel,
        grid=(bsz, s // tm),
        in_specs=[tok(nb), row, full((W_LORA, B_WIDTH)), row, full((A_LORA, B_WIDTH)),
                  full((G_LORA, B_WIDTH)), row, row, row, full((SEG_K, SEG_K)), full((tm, tm))],
        out_specs=[tok(B_WIDTH)] * 7 + [pl.BlockSpec((1, 1, tm // CHUNK, B_WIDTH), lambda b, j: (b, j, 0, 0))],
        out_shape=[bf, bf, bf, bf, bf, ff, ff,
                   jax.ShapeDtypeStruct((bsz, s // tm, tm // CHUNK, B_WIDTH), F32)],
        compiler_params=_params(("arbitrary", "arbitrary")),
    )(pb, w0, w2, a0, a2, g2, k_k, k_a, r_k, eb, tri)


def _rwkv_kernel(rt_ref, kt_ref, bt_ref, kl_ref, v_ref, g_ref, bv_ref, pc_ref, lnw_ref, lnb_ref, eb_ref,
                 o_ref, h_ref):
    j = pl.program_id(1)

    @pl.when(j == 0)
    def _():
        h_ref[...] = jnp.zeros_like(h_ref)

    tm = rt_ref.shape[1]
    nch = tm // CHUNK
    ri = lax.broadcasted_iota(jnp.int32, (tm, tm), 0)
    ci = lax.broadcasted_iota(jnp.int32, (tm, tm), 1)
    same = (ri // CHUNK) == (ci // CHUNK)
    strict = same & (ri > ci)
    incl = same & (ri >= ci)
    eye_t = jnp.where(ri == ci, 1.0, 0.0)
    r2 = lax.broadcasted_iota(jnp.int32, (LANES, LANES), 0)
    c2 = lax.broadcasted_iota(jnp.int32, (LANES, LANES), 1)
    blk = (r2 // B_HEAD_DIM) == (c2 // B_HEAD_DIM)
    diag = r2 == c2
    lane = lax.broadcasted_iota(jnp.int32, (1, LANES), 1)
    zero_b = jnp.zeros((), BF16)

    npair = B_HEADS // 2
    heads = [(p, hh) for p in range(npair) for hh in range(2)]
    head0 = (lane // B_HEAD_DIM) == 0
    rt, kt, bt, kl, v = [], [], [], [], []
    a_ab, a_ak, m_rb, m_rk = [], [], [], []
    for p in range(npair):
        sl = slice(p * LANES, (p + 1) * LANES)
        rt.append(rt_ref[0, :, sl])
        kt.append(kt_ref[0, :, sl])
        bt.append(bt_ref[0, :, sl])
        kl.append(kl_ref[0, :, sl])
        v.append(v_ref[0, :, sl])
        lhs = jnp.concatenate([jnp.where(head0, kt[p], zero_b), jnp.where(head0, zero_b, kt[p]),
                               jnp.where(head0, rt[p], zero_b), jnp.where(head0, zero_b, rt[p])], axis=0)
        prod = _dot_nt(lhs, jnp.concatenate([bt[p], kl[p]], axis=0))
        for hh in range(2):
            a_ab.append(jnp.where(strict, prod[hh * tm:(hh + 1) * tm, :tm], 0.0))
            a_ak.append(jnp.where(strict, prod[hh * tm:(hh + 1) * tm, tm:], 0.0).astype(BF16))
            m_rb.append(jnp.where(incl, prod[(2 + hh) * tm:(3 + hh) * tm, :tm], 0.0).astype(BF16))
            m_rk.append(jnp.where(incl, prod[(2 + hh) * tm:(3 + hh) * tm, tm:], 0.0).astype(BF16))

    t_inv = [(eye_t - a).astype(BF16) for a in a_ab]
    a_pow = [a.astype(BF16) for a in a_ab]
    for _ in range(5):
        a_sq = [_dot(a, a) for a in a_pow]
        a_pow = [a.astype(BF16) for a in a_sq]
        t_inv = [_dot(t, (eye_t + a).astype(BF16)).astype(BF16) for t, a in zip(t_inv, a_sq)]

    avm = [_dot(jnp.concatenate([a_ak[i], m_rk[i]], axis=0), v[p]) for i, (p, _) in enumerate(heads)]
    x = [_dot(t_inv[i], jnp.concatenate([kt[p], avm[i][:tm].astype(BF16)], axis=1))
         for i, (p, _) in enumerate(heads)]
    y = [_dot(m_rb[i], x[i].astype(BF16)) for i in range(len(heads))]

    wu_b, q_b, ol = [], [], []
    for p in range(npair):
        i0, i1 = 2 * p, 2 * p + 1
        head0_2 = jnp.concatenate([head0, head0], axis=1)
        wu_b.append((-jnp.where(head0_2, x[i0], x[i1])).astype(BF16))
        yy = jnp.where(head0_2, y[i0], y[i1])
        q_b.append((rt[p].astype(F32) - yy[:, :LANES]).astype(BF16))
        ol.append(jnp.where(head0, avm[i0][tm:], avm[i1][tm:]) - yy[:, LANES:])

    g_mat, f_mat = [], []
    zeros_b = jnp.zeros((CHUNK, LANES), BF16)
    for p in range(npair):
        sl = slice(p * LANES, (p + 1) * LANES)
        gp, fp = [], []
        for c in range(nch):
            rows = slice(c * CHUNK, (c + 1) * CHUNK)
            pc = pc_ref[0, 0, c:c + 1, sl]
            bh = (bt[p][rows].astype(F32) * pc).astype(BF16)
            kh = (kl[p][rows].astype(F32) * pc).astype(BF16)
            rhs = jnp.concatenate([wu_b[p][rows], jnp.concatenate([zeros_b, v[p][rows]], axis=1)], axis=0)
            bw = _dot_tn(jnp.concatenate([bh, kh], axis=0), rhs)
            gp.append((jnp.where(diag, pc, 0.0) + jnp.where(blk, bw[:, :LANES], 0.0)).astype(BF16))
            fp.append(jnp.where(blk, bw[:, LANES:], 0.0))
        g_mat.append(gp)
        f_mat.append(fp)

    h = [h_ref[p] for p in range(npair)]
    o_chunks = [[] for _ in range(npair)]
    for c in range(nch):
        rows = slice(c * CHUNK, (c + 1) * CHUNK)
        for p in range(npair):
            h_b = h[p].astype(BF16)
            o_chunks[p].append(_dot(q_b[p][rows], h_b) + ol[p][rows])
            h[p] = _dot(g_mat[p][c], h_b) + f_mat[p][c]
    for p in range(npair):
        h_ref[p] = h[p]
    out = jnp.concatenate([jnp.concatenate(oc, axis=0) for oc in o_chunks], axis=1)

    eb = eb_ref[...]
    mean = _seg_dot_hl(out, eb) * (1.0 / B_HEAD_DIM)
    d = out - mean
    var = _seg_dot_hl(d * d, eb) * (1.0 / B_HEAD_DIM)
    y = d * lax.rsqrt(var + GN_EPS) * lnw_ref[...] + lnb_ref[...] + bv_ref[0]
    o_ref[0] = (y * g_ref[0]).astype(o_ref.dtype)


def _rwkv_call(rt, kt, bt, kl, v, g, bv, pc, ln_w, ln_b, eb):
    bsz, s, _ = rt.shape
    tm = TOK_TILE
    tok = pl.BlockSpec((1, tm, B_WIDTH), lambda b, j: (b, j, 0))
    row = pl.BlockSpec((1, B_WIDTH), lambda b, j: (0, 0))
    return pl.pallas_call(
        _rwkv_kernel,
        grid=(bsz, s // tm),
        in_specs=[tok] * 7 + [pl.BlockSpec((1, 1, tm // CHUNK, B_WIDTH), lambda b, j: (b, j, 0, 0)),
                              row, row, pl.BlockSpec((SEG_K, SEG_K), lambda b, j: (0, 0))],
        out_specs=tok,
        out_shape=jax.ShapeDtypeStruct((bsz, s, B_WIDTH), BF16),
        scratch_shapes=[pltpu.VMEM((B_HEADS // 2, LANES, LANES), F32)],
        compiler_params=_params(("arbitrary", "arbitrary")),
    )(rt, kt, bt, kl, v, g, bv, pc, ln_w, ln_b, eb)


def _ffn_kernel(x_ref, oa_ref, ob_ref, gt1_ref, sh2_ref, sc2_ref, gt2_ref, gf_ref, woa_ref, wob_ref,
                w1_ref, w2_ref, o_ref):
    for r0 in range(0, x_ref.shape[1], FFN_ROWS):
        rows = slice(r0, r0 + FFN_ROWS)
        mix = _dot(oa_ref[0, rows], woa_ref[...]) + _dot(ob_ref[0, rows], wob_ref[...])
        x1 = x_ref[0, rows] + gt1_ref[0, 0] * mix
        y = x1 * lax.rsqrt(jnp.mean(x1 * x1, axis=-1, keepdims=True) + RMS_EPS) * gf_ref[...]
        h2 = (y * (1.0 + sc2_ref[0, 0]) + sh2_ref[0, 0]).astype(BF16)
        u = jnp.maximum(_dot(h2, w1_ref[...]), 0.0)
        o_ref[0, rows] = x1 + gt2_ref[0, 0] * _dot((u * u).astype(BF16), w2_ref[...])


def _ffn_call(x, oa, ob, mod4, g_ffn, w_out_a, w_out_b, w1, w2):
    bsz, s, d = x.shape
    dff = w1.shape[1]
    tm = FFN_TILE
    tok = lambda w: pl.BlockSpec((1, tm, w), lambda b, j: (b, j, 0))
    modk = lambda k: pl.BlockSpec((1, 1, 1, d), lambda b, j, k=k: (b, k, 0, 0))
    res = lambda shape: pl.BlockSpec(shape, lambda b, j: (0, 0), pipeline_mode=pl.Buffered(1))
    return pl.pallas_call(
        _ffn_kernel,
        grid=(bsz, s // tm),
        in_specs=[tok(d), tok(A_WIDTH), tok(B_WIDTH), modk(2), modk(3), modk(4), modk(5),
                  pl.BlockSpec((1, d), lambda b, j: (0, 0)),
                  res((A_WIDTH, d)), res((B_WIDTH, d)), res((d, dff)), res((dff, d))],
        out_specs=tok(d),
        out_shape=jax.ShapeDtypeStruct((bsz, s, d), F32),
        compiler_params=_params(("arbitrary", "arbitrary")),
    )(x, oa, ob, mod4, mod4, mod4, mod4, g_ffn, w_out_a, w_out_b, w1, w2)


def _block_ones(n, blk, dtype=BF16):
    i = jnp.arange(n)
    return ((i[:, None] // blk) == (i[None, :] // blk)).astype(dtype)


def kernel(x, c, w_ada, b_ada, g_mix, g_ffn, w_in, g_q, g_k, g_kv, w_uk, w_uv, mu_shift, w0, w2, a0, a2, g2,
           k_k, k_a, r_k, ln_w, ln_b, w_out, w_ff1, w_ff2):
    bsz, s, d = x.shape
    depth = w_ada.shape[0]
    assert s % Q_TILE == 0 and s % TOK_TILE == 0 and s % FFN_TILE == 0
    topk = min(TOPK_MAX, s // 4)

    eb = _block_ones(SEG_K, B_HEAD_DIM)
    ex = (jnp.arange(2 * A_HEAD_DIM)[:, None] // A_HEAD_DIM == jnp.arange(2 * KV_LATENT)[None, :] // KV_LATENT
          ).astype(BF16)
    sel = (jnp.arange(LANES)[None, :] == IDX_DIM + jnp.arange(IDX_HEADS)[:, None]).astype(BF16)
    eye_l = jnp.eye(KV_LATENT, dtype=BF16)
    ti = jnp.arange(TOK_TILE)
    tri = (((ti[:, None] // CHUNK) == (ti[None, :] // CHUNK)) & (ti[:, None] >= ti[None, :])).astype(BF16)
    ki = jnp.arange(K_TILE)
    lstrict = (ki[None, :] < ki[:, None]).astype(BF16)

    for l in range(depth):
        w_a = jnp.pad(w_in[l][:, :N_IN_A], ((0, 0), (0, N_A_PAD - N_IN_A)))
        w_in_p = jnp.concatenate([w_a, w_in[l][:, N_IN_A:]], axis=1).astype(BF16)
        wuk_flat = w_uk[l].reshape(KV_LATENT, A_WIDTH).astype(BF16)
        wuk_t = jnp.transpose(w_uk[l], (1, 2, 0)).reshape(A_HEADS // 2, 2, A_HEAD_DIM, KV_LATENT)
        wuk_bd = (jnp.eye(2, dtype=F32)[None, :, None, :, None] * wuk_t[:, :, :, None, :]).reshape(
            A_HEADS // 2, 2 * A_HEAD_DIM, 2 * KV_LATENT).astype(BF16)
        wuv_t = jnp.transpose(w_uv[l], (1, 0, 2)).reshape(A_HEADS // 2, 2, KV_LATENT, A_HEAD_DIM)
        wuv_pair = (jnp.eye(2, dtype=F32)[None, :, None, :, None] * wuv_t[:, :, :, None, :]).reshape(
            A_HEADS // 2, 2 * KV_LATENT, 2 * A_HEAD_DIM).astype(BF16)
        gqk = jnp.tile(g_q[l] * g_k[l], A_HEADS).reshape(1, A_WIDTH)
        r1 = lambda t: t.reshape(1, -1)

        mod = _mod_call(c, w_ada[l], b_ada[l])
        mod4 = mod.reshape(bsz, 6, 1, d)
        pa, pb = _proj_call(x, mod4, r1(g_mix[l]), w_in_p, r1(mu_shift[l]))
        ckr, cvt, qabs, qidx, kidx, widx = _prep_a_call(pa, r1(g_kv[l]), gqk, wuk_flat, wuk_bd, eb, ex, sel, eye_l)
        o_a = _dsa_call(topk, qabs, qidx, widx, ckr, cvt, kidx, wuv_pair, lstrict)
        rt, kt, bt, kl, v, g, bv, pc = _prep_b_call(
            pb, r1(w0[l]), w2[l].astype(BF16), r1(a0[l]), a2[l].astype(BF16), g2[l].astype(BF16),
            r1(k_k[l]), r1(k_a[l]), r1(r_k[l]), eb, tri)
        o_b = _rwkv_call(rt, kt, bt, kl, v, g, bv, pc, r1(ln_w[l]), r1(ln_b[l]), eb)
        x = _ffn_call(x, o_a, o_b, mod4, r1(g_ffn[l]), w_out[l][:A_WIDTH].astype(BF16),
                      w_out[l][A_WIDTH:].astype(BF16), w_ff1[l].astype(BF16), w_ff2[l].astype(BF16))
    return x
```

```python
import functools

import jax
import jax.numpy as jnp
from jax import lax
from jax.experimental import pallas as pl
from jax.experimental.pallas import tpu as pltpu

F32 = jnp.float32
BF16 = jnp.bfloat16

CHUNK = 64
A_HEADS = 8
A_HEAD_DIM = 64
A_WIDTH = A_HEADS * A_HEAD_DIM
KV_LATENT = 128
IDX_HEADS = 8
IDX_DIM = 64
TOPK_MAX = 256
B_HEADS = 8
B_HEAD_DIM = 64
B_WIDTH = B_HEADS * B_HEAD_DIM
W_LORA = 64
A_LORA = 64
G_LORA = 128
RMS_EPS = 1e-6
GN_EPS = 64e-5
N_IN_A = A_WIDTH + KV_LATENT + IDX_HEADS * IDX_DIM + IDX_DIM + IDX_HEADS
N_IN_B = 3 * B_WIDTH + W_LORA + A_LORA + G_LORA
N_A_PAD = 1280

LANES = 128
SEG_K = 256
TOK_TILE = 256
Q_TILE = 256
FFN_TILE = 512
FFN_ROWS = 256
K_TILE = 256
DIST_BIG = 1e30
ONES_ROWS = 16
LOG2E = 1.4426950408889634
SEARCH_PROBES = 20
SEARCH_TRIPS_MAX = 64
VMEM_LIMIT = 56 * 1024 * 1024


def _dot(a, b):
    return jnp.dot(a, b, preferred_element_type=F32)


def _dot_nt(a, b):
    return lax.dot_general(a, b, (((1,), (1,)), ((), ())), preferred_element_type=F32)


def _dot_tn(a, b):
    return lax.dot_general(a, b, (((0,), (0,)), ((), ())), preferred_element_type=F32)


def _split(x):
    hi = x.astype(BF16)
    lo = (x - hi.astype(F32)).astype(BF16)
    return hi, lo


def _dot_hl(x, e):
    hi, lo = _split(x)
    return _dot(hi, e) + _dot(lo, e)


def _seg_dot_hl(x, e):
    k = e.shape[0]
    return jnp.concatenate([_dot_hl(x[:, j:j + k], e) for j in range(0, x.shape[1], k)], axis=1)


def _params(sem):
    return pltpu.CompilerParams(dimension_semantics=sem, vmem_limit_bytes=VMEM_LIMIT)


def _mod_kernel(c_ref, w_ref, b_ref, o_ref):
    c = c_ref[...]
    s = c * jax.nn.sigmoid(c)
    s_hi, s_lo = _split(s)
    w_hi, w_lo = _split(w_ref[...])
    o_ref[...] = _dot(s_hi, w_hi) + _dot(s_hi, w_lo) + _dot(s_lo, w_hi) + b_ref[...]


def _mod_call(c, w_ada, b_ada):
    bsz, d = c.shape
    n = w_ada.shape[1]
    tn = 1024
    return pl.pallas_call(
        _mod_kernel,
        grid=(n // tn,),
        in_specs=[pl.BlockSpec((bsz, d), lambda j: (0, 0)),
                  pl.BlockSpec((d, tn), lambda j: (0, j)),
                  pl.BlockSpec((1, tn), lambda j: (0, j))],
        out_specs=pl.BlockSpec((bsz, tn), lambda j: (0, j)),
        out_shape=jax.ShapeDtypeStruct((bsz, n), F32),
        compiler_params=_params(("arbitrary",)),
    )(c, w_ada, b_ada.reshape(1, n))


def _prep_a(pa, gkv_ref, gqk_ref, wuk_ref, wukbd_ref, eb_ref, ex_ref, sel_ref, eye_ref,
            ckr_ref, cvt_ref, qabs_ref, qidx_ref, kidx_ref, widx_ref):
    tm = pa.shape[0]
    q = pa[:, :A_WIDTH]
    cl = pa[:, A_WIDTH:A_WIDTH + KV_LATENT]
    o_qi = A_WIDTH + KV_LATENT
    qi = pa[:, o_qi:o_qi + IDX_HEADS * IDX_DIM]
    o_kw = o_qi + IDX_HEADS * IDX_DIM
    kw = pa[:, o_kw:o_kw + LANES]

    ckv = cl * lax.rsqrt(jnp.mean(cl * cl, axis=-1, keepdims=True) + RMS_EPS) * gkv_ref[...]
    ckv_b = ckv.astype(BF16)
    cvt_ref[0, 0, :KV_LATENT, :] = _dot_nt(eye_ref[...], ckv_b).astype(BF16)
    cvt_ref[0, 0, KV_LATENT:, :] = jnp.ones((ONES_ROWS, tm), BF16)
    kf = _dot(ckv_b, wuk_ref[...])
    ss = _seg_dot_hl(kf * kf, ex_ref[...])
    inv_rms = lax.rsqrt(ss * (1.0 / A_HEAD_DIM) + RMS_EPS)
    ckr_ref[0] = (jnp.concatenate([ckv] * A_HEADS, axis=1) * inv_rms).astype(BF16)

    ssq = _seg_dot_hl(q * q, eb_ref[...])
    qh = q * lax.rsqrt(ssq * (1.0 / A_HEAD_DIM) + RMS_EPS) * gqk_ref[...]
    qh_b = qh.astype(BF16)
    for j in range(A_HEADS // 2):
        qabs = _dot(qh_b[:, j * LANES:(j + 1) * LANES], wukbd_ref[j]) * (A_HEAD_DIM ** -0.5 * LOG2E)
        qabs_ref[0, :, 2 * j * KV_LATENT:2 * (j + 1) * KV_LATENT] = qabs.astype(BF16)
    for h in range(IDX_HEADS):
        qidx_ref[0, h] = qi[:, h * IDX_DIM:(h + 1) * IDX_DIM].astype(BF16)
    kidx_ref[0] = kw[:, :IDX_DIM].astype(BF16)
    kw_hi, kw_lo = _split(kw)
    w_t = _dot_nt(sel_ref[...], kw_hi) + _dot_nt(sel_ref[...], kw_lo)
    widx_ref[0, 0] = w_t * (IDX_HEADS ** -0.5 * IDX_DIM ** -0.5)


def _prep_b(pb, w0_ref, w2_ref, a0_ref, a2_ref, g2_ref, kk_ref, ka_ref, rk_ref, eb_ref, tri_ref,
            rt_ref, kt_ref, bt_ref, kl_ref, v_ref, g_ref, bv_ref, pc_ref):
    r = pb[:, :B_WIDTH]
    k = pb[:, B_WIDTH:2 * B_WIDTH]
    v = pb[:, 2 * B_WIDTH:3 * B_WIDTH]
    o = 3 * B_WIDTH
    xw = pb[:, o:o + W_LORA]
    xa = pb[:, o + W_LORA:o + W_LORA + A_LORA]
    xg = pb[:, o + W_LORA + A_LORA:o + W_LORA + A_LORA + G_LORA]

    z = w0_ref[...] + _dot(jnp.tanh(xw).astype(BF16), w2_ref[...])
    nz = -z
    softplus = jnp.maximum(nz, 0.0) + jnp.log(1.0 + jnp.exp(-jnp.abs(nz)))
    lw = -jnp.exp(-softplus - 0.5)
    a = jax.nn.sigmoid(a0_ref[...] + _dot(xa.astype(BF16), a2_ref[...]))
    g = _dot(jax.nn.sigmoid(xg).astype(BF16), g2_ref[...])
    kk = k * kk_ref[...]
    kkn = kk / jnp.maximum(jnp.sqrt(_seg_dot_hl(kk * kk, eb_ref[...])), 1e-12)
    kp = k * (1.0 + (a - 1.0) * ka_ref[...])
    bonus = _seg_dot_hl(r * kp * rk_ref[...], eb_ref[...])

    lw_hi, lw_lo = _split(lw)
    cum = _dot(tri_ref[...], lw_hi) + _dot(tri_ref[...], lw_lo)
    e_pos = jnp.exp(cum)
    e_neg = jnp.exp(-cum)
    rt_ref[0] = (r * e_pos).astype(BF16)
    kt_ref[0] = (kkn * jnp.exp(cum - lw)).astype(BF16)
    bt_ref[0] = (kkn * a * e_neg).astype(BF16)
    kl_ref[0] = (kp * e_neg).astype(BF16)
    v_ref[0] = v.astype(BF16)
    g_ref[0] = g
    bv_ref[0] = bonus * v
    for c in range(pb.shape[0] // CHUNK):
        pc_ref[0, 0, c:c + 1, :] = e_pos[(c + 1) * CHUNK - 1:(c + 1) * CHUNK, :]


N_FRONT_IN = 6
N_PREP_A_IN = 8
N_PREP_B_IN = 10
N_PREP_A_OUT = 6


def _front_kernel(*refs):
    x_ref, sh_ref, sc_ref, g_ref, w_ref, mu_ref = refs[:N_FRONT_IN]
    a_in = refs[N_FRONT_IN:N_FRONT_IN + N_PREP_A_IN]
    b_in = refs[N_FRONT_IN + N_PREP_A_IN:N_FRONT_IN + N_PREP_A_IN + N_PREP_B_IN]
    outs = refs[N_FRONT_IN + N_PREP_A_IN + N_PREP_B_IN:-1]
    carry_ref = refs[-1]
    j = pl.program_id(1)

    @pl.when(j == 0)
    def _():
        carry_ref[...] = jnp.zeros_like(carry_ref)

    x = x_ref[0]
    y = x * lax.rsqrt(jnp.mean(x * x, axis=-1, keepdims=True) + RMS_EPS) * g_ref[...]
    h = y * (1.0 + sc_ref[0, 0]) + sh_ref[0, 0]
    p = _dot(h.astype(BF16), w_ref[...])
    _prep_a(p[:, :N_A_PAD], *a_in, *outs[:N_PREP_A_OUT])
    pb = p[:, N_A_PAD:]
    tm = pb.shape[0]
    row = lax.broadcasted_iota(jnp.int32, (tm, 1), 0)
    prev = jnp.where(row == 0, carry_ref[...], pltpu.roll(pb, 1, axis=0))
    carry_ref[...] = pb[tm - 1:tm, :]
    _prep_b(pb + mu_ref[...] * (prev - pb), *b_in, *outs[N_PREP_A_OUT:])


def _front_call(x, mod4, g_mix, w_in_p, mu, a_consts, b_consts):
    bsz, s, d = x.shape
    n = w_in_p.shape[1]
    nb = n - N_A_PAD
    tm = TOK_TILE
    full = lambda arr: pl.BlockSpec(arr.shape, lambda b, j, nd=arr.ndim: (0,) * nd)
    tok = lambda w: pl.BlockSpec((1, tm, w), lambda b, j: (b, j, 0))
    per_tile = lambda r, c: pl.BlockSpec((1, 1, r, c), lambda b, j: (b, j, 0, 0))
    bf = lambda w: jax.ShapeDtypeStruct((bsz, s, w), BF16)
    ff = lambda w: jax.ShapeDtypeStruct((bsz, s, w), F32)
    nt = s // tm
    out_specs = [tok(A_HEADS * KV_LATENT), per_tile(KV_LATENT + ONES_ROWS, tm), tok(A_HEADS * KV_LATENT),
                 pl.BlockSpec((1, IDX_HEADS, tm, IDX_DIM), lambda b, j: (b, 0, j, 0)),
                 tok(IDX_DIM), per_tile(IDX_HEADS, tm)] + [tok(B_WIDTH)] * 7 + [per_tile(tm // CHUNK, B_WIDTH)]
    out_shape = [bf(A_HEADS * KV_LATENT),
                 jax.ShapeDtypeStruct((bsz, nt, KV_LATENT + ONES_ROWS, tm), BF16),
                 bf(A_HEADS * KV_LATENT),
                 jax.ShapeDtypeStruct((bsz, IDX_HEADS, s, IDX_DIM), BF16),
                 bf(IDX_DIM),
                 jax.ShapeDtypeStruct((bsz, nt, IDX_HEADS, tm), F32),
                 bf(B_WIDTH), bf(B_WIDTH), bf(B_WIDTH), bf(B_WIDTH), bf(B_WIDTH), ff(B_WIDTH), ff(B_WIDTH),
                 jax.ShapeDtypeStruct((bsz, nt, tm // CHUNK, B_WIDTH), F32)]
    assert len(a_consts) == N_PREP_A_IN and len(b_consts) == N_PREP_B_IN
    return pl.pallas_call(
        _front_kernel,
        grid=(bsz, nt),
        in_specs=[pl.BlockSpec((1, tm, d), lambda b, j: (b, j, 0)),
                  pl.BlockSpec((1, 1, 1, d), lambda b, j: (b, 0, 0, 0)),
                  pl.BlockSpec((1, 1, 1, d), lambda b, j: (b, 1, 0, 0)),
                  full(g_mix),
                  pl.BlockSpec(w_in_p.shape, lambda b, j: (0, 0), pipeline_mode=pl.Buffered(1)),
                  full(mu)] + [full(t) for t in a_consts] + [full(t) for t in b_consts],
        out_specs=out_specs,
        out_shape=out_shape,
        scratch_shapes=[pltpu.VMEM((1, nb), F32)],
        compiler_params=_params(("arbitrary", "arbitrary")),
    )(x, mod4, mod4, g_mix, w_in_p, mu, *a_consts, *b_consts)


def _colsum8(x):
    y = x.reshape(4, K_TILE // 32, 8, Q_TILE)
    return jnp.sum(jnp.sum(y, axis=1), axis=0)


def _colmin8(x):
    y = x.reshape(4, K_TILE // 32, 8, Q_TILE)
    return jnp.min(jnp.min(y, axis=1), axis=0)


def _colmax8(x):
    y = x.reshape(4, K_TILE // 32, 8, Q_TILE)
    return jnp.max(jnp.max(y, axis=1), axis=0)


def _dsa_kernel(topk, qabs_ref, qidx_ref, widx_ref, ckr_ref, cvt_ref, kidx_ref, wuv_ref, lstrict_ref,
                o_ref, score_ref, dist_ref, logit_ref, m_ref, acc_ref):
    i = pl.program_id(1)
    nkc = i + 1
    t0 = i * Q_TILE
    krow = lax.broadcasted_iota(jnp.int32, (K_TILE, 1), 0)
    qcol = lax.broadcasted_iota(jnp.int32, (1, Q_TILE), 1)
    limit = ((t0 + qcol) // CHUNK + 1) * CHUNK
    kp = jnp.minimum(limit, topk).astype(F32)
    rel = (qcol - krow).astype(F32)

    def p1(kc, carry):
        rmin, rmax = carry
        k = kidx_ref[0, pl.ds(pl.multiple_of(kc * K_TILE, K_TILE), K_TILE), :]
        acc = jnp.zeros((K_TILE, Q_TILE), F32)
        for h in range(IDX_HEADS):
            s = _dot_nt(k, qidx_ref[0, h])
            acc = acc + widx_ref[0, 0, h:h + 1, :] * jnp.maximum(s, 0.0)
        adm = (kc * K_TILE + krow) < limit
        score_ref[kc] = jnp.where(adm, acc, -jnp.inf)
        rmin = jnp.minimum(rmin, _colmin8(jnp.where(adm, acc, jnp.inf)))
        rmax = jnp.maximum(rmax, _colmax8(jnp.where(adm, acc, -jnp.inf)))
        return rmin, rmax

    rmin, rmax = lax.fori_loop(
        0, nkc, p1, (jnp.full((8, Q_TILE), jnp.inf, F32), jnp.full((8, Q_TILE), -jnp.inf, F32)))
    lo = jnp.min(rmin, axis=0, keepdims=True)
    hi = jnp.max(rmax, axis=0, keepdims=True)

    def count(pred):
        def body(kc, acc):
            return acc + _colsum8(jnp.where(pred(score_ref[kc]), 1.0, 0.0))
        return jnp.sum(lax.fori_loop(0, nkc, body, jnp.zeros((8, Q_TILE), F32)), axis=0, keepdims=True)

    def probe(c):
        lo, hi, cnt_lo = c
        mid = lo + 0.5 * (hi - lo)
        cnt = count(lambda sc: sc >= mid)
        ge = cnt >= kp
        return jnp.where(ge, mid, lo), jnp.where(ge, hi, mid), jnp.where(ge, cnt, cnt_lo)

    def thr_of(lo):
        def thr_body(kc, acc):
            sc = score_ref[kc]
            return jnp.minimum(acc, _colmin8(jnp.where(sc >= lo, sc, jnp.inf)))
        thr = jnp.min(lax.fori_loop(0, nkc, thr_body, jnp.full((8, Q_TILE), jnp.inf, F32)), axis=0, keepdims=True)
        return thr, count(lambda sc: sc > thr)

    def any_true(x):
        return jnp.max(jnp.where(x, 1.0, 0.0)) > 0.0

    bracket = lax.fori_loop(0, SEARCH_PROBES, lambda _, c: probe(c), (lo, hi, limit.astype(F32)))
    search = lax.while_loop(
        lambda c: jnp.logical_and(c[0] < SEARCH_TRIPS_MAX, any_true(c[5] >= kp)),
        lambda c: (c[0] + 1,) + (lambda br: br + thr_of(br[0]))(probe(probe(c[1:4]))),
        (jnp.int32(0),) + bracket + thr_of(bracket[0]))
    lo, cnt_lo, thr, need = search[1], search[3], search[4], kp - search[5]

    def dist_tile(kc):
        return jnp.abs(rel + (t0 - kc * K_TILE).astype(F32))

    def sel_plain():
        def body(kc, _):
            dist_ref[kc] = jnp.where(score_ref[kc] >= lo, dist_tile(kc), DIST_BIG)
            return 0
        lax.fori_loop(0, nkc, body, 0)

    def sel_ties():
        def body(kc, run):
            sc = score_ref[kc]
            eq = sc == thr
            eq_f = jnp.where(eq, 1.0, 0.0)
            pre = run + _dot(lstrict_ref[...], eq_f.astype(BF16))
            keep = (sc > thr) | (eq & (pre < need))
            dist_ref[kc] = jnp.where(keep, dist_tile(kc), DIST_BIG)
            return run + jnp.sum(_colsum8(eq_f), axis=0, keepdims=True)
        lax.fori_loop(0, nkc, body, jnp.zeros((1, Q_TILE), F32))

    lax.cond(any_true(cnt_lo != kp), sel_ties, sel_plain)

    m_ref[...] = jnp.full(m_ref.shape, -jnp.inf, F32)
    acc_ref[...] = jnp.zeros(acc_ref.shape, F32)

    def att(kc, _):
        dist = dist_ref[kc]
        m_new = []
        for h in range(A_HEADS):
            slope = 2.0 ** (-8.0 * (h + 1) / A_HEADS) * LOG2E
            ck = ckr_ref[0, pl.ds(pl.multiple_of(kc * K_TILE, K_TILE), K_TILE), h * KV_LATENT:(h + 1) * KV_LATENT]
            logit = _dot_nt(ck, qabs_ref[0, :, h * KV_LATENT:(h + 1) * KV_LATENT]) - slope * dist
            logit_ref[h] = logit
            m_new.append(jnp.maximum(m_ref[h], jnp.max(_colmax8(logit), axis=0, keepdims=True)))
        cv = cvt_ref[0, kc]
        for h in range(A_HEADS):
            p = jnp.exp2(logit_ref[h] - m_new[h])
            acc_ref[h] = acc_ref[h] * jnp.exp2(m_ref[h] - m_new[h]) + _dot(cv, p.astype(BF16))
            m_ref[h] = m_new[h]
        return 0

    lax.fori_loop(0, nkc, att, 0)

    for pair in range(A_HEADS // 2):
        o_pair = []
        for hh in range(2):
            a = acc_ref[2 * pair + hh]
            o_t = a[:KV_LATENT] * (1.0 / a[KV_LATENT:KV_LATENT + 1])
            o_pair.append(o_t.T.astype(BF16))
        o_lat = jnp.concatenate(o_pair, axis=1)
        o_ref[0, :, pair * LANES:(pair + 1) * LANES] = _dot(o_lat, wuv_ref[pair]).astype(o_ref.dtype)


def _dsa_call(topk, qabs, qidx, widx, ckr, cvt, kidx, wuv_pair, lstrict):
    bsz, s, _ = qabs.shape
    nq = s // Q_TILE
    nk = s // K_TILE
    qt = lambda w: pl.BlockSpec((1, Q_TILE, w), lambda b, i: (b, i, 0))
    return pl.pallas_call(
        functools.partial(_dsa_kernel, topk),
        grid=(bsz, nq),
        in_specs=[qt(A_HEADS * KV_LATENT),
                  pl.BlockSpec((1, IDX_HEADS, Q_TILE, IDX_DIM), lambda b, i: (b, 0, i, 0)),
                  pl.BlockSpec((1, 1, IDX_HEADS, Q_TILE), lambda b, i: (b, i, 0, 0)),
                  pl.BlockSpec((1, s, A_HEADS * KV_LATENT), lambda b, i: (b, 0, 0)),
                  pl.BlockSpec((1, nk, KV_LATENT + ONES_ROWS, K_TILE), lambda b, i: (b, 0, 0, 0)),
                  pl.BlockSpec((1, s, IDX_DIM), lambda b, i: (b, 0, 0)),
                  pl.BlockSpec((A_HEADS // 2, 2 * KV_LATENT, LANES), lambda b, i: (0, 0, 0)),
                  pl.BlockSpec((K_TILE, K_TILE), lambda b, i: (0, 0))],
        out_specs=qt(A_WIDTH),
        out_shape=jax.ShapeDtypeStruct((bsz, s, A_WIDTH), BF16),
        scratch_shapes=[pltpu.VMEM((nk, K_TILE, Q_TILE), F32),
                        pltpu.VMEM((nk, K_TILE, Q_TILE), F32),
                        pltpu.VMEM((A_HEADS, K_TILE, Q_TILE), F32),
                        pltpu.VMEM((A_HEADS, 1, Q_TILE), F32),
                        pltpu.VMEM((A_HEADS, KV_LATENT + ONES_ROWS, Q_TILE), F32)],
        compiler_params=_params(("arbitrary", "arbitrary")),
    )(qabs, qidx, widx, ckr, cvt, kidx, wuv_pair, lstrict)


def _rwkv_kernel(rt_ref, kt_ref, bt_ref, kl_ref, v_ref, g_ref, bv_ref, pc_ref, lnw_ref, lnb_ref, eb_ref,
                 o_ref, h_ref):
    j = pl.program_id(1)

    @pl.when(j == 0)
    def _():
        h_ref[...] = jnp.zeros_like(h_ref)

    tm = rt_ref.shape[1]
    nch = tm // CHUNK
    ri = lax.broadcasted_iota(jnp.int32, (tm, tm), 0)
    ci = lax.broadcasted_iota(jnp.int32, (tm, tm), 1)
    same = (ri // CHUNK) == (ci // CHUNK)
    strict = same & (ri > ci)
    incl = same & (ri >= ci)
    eye_t = jnp.where(ri == ci, 1.0, 0.0)
    r2 = lax.broadcasted_iota(jnp.int32, (LANES, LANES), 0)
    c2 = lax.broadcasted_iota(jnp.int32, (LANES, LANES), 1)
    blk = (r2 // B_HEAD_DIM) == (c2 // B_HEAD_DIM)
    diag = r2 == c2
    lane = lax.broadcasted_iota(jnp.int32, (1, LANES), 1)
    zero_b = jnp.zeros((), BF16)

    npair = B_HEADS // 2
    heads = [(p, hh) for p in range(npair) for hh in range(2)]
    head0 = (lane // B_HEAD_DIM) == 0
    rt, kt, bt, kl, v = [], [], [], [], []
    a_ab, a_ak, m_rb, m_rk = [], [], [], []
    for p in range(npair):
        sl = slice(p * LANES, (p + 1) * LANES)
        rt.append(rt_ref[0, :, sl])
        kt.append(kt_ref[0, :, sl])
        bt.append(bt_ref[0, :, sl])
        kl.append(kl_ref[0, :, sl])
        v.append(v_ref[0, :, sl])
        lhs = jnp.concatenate([jnp.where(head0, kt[p], zero_b), jnp.where(head0, zero_b, kt[p]),
                               jnp.where(head0, rt[p], zero_b), jnp.where(head0, zero_b, rt[p])], axis=0)
        prod = _dot_nt(lhs, jnp.concatenate([bt[p], kl[p]], axis=0))
        for hh in range(2):
            a_ab.append(jnp.where(strict, prod[hh * tm:(hh + 1) * tm, :tm], 0.0))
            a_ak.append(jnp.where(strict, prod[hh * tm:(hh + 1) * tm, tm:], 0.0).astype(BF16))
            m_rb.append(jnp.where(incl, prod[(2 + hh) * tm:(3 + hh) * tm, :tm], 0.0).astype(BF16))
            m_rk.append(jnp.where(incl, prod[(2 + hh) * tm:(3 + hh) * tm, tm:], 0.0).astype(BF16))

    t_inv = [(eye_t - a).astype(BF16) for a in a_ab]
    a_pow = [a.astype(BF16) for a in a_ab]
    for _ in range(5):
        a_sq = [_dot(a, a) for a in a_pow]
        a_pow = [a.astype(BF16) for a in a_sq]
        t_inv = [_dot(t, (eye_t + a).astype(BF16)).astype(BF16) for t, a in zip(t_inv, a_sq)]

    avm = [_dot(jnp.concatenate([a_ak[i], m_rk[i]], axis=0), v[p]) for i, (p, _) in enumerate(heads)]
    x = [_dot(t_inv[i], jnp.concatenate([kt[p], avm[i][:tm].astype(BF16)], axis=1))
         for i, (p, _) in enumerate(heads)]
    y = [_dot(m_rb[i], x[i].astype(BF16)) for i in range(len(heads))]

    wu_b, q_b, ol = [], [], []
    for p in range(npair):
        i0, i1 = 2 * p, 2 * p + 1
        head0_2 = jnp.concatenate([head0, head0], axis=1)
        wu_b.append((-jnp.where(head0_2, x[i0], x[i1])).astype(BF16))
        yy = jnp.where(head0_2, y[i0], y[i1])
        q_b.append((rt[p].astype(F32) - yy[:, :LANES]).astype(BF16))
        ol.append(jnp.where(head0, avm[i0][tm:], avm[i1][tm:]) - yy[:, LANES:])

    g_mat, f_mat = [], []
    zeros_b = jnp.zeros((CHUNK, LANES), BF16)
    for p in range(npair):
        sl = slice(p * LANES, (p + 1) * LANES)
        gp, fp = [], []
        for c in range(nch):
            rows = slice(c * CHUNK, (c + 1) * CHUNK)
            pc = pc_ref[0, 0, c:c + 1, sl]
            bh = (bt[p][rows].astype(F32) * pc).astype(BF16)
            kh = (kl[p][rows].astype(F32) * pc).astype(BF16)
            rhs = jnp.concatenate([wu_b[p][rows], jnp.concatenate([zeros_b, v[p][rows]], axis=1)], axis=0)
            bw = _dot_tn(jnp.concatenate([bh, kh], axis=0), rhs)
            gp.append((jnp.where(diag, pc, 0.0) + jnp.where(blk, bw[:, :LANES], 0.0)).astype(BF16))
            fp.append(jnp.where(blk, bw[:, LANES:], 0.0))
        g_mat.append(gp)
        f_mat.append(fp)

    h = [h_ref[p] for p in range(npair)]
    o_chunks = [[] for _ in range(npair)]
    for c in range(nch):
        rows = slice(c * CHUNK, (c + 1) * CHUNK)
        for p in range(npair):
            h_b = h[p].astype(BF16)
            o_chunks[p].append(_dot(q_b[p][rows], h_b) + ol[p][rows])
            h[p] = _dot(g_mat[p][c], h_b) + f_mat[p][c]
    for p in range(npair):
        h_ref[p] = h[p]
    out = jnp.concatenate([jnp.concatenate(oc, axis=0) for oc in o_chunks], axis=1)

    eb = eb_ref[...]
    mean = _seg_dot_hl(out, eb) * (1.0 / B_HEAD_DIM)
    d = out - mean
    var = _seg_dot_hl(d * d, eb) * (1.0 / B_HEAD_DIM)
    y = d * lax.rsqrt(var + GN_EPS) * lnw_ref[...] + lnb_ref[...] + bv_ref[0]
    o_ref[0] = (y * g_ref[0]).astype(o_ref.dtype)


def _rwkv_call(rt, kt, bt, kl, v, g, bv, pc, ln_w, ln_b, eb):
    bsz, s, _ = rt.shape
    tm = TOK_TILE
    tok = pl.BlockSpec((1, tm, B_WIDTH), lambda b, j: (b, j, 0))
    row = pl.BlockSpec((1, B_WIDTH), lambda b, j: (0, 0))
    return pl.pallas_call(
        _rwkv_kernel,
        grid=(bsz, s // tm),
        in_specs=[tok] * 7 + [pl.BlockSpec((1, 1, tm // CHUNK, B_WIDTH), lambda b, j: (b, j, 0, 0)),
                              row, row, pl.BlockSpec((SEG_K, SEG_K), lambda b, j: (0, 0))],
        out_specs=tok,
        out_shape=jax.ShapeDtypeStruct((bsz, s, B_WIDTH), BF16),
        scratch_shapes=[pltpu.VMEM((B_HEADS // 2, LANES, LANES), F32)],
        compiler_params=_params(("arbitrary", "arbitrary")),
    )(rt, kt, bt, kl, v, g, bv, pc, ln_w, ln_b, eb)


def _ffn_kernel(x_ref, oa_ref, ob_ref, gt1_ref, sh2_ref, sc2_ref, gt2_ref, gf_ref, woa_ref, wob_ref,
                w1_ref, w2_ref, o_ref):
    for r0 in range(0, x_ref.shape[1], FFN_ROWS):
        rows = slice(r0, r0 + FFN_ROWS)
        mix = _dot(oa_ref[0, rows], woa_ref[...]) + _dot(ob_ref[0, rows], wob_ref[...])
        x1 = x_ref[0, rows] + gt1_ref[0, 0] * mix
        y = x1 * lax.rsqrt(jnp.mean(x1 * x1, axis=-1, keepdims=True) + RMS_EPS) * gf_ref[...]
        h2 = (y * (1.0 + sc2_ref[0, 0]) + sh2_ref[0, 0]).astype(BF16)
        u = jnp.maximum(_dot(h2, w1_ref[...]), 0.0)
        o_ref[0, rows] = x1 + gt2_ref[0, 0] * _dot((u * u).astype(BF16), w2_ref[...])


def _ffn_call(x, oa, ob, mod4, g_ffn, w_out_a, w_out_b, w1, w2):
    bsz, s, d = x.shape
    dff = w1.shape[1]
    tm = FFN_TILE
    tok = lambda w: pl.BlockSpec((1, tm, w), lambda b, j: (b, j, 0))
    modk = lambda k: pl.BlockSpec((1, 1, 1, d), lambda b, j, k=k: (b, k, 0, 0))
    res = lambda shape: pl.BlockSpec(shape, lambda b, j: (0, 0), pipeline_mode=pl.Buffered(1))
    return pl.pallas_call(
        _ffn_kernel,
        grid=(bsz, s // tm),
        in_specs=[tok(d), tok(A_WIDTH), tok(B_WIDTH), modk(2), modk(3), modk(4), modk(5),
                  pl.BlockSpec((1, d), lambda b, j: (0, 0)),
                  res((A_WIDTH, d)), res((B_WIDTH, d)), res((d, dff)), res((dff, d))],
        out_specs=tok(d),
        out_shape=jax.ShapeDtypeStruct((bsz, s, d), F32),
        compiler_params=_params(("arbitrary", "arbitrary")),
    )(x, oa, ob, mod4, mod4, mod4, mod4, g_ffn, w_out_a, w_out_b, w1, w2)


def _block_ones(n, blk, dtype=BF16):
    i = jnp.arange(n)
    return ((i[:, None] // blk) == (i[None, :] // blk)).astype(dtype)


def kernel(x, c, w_ada, b_ada, g_mix, g_ffn, w_in, g_q, g_k, g_kv, w_uk, w_uv, mu_shift, w0, w2, a0, a2, g2,
           k_k, k_a, r_k, ln_w, ln_b, w_out, w_ff1, w_ff2):
    bsz, s, d = x.shape
    depth = w_ada.shape[0]
    assert s % Q_TILE == 0 and s % TOK_TILE == 0 and s % FFN_TILE == 0
    topk = min(TOPK_MAX, s // 4)

    eb = _block_ones(SEG_K, B_HEAD_DIM)
    ex = (jnp.arange(2 * A_HEAD_DIM)[:, None] // A_HEAD_DIM == jnp.arange(2 * KV_LATENT)[None, :] // KV_LATENT
          ).astype(BF16)
    sel = (jnp.arange(LANES)[None, :] == IDX_DIM + jnp.arange(IDX_HEADS)[:, None]).astype(BF16)
    eye_l = jnp.eye(KV_LATENT, dtype=BF16)
    ti = jnp.arange(TOK_TILE)
    tri = (((ti[:, None] // CHUNK) == (ti[None, :] // CHUNK)) & (ti[:, None] >= ti[None, :])).astype(BF16)
    ki = jnp.arange(K_TILE)
    lstrict = (ki[None, :] < ki[:, None]).astype(BF16)

    for l in range(depth):
        w_a = jnp.pad(w_in[l][:, :N_IN_A], ((0, 0), (0, N_A_PAD - N_IN_A)))
        w_in_p = jnp.concatenate([w_a, w_in[l][:, N_IN_A:]], axis=1).astype(BF16)
        wuk_flat = w_uk[l].reshape(KV_LATENT, A_WIDTH).astype(BF16)
        wuk_t = jnp.transpose(w_uk[l], (1, 2, 0)).reshape(A_HEADS // 2, 2, A_HEAD_DIM, KV_LATENT)
        wuk_bd = (jnp.eye(2, dtype=F32)[None, :, None, :, None] * wuk_t[:, :, :, None, :]).reshape(
            A_HEADS // 2, 2 * A_HEAD_DIM, 2 * KV_LATENT).astype(BF16)
        wuv_t = jnp.transpose(w_uv[l], (1, 0, 2)).reshape(A_HEADS // 2, 2, KV_LATENT, A_HEAD_DIM)
        wuv_pair = (jnp.eye(2, dtype=F32)[None, :, None, :, None] * wuv_t[:, :, :, None, :]).reshape(
            A_HEADS // 2, 2 * KV_LATENT, 2 * A_HEAD_DIM).astype(BF16)
        gqk = jnp.tile(g_q[l] * g_k[l], A_HEADS).reshape(1, A_WIDTH)
        r1 = lambda t: t.reshape(1, -1)

        mod = _mod_call(c, w_ada[l], b_ada[l])
        mod4 = mod.reshape(bsz, 6, 1, d)
        a_consts = (r1(g_kv[l]), gqk, wuk_flat, wuk_bd, eb, ex, sel, eye_l)
        b_consts = (r1(w0[l]), w2[l].astype(BF16), r1(a0[l]), a2[l].astype(BF16), g2[l].astype(BF16),
                    r1(k_k[l]), r1(k_a[l]), r1(r_k[l]), eb, tri)
        ckr, cvt, qabs, qidx, kidx, widx, rt, kt, bt, kl, v, g, bv, pc = _front_call(
            x, mod4, r1(g_mix[l]), w_in_p, r1(mu_shift[l]), a_consts, b_consts)
        o_a = _dsa_call(topk, qabs, qidx, widx, ckr, cvt, kidx, wuv_pair, lstrict)
        o_b = _rwkv_call(rt, kt, bt, kl, v, g, bv, pc, r1(ln_w[l]), r1(ln_b[l]), eb)
        x = _ffn_call(x, o_a, o_b, mod4, r1(g_ffn[l]), w_out[l][:A_WIDTH].astype(BF16),
                      w_out[l][A_WIDTH:].astype(BF16), w_ff1[l].astype(BF16), w_ff2[l].astype(BF16))
    return x
```

```python
import functools

import jax
import jax.numpy as jnp
from jax import lax
from jax.experimental import pallas as pl
from jax.experimental.pallas import tpu as pltpu

F32 = jnp.float32
BF16 = jnp.bfloat16

CHUNK = 64
A_HEADS = 8
A_HEAD_DIM = 64
A_WIDTH = A_HEADS * A_HEAD_DIM
KV_LATENT = 128
IDX_HEADS = 8
IDX_DIM = 64
TOPK_MAX = 256
B_HEADS = 8
B_HEAD_DIM = 64
B_WIDTH = B_HEADS * B_HEAD_DIM
W_LORA = 64
A_LORA = 64
G_LORA = 128
RMS_EPS = 1e-6
GN_EPS = 64e-5
N_IN_A = A_WIDTH + KV_LATENT + IDX_HEADS * IDX_DIM + IDX_DIM + IDX_HEADS
N_IN_B = 3 * B_WIDTH + W_LORA + A_LORA + G_LORA
N_A_PAD = 1280

LANES = 128
SEG_K = 256
TOK_TILE = 256
FRONT_TILE = 512
Q_TILE = 256
FFN_TILE = 512
FFN_ROWS = 256
K_TILE = 256
DIST_BIG = 1e30
ONES_ROWS = 16
LOG2E = 1.4426950408889634
SEARCH_PROBES = 14
VMEM_LIMIT = 56 * 1024 * 1024


def _dot(a, b):
    return jnp.dot(a, b, preferred_element_type=F32)


def _dot_nt(a, b):
    return lax.dot_general(a, b, (((1,), (1,)), ((), ())), preferred_element_type=F32)


def _dot_tn(a, b):
    return lax.dot_general(a, b, (((0,), (0,)), ((), ())), preferred_element_type=F32)


def _split(x):
    hi = x.astype(BF16)
    lo = (x - hi.astype(F32)).astype(BF16)
    return hi, lo


def _dot_hl(x, e):
    hi, lo = _split(x)
    return _dot(hi, e) + _dot(lo, e)


def _seg_dot_hl(x, e):
    k = e.shape[0]
    return jnp.concatenate([_dot_hl(x[:, j:j + k], e) for j in range(0, x.shape[1], k)], axis=1)


def _params(sem):
    return pltpu.CompilerParams(dimension_semantics=sem, vmem_limit_bytes=VMEM_LIMIT)


def _mod_kernel(c_ref, w_ref, b_ref, o_ref):
    c = c_ref[...]
    s = c * jax.nn.sigmoid(c)
    s_hi, s_lo = _split(s)
    w_hi, w_lo = _split(w_ref[...])
    o_ref[...] = _dot(s_hi, w_hi) + _dot(s_hi, w_lo) + _dot(s_lo, w_hi) + b_ref[...]


def _mod_call(c, w_ada, b_ada):
    bsz, d = c.shape
    n = w_ada.shape[1]
    tn = 1024
    return pl.pallas_call(
        _mod_kernel,
        grid=(n // tn,),
        in_specs=[pl.BlockSpec((bsz, d), lambda j: (0, 0)),
                  pl.BlockSpec((d, tn), lambda j: (0, j)),
                  pl.BlockSpec((1, tn), lambda j: (0, j))],
        out_specs=pl.BlockSpec((bsz, tn), lambda j: (0, j)),
        out_shape=jax.ShapeDtypeStruct((bsz, n), F32),
        compiler_params=_params(("arbitrary",)),
    )(c, w_ada, b_ada.reshape(1, n))


def _prep_a(pa, rows, blk, gkv_ref, gqk_ref, wuk_ref, wukbd_ref, eb_ref, ex_ref, sel_ref, eye_ref,
            ckr_ref, cvt_ref, qabs_ref, qidx_ref, kidx_ref, widx_ref):
    tm = pa.shape[0]
    q = pa[:, :A_WIDTH]
    cl = pa[:, A_WIDTH:A_WIDTH + KV_LATENT]
    o_qi = A_WIDTH + KV_LATENT
    qi = pa[:, o_qi:o_qi + IDX_HEADS * IDX_DIM]
    o_kw = o_qi + IDX_HEADS * IDX_DIM
    kw = pa[:, o_kw:o_kw + LANES]

    ckv = cl * lax.rsqrt(jnp.mean(cl * cl, axis=-1, keepdims=True) + RMS_EPS) * gkv_ref[...]
    ckv_b = ckv.astype(BF16)
    cvt_ref[0, blk, :KV_LATENT, :] = _dot_nt(eye_ref[...], ckv_b).astype(BF16)
    cvt_ref[0, blk, KV_LATENT:, :] = jnp.ones((ONES_ROWS, tm), BF16)
    kf = _dot(ckv_b, wuk_ref[...])
    ss = _seg_dot_hl(kf * kf, ex_ref[...])
    inv_rms = lax.rsqrt(ss * (1.0 / A_HEAD_DIM) + RMS_EPS)
    ckr_ref[0, rows] = (jnp.concatenate([ckv] * A_HEADS, axis=1) * inv_rms).astype(BF16)

    ssq = _seg_dot_hl(q * q, eb_ref[...])
    qh = q * lax.rsqrt(ssq * (1.0 / A_HEAD_DIM) + RMS_EPS) * gqk_ref[...]
    qh_b = qh.astype(BF16)
    for j in range(A_HEADS // 2):
        qabs = _dot(qh_b[:, j * LANES:(j + 1) * LANES], wukbd_ref[j]) * (A_HEAD_DIM ** -0.5 * LOG2E)
        qabs_ref[0, rows, 2 * j * KV_LATENT:2 * (j + 1) * KV_LATENT] = qabs.astype(BF16)
    for h in range(IDX_HEADS):
        qidx_ref[0, h, rows] = qi[:, h * IDX_DIM:(h + 1) * IDX_DIM].astype(BF16)
    kidx_ref[0, rows] = kw[:, :IDX_DIM].astype(BF16)
    kw_hi, kw_lo = _split(kw)
    w_t = _dot_nt(sel_ref[...], kw_hi) + _dot_nt(sel_ref[...], kw_lo)
    widx_ref[0, blk] = w_t * (IDX_HEADS ** -0.5 * IDX_DIM ** -0.5)


def _prep_b(pb, rows, blk, w0_ref, w2_ref, a0_ref, a2_ref, g2_ref, kk_ref, ka_ref, rk_ref, eb_ref, tri_ref,
            rt_ref, kt_ref, bt_ref, kl_ref, v_ref, g_ref, bv_ref, pc_ref):
    r = pb[:, :B_WIDTH]
    k = pb[:, B_WIDTH:2 * B_WIDTH]
    v = pb[:, 2 * B_WIDTH:3 * B_WIDTH]
    o = 3 * B_WIDTH
    xw = pb[:, o:o + W_LORA]
    xa = pb[:, o + W_LORA:o + W_LORA + A_LORA]
    xg = pb[:, o + W_LORA + A_LORA:o + W_LORA + A_LORA + G_LORA]

    z = w0_ref[...] + _dot(jnp.tanh(xw).astype(BF16), w2_ref[...])
    nz = -z
    softplus = jnp.maximum(nz, 0.0) + jnp.log(1.0 + jnp.exp(-jnp.abs(nz)))
    lw = -jnp.exp(-softplus - 0.5)
    a = jax.nn.sigmoid(a0_ref[...] + _dot(xa.astype(BF16), a2_ref[...]))
    g = _dot(jax.nn.sigmoid(xg).astype(BF16), g2_ref[...])
    kk = k * kk_ref[...]
    kkn = kk / jnp.maximum(jnp.sqrt(_seg_dot_hl(kk * kk, eb_ref[...])), 1e-12)
    kp = k * (1.0 + (a - 1.0) * ka_ref[...])
    bonus = _seg_dot_hl(r * kp * rk_ref[...], eb_ref[...])

    lw_hi, lw_lo = _split(lw)
    cum = _dot(tri_ref[...], lw_hi) + _dot(tri_ref[...], lw_lo)
    e_pos = jnp.exp(cum)
    e_neg = jnp.exp(-cum)
    rt_ref[0, rows] = (r * e_pos).astype(BF16)
    kt_ref[0, rows] = (kkn * jnp.exp(cum - lw)).astype(BF16)
    bt_ref[0, rows] = (kkn * a * e_neg).astype(BF16)
    kl_ref[0, rows] = (kp * e_neg).astype(BF16)
    v_ref[0, rows] = v.astype(BF16)
    g_ref[0, rows] = g
    bv_ref[0, rows] = bonus * v
    for c in range(pb.shape[0] // CHUNK):
        pc_ref[0, blk, c:c + 1, :] = e_pos[(c + 1) * CHUNK - 1:(c + 1) * CHUNK, :]


N_FRONT_IN = 6
N_PREP_A_IN = 8
N_PREP_B_IN = 10
N_PREP_A_OUT = 6


def _front_kernel(*refs):
    x_ref, sh_ref, sc_ref, g_ref, w_ref, mu_ref = refs[:N_FRONT_IN]
    a_in = refs[N_FRONT_IN:N_FRONT_IN + N_PREP_A_IN]
    b_in = refs[N_FRONT_IN + N_PREP_A_IN:N_FRONT_IN + N_PREP_A_IN + N_PREP_B_IN]
    outs = refs[N_FRONT_IN + N_PREP_A_IN + N_PREP_B_IN:-1]
    carry_ref = refs[-1]
    j = pl.program_id(1)

    @pl.when(j == 0)
    def _():
        carry_ref[...] = jnp.zeros_like(carry_ref)

    x = x_ref[0]
    y = x * lax.rsqrt(jnp.mean(x * x, axis=-1, keepdims=True) + RMS_EPS) * g_ref[...]
    h = y * (1.0 + sc_ref[0, 0]) + sh_ref[0, 0]
    p = _dot(h.astype(BF16), w_ref[...])
    pb = p[:, N_A_PAD:]
    tm = pb.shape[0]
    row = lax.broadcasted_iota(jnp.int32, (tm, 1), 0)
    prev = jnp.where(row == 0, carry_ref[...], pltpu.roll(pb, 1, axis=0))
    carry_ref[...] = pb[tm - 1:tm, :]
    pb = pb + mu_ref[...] * (prev - pb)
    for blk in range(tm // TOK_TILE):
        rows = slice(blk * TOK_TILE, (blk + 1) * TOK_TILE)
        _prep_a(p[rows, :N_A_PAD], rows, blk, *a_in, *outs[:N_PREP_A_OUT])
        _prep_b(pb[rows], rows, blk, *b_in, *outs[N_PREP_A_OUT:])


def _front_call(x, mod4, g_mix, w_in_p, mu, a_consts, b_consts):
    bsz, s, d = x.shape
    n = w_in_p.shape[1]
    nb = n - N_A_PAD
    tm = FRONT_TILE
    tt = TOK_TILE
    full = lambda arr: pl.BlockSpec(arr.shape, lambda b, j, nd=arr.ndim: (0,) * nd)
    tok = lambda w: pl.BlockSpec((1, tm, w), lambda b, j: (b, j, 0))
    per_tile = lambda r, c: pl.BlockSpec((1, tm // tt, r, c), lambda b, j: (b, j, 0, 0))
    bf = lambda w: jax.ShapeDtypeStruct((bsz, s, w), BF16)
    ff = lambda w: jax.ShapeDtypeStruct((bsz, s, w), F32)
    nt = s // tt
    out_specs = [tok(A_HEADS * KV_LATENT), per_tile(KV_LATENT + ONES_ROWS, tt), tok(A_HEADS * KV_LATENT),
                 pl.BlockSpec((1, IDX_HEADS, tm, IDX_DIM), lambda b, j: (b, 0, j, 0)),
                 tok(IDX_DIM), per_tile(IDX_HEADS, tt)] + [tok(B_WIDTH)] * 7 + [per_tile(tt // CHUNK, B_WIDTH)]
    out_shape = [bf(A_HEADS * KV_LATENT),
                 jax.ShapeDtypeStruct((bsz, nt, KV_LATENT + ONES_ROWS, tt), BF16),
                 bf(A_HEADS * KV_LATENT),
                 jax.ShapeDtypeStruct((bsz, IDX_HEADS, s, IDX_DIM), BF16),
                 bf(IDX_DIM),
                 jax.ShapeDtypeStruct((bsz, nt, IDX_HEADS, tt), F32),
                 bf(B_WIDTH), bf(B_WIDTH), bf(B_WIDTH), bf(B_WIDTH), bf(B_WIDTH), ff(B_WIDTH), ff(B_WIDTH),
                 jax.ShapeDtypeStruct((bsz, nt, tt // CHUNK, B_WIDTH), F32)]
    assert len(a_consts) == N_PREP_A_IN and len(b_consts) == N_PREP_B_IN
    return pl.pallas_call(
        _front_kernel,
        grid=(bsz, s // tm),
        in_specs=[pl.BlockSpec((1, tm, d), lambda b, j: (b, j, 0)),
                  pl.BlockSpec((1, 1, 1, d), lambda b, j: (b, 0, 0, 0)),
                  pl.BlockSpec((1, 1, 1, d), lambda b, j: (b, 1, 0, 0)),
                  full(g_mix),
                  pl.BlockSpec(w_in_p.shape, lambda b, j: (0, 0), pipeline_mode=pl.Buffered(1)),
                  full(mu)] + [full(t) for t in a_consts] + [full(t) for t in b_consts],
        out_specs=out_specs,
        out_shape=out_shape,
        scratch_shapes=[pltpu.VMEM((1, nb), F32)],
        compiler_params=_params(("arbitrary", "arbitrary")),
    )(x, mod4, mod4, g_mix, w_in_p, mu, *a_consts, *b_consts)


def _colsum8(x):
    y = x.reshape(4, K_TILE // 32, 8, Q_TILE)
    return jnp.sum(jnp.sum(y, axis=1), axis=0)


def _colmin8(x):
    y = x.reshape(4, K_TILE // 32, 8, Q_TILE)
    return jnp.min(jnp.min(y, axis=1), axis=0)


def _colmax8(x):
    y = x.reshape(4, K_TILE // 32, 8, Q_TILE)
    return jnp.max(jnp.max(y, axis=1), axis=0)


def _dsa_kernel(topk, qabs_ref, qidx_ref, widx_ref, ckr_ref, cvt_ref, kidx_ref, wuv_ref, lstrict_ref,
                o_ref, score_ref, dist_ref, logit_ref, m_ref, acc_ref):
    i = pl.program_id(1)
    nkc = i + 1
    t0 = i * Q_TILE
    krow = lax.broadcasted_iota(jnp.int32, (K_TILE, 1), 0)
    qcol = lax.broadcasted_iota(jnp.int32, (1, Q_TILE), 1)
    limit = ((t0 + qcol) // CHUNK + 1) * CHUNK
    kp = jnp.minimum(limit, topk).astype(F32)
    rel = (qcol - krow).astype(F32)

    def p1(kc, carry):
        rmin, rmax = carry
        k = kidx_ref[0, pl.ds(pl.multiple_of(kc * K_TILE, K_TILE), K_TILE), :]
        acc = jnp.zeros((K_TILE, Q_TILE), F32)
        for h in range(IDX_HEADS):
            s = _dot_nt(k, qidx_ref[0, h])
            acc = acc + widx_ref[0, 0, h:h + 1, :] * jnp.maximum(s, 0.0)
        adm = (kc * K_TILE + krow) < limit
        score_ref[kc] = jnp.where(adm, acc, -jnp.inf)
        rmin = jnp.minimum(rmin, _colmin8(jnp.where(adm, acc, jnp.inf)))
        rmax = jnp.maximum(rmax, _colmax8(jnp.where(adm, acc, -jnp.inf)))
        return rmin, rmax

    rmin, rmax = lax.fori_loop(
        0, nkc, p1, (jnp.full((8, Q_TILE), jnp.inf, F32), jnp.full((8, Q_TILE), -jnp.inf, F32)))
    lo = jnp.min(rmin, axis=0, keepdims=True)
    hi = jnp.max(rmax, axis=0, keepdims=True)

    def count(pred):
        def body(kc, acc):
            return acc + _colsum8(jnp.where(pred(score_ref[kc]), 1.0, 0.0))
        return jnp.sum(lax.fori_loop(0, nkc, body, jnp.zeros((8, Q_TILE), F32)), axis=0, keepdims=True)

    def probe(c):
        lo, hi, cnt_lo = c
        mid = lo + 0.5 * (hi - lo)
        cnt = count(lambda sc: sc >= mid)
        ge = cnt >= kp
        return jnp.where(ge, mid, lo), jnp.where(ge, hi, mid), jnp.where(ge, cnt, cnt_lo)

    def smallest(pred):
        def body(kc, acc):
            sc = score_ref[kc]
            return jnp.minimum(acc, _colmin8(jnp.where(pred(sc), sc, jnp.inf)))
        return jnp.min(lax.fori_loop(0, nkc, body, jnp.full((8, Q_TILE), jnp.inf, F32)), axis=0, keepdims=True)

    def any_true(x):
        return jnp.max(jnp.where(x, 1.0, 0.0)) > 0.0

    lo, _, cnt_lo = lax.fori_loop(0, SEARCH_PROBES, lambda _, c: probe(c), (lo, hi, limit.astype(F32)))

    def step_up(c):
        it, thr, cnt_gt, cnt_ge = c
        up = cnt_gt >= kp
        thr = jnp.where(up, smallest(lambda sc: sc > thr), thr)
        return it + 1, thr, count(lambda sc: sc > thr), jnp.where(up, cnt_gt, cnt_ge)

    thr = smallest(lambda sc: sc >= lo)
    search = lax.while_loop(
        lambda c: jnp.logical_and(c[0] < nkc * K_TILE, any_true(c[2] >= kp)),
        step_up, (jnp.int32(0), thr, count(lambda sc: sc > thr), cnt_lo))
    thr, need, cnt_ge = search[1], kp - search[2], search[3]

    def dist_tile(kc):
        return jnp.abs(rel + (t0 - kc * K_TILE).astype(F32))

    def sel_plain():
        def body(kc, _):
            dist_ref[kc] = jnp.where(score_ref[kc] >= thr, dist_tile(kc), DIST_BIG)
            return 0
        lax.fori_loop(0, nkc, body, 0)

    def sel_ties():
        def body(kc, run):
            sc = score_ref[kc]
            eq = sc == thr
            eq_f = jnp.where(eq, 1.0, 0.0)
            pre = run + _dot(lstrict_ref[...], eq_f.astype(BF16))
            keep = (sc > thr) | (eq & (pre < need))
            dist_ref[kc] = jnp.where(keep, dist_tile(kc), DIST_BIG)
            return run + jnp.sum(_colsum8(eq_f), axis=0, keepdims=True)
        lax.fori_loop(0, nkc, body, jnp.zeros((1, Q_TILE), F32))

    lax.cond(any_true(cnt_ge != kp), sel_ties, sel_plain)

    m_ref[...] = jnp.full(m_ref.shape, -jnp.inf, F32)
    acc_ref[...] = jnp.zeros(acc_ref.shape, F32)

    def att(kc, _):
        dist = dist_ref[kc]
        m_new = []
        for h in range(A_HEADS):
            slope = 2.0 ** (-8.0 * (h + 1) / A_HEADS) * LOG2E
            ck = ckr_ref[0, pl.ds(pl.multiple_of(kc * K_TILE, K_TILE), K_TILE), h * KV_LATENT:(h + 1) * KV_LATENT]
            logit = _dot_nt(ck, qabs_ref[0, :, h * KV_LATENT:(h + 1) * KV_LATENT]) - slope * dist
            logit_ref[h] = logit
            m_new.append(jnp.maximum(m_ref[h], jnp.max(_colmax8(logit), axis=0, keepdims=True)))
        cv = cvt_ref[0, kc]
        for h in range(A_HEADS):
            p = jnp.exp2(logit_ref[h] - m_new[h])
            acc_ref[h] = acc_ref[h] * jnp.exp2(m_ref[h] - m_new[h]) + _dot(cv, p.astype(BF16))
            m_ref[h] = m_new[h]
        return 0

    lax.fori_loop(0, nkc, att, 0)

    for pair in range(A_HEADS // 2):
        o_pair = []
        for hh in range(2):
            a = acc_ref[2 * pair + hh]
            o_t = a[:KV_LATENT] * (1.0 / a[KV_LATENT:KV_LATENT + 1])
            o_pair.append(o_t.T.astype(BF16))
        o_lat = jnp.concatenate(o_pair, axis=1)
        o_ref[0, :, pair * LANES:(pair + 1) * LANES] = _dot(o_lat, wuv_ref[pair]).astype(o_ref.dtype)


def _dsa_call(topk, qabs, qidx, widx, ckr, cvt, kidx, wuv_pair, lstrict):
    bsz, s, _ = qabs.shape
    nq = s // Q_TILE
    nk = s // K_TILE
    qt = lambda w: pl.BlockSpec((1, Q_TILE, w), lambda b, i: (b, i, 0))
    return pl.pallas_call(
        functools.partial(_dsa_kernel, topk),
        grid=(bsz, nq),
        in_specs=[qt(A_HEADS * KV_LATENT),
                  pl.BlockSpec((1, IDX_HEADS, Q_TILE, IDX_DIM), lambda b, i: (b, 0, i, 0)),
                  pl.BlockSpec((1, 1, IDX_HEADS, Q_TILE), lambda b, i: (b, i, 0, 0)),
                  pl.BlockSpec((1, s, A_HEADS * KV_LATENT), lambda b, i: (b, 0, 0)),
                  pl.BlockSpec((1, nk, KV_LATENT + ONES_ROWS, K_TILE), lambda b, i: (b, 0, 0, 0)),
                  pl.BlockSpec((1, s, IDX_DIM), lambda b, i: (b, 0, 0)),
                  pl.BlockSpec((A_HEADS // 2, 2 * KV_LATENT, LANES), lambda b, i: (0, 0, 0)),
                  pl.BlockSpec((K_TILE, K_TILE), lambda b, i: (0, 0))],
        out_specs=qt(A_WIDTH),
        out_shape=jax.ShapeDtypeStruct((bsz, s, A_WIDTH), BF16),
        scratch_shapes=[pltpu.VMEM((nk, K_TILE, Q_TILE), F32),
                        pltpu.VMEM((nk, K_TILE, Q_TILE), F32),
                        pltpu.VMEM((A_HEADS, K_TILE, Q_TILE), F32),
                        pltpu.VMEM((A_HEADS, 1, Q_TILE), F32),
                        pltpu.VMEM((A_HEADS, KV_LATENT + ONES_ROWS, Q_TILE), F32)],
        compiler_params=_params(("arbitrary", "arbitrary")),
    )(qabs, qidx, widx, ckr, cvt, kidx, wuv_pair, lstrict)


def _rwkv_kernel(rt_ref, kt_ref, bt_ref, kl_ref, v_ref, g_ref, bv_ref, pc_ref, lnw_ref, lnb_ref, eb_ref,
                 o_ref, h_ref):
    j = pl.program_id(1)

    @pl.when(j == 0)
    def _():
        h_ref[...] = jnp.zeros_like(h_ref)

    tm = rt_ref.shape[1]
    nch = tm // CHUNK
    ri = lax.broadcasted_iota(jnp.int32, (tm, tm), 0)
    ci = lax.broadcasted_iota(jnp.int32, (tm, tm), 1)
    same = (ri // CHUNK) == (ci // CHUNK)
    strict = same & (ri > ci)
    incl = same & (ri >= ci)
    eye_t = jnp.where(ri == ci, 1.0, 0.0)
    r2 = lax.broadcasted_iota(jnp.int32, (LANES, LANES), 0)
    c2 = lax.broadcasted_iota(jnp.int32, (LANES, LANES), 1)
    blk = (r2 // B_HEAD_DIM) == (c2 // B_HEAD_DIM)
    diag = r2 == c2
    lane = lax.broadcasted_iota(jnp.int32, (1, LANES), 1)
    zero_b = jnp.zeros((), BF16)

    npair = B_HEADS // 2
    heads = [(p, hh) for p in range(npair) for hh in range(2)]
    head0 = (lane // B_HEAD_DIM) == 0
    rt, kt, bt, kl, v = [], [], [], [], []
    a_ab, a_ak, m_rb, m_rk = [], [], [], []
    for p in range(npair):
        sl = slice(p * LANES, (p + 1) * LANES)
        rt.append(rt_ref[0, :, sl])
        kt.append(kt_ref[0, :, sl])
        bt.append(bt_ref[0, :, sl])
        kl.append(kl_ref[0, :, sl])
        v.append(v_ref[0, :, sl])
        lhs = jnp.concatenate([jnp.where(head0, kt[p], zero_b), jnp.where(head0, zero_b, kt[p]),
                               jnp.where(head0, rt[p], zero_b), jnp.where(head0, zero_b, rt[p])], axis=0)
        prod = _dot_nt(lhs, jnp.concatenate([bt[p], kl[p]], axis=0))
        for hh in range(2):
            a_ab.append(jnp.where(strict, prod[hh * tm:(hh + 1) * tm, :tm], 0.0))
            a_ak.append(jnp.where(strict, prod[hh * tm:(hh + 1) * tm, tm:], 0.0).astype(BF16))
            m_rb.append(jnp.where(incl, prod[(2 + hh) * tm:(3 + hh) * tm, :tm], 0.0).astype(BF16))
            m_rk.append(jnp.where(incl, prod[(2 + hh) * tm:(3 + hh) * tm, tm:], 0.0).astype(BF16))

    t_inv = [(eye_t - a).astype(BF16) for a in a_ab]
    a_pow = [a.astype(BF16) for a in a_ab]
    for _ in range(5):
        a_sq = [_dot(a, a) for a in a_pow]
        a_pow = [a.astype(BF16) for a in a_sq]
        t_inv = [_dot(t, (eye_t + a).astype(BF16)).astype(BF16) for t, a in zip(t_inv, a_sq)]

    avm = [_dot(jnp.concatenate([a_ak[i], m_rk[i]], axis=0), v[p]) for i, (p, _) in enumerate(heads)]
    x = [_dot(t_inv[i], jnp.concatenate([kt[p], avm[i][:tm].astype(BF16)], axis=1))
         for i, (p, _) in enumerate(heads)]
    y = [_dot(m_rb[i], x[i].astype(BF16)) for i in range(len(heads))]

    wu_b, q_b, ol = [], [], []
    for p in range(npair):
        i0, i1 = 2 * p, 2 * p + 1
        head0_2 = jnp.concatenate([head0, head0], axis=1)
        wu_b.append((-jnp.where(head0_2, x[i0], x[i1])).astype(BF16))
        yy = jnp.where(head0_2, y[i0], y[i1])
        q_b.append((rt[p].astype(F32) - yy[:, :LANES]).astype(BF16))
        ol.append(jnp.where(head0, avm[i0][tm:], avm[i1][tm:]) - yy[:, LANES:])

    g_mat, f_mat = [], []
    zeros_b = jnp.zeros((CHUNK, LANES), BF16)
    for p in range(npair):
        sl = slice(p * LANES, (p + 1) * LANES)
        gp, fp = [], []
        for c in range(nch):
            rows = slice(c * CHUNK, (c + 1) * CHUNK)
            pc = pc_ref[0, 0, c:c + 1, sl]
            bh = (bt[p][rows].astype(F32) * pc).astype(BF16)
            kh = (kl[p][rows].astype(F32) * pc).astype(BF16)
            rhs = jnp.concatenate([wu_b[p][rows], jnp.concatenate([zeros_b, v[p][rows]], axis=1)], axis=0)
            bw = _dot_tn(jnp.concatenate([bh, kh], axis=0), rhs)
            gp.append((jnp.where(diag, pc, 0.0) + jnp.where(blk, bw[:, :LANES], 0.0)).astype(BF16))
            fp.append(jnp.where(blk, bw[:, LANES:], 0.0))
        g_mat.append(gp)
        f_mat.append(fp)

    h = [h_ref[p] for p in range(npair)]
    o_chunks = [[] for _ in range(npair)]
    for c in range(nch):
        rows = slice(c * CHUNK, (c + 1) * CHUNK)
        for p in range(npair):
            h_b = h[p].astype(BF16)
            o_chunks[p].append(_dot(q_b[p][rows], h_b) + ol[p][rows])
            h[p] = _dot(g_mat[p][c], h_b) + f_mat[p][c]
    for p in range(npair):
        h_ref[p] = h[p]
    out = jnp.concatenate([jnp.concatenate(oc, axis=0) for oc in o_chunks], axis=1)

    eb = eb_ref[...]
    mean = _seg_dot_hl(out, eb) * (1.0 / B_HEAD_DIM)
    d = out - mean
    var = _seg_dot_hl(d * d, eb) * (1.0 / B_HEAD_DIM)
    y = d * lax.rsqrt(var + GN_EPS) * lnw_ref[...] + lnb_ref[...] + bv_ref[0]
    o_ref[0] = (y * g_ref[0]).astype(o_ref.dtype)


def _rwkv_call(rt, kt, bt, kl, v, g, bv, pc, ln_w, ln_b, eb):
    bsz, s, _ = rt.shape
    tm = TOK_TILE
    tok = pl.BlockSpec((1, tm, B_WIDTH), lambda b, j: (b, j, 0))
    row = pl.BlockSpec((1, B_WIDTH), lambda b, j: (0, 0))
    return pl.pallas_call(
        _rwkv_kernel,
        grid=(bsz, s // tm),
        in_specs=[tok] * 7 + [pl.BlockSpec((1, 1, tm // CHUNK, B_WIDTH), lambda b, j: (b, j, 0, 0)),
                              row, row, pl.BlockSpec((SEG_K, SEG_K), lambda b, j: (0, 0))],
        out_specs=tok,
        out_shape=jax.ShapeDtypeStruct((bsz, s, B_WIDTH), BF16),
        scratch_shapes=[pltpu.VMEM((B_HEADS // 2, LANES, LANES), F32)],
        compiler_params=_params(("arbitrary", "arbitrary")),
    )(rt, kt, bt, kl, v, g, bv, pc, ln_w, ln_b, eb)


def _ffn_kernel(x_ref, oa_ref, ob_ref, gt1_ref, sh2_ref, sc2_ref, gt2_ref, gf_ref, woa_ref, wob_ref,
                w1_ref, w2_ref, o_ref):
    for r0 in range(0, x_ref.shape[1], FFN_ROWS):
        rows = slice(r0, r0 + FFN_ROWS)
        mix = _dot(oa_ref[0, rows], woa_ref[...]) + _dot(ob_ref[0, rows], wob_ref[...])
        x1 = x_ref[0, rows] + gt1_ref[0, 0] * mix
        y = x1 * lax.rsqrt(jnp.mean(x1 * x1, axis=-1, keepdims=True) + RMS_EPS) * gf_ref[...]
        h2 = (y * (1.0 + sc2_ref[0, 0]) + sh2_ref[0, 0]).astype(BF16)
        u = jnp.maximum(_dot(h2, w1_ref[...]), 0.0)
        o_ref[0, rows] = x1 + gt2_ref[0, 0] * _dot((u * u).astype(BF16), w2_ref[...])


def _ffn_call(x, oa, ob, mod4, g_ffn, w_out_a, w_out_b, w1, w2):
    bsz, s, d = x.shape
    dff = w1.shape[1]
    tm = FFN_TILE
    tok = lambda w: pl.BlockSpec((1, tm, w), lambda b, j: (b, j, 0))
    modk = lambda k: pl.BlockSpec((1, 1, 1, d), lambda b, j, k=k: (b, k, 0, 0))
    res = lambda shape: pl.BlockSpec(shape, lambda b, j: (0, 0), pipeline_mode=pl.Buffered(1))
    return pl.pallas_call(
        _ffn_kernel,
        grid=(bsz, s // tm),
        in_specs=[tok(d), tok(A_WIDTH), tok(B_WIDTH), modk(2), modk(3), modk(4), modk(5),
                  pl.BlockSpec((1, d), lambda b, j: (0, 0)),
                  res((A_WIDTH, d)), res((B_WIDTH, d)), res((d, dff)), res((dff, d))],
        out_specs=tok(d),
        out_shape=jax.ShapeDtypeStruct((bsz, s, d), F32),
        compiler_params=_params(("arbitrary", "arbitrary")),
    )(x, oa, ob, mod4, mod4, mod4, mod4, g_ffn, w_out_a, w_out_b, w1, w2)


def _block_ones(n, blk, dtype=BF16):
    i = jnp.arange(n)
    return ((i[:, None] // blk) == (i[None, :] // blk)).astype(dtype)


def kernel(x, c, w_ada, b_ada, g_mix, g_ffn, w_in, g_q, g_k, g_kv, w_uk, w_uv, mu_shift, w0, w2, a0, a2, g2,
           k_k, k_a, r_k, ln_w, ln_b, w_out, w_ff1, w_ff2):
    bsz, s, d = x.shape
    depth = w_ada.shape[0]
    assert s % Q_TILE == 0 and s % FRONT_TILE == 0 and s % FFN_TILE == 0
    topk = min(TOPK_MAX, s // 4)

    eb = _block_ones(SEG_K, B_HEAD_DIM)
    ex = (jnp.arange(2 * A_HEAD_DIM)[:, None] // A_HEAD_DIM == jnp.arange(2 * KV_LATENT)[None, :] // KV_LATENT
          ).astype(BF16)
    sel = (jnp.arange(LANES)[None, :] == IDX_DIM + jnp.arange(IDX_HEADS)[:, None]).astype(BF16)
    eye_l = jnp.eye(KV_LATENT, dtype=BF16)
    ti = jnp.arange(TOK_TILE)
    tri = (((ti[:, None] // CHUNK) == (ti[None, :] // CHUNK)) & (ti[:, None] >= ti[None, :])).astype(BF16)
    ki = jnp.arange(K_TILE)
    lstrict = (ki[None, :] < ki[:, None]).astype(BF16)

    for l in range(depth):
        w_a = jnp.pad(w_in[l][:, :N_IN_A], ((0, 0), (0, N_A_PAD - N_IN_A)))
        w_in_p = jnp.concatenate([w_a, w_in[l][:, N_IN_A:]], axis=1).astype(BF16)
        wuk_flat = w_uk[l].reshape(KV_LATENT, A_WIDTH).astype(BF16)
        wuk_t = jnp.transpose(w_uk[l], (1, 2, 0)).reshape(A_HEADS // 2, 2, A_HEAD_DIM, KV_LATENT)
        wuk_bd = (jnp.eye(2, dtype=F32)[None, :, None, :, None] * wuk_t[:, :, :, None, :]).reshape(
            A_HEADS // 2, 2 * A_HEAD_DIM, 2 * KV_LATENT).astype(BF16)
        wuv_t = jnp.transpose(w_uv[l], (1, 0, 2)).reshape(A_HEADS // 2, 2, KV_LATENT, A_HEAD_DIM)
        wuv_pair = (jnp.eye(2, dtype=F32)[None, :, None, :, None] * wuv_t[:, :, :, None, :]).reshape(
            A_HEADS // 2, 2 * KV_LATENT, 2 * A_HEAD_DIM).astype(BF16)
        gqk = jnp.tile(g_q[l] * g_k[l], A_HEADS).reshape(1, A_WIDTH)
        r1 = lambda t: t.reshape(1, -1)

        mod = _mod_call(c, w_ada[l], b_ada[l])
        mod4 = mod.reshape(bsz, 6, 1, d)
        a_consts = (r1(g_kv[l]), gqk, wuk_flat, wuk_bd, eb, ex, sel, eye_l)
        b_consts = (r1(w0[l]), w2[l].astype(BF16), r1(a0[l]), a2[l].astype(BF16), g2[l].astype(BF16),
                    r1(k_k[l]), r1(k_a[l]), r1(r_k[l]), eb, tri)
        ckr, cvt, qabs, qidx, kidx, widx, rt, kt, bt, kl, v, g, bv, pc = _front_call(
            x, mod4, r1(g_mix[l]), w_in_p, r1(mu_shift[l]), a_consts, b_consts)
        o_a = _dsa_call(topk, qabs, qidx, widx, ckr, cvt, kidx, wuv_pair, lstrict)
        o_b = _rwkv_call(rt, kt, bt, kl, v, g, bv, pc, r1(ln_w[l]), r1(ln_b[l]), eb)
        x = _ffn_call(x, o_a, o_b, mod4, r1(g_ffn[l]), w_out[l][:A_WIDTH].astype(BF16),
                      w_out[l][A_WIDTH:].astype(BF16), w_ff1[l].astype(BF16), w_ff2[l].astype(BF16))
    return x
```

```python
import functools

import jax
import jax.numpy as jnp
from jax import lax
from jax.experimental import pallas as pl
from jax.experimental.pallas import tpu as pltpu

F32 = jnp.float32
BF16 = jnp.bfloat16

CHUNK = 64
A_HEADS = 8
A_HEAD_DIM = 64
A_WIDTH = A_HEADS * A_HEAD_DIM
KV_LATENT = 128
IDX_HEADS = 8
IDX_DIM = 64
TOPK_MAX = 256
B_HEADS = 8
B_HEAD_DIM = 64
B_WIDTH = B_HEADS * B_HEAD_DIM
W_LORA = 64
A_LORA = 64
G_LORA = 128
RMS_EPS = 1e-6
GN_EPS = 64e-5
N_IN_A = A_WIDTH + KV_LATENT + IDX_HEADS * IDX_DIM + IDX_DIM + IDX_HEADS
N_IN_B = 3 * B_WIDTH + W_LORA + A_LORA + G_LORA
N_A_PAD = 1280

LANES = 128
SEG_K = 256
TOK_TILE = 256
FRONT_TILE = 512
Q_TILE = 256
FFN_TILE = 512
FFN_ROWS = 256
K_TILE = 256
DIST_BIG = 1e30
ONES_ROWS = 16
LOG2E = 1.4426950408889634
EXP_RANGE = 90.0
BOUND_MARGIN = 1.02
SEARCH_PROBES = 14
VMEM_LIMIT = 56 * 1024 * 1024


def _dot(a, b):
    return jnp.dot(a, b, preferred_element_type=F32)


def _dot_nt(a, b):
    return lax.dot_general(a, b, (((1,), (1,)), ((), ())), preferred_element_type=F32)


def _dot_tn(a, b):
    return lax.dot_general(a, b, (((0,), (0,)), ((), ())), preferred_element_type=F32)


def _split(x):
    hi = x.astype(BF16)
    lo = (x - hi.astype(F32)).astype(BF16)
    return hi, lo


def _dot_hl(x, e):
    hi, lo = _split(x)
    return _dot(hi, e) + _dot(lo, e)


def _seg_dot_hl(x, e):
    k = e.shape[0]
    return jnp.concatenate([_dot_hl(x[:, j:j + k], e) for j in range(0, x.shape[1], k)], axis=1)


def _params(sem):
    return pltpu.CompilerParams(dimension_semantics=sem, vmem_limit_bytes=VMEM_LIMIT)


def _mod_kernel(c_ref, w_ref, b_ref, o_ref):
    c = c_ref[...]
    s = c * jax.nn.sigmoid(c)
    s_hi, s_lo = _split(s)
    w_hi, w_lo = _split(w_ref[...])
    o_ref[...] = _dot(s_hi, w_hi) + _dot(s_hi, w_lo) + _dot(s_lo, w_hi) + b_ref[...]


def _mod_call(c, w_ada, b_ada):
    bsz, d = c.shape
    n = w_ada.shape[1]
    tn = 1024
    return pl.pallas_call(
        _mod_kernel,
        grid=(n // tn,),
        in_specs=[pl.BlockSpec((bsz, d), lambda j: (0, 0)),
                  pl.BlockSpec((d, tn), lambda j: (0, j)),
                  pl.BlockSpec((1, tn), lambda j: (0, j))],
        out_specs=pl.BlockSpec((bsz, tn), lambda j: (0, j)),
        out_shape=jax.ShapeDtypeStruct((bsz, n), F32),
        compiler_params=_params(("arbitrary",)),
    )(c, w_ada, b_ada.reshape(1, n))


def _prep_a(pa, rows, blk, gkv_ref, gqk_ref, wuk_ref, wukbd_ref, eb_ref, ex_ref, sel_ref, eye_ref,
            ckr_ref, cvt_ref, qabs_ref, qidx_ref, kidx_ref, widx_ref):
    tm = pa.shape[0]
    q = pa[:, :A_WIDTH]
    cl = pa[:, A_WIDTH:A_WIDTH + KV_LATENT]
    o_qi = A_WIDTH + KV_LATENT
    qi = pa[:, o_qi:o_qi + IDX_HEADS * IDX_DIM]
    o_kw = o_qi + IDX_HEADS * IDX_DIM
    kw = pa[:, o_kw:o_kw + LANES]

    ckv = cl * lax.rsqrt(jnp.mean(cl * cl, axis=-1, keepdims=True) + RMS_EPS) * gkv_ref[...]
    ckv_b = ckv.astype(BF16)
    cvt_ref[0, blk, :KV_LATENT, :] = _dot_nt(eye_ref[...], ckv_b).astype(BF16)
    cvt_ref[0, blk, KV_LATENT:, :] = jnp.ones((ONES_ROWS, tm), BF16)
    kf = _dot(ckv_b, wuk_ref[...])
    ss = _seg_dot_hl(kf * kf, ex_ref[...])
    inv_rms = lax.rsqrt(ss * (1.0 / A_HEAD_DIM) + RMS_EPS)
    ckr_ref[0, rows] = (jnp.concatenate([ckv] * A_HEADS, axis=1) * inv_rms).astype(BF16)

    ssq = _seg_dot_hl(q * q, eb_ref[...])
    qh = q * lax.rsqrt(ssq * (1.0 / A_HEAD_DIM) + RMS_EPS) * gqk_ref[...]
    qh_b = qh.astype(BF16)
    for j in range(A_HEADS // 2):
        qabs = _dot(qh_b[:, j * LANES:(j + 1) * LANES], wukbd_ref[j]) * (A_HEAD_DIM ** -0.5 * LOG2E)
        qabs_ref[0, rows, 2 * j * KV_LATENT:2 * (j + 1) * KV_LATENT] = qabs.astype(BF16)
    for h in range(IDX_HEADS):
        qidx_ref[0, h, rows] = qi[:, h * IDX_DIM:(h + 1) * IDX_DIM].astype(BF16)
    kidx_ref[0, rows] = kw[:, :IDX_DIM].astype(BF16)
    kw_hi, kw_lo = _split(kw)
    w_t = _dot_nt(sel_ref[...], kw_hi) + _dot_nt(sel_ref[...], kw_lo)
    widx_ref[0, blk] = w_t * (IDX_HEADS ** -0.5 * IDX_DIM ** -0.5)


def _prep_b(pb, rows, blk, w0_ref, w2_ref, a0_ref, a2_ref, g2_ref, kk_ref, ka_ref, rk_ref, eb_ref, tri_ref,
            rt_ref, kt_ref, bt_ref, kl_ref, v_ref, g_ref, bv_ref, pc_ref):
    r = pb[:, :B_WIDTH]
    k = pb[:, B_WIDTH:2 * B_WIDTH]
    v = pb[:, 2 * B_WIDTH:3 * B_WIDTH]
    o = 3 * B_WIDTH
    xw = pb[:, o:o + W_LORA]
    xa = pb[:, o + W_LORA:o + W_LORA + A_LORA]
    xg = pb[:, o + W_LORA + A_LORA:o + W_LORA + A_LORA + G_LORA]

    z = w0_ref[...] + _dot(jnp.tanh(xw).astype(BF16), w2_ref[...])
    nz = -z
    softplus = jnp.maximum(nz, 0.0) + jnp.log(1.0 + jnp.exp(-jnp.abs(nz)))
    lw = -jnp.exp(-softplus - 0.5)
    a = jax.nn.sigmoid(a0_ref[...] + _dot(xa.astype(BF16), a2_ref[...]))
    g = _dot(jax.nn.sigmoid(xg).astype(BF16), g2_ref[...])
    kk = k * kk_ref[...]
    kkn = kk / jnp.maximum(jnp.sqrt(_seg_dot_hl(kk * kk, eb_ref[...])), 1e-12)
    kp = k * (1.0 + (a - 1.0) * ka_ref[...])
    bonus = _seg_dot_hl(r * kp * rk_ref[...], eb_ref[...])

    lw_hi, lw_lo = _split(lw)
    cum = _dot(tri_ref[...], lw_hi) + _dot(tri_ref[...], lw_lo)
    e_pos = jnp.exp(cum)
    e_neg = jnp.exp(-cum)
    rt_ref[0, rows] = (r * e_pos).astype(BF16)
    kt_ref[0, rows] = (kkn * jnp.exp(cum - lw)).astype(BF16)
    bt_ref[0, rows] = (kkn * a * e_neg).astype(BF16)
    kl_ref[0, rows] = (kp * e_neg).astype(BF16)
    v_ref[0, rows] = v.astype(BF16)
    g_ref[0, rows] = g
    bv_ref[0, rows] = bonus * v
    for c in range(pb.shape[0] // CHUNK):
        pc_ref[0, blk, c:c + 1, :] = e_pos[(c + 1) * CHUNK - 1:(c + 1) * CHUNK, :]


N_FRONT_IN = 6
N_PREP_A_IN = 8
N_PREP_B_IN = 10
N_PREP_A_OUT = 6


def _front_kernel(*refs):
    x_ref, sh_ref, sc_ref, g_ref, w_ref, mu_ref = refs[:N_FRONT_IN]
    a_in = refs[N_FRONT_IN:N_FRONT_IN + N_PREP_A_IN]
    b_in = refs[N_FRONT_IN + N_PREP_A_IN:N_FRONT_IN + N_PREP_A_IN + N_PREP_B_IN]
    outs = refs[N_FRONT_IN + N_PREP_A_IN + N_PREP_B_IN:-1]
    carry_ref = refs[-1]
    j = pl.program_id(1)

    @pl.when(j == 0)
    def _():
        carry_ref[...] = jnp.zeros_like(carry_ref)

    x = x_ref[0]
    y = x * lax.rsqrt(jnp.mean(x * x, axis=-1, keepdims=True) + RMS_EPS) * g_ref[...]
    h = y * (1.0 + sc_ref[0, 0]) + sh_ref[0, 0]
    p = _dot(h.astype(BF16), w_ref[...])
    pb = p[:, N_A_PAD:]
    tm = pb.shape[0]
    row = lax.broadcasted_iota(jnp.int32, (tm, 1), 0)
    prev = jnp.where(row == 0, carry_ref[...], pltpu.roll(pb, 1, axis=0))
    carry_ref[...] = pb[tm - 1:tm, :]
    pb = pb + mu_ref[...] * (prev - pb)
    for blk in range(tm // TOK_TILE):
        rows = slice(blk * TOK_TILE, (blk + 1) * TOK_TILE)
        _prep_a(p[rows, :N_A_PAD], rows, blk, *a_in, *outs[:N_PREP_A_OUT])
        _prep_b(pb[rows], rows, blk, *b_in, *outs[N_PREP_A_OUT:])


def _front_call(x, mod4, g_mix, w_in_p, mu, a_consts, b_consts):
    bsz, s, d = x.shape
    n = w_in_p.shape[1]
    nb = n - N_A_PAD
    tm = FRONT_TILE
    tt = TOK_TILE
    full = lambda arr: pl.BlockSpec(arr.shape, lambda b, j, nd=arr.ndim: (0,) * nd)
    tok = lambda w: pl.BlockSpec((1, tm, w), lambda b, j: (b, j, 0))
    per_tile = lambda r, c: pl.BlockSpec((1, tm // tt, r, c), lambda b, j: (b, j, 0, 0))
    bf = lambda w: jax.ShapeDtypeStruct((bsz, s, w), BF16)
    ff = lambda w: jax.ShapeDtypeStruct((bsz, s, w), F32)
    nt = s // tt
    out_specs = [tok(A_HEADS * KV_LATENT), per_tile(KV_LATENT + ONES_ROWS, tt), tok(A_HEADS * KV_LATENT),
                 pl.BlockSpec((1, IDX_HEADS, tm, IDX_DIM), lambda b, j: (b, 0, j, 0)),
                 tok(IDX_DIM), per_tile(IDX_HEADS, tt)] + [tok(B_WIDTH)] * 7 + [per_tile(tt // CHUNK, B_WIDTH)]
    out_shape = [bf(A_HEADS * KV_LATENT),
                 jax.ShapeDtypeStruct((bsz, nt, KV_LATENT + ONES_ROWS, tt), BF16),
                 bf(A_HEADS * KV_LATENT),
                 jax.ShapeDtypeStruct((bsz, IDX_HEADS, s, IDX_DIM), BF16),
                 bf(IDX_DIM),
                 jax.ShapeDtypeStruct((bsz, nt, IDX_HEADS, tt), F32),
                 bf(B_WIDTH), bf(B_WIDTH), bf(B_WIDTH), bf(B_WIDTH), bf(B_WIDTH), ff(B_WIDTH), ff(B_WIDTH),
                 jax.ShapeDtypeStruct((bsz, nt, tt // CHUNK, B_WIDTH), F32)]
    assert len(a_consts) == N_PREP_A_IN and len(b_consts) == N_PREP_B_IN
    return pl.pallas_call(
        _front_kernel,
        grid=(bsz, s // tm),
        in_specs=[pl.BlockSpec((1, tm, d), lambda b, j: (b, j, 0)),
                  pl.BlockSpec((1, 1, 1, d), lambda b, j: (b, 0, 0, 0)),
                  pl.BlockSpec((1, 1, 1, d), lambda b, j: (b, 1, 0, 0)),
                  full(g_mix),
                  pl.BlockSpec(w_in_p.shape, lambda b, j: (0, 0), pipeline_mode=pl.Buffered(1)),
                  full(mu)] + [full(t) for t in a_consts] + [full(t) for t in b_consts],
        out_specs=out_specs,
        out_shape=out_shape,
        scratch_shapes=[pltpu.VMEM((1, nb), F32)],
        compiler_params=_params(("arbitrary", "arbitrary")),
    )(x, mod4, mod4, g_mix, w_in_p, mu, *a_consts, *b_consts)


def _colsum8(x):
    y = x.reshape(4, K_TILE // 32, 8, Q_TILE)
    return jnp.sum(jnp.sum(y, axis=1), axis=0)


def _colmin8(x):
    y = x.reshape(4, K_TILE // 32, 8, Q_TILE)
    return jnp.min(jnp.min(y, axis=1), axis=0)


def _colmax8(x):
    y = x.reshape(4, K_TILE // 32, 8, Q_TILE)
    return jnp.max(jnp.max(y, axis=1), axis=0)


def _head_norm_max(x, hsel_ref):
    return jnp.sqrt(jnp.max(_dot(x * x, hsel_ref[...])))


def _dsa_kernel(topk, qabs_ref, qidx_ref, widx_ref, ckr_ref, cvt_ref, kidx_ref, wuv_ref, lstrict_ref, hsel_ref,
                o_ref, score_ref, dist_ref, logit_ref, p_ref, m_ref, acc_ref, kmax_ref):
    i = pl.program_id(1)
    nkc = i + 1
    t0 = i * Q_TILE
    krow = lax.broadcasted_iota(jnp.int32, (K_TILE, 1), 0)
    qcol = lax.broadcasted_iota(jnp.int32, (1, Q_TILE), 1)
    limit = ((t0 + qcol) // CHUNK + 1) * CHUNK
    kp = jnp.minimum(limit, topk).astype(F32)
    rel = (qcol - krow).astype(F32)

    def p1(kc, carry):
        rmin, rmax = carry
        k = kidx_ref[0, pl.ds(pl.multiple_of(kc * K_TILE, K_TILE), K_TILE), :]
        acc = jnp.zeros((K_TILE, Q_TILE), F32)
        for h in range(IDX_HEADS):
            s = _dot_nt(k, qidx_ref[0, h])
            acc = acc + widx_ref[0, 0, h:h + 1, :] * jnp.maximum(s, 0.0)
        adm = (kc * K_TILE + krow) < limit
        score_ref[kc] = jnp.where(adm, acc, -jnp.inf)
        rmin = jnp.minimum(rmin, _colmin8(jnp.where(adm, acc, jnp.inf)))
        rmax = jnp.maximum(rmax, _colmax8(jnp.where(adm, acc, -jnp.inf)))
        return rmin, rmax

    rmin, rmax = lax.fori_loop(
        0, nkc, p1, (jnp.full((8, Q_TILE), jnp.inf, F32), jnp.full((8, Q_TILE), -jnp.inf, F32)))
    lo = jnp.min(rmin, axis=0, keepdims=True)
    hi = jnp.max(rmax, axis=0, keepdims=True)

    def count(pred):
        def body(kc, acc):
            return acc + _colsum8(jnp.where(pred(score_ref[kc]), 1.0, 0.0))
        return jnp.sum(lax.fori_loop(0, nkc, body, jnp.zeros((8, Q_TILE), F32)), axis=0, keepdims=True)

    def probe(c):
        lo, hi, cnt_lo = c
        mid = lo + 0.5 * (hi - lo)
        cnt = count(lambda sc: sc >= mid)
        ge = cnt >= kp
        return jnp.where(ge, mid, lo), jnp.where(ge, hi, mid), jnp.where(ge, cnt, cnt_lo)

    def smallest(pred):
        def body(kc, acc):
            sc = score_ref[kc]
            return jnp.minimum(acc, _colmin8(jnp.where(pred(sc), sc, jnp.inf)))
        return jnp.min(lax.fori_loop(0, nkc, body, jnp.full((8, Q_TILE), jnp.inf, F32)), axis=0, keepdims=True)

    def any_true(x):
        return jnp.max(jnp.where(x, 1.0, 0.0)) > 0.0

    lo, _, cnt_lo = lax.fori_loop(0, SEARCH_PROBES, lambda _, c: probe(c), (lo, hi, limit.astype(F32)))

    def step_up(c):
        it, thr, cnt_gt, cnt_ge = c
        up = cnt_gt >= kp
        thr = jnp.where(up, smallest(lambda sc: sc > thr), thr)
        return it + 1, thr, count(lambda sc: sc > thr), jnp.where(up, cnt_gt, cnt_ge)

    thr = smallest(lambda sc: sc >= lo)
    search = lax.while_loop(
        lambda c: jnp.logical_and(c[0] < nkc * K_TILE, any_true(c[2] >= kp)),
        step_up, (jnp.int32(0), thr, count(lambda sc: sc > thr), cnt_lo))
    thr, need, cnt_ge = search[1], kp - search[2], search[3]

    def dist_tile(kc):
        return jnp.abs(rel + (t0 - kc * K_TILE).astype(F32))

    big8 = jnp.full((8, Q_TILE), DIST_BIG, F32)

    def sel_plain():
        def body(kc, near):
            d = jnp.where(score_ref[kc] >= thr, dist_tile(kc), DIST_BIG)
            dist_ref[kc] = d
            return jnp.minimum(near, _colmin8(d))
        return jnp.min(lax.fori_loop(0, nkc, body, big8), axis=0, keepdims=True)

    def sel_ties():
        def body(kc, c):
            run, near = c
            sc = score_ref[kc]
            eq = sc == thr
            eq_f = jnp.where(eq, 1.0, 0.0)
            pre = run + _dot(lstrict_ref[...], eq_f.astype(BF16))
            keep = (sc > thr) | (eq & (pre < need))
            d = jnp.where(keep, dist_tile(kc), DIST_BIG)
            dist_ref[kc] = d
            return run + jnp.sum(_colsum8(eq_f), axis=0, keepdims=True), jnp.minimum(near, _colmin8(d))
        _, near = lax.fori_loop(0, nkc, body, (jnp.zeros((1, Q_TILE), F32), big8))
        return jnp.min(near, axis=0, keepdims=True)

    near = lax.cond(any_true(cnt_ge != kp), sel_ties, sel_plain)

    acc_ref[...] = jnp.zeros(acc_ref.shape, F32)

    @pl.when(i == 0)
    def _():
        kmax = _head_norm_max(ckr_ref[0, :K_TILE], hsel_ref)
        for kc in range(1, ckr_ref.shape[1] // K_TILE):
            kmax = jnp.maximum(kmax, _head_norm_max(ckr_ref[0, kc * K_TILE:(kc + 1) * K_TILE], hsel_ref))
        kmax_ref[0] = kmax

    bound = _head_norm_max(qabs_ref[0], hsel_ref) * kmax_ref[0] * BOUND_MARGIN

    def att_shifted():
        def body(kc, _):
            d = dist_ref[kc] - near
            for h in range(A_HEADS):
                slope = 2.0 ** (-8.0 * (h + 1) / A_HEADS) * LOG2E
                ck = ckr_ref[0, pl.ds(pl.multiple_of(kc * K_TILE, K_TILE), K_TILE),
                             h * KV_LATENT:(h + 1) * KV_LATENT]
                logit = _dot_nt(ck, qabs_ref[0, :, h * KV_LATENT:(h + 1) * KV_LATENT]) - slope * d
                p_ref[h] = jnp.exp2(logit).astype(BF16)
            cv = cvt_ref[0, kc]
            for h in range(A_HEADS):
                acc_ref[h] = acc_ref[h] + _dot(cv, p_ref[h])
            return 0
        lax.fori_loop(0, nkc, body, 0)

    def att_online():
        m_ref[...] = jnp.full(m_ref.shape, -jnp.inf, F32)
        lax.fori_loop(0, nkc, att, 0)

    def att(kc, _):
        dist = dist_ref[kc]
        m_new = []
        for h in range(A_HEADS):
            slope = 2.0 ** (-8.0 * (h + 1) / A_HEADS) * LOG2E
            ck = ckr_ref[0, pl.ds(pl.multiple_of(kc * K_TILE, K_TILE), K_TILE), h * KV_LATENT:(h + 1) * KV_LATENT]
            logit = _dot_nt(ck, qabs_ref[0, :, h * KV_LATENT:(h + 1) * KV_LATENT]) - slope * dist
            logit_ref[h] = logit
            m_new.append(jnp.maximum(m_ref[h], jnp.max(_colmax8(logit), axis=0, keepdims=True)))
        cv = cvt_ref[0, kc]
        for h in range(A_HEADS):
            p = jnp.exp2(logit_ref[h] - m_new[h])
            acc_ref[h] = acc_ref[h] * jnp.exp2(m_ref[h] - m_new[h]) + _dot(cv, p.astype(BF16))
            m_ref[h] = m_new[h]
        return 0

    lax.cond(bound <= EXP_RANGE, att_shifted, att_online)

    for pair in range(A_HEADS // 2):
        o_pair = []
        for hh in range(2):
            a = acc_ref[2 * pair + hh]
            o_t = a[:KV_LATENT] * (1.0 / a[KV_LATENT:KV_LATENT + 1])
            o_pair.append(o_t.T.astype(BF16))
        o_lat = jnp.concatenate(o_pair, axis=1)
        o_ref[0, :, pair * LANES:(pair + 1) * LANES] = _dot(o_lat, wuv_ref[pair]).astype(o_ref.dtype)


def _dsa_call(topk, qabs, qidx, widx, ckr, cvt, kidx, wuv_pair, lstrict, hsel):
    bsz, s, _ = qabs.shape
    nq = s // Q_TILE
    nk = s // K_TILE
    qt = lambda w: pl.BlockSpec((1, Q_TILE, w), lambda b, i: (b, i, 0))
    return pl.pallas_call(
        functools.partial(_dsa_kernel, topk),
        grid=(bsz, nq),
        in_specs=[qt(A_HEADS * KV_LATENT),
                  pl.BlockSpec((1, IDX_HEADS, Q_TILE, IDX_DIM), lambda b, i: (b, 0, i, 0)),
                  pl.BlockSpec((1, 1, IDX_HEADS, Q_TILE), lambda b, i: (b, i, 0, 0)),
                  pl.BlockSpec((1, s, A_HEADS * KV_LATENT), lambda b, i: (b, 0, 0)),
                  pl.BlockSpec((1, nk, KV_LATENT + ONES_ROWS, K_TILE), lambda b, i: (b, 0, 0, 0)),
                  pl.BlockSpec((1, s, IDX_DIM), lambda b, i: (b, 0, 0)),
                  pl.BlockSpec((A_HEADS // 2, 2 * KV_LATENT, LANES), lambda b, i: (0, 0, 0)),
                  pl.BlockSpec((K_TILE, K_TILE), lambda b, i: (0, 0)),
                  pl.BlockSpec((A_HEADS * KV_LATENT, LANES), lambda b, i: (0, 0))],
        out_specs=qt(A_WIDTH),
        out_shape=jax.ShapeDtypeStruct((bsz, s, A_WIDTH), BF16),
        scratch_shapes=[pltpu.VMEM((nk, K_TILE, Q_TILE), F32),
                        pltpu.VMEM((nk, K_TILE, Q_TILE), F32),
                        pltpu.VMEM((A_HEADS, K_TILE, Q_TILE), F32),
                        pltpu.VMEM((A_HEADS, K_TILE, Q_TILE), BF16),
                        pltpu.VMEM((A_HEADS, 1, Q_TILE), F32),
                        pltpu.VMEM((A_HEADS, KV_LATENT + ONES_ROWS, Q_TILE), F32),
                        pltpu.SMEM((1,), F32)],
        compiler_params=_params(("arbitrary", "arbitrary")),
    )(qabs, qidx, widx, ckr, cvt, kidx, wuv_pair, lstrict, hsel)


def _rwkv_kernel(rt_ref, kt_ref, bt_ref, kl_ref, v_ref, g_ref, bv_ref, pc_ref, lnw_ref, lnb_ref, eb_ref,
                 o_ref, h_ref):
    j = pl.program_id(1)

    @pl.when(j == 0)
    def _():
        h_ref[...] = jnp.zeros_like(h_ref)

    tm = rt_ref.shape[1]
    nch = tm // CHUNK
    ri = lax.broadcasted_iota(jnp.int32, (tm, tm), 0)
    ci = lax.broadcasted_iota(jnp.int32, (tm, tm), 1)
    same = (ri // CHUNK) == (ci // CHUNK)
    strict = same & (ri > ci)
    incl = same & (ri >= ci)
    eye_t = jnp.where(ri == ci, 1.0, 0.0)
    r2 = lax.broadcasted_iota(jnp.int32, (LANES, LANES), 0)
    c2 = lax.broadcasted_iota(jnp.int32, (LANES, LANES), 1)
    blk = (r2 // B_HEAD_DIM) == (c2 // B_HEAD_DIM)
    diag = r2 == c2
    lane = lax.broadcasted_iota(jnp.int32, (1, LANES), 1)
    zero_b = jnp.zeros((), BF16)

    npair = B_HEADS // 2
    heads = [(p, hh) for p in range(npair) for hh in range(2)]
    head0 = (lane // B_HEAD_DIM) == 0
    rt, kt, bt, kl, v = [], [], [], [], []
    a_ab, a_ak, m_rb, m_rk = [], [], [], []
    for p in range(npair):
        sl = slice(p * LANES, (p + 1) * LANES)
        rt.append(rt_ref[0, :, sl])
        kt.append(kt_ref[0, :, sl])
        bt.append(bt_ref[0, :, sl])
        kl.append(kl_ref[0, :, sl])
        v.append(v_ref[0, :, sl])
        lhs = jnp.concatenate([jnp.where(head0, kt[p], zero_b), jnp.where(head0, zero_b, kt[p]),
                               jnp.where(head0, rt[p], zero_b), jnp.where(head0, zero_b, rt[p])], axis=0)
        prod = _dot_nt(lhs, jnp.concatenate([bt[p], kl[p]], axis=0))
        for hh in range(2):
            a_ab.append(jnp.where(strict, prod[hh * tm:(hh + 1) * tm, :tm], 0.0))
            a_ak.append(jnp.where(strict, prod[hh * tm:(hh + 1) * tm, tm:], 0.0).astype(BF16))
            m_rb.append(jnp.where(incl, prod[(2 + hh) * tm:(3 + hh) * tm, :tm], 0.0).astype(BF16))
            m_rk.append(jnp.where(incl, prod[(2 + hh) * tm:(3 + hh) * tm, tm:], 0.0).astype(BF16))

    t_inv = [(eye_t - a).astype(BF16) for a in a_ab]
    a_pow = [a.astype(BF16) for a in a_ab]
    for _ in range(5):
        a_sq = [_dot(a, a) for a in a_pow]
        a_pow = [a.astype(BF16) for a in a_sq]
        t_inv = [_dot(t, (eye_t + a).astype(BF16)).astype(BF16) for t, a in zip(t_inv, a_sq)]

    avm = [_dot(jnp.concatenate([a_ak[i], m_rk[i]], axis=0), v[p]) for i, (p, _) in enumerate(heads)]
    x = [_dot(t_inv[i], jnp.concatenate([kt[p], avm[i][:tm].astype(BF16)], axis=1))
         for i, (p, _) in enumerate(heads)]
    y = [_dot(m_rb[i], x[i].astype(BF16)) for i in range(len(heads))]

    wu_b, q_b, ol = [], [], []
    for p in range(npair):
        i0, i1 = 2 * p, 2 * p + 1
        head0_2 = jnp.concatenate([head0, head0], axis=1)
        wu_b.append((-jnp.where(head0_2, x[i0], x[i1])).astype(BF16))
        yy = jnp.where(head0_2, y[i0], y[i1])
        q_b.append((rt[p].astype(F32) - yy[:, :LANES]).astype(BF16))
        ol.append(jnp.where(head0, avm[i0][tm:], avm[i1][tm:]) - yy[:, LANES:])

    g_mat, f_mat = [], []
    zeros_b = jnp.zeros((CHUNK, LANES), BF16)
    for p in range(npair):
        sl = slice(p * LANES, (p + 1) * LANES)
        gp, fp = [], []
        for c in range(nch):
            rows = slice(c * CHUNK, (c + 1) * CHUNK)
            pc = pc_ref[0, 0, c:c + 1, sl]
            bh = (bt[p][rows].astype(F32) * pc).astype(BF16)
            kh = (kl[p][rows].astype(F32) * pc).astype(BF16)
            rhs = jnp.concatenate([wu_b[p][rows], jnp.concatenate([zeros_b, v[p][rows]], axis=1)], axis=0)
            bw = _dot_tn(jnp.concatenate([bh, kh], axis=0), rhs)
            gp.append((jnp.where(diag, pc, 0.0) + jnp.where(blk, bw[:, :LANES], 0.0)).astype(BF16))
            fp.append(jnp.where(blk, bw[:, LANES:], 0.0))
        g_mat.append(gp)
        f_mat.append(fp)

    h = [h_ref[p] for p in range(npair)]
    o_chunks = [[] for _ in range(npair)]
    for c in range(nch):
        rows = slice(c * CHUNK, (c + 1) * CHUNK)
        for p in range(npair):
            h_b = h[p].astype(BF16)
            o_chunks[p].append(_dot(q_b[p][rows], h_b) + ol[p][rows])
            h[p] = _dot(g_mat[p][c], h_b) + f_mat[p][c]
    for p in range(npair):
        h_ref[p] = h[p]
    out = jnp.concatenate([jnp.concatenate(oc, axis=0) for oc in o_chunks], axis=1)

    eb = eb_ref[...]
    mean = _seg_dot_hl(out, eb) * (1.0 / B_HEAD_DIM)
    d = out - mean
    var = _seg_dot_hl(d * d, eb) * (1.0 / B_HEAD_DIM)
    y = d * lax.rsqrt(var + GN_EPS) * lnw_ref[...] + lnb_ref[...] + bv_ref[0]
    o_ref[0] = (y * g_ref[0]).astype(o_ref.dtype)


def _rwkv_call(rt, kt, bt, kl, v, g, bv, pc, ln_w, ln_b, eb):
    bsz, s, _ = rt.shape
    tm = TOK_TILE
    tok = pl.BlockSpec((1, tm, B_WIDTH), lambda b, j: (b, j, 0))
    row = pl.BlockSpec((1, B_WIDTH), lambda b, j: (0, 0))
    return pl.pallas_call(
        _rwkv_kernel,
        grid=(bsz, s // tm),
        in_specs=[tok] * 7 + [pl.BlockSpec((1, 1, tm // CHUNK, B_WIDTH), lambda b, j: (b, j, 0, 0)),
                              row, row, pl.BlockSpec((SEG_K, SEG_K), lambda b, j: (0, 0))],
        out_specs=tok,
        out_shape=jax.ShapeDtypeStruct((bsz, s, B_WIDTH), BF16),
        scratch_shapes=[pltpu.VMEM((B_HEADS // 2, LANES, LANES), F32)],
        compiler_params=_params(("arbitrary", "arbitrary")),
    )(rt, kt, bt, kl, v, g, bv, pc, ln_w, ln_b, eb)


def _ffn_kernel(x_ref, oa_ref, ob_ref, gt1_ref, sh2_ref, sc2_ref, gt2_ref, gf_ref, woa_ref, wob_ref,
                w1_ref, w2_ref, o_ref):
    for r0 in range(0, x_ref.shape[1], FFN_ROWS):
        rows = slice(r0, r0 + FFN_ROWS)
        mix = _dot(oa_ref[0, rows], woa_ref[...]) + _dot(ob_ref[0, rows], wob_ref[...])
        x1 = x_ref[0, rows] + gt1_ref[0, 0] * mix
        y = x1 * lax.rsqrt(jnp.mean(x1 * x1, axis=-1, keepdims=True) + RMS_EPS) * gf_ref[...]
        h2 = (y * (1.0 + sc2_ref[0, 0]) + sh2_ref[0, 0]).astype(BF16)
        u = jnp.maximum(_dot(h2, w1_ref[...]), 0.0)
        o_ref[0, rows] = x1 + gt2_ref[0, 0] * _dot((u * u).astype(BF16), w2_ref[...])


def _ffn_call(x, oa, ob, mod4, g_ffn, w_out_a, w_out_b, w1, w2):
    bsz, s, d = x.shape
    dff = w1.shape[1]
    tm = FFN_TILE
    tok = lambda w: pl.BlockSpec((1, tm, w), lambda b, j: (b, j, 0))
    modk = lambda k: pl.BlockSpec((1, 1, 1, d), lambda b, j, k=k: (b, k, 0, 0))
    res = lambda shape: pl.BlockSpec(shape, lambda b, j: (0, 0), pipeline_mode=pl.Buffered(1))
    return pl.pallas_call(
        _ffn_kernel,
        grid=(bsz, s // tm),
        in_specs=[tok(d), tok(A_WIDTH), tok(B_WIDTH), modk(2), modk(3), modk(4), modk(5),
                  pl.BlockSpec((1, d), lambda b, j: (0, 0)),
                  res((A_WIDTH, d)), res((B_WIDTH, d)), res((d, dff)), res((dff, d))],
        out_specs=tok(d),
        out_shape=jax.ShapeDtypeStruct((bsz, s, d), F32),
        compiler_params=_params(("arbitrary", "arbitrary")),
    )(x, oa, ob, mod4, mod4, mod4, mod4, g_ffn, w_out_a, w_out_b, w1, w2)


def _block_ones(n, blk, dtype=BF16):
    i = jnp.arange(n)
    return ((i[:, None] // blk) == (i[None, :] // blk)).astype(dtype)


def kernel(x, c, w_ada, b_ada, g_mix, g_ffn, w_in, g_q, g_k, g_kv, w_uk, w_uv, mu_shift, w0, w2, a0, a2, g2,
           k_k, k_a, r_k, ln_w, ln_b, w_out, w_ff1, w_ff2):
    bsz, s, d = x.shape
    depth = w_ada.shape[0]
    assert s % Q_TILE == 0 and s % FRONT_TILE == 0 and s % FFN_TILE == 0
    topk = min(TOPK_MAX, s // 4)

    eb = _block_ones(SEG_K, B_HEAD_DIM)
    ex = (jnp.arange(2 * A_HEAD_DIM)[:, None] // A_HEAD_DIM == jnp.arange(2 * KV_LATENT)[None, :] // KV_LATENT
          ).astype(BF16)
    sel = (jnp.arange(LANES)[None, :] == IDX_DIM + jnp.arange(IDX_HEADS)[:, None]).astype(BF16)
    eye_l = jnp.eye(KV_LATENT, dtype=BF16)
    ti = jnp.arange(TOK_TILE)
    tri = (((ti[:, None] // CHUNK) == (ti[None, :] // CHUNK)) & (ti[:, None] >= ti[None, :])).astype(BF16)
    ki = jnp.arange(K_TILE)
    lstrict = (ki[None, :] < ki[:, None]).astype(BF16)
    hsel = (jnp.arange(A_HEADS * KV_LATENT)[:, None] // KV_LATENT == jnp.arange(LANES)[None, :]).astype(BF16)

    for l in range(depth):
        w_a = jnp.pad(w_in[l][:, :N_IN_A], ((0, 0), (0, N_A_PAD - N_IN_A)))
        w_in_p = jnp.concatenate([w_a, w_in[l][:, N_IN_A:]], axis=1).astype(BF16)
        wuk_flat = w_uk[l].reshape(KV_LATENT, A_WIDTH).astype(BF16)
        wuk_t = jnp.transpose(w_uk[l], (1, 2, 0)).reshape(A_HEADS // 2, 2, A_HEAD_DIM, KV_LATENT)
        wuk_bd = (jnp.eye(2, dtype=F32)[None, :, None, :, None] * wuk_t[:, :, :, None, :]).reshape(
            A_HEADS // 2, 2 * A_HEAD_DIM, 2 * KV_LATENT).astype(BF16)
        wuv_t = jnp.transpose(w_uv[l], (1, 0, 2)).reshape(A_HEADS // 2, 2, KV_LATENT, A_HEAD_DIM)
        wuv_pair = (jnp.eye(2, dtype=F32)[None, :, None, :, None] * wuv_t[:, :, :, None, :]).reshape(
            A_HEADS // 2, 2 * KV_LATENT, 2 * A_HEAD_DIM).astype(BF16)
        gqk = jnp.tile(g_q[l] * g_k[l], A_HEADS).reshape(1, A_WIDTH)
        r1 = lambda t: t.reshape(1, -1)

        mod = _mod_call(c, w_ada[l], b_ada[l])
        mod4 = mod.reshape(bsz, 6, 1, d)
        a_consts = (r1(g_kv[l]), gqk, wuk_flat, wuk_bd, eb, ex, sel, eye_l)
        b_consts = (r1(w0[l]), w2[l].astype(BF16), r1(a0[l]), a2[l].astype(BF16), g2[l].astype(BF16),
                    r1(k_k[l]), r1(k_a[l]), r1(r_k[l]), eb, tri)
        ckr, cvt, qabs, qidx, kidx, widx, rt, kt, bt, kl, v, g, bv, pc = _front_call(
            x, mod4, r1(g_mix[l]), w_in_p, r1(mu_shift[l]), a_consts, b_consts)
        o_a = _dsa_call(topk, qabs, qidx, widx, ckr, cvt, kidx, wuv_pair, lstrict, hsel)
        o_b = _rwkv_call(rt, kt, bt, kl, v, g, bv, pc, r1(ln_w[l]), r1(ln_b[l]), eb)
        x = _ffn_call(x, o_a, o_b, mod4, r1(g_ffn[l]), w_out[l][:A_WIDTH].astype(BF16),
                      w_out[l][A_WIDTH:].astype(BF16), w_ff1[l].astype(BF16), w_ff2[l].astype(BF16))
    return x
```

```python
import functools

import jax
import jax.numpy as jnp
from jax import lax
from jax.experimental import pallas as pl
from jax.experimental.pallas import tpu as pltpu

F32 = jnp.float32
BF16 = jnp.bfloat16

CHUNK = 64
A_HEADS = 8
A_HEAD_DIM = 64
A_WIDTH = A_HEADS * A_HEAD_DIM
KV_LATENT = 128
IDX_HEADS = 8
IDX_DIM = 64
TOPK_MAX = 256
B_HEADS = 8
B_HEAD_DIM = 64
B_WIDTH = B_HEADS * B_HEAD_DIM
W_LORA = 64
A_LORA = 64
G_LORA = 128
RMS_EPS = 1e-6
GN_EPS = 64e-5
N_IN_A = A_WIDTH + KV_LATENT + IDX_HEADS * IDX_DIM + IDX_DIM + IDX_HEADS
N_IN_B = 3 * B_WIDTH + W_LORA + A_LORA + G_LORA
N_A_PAD = 1280

LANES = 128
SEG_K = 256
TOK_TILE = 256
FRONT_TILE = 512
Q_TILE = 256
FFN_TILE = 512
FFN_ROWS = 256
K_TILE = 256
DIST_BIG = 1e30
ONES_ROWS = 16
LOG2E = 1.4426950408889634
EXP_RANGE = 90.0
BOUND_MARGIN = 1.02
SEARCH_PROBES = 14
VMEM_LIMIT = 56 * 1024 * 1024


def _dot(a, b):
    return jnp.dot(a, b, preferred_element_type=F32)


def _dot_nt(a, b):
    return lax.dot_general(a, b, (((1,), (1,)), ((), ())), preferred_element_type=F32)


def _dot_tn(a, b):
    return lax.dot_general(a, b, (((0,), (0,)), ((), ())), preferred_element_type=F32)


def _split(x):
    hi = x.astype(BF16)
    lo = (x - hi.astype(F32)).astype(BF16)
    return hi, lo


def _dot_hl(x, e):
    hi, lo = _split(x)
    return _dot(hi, e) + _dot(lo, e)


def _seg_dot_hl(x, e):
    k = e.shape[0]
    return jnp.concatenate([_dot_hl(x[:, j:j + k], e) for j in range(0, x.shape[1], k)], axis=1)


def _params(sem):
    return pltpu.CompilerParams(dimension_semantics=sem, vmem_limit_bytes=VMEM_LIMIT)


def _mod_kernel(c_ref, w_ref, b_ref, o_ref):
    c = c_ref[...]
    s = c * jax.nn.sigmoid(c)
    s_hi, s_lo = _split(s)
    w_hi, w_lo = _split(w_ref[...])
    o_ref[...] = _dot(s_hi, w_hi) + _dot(s_hi, w_lo) + _dot(s_lo, w_hi) + b_ref[...]


def _mod_call(c, w_ada, b_ada):
    bsz, d = c.shape
    n = w_ada.shape[1]
    tn = 1024
    return pl.pallas_call(
        _mod_kernel,
        grid=(n // tn,),
        in_specs=[pl.BlockSpec((bsz, d), lambda j: (0, 0)),
                  pl.BlockSpec((d, tn), lambda j: (0, j)),
                  pl.BlockSpec((1, tn), lambda j: (0, j))],
        out_specs=pl.BlockSpec((bsz, tn), lambda j: (0, j)),
        out_shape=jax.ShapeDtypeStruct((bsz, n), F32),
        compiler_params=_params(("arbitrary",)),
    )(c, w_ada, b_ada.reshape(1, n))


def _prep_a(pa, rows, blk, gkv_ref, gqk_ref, wuk_ref, wukbd_ref, eb_ref, ex_ref, sel_ref, eye_ref,
            ckr_ref, cvt_ref, qabs_ref, qidx_ref, kidx_ref, widx_ref):
    tm = pa.shape[0]
    q = pa[:, :A_WIDTH]
    cl = pa[:, A_WIDTH:A_WIDTH + KV_LATENT]
    o_qi = A_WIDTH + KV_LATENT
    qi = pa[:, o_qi:o_qi + IDX_HEADS * IDX_DIM]
    o_kw = o_qi + IDX_HEADS * IDX_DIM
    kw = pa[:, o_kw:o_kw + LANES]

    ckv = cl * lax.rsqrt(jnp.mean(cl * cl, axis=-1, keepdims=True) + RMS_EPS) * gkv_ref[...]
    ckv_b = ckv.astype(BF16)
    cvt_ref[0, blk, :KV_LATENT, :] = _dot_nt(eye_ref[...], ckv_b).astype(BF16)
    cvt_ref[0, blk, KV_LATENT:, :] = jnp.ones((ONES_ROWS, tm), BF16)
    kf = _dot(ckv_b, wuk_ref[...])
    ss = _seg_dot_hl(kf * kf, ex_ref[...])
    inv_rms = lax.rsqrt(ss * (1.0 / A_HEAD_DIM) + RMS_EPS)
    ckr_ref[0, rows] = (jnp.concatenate([ckv] * A_HEADS, axis=1) * inv_rms).astype(BF16)

    ssq = _seg_dot_hl(q * q, eb_ref[...])
    qh = q * lax.rsqrt(ssq * (1.0 / A_HEAD_DIM) + RMS_EPS) * gqk_ref[...]
    qh_b = qh.astype(BF16)
    for j in range(A_HEADS // 2):
        qabs = _dot(qh_b[:, j * LANES:(j + 1) * LANES], wukbd_ref[j]) * (A_HEAD_DIM ** -0.5 * LOG2E)
        qabs_ref[0, rows, 2 * j * KV_LATENT:2 * (j + 1) * KV_LATENT] = qabs.astype(BF16)
    for h in range(IDX_HEADS):
        qidx_ref[0, h, rows] = qi[:, h * IDX_DIM:(h + 1) * IDX_DIM].astype(BF16)
    kidx_ref[0, rows] = kw[:, :IDX_DIM].astype(BF16)
    kw_hi, kw_lo = _split(kw)
    w_t = _dot_nt(sel_ref[...], kw_hi) + _dot_nt(sel_ref[...], kw_lo)
    widx_ref[0, blk] = w_t * (IDX_HEADS ** -0.5 * IDX_DIM ** -0.5)


def _prep_b(pb, rows, blk, w0_ref, w2_ref, a0_ref, a2_ref, g2_ref, kk_ref, ka_ref, rk_ref, eb_ref, tri_ref,
            rt_ref, kt_ref, bt_ref, kl_ref, v_ref, g_ref, bv_ref, pc_ref):
    r = pb[:, :B_WIDTH]
    k = pb[:, B_WIDTH:2 * B_WIDTH]
    v = pb[:, 2 * B_WIDTH:3 * B_WIDTH]
    o = 3 * B_WIDTH
    xw = pb[:, o:o + W_LORA]
    xa = pb[:, o + W_LORA:o + W_LORA + A_LORA]
    xg = pb[:, o + W_LORA + A_LORA:o + W_LORA + A_LORA + G_LORA]

    z = w0_ref[...] + _dot(jnp.tanh(xw).astype(BF16), w2_ref[...])
    nz = -z
    softplus = jnp.maximum(nz, 0.0) + jnp.log(1.0 + jnp.exp(-jnp.abs(nz)))
    lw = -jnp.exp(-softplus - 0.5)
    a = jax.nn.sigmoid(a0_ref[...] + _dot(xa.astype(BF16), a2_ref[...]))
    g = _dot(jax.nn.sigmoid(xg).astype(BF16), g2_ref[...])
    kk = k * kk_ref[...]
    kkn = kk / jnp.maximum(jnp.sqrt(_seg_dot_hl(kk * kk, eb_ref[...])), 1e-12)
    kp = k * (1.0 + (a - 1.0) * ka_ref[...])
    bonus = _seg_dot_hl(r * kp * rk_ref[...], eb_ref[...])

    lw_hi, lw_lo = _split(lw)
    cum = _dot(tri_ref[...], lw_hi) + _dot(tri_ref[...], lw_lo)
    e_pos = jnp.exp(cum)
    e_neg = jnp.exp(-cum)
    rt_ref[0, rows] = (r * e_pos).astype(BF16)
    kt_ref[0, rows] = (kkn * jnp.exp(cum - lw)).astype(BF16)
    bt_ref[0, rows] = (kkn * a * e_neg).astype(BF16)
    kl_ref[0, rows] = (kp * e_neg).astype(BF16)
    v_ref[0, rows] = v.astype(BF16)
    g_ref[0, rows] = g
    bv_ref[0, rows] = bonus * v
    for c in range(pb.shape[0] // CHUNK):
        pc_ref[0, blk, c:c + 1, :] = e_pos[(c + 1) * CHUNK - 1:(c + 1) * CHUNK, :]


N_FRONT_IN = 6
N_PREP_A_IN = 8
N_PREP_B_IN = 10
N_PREP_A_OUT = 6


def _front_kernel(*refs):
    x_ref, sh_ref, sc_ref, g_ref, w_ref, mu_ref = refs[:N_FRONT_IN]
    a_in = refs[N_FRONT_IN:N_FRONT_IN + N_PREP_A_IN]
    b_in = refs[N_FRONT_IN + N_PREP_A_IN:N_FRONT_IN + N_PREP_A_IN + N_PREP_B_IN]
    outs = refs[N_FRONT_IN + N_PREP_A_IN + N_PREP_B_IN:-1]
    carry_ref = refs[-1]
    j = pl.program_id(1)

    @pl.when(j == 0)
    def _():
        carry_ref[...] = jnp.zeros_like(carry_ref)

    x = x_ref[0]
    y = x * lax.rsqrt(jnp.mean(x * x, axis=-1, keepdims=True) + RMS_EPS) * g_ref[...]
    h = y * (1.0 + sc_ref[0, 0]) + sh_ref[0, 0]
    p = _dot(h.astype(BF16), w_ref[...])
    pb = p[:, N_A_PAD:]
    tm = pb.shape[0]
    row = lax.broadcasted_iota(jnp.int32, (tm, 1), 0)
    prev = jnp.where(row == 0, carry_ref[...], pltpu.roll(pb, 1, axis=0))
    carry_ref[...] = pb[tm - 1:tm, :]
    pb = pb + mu_ref[...] * (prev - pb)
    for blk in range(tm // TOK_TILE):
        rows = slice(blk * TOK_TILE, (blk + 1) * TOK_TILE)
        _prep_a(p[rows, :N_A_PAD], rows, blk, *a_in, *outs[:N_PREP_A_OUT])
        _prep_b(pb[rows], rows, blk, *b_in, *outs[N_PREP_A_OUT:])


def _front_call(x, mod4, g_mix, w_in_p, mu, a_consts, b_consts):
    bsz, s, d = x.shape
    n = w_in_p.shape[1]
    nb = n - N_A_PAD
    tm = FRONT_TILE
    tt = TOK_TILE
    full = lambda arr: pl.BlockSpec(arr.shape, lambda b, j, nd=arr.ndim: (0,) * nd)
    tok = lambda w: pl.BlockSpec((1, tm, w), lambda b, j: (b, j, 0))
    per_tile = lambda r, c: pl.BlockSpec((1, tm // tt, r, c), lambda b, j: (b, j, 0, 0))
    bf = lambda w: jax.ShapeDtypeStruct((bsz, s, w), BF16)
    ff = lambda w: jax.ShapeDtypeStruct((bsz, s, w), F32)
    nt = s // tt
    out_specs = [tok(A_HEADS * KV_LATENT), per_tile(KV_LATENT + ONES_ROWS, tt), tok(A_HEADS * KV_LATENT),
                 pl.BlockSpec((1, IDX_HEADS, tm, IDX_DIM), lambda b, j: (b, 0, j, 0)),
                 tok(IDX_DIM), per_tile(IDX_HEADS, tt)] + [tok(B_WIDTH)] * 7 + [per_tile(tt // CHUNK, B_WIDTH)]
    out_shape = [bf(A_HEADS * KV_LATENT),
                 jax.ShapeDtypeStruct((bsz, nt, KV_LATENT + ONES_ROWS, tt), BF16),
                 bf(A_HEADS * KV_LATENT),
                 jax.ShapeDtypeStruct((bsz, IDX_HEADS, s, IDX_DIM), BF16),
                 bf(IDX_DIM),
                 jax.ShapeDtypeStruct((bsz, nt, IDX_HEADS, tt), F32),
                 bf(B_WIDTH), bf(B_WIDTH), bf(B_WIDTH), bf(B_WIDTH), bf(B_WIDTH), ff(B_WIDTH), ff(B_WIDTH),
                 jax.ShapeDtypeStruct((bsz, nt, tt // CHUNK, B_WIDTH), F32)]
    assert len(a_consts) == N_PREP_A_IN and len(b_consts) == N_PREP_B_IN
    return pl.pallas_call(
        _front_kernel,
        grid=(bsz, s // tm),
        in_specs=[pl.BlockSpec((1, tm, d), lambda b, j: (b, j, 0)),
                  pl.BlockSpec((1, 1, 1, d), lambda b, j: (b, 0, 0, 0)),
                  pl.BlockSpec((1, 1, 1, d), lambda b, j: (b, 1, 0, 0)),
                  full(g_mix),
                  pl.BlockSpec(w_in_p.shape, lambda b, j: (0, 0), pipeline_mode=pl.Buffered(1)),
                  full(mu)] + [full(t) for t in a_consts] + [full(t) for t in b_consts],
        out_specs=out_specs,
        out_shape=out_shape,
        scratch_shapes=[pltpu.VMEM((1, nb), F32)],
        compiler_params=_params(("arbitrary", "arbitrary")),
    )(x, mod4, mod4, g_mix, w_in_p, mu, *a_consts, *b_consts)


def _colsum8(x):
    y = x.reshape(4, K_TILE // 32, 8, Q_TILE)
    return jnp.sum(jnp.sum(y, axis=1), axis=0)


def _colmin8(x):
    y = x.reshape(4, K_TILE // 32, 8, Q_TILE)
    return jnp.min(jnp.min(y, axis=1), axis=0)


def _colmax8(x):
    y = x.reshape(4, K_TILE // 32, 8, Q_TILE)
    return jnp.max(jnp.max(y, axis=1), axis=0)


def _for_key_tiles(nkc, body, init):
    def pair(j, c):
        return body(2 * j + 1, body(2 * j, c))
    c = lax.fori_loop(0, nkc // 2, pair, init)
    return lax.cond(nkc % 2 == 1, lambda c: body(nkc - 1, c), lambda c: c, c)


def _head_norm_max(x, hsel_ref):
    return jnp.sqrt(jnp.max(_dot(x * x, hsel_ref[...])))


def _dsa_kernel(topk, qabs_ref, qidx_ref, widx_ref, ckr_ref, cvt_ref, kidx_ref, wuv_ref, lstrict_ref, hsel_ref,
                o_ref, score_ref, dist_ref, logit_ref, p_ref, m_ref, acc_ref, kmax_ref):
    i = pl.program_id(1)
    nkc = i + 1
    t0 = i * Q_TILE
    krow = lax.broadcasted_iota(jnp.int32, (K_TILE, 1), 0)
    qcol = lax.broadcasted_iota(jnp.int32, (1, Q_TILE), 1)
    limit = ((t0 + qcol) // CHUNK + 1) * CHUNK
    kp = jnp.minimum(limit, topk).astype(F32)
    rel = (qcol - krow).astype(F32)

    def p1(kc, carry):
        rmin, rmax = carry
        k = kidx_ref[0, pl.ds(pl.multiple_of(kc * K_TILE, K_TILE), K_TILE), :]
        acc = jnp.zeros((K_TILE, Q_TILE), F32)
        for h in range(IDX_HEADS):
            s = _dot_nt(k, qidx_ref[0, h])
            acc = acc + widx_ref[0, 0, h:h + 1, :] * jnp.maximum(s, 0.0)
        adm = (kc * K_TILE + krow) < limit
        score_ref[kc] = jnp.where(adm, acc, -jnp.inf)
        rmin = jnp.minimum(rmin, _colmin8(jnp.where(adm, acc, jnp.inf)))
        rmax = jnp.maximum(rmax, _colmax8(jnp.where(adm, acc, -jnp.inf)))
        return rmin, rmax

    rmin, rmax = _for_key_tiles(
        nkc, p1, (jnp.full((8, Q_TILE), jnp.inf, F32), jnp.full((8, Q_TILE), -jnp.inf, F32)))
    lo = jnp.min(rmin, axis=0, keepdims=True)
    hi = jnp.max(rmax, axis=0, keepdims=True)

    def count(pred):
        def body(kc, acc):
            return acc + _colsum8(jnp.where(pred(score_ref[kc]), 1.0, 0.0))
        return jnp.sum(lax.fori_loop(0, nkc, body, jnp.zeros((8, Q_TILE), F32)), axis=0, keepdims=True)

    def probe(c):
        lo, hi, cnt_lo = c
        mid = lo + 0.5 * (hi - lo)
        cnt = count(lambda sc: sc >= mid)
        ge = cnt >= kp
        return jnp.where(ge, mid, lo), jnp.where(ge, hi, mid), jnp.where(ge, cnt, cnt_lo)

    def smallest(pred):
        def body(kc, acc):
            sc = score_ref[kc]
            return jnp.minimum(acc, _colmin8(jnp.where(pred(sc), sc, jnp.inf)))
        return jnp.min(lax.fori_loop(0, nkc, body, jnp.full((8, Q_TILE), jnp.inf, F32)), axis=0, keepdims=True)

    def any_true(x):
        return jnp.max(jnp.where(x, 1.0, 0.0)) > 0.0

    lo, _, cnt_lo = lax.fori_loop(0, SEARCH_PROBES, lambda _, c: probe(c), (lo, hi, limit.astype(F32)))

    def step_up(c):
        it, thr, cnt_gt, cnt_ge = c
        up = cnt_gt >= kp
        thr = jnp.where(up, smallest(lambda sc: sc > thr), thr)
        return it + 1, thr, count(lambda sc: sc > thr), jnp.where(up, cnt_gt, cnt_ge)

    thr = smallest(lambda sc: sc >= lo)
    search = lax.while_loop(
        lambda c: jnp.logical_and(c[0] < nkc * K_TILE, any_true(c[2] >= kp)),
        step_up, (jnp.int32(0), thr, count(lambda sc: sc > thr), cnt_lo))
    thr, need, cnt_ge = search[1], kp - search[2], search[3]

    def dist_tile(kc):
        return jnp.abs(rel + (t0 - kc * K_TILE).astype(F32))

    big8 = jnp.full((8, Q_TILE), DIST_BIG, F32)

    def sel_plain():
        def body(kc, near):
            d = jnp.where(score_ref[kc] >= thr, dist_tile(kc), DIST_BIG)
            dist_ref[kc] = d
            return jnp.minimum(near, _colmin8(d))
        return jnp.min(lax.fori_loop(0, nkc, body, big8), axis=0, keepdims=True)

    def sel_ties():
        def body(kc, c):
            run, near = c
            sc = score_ref[kc]
            eq = sc == thr
            eq_f = jnp.where(eq, 1.0, 0.0)
            pre = run + _dot(lstrict_ref[...], eq_f.astype(BF16))
            keep = (sc > thr) | (eq & (pre < need))
            d = jnp.where(keep, dist_tile(kc), DIST_BIG)
            dist_ref[kc] = d
            return run + jnp.sum(_colsum8(eq_f), axis=0, keepdims=True), jnp.minimum(near, _colmin8(d))
        _, near = lax.fori_loop(0, nkc, body, (jnp.zeros((1, Q_TILE), F32), big8))
        return jnp.min(near, axis=0, keepdims=True)

    near = lax.cond(any_true(cnt_ge != kp), sel_ties, sel_plain)

    acc_ref[...] = jnp.zeros(acc_ref.shape, F32)

    @pl.when(i == 0)
    def _():
        kmax = _head_norm_max(ckr_ref[0, :K_TILE], hsel_ref)
        for kc in range(1, ckr_ref.shape[1] // K_TILE):
            kmax = jnp.maximum(kmax, _head_norm_max(ckr_ref[0, kc * K_TILE:(kc + 1) * K_TILE], hsel_ref))
        kmax_ref[0] = kmax

    bound = _head_norm_max(qabs_ref[0], hsel_ref) * kmax_ref[0] * BOUND_MARGIN

    def att_shifted():
        def body(kc, _):
            d = dist_ref[kc] - near
            for h in range(A_HEADS):
                slope = 2.0 ** (-8.0 * (h + 1) / A_HEADS) * LOG2E
                ck = ckr_ref[0, pl.ds(pl.multiple_of(kc * K_TILE, K_TILE), K_TILE),
                             h * KV_LATENT:(h + 1) * KV_LATENT]
                logit = _dot_nt(ck, qabs_ref[0, :, h * KV_LATENT:(h + 1) * KV_LATENT]) - slope * d
                p_ref[h] = jnp.exp2(logit).astype(BF16)
            cv = cvt_ref[0, kc]
            for h in range(A_HEADS):
                acc_ref[h] = acc_ref[h] + _dot(cv, p_ref[h])
            return 0
        _for_key_tiles(nkc, body, 0)

    def att_online():
        m_ref[...] = jnp.full(m_ref.shape, -jnp.inf, F32)
        _for_key_tiles(nkc, att, 0)

    def att(kc, _):
        dist = dist_ref[kc]
        m_new = []
        for h in range(A_HEADS):
            slope = 2.0 ** (-8.0 * (h + 1) / A_HEADS) * LOG2E
            ck = ckr_ref[0, pl.ds(pl.multiple_of(kc * K_TILE, K_TILE), K_TILE), h * KV_LATENT:(h + 1) * KV_LATENT]
            logit = _dot_nt(ck, qabs_ref[0, :, h * KV_LATENT:(h + 1) * KV_LATENT]) - slope * dist
            logit_ref[h] = logit
            m_new.append(jnp.maximum(m_ref[h], jnp.max(_colmax8(logit), axis=0, keepdims=True)))
        cv = cvt_ref[0, kc]
        for h in range(A_HEADS):
            p = jnp.exp2(logit_ref[h] - m_new[h])
            acc_ref[h] = acc_ref[h] * jnp.exp2(m_ref[h] - m_new[h]) + _dot(cv, p.astype(BF16))
            m_ref[h] = m_new[h]
        return 0

    lax.cond(bound <= EXP_RANGE, att_shifted, att_online)

    for pair in range(A_HEADS // 2):
        o_pair = []
        for hh in range(2):
            a = acc_ref[2 * pair + hh]
            o_t = a[:KV_LATENT] * (1.0 / a[KV_LATENT:KV_LATENT + 1])
            o_pair.append(o_t.T.astype(BF16))
        o_lat = jnp.concatenate(o_pair, axis=1)
        o_ref[0, :, pair * LANES:(pair + 1) * LANES] = _dot(o_lat, wuv_ref[pair]).astype(o_ref.dtype)


def _dsa_call(topk, qabs, qidx, widx, ckr, cvt, kidx, wuv_pair, lstrict, hsel):
    bsz, s, _ = qabs.shape
    nq = s // Q_TILE
    nk = s // K_TILE
    qt = lambda w: pl.BlockSpec((1, Q_TILE, w), lambda b, i: (b, i, 0))
    return pl.pallas_call(
        functools.partial(_dsa_kernel, topk),
        grid=(bsz, nq),
        in_specs=[qt(A_HEADS * KV_LATENT),
                  pl.BlockSpec((1, IDX_HEADS, Q_TILE, IDX_DIM), lambda b, i: (b, 0, i, 0)),
                  pl.BlockSpec((1, 1, IDX_HEADS, Q_TILE), lambda b, i: (b, i, 0, 0)),
                  pl.BlockSpec((1, s, A_HEADS * KV_LATENT), lambda b, i: (b, 0, 0)),
                  pl.BlockSpec((1, nk, KV_LATENT + ONES_ROWS, K_TILE), lambda b, i: (b, 0, 0, 0)),
                  pl.BlockSpec((1, s, IDX_DIM), lambda b, i: (b, 0, 0)),
                  pl.BlockSpec((A_HEADS // 2, 2 * KV_LATENT, LANES), lambda b, i: (0, 0, 0)),
                  pl.BlockSpec((K_TILE, K_TILE), lambda b, i: (0, 0)),
                  pl.BlockSpec((A_HEADS * KV_LATENT, LANES), lambda b, i: (0, 0))],
        out_specs=qt(A_WIDTH),
        out_shape=jax.ShapeDtypeStruct((bsz, s, A_WIDTH), BF16),
        scratch_shapes=[pltpu.VMEM((nk, K_TILE, Q_TILE), F32),
                        pltpu.VMEM((nk, K_TILE, Q_TILE), F32),
                        pltpu.VMEM((A_HEADS, K_TILE, Q_TILE), F32),
                        pltpu.VMEM((A_HEADS, K_TILE, Q_TILE), BF16),
                        pltpu.VMEM((A_HEADS, 1, Q_TILE), F32),
                        pltpu.VMEM((A_HEADS, KV_LATENT + ONES_ROWS, Q_TILE), F32),
                        pltpu.SMEM((1,), F32)],
        compiler_params=_params(("arbitrary", "arbitrary")),
    )(qabs, qidx, widx, ckr, cvt, kidx, wuv_pair, lstrict, hsel)


def _rwkv_kernel(rt_ref, kt_ref, bt_ref, kl_ref, v_ref, g_ref, bv_ref, pc_ref, lnw_ref, lnb_ref, eb_ref,
                 o_ref, h_ref):
    j = pl.program_id(1)

    @pl.when(j == 0)
    def _():
        h_ref[...] = jnp.zeros_like(h_ref)

    tm = rt_ref.shape[1]
    nch = tm // CHUNK
    ri = lax.broadcasted_iota(jnp.int32, (tm, tm), 0)
    ci = lax.broadcasted_iota(jnp.int32, (tm, tm), 1)
    same = (ri // CHUNK) == (ci // CHUNK)
    strict = same & (ri > ci)
    incl = same & (ri >= ci)
    eye_t = jnp.where(ri == ci, 1.0, 0.0)
    r2 = lax.broadcasted_iota(jnp.int32, (LANES, LANES), 0)
    c2 = lax.broadcasted_iota(jnp.int32, (LANES, LANES), 1)
    blk = (r2 // B_HEAD_DIM) == (c2 // B_HEAD_DIM)
    diag = r2 == c2
    lane = lax.broadcasted_iota(jnp.int32, (1, LANES), 1)
    zero_b = jnp.zeros((), BF16)

    npair = B_HEADS // 2
    heads = [(p, hh) for p in range(npair) for hh in range(2)]
    head0 = (lane // B_HEAD_DIM) == 0
    rt, kt, bt, kl, v = [], [], [], [], []
    a_ab, a_ak, m_rb, m_rk = [], [], [], []
    for p in range(npair):
        sl = slice(p * LANES, (p + 1) * LANES)
        rt.append(rt_ref[0, :, sl])
        kt.append(kt_ref[0, :, sl])
        bt.append(bt_ref[0, :, sl])
        kl.append(kl_ref[0, :, sl])
        v.append(v_ref[0, :, sl])
        lhs = jnp.concatenate([jnp.where(head0, kt[p], zero_b), jnp.where(head0, zero_b, kt[p]),
                               jnp.where(head0, rt[p], zero_b), jnp.where(head0, zero_b, rt[p])], axis=0)
        prod = _dot_nt(lhs, jnp.concatenate([bt[p], kl[p]], axis=0))
        for hh in range(2):
            a_ab.append(jnp.where(strict, prod[hh * tm:(hh + 1) * tm, :tm], 0.0))
            a_ak.append(jnp.where(strict, prod[hh * tm:(hh + 1) * tm, tm:], 0.0).astype(BF16))
            m_rb.append(jnp.where(incl, prod[(2 + hh) * tm:(3 + hh) * tm, :tm], 0.0).astype(BF16))
            m_rk.append(jnp.where(incl, prod[(2 + hh) * tm:(3 + hh) * tm, tm:], 0.0).astype(BF16))

    t_inv = [(eye_t - a).astype(BF16) for a in a_ab]
    a_pow = [a.astype(BF16) for a in a_ab]
    for _ in range(5):
        a_sq = [_dot(a, a) for a in a_pow]
        a_pow = [a.astype(BF16) for a in a_sq]
        t_inv = [_dot(t, (eye_t + a).astype(BF16)).astype(BF16) for t, a in zip(t_inv, a_sq)]

    avm = [_dot(jnp.concatenate([a_ak[i], m_rk[i]], axis=0), v[p]) for i, (p, _) in enumerate(heads)]
    x = [_dot(t_inv[i], jnp.concatenate([kt[p], avm[i][:tm].astype(BF16)], axis=1))
         for i, (p, _) in enumerate(heads)]
    y = [_dot(m_rb[i], x[i].astype(BF16)) for i in range(len(heads))]

    wu_b, q_b, ol = [], [], []
    for p in range(npair):
        i0, i1 = 2 * p, 2 * p + 1
        head0_2 = jnp.concatenate([head0, head0], axis=1)
        wu_b.append((-jnp.where(head0_2, x[i0], x[i1])).astype(BF16))
        yy = jnp.where(head0_2, y[i0], y[i1])
        q_b.append((rt[p].astype(F32) - yy[:, :LANES]).astype(BF16))
        ol.append(jnp.where(head0, avm[i0][tm:], avm[i1][tm:]) - yy[:, LANES:])

    g_mat, f_mat = [], []
    zeros_b = jnp.zeros((CHUNK, LANES), BF16)
    for p in range(npair):
        sl = slice(p * LANES, (p + 1) * LANES)
        gp, fp = [], []
        for c in range(nch):
            rows = slice(c * CHUNK, (c + 1) * CHUNK)
            pc = pc_ref[0, 0, c:c + 1, sl]
            bh = (bt[p][rows].astype(F32) * pc).astype(BF16)
            kh = (kl[p][rows].astype(F32) * pc).astype(BF16)
            rhs = jnp.concatenate([wu_b[p][rows], jnp.concatenate([zeros_b, v[p][rows]], axis=1)], axis=0)
            bw = _dot_tn(jnp.concatenate([bh, kh], axis=0), rhs)
            gp.append((jnp.where(diag, pc, 0.0) + jnp.where(blk, bw[:, :LANES], 0.0)).astype(BF16))
            fp.append(jnp.where(blk, bw[:, LANES:], 0.0))
        g_mat.append(gp)
        f_mat.append(fp)

    h = [h_ref[p] for p in range(npair)]
    o_chunks = [[] for _ in range(npair)]
    for c in range(nch):
        rows = slice(c * CHUNK, (c + 1) * CHUNK)
        for p in range(npair):
            h_b = h[p].astype(BF16)
            o_chunks[p].append(_dot(q_b[p][rows], h_b) + ol[p][rows])
            h[p] = _dot(g_mat[p][c], h_b) + f_mat[p][c]
    for p in range(npair):
        h_ref[p] = h[p]
    out = jnp.concatenate([jnp.concatenate(oc, axis=0) for oc in o_chunks], axis=1)

    eb = eb_ref[...]
    mean = _seg_dot_hl(out, eb) * (1.0 / B_HEAD_DIM)
    d = out - mean
    var = _seg_dot_hl(d * d, eb) * (1.0 / B_HEAD_DIM)
    y = d * lax.rsqrt(var + GN_EPS) * lnw_ref[...] + lnb_ref[...] + bv_ref[0]
    o_ref[0] = (y * g_ref[0]).astype(o_ref.dtype)


def _rwkv_call(rt, kt, bt, kl, v, g, bv, pc, ln_w, ln_b, eb):
    bsz, s, _ = rt.shape
    tm = TOK_TILE
    tok = pl.BlockSpec((1, tm, B_WIDTH), lambda b, j: (b, j, 0))
    row = pl.BlockSpec((1, B_WIDTH), lambda b, j: (0, 0))
    return pl.pallas_call(
        _rwkv_kernel,
        grid=(bsz, s // tm),
        in_specs=[tok] * 7 + [pl.BlockSpec((1, 1, tm // CHUNK, B_WIDTH), lambda b, j: (b, j, 0, 0)),
                              row, row, pl.BlockSpec((SEG_K, SEG_K), lambda b, j: (0, 0))],
        out_specs=tok,
        out_shape=jax.ShapeDtypeStruct((bsz, s, B_WIDTH), BF16),
        scratch_shapes=[pltpu.VMEM((B_HEADS // 2, LANES, LANES), F32)],
        compiler_params=_params(("arbitrary", "arbitrary")),
    )(rt, kt, bt, kl, v, g, bv, pc, ln_w, ln_b, eb)


def _ffn_kernel(x_ref, oa_ref, ob_ref, gt1_ref, sh2_ref, sc2_ref, gt2_ref, gf_ref, woa_ref, wob_ref,
                w1_ref, w2_ref, o_ref):
    for r0 in range(0, x_ref.shape[1], FFN_ROWS):
        rows = slice(r0, r0 + FFN_ROWS)
        mix = _dot(oa_ref[0, rows], woa_ref[...]) + _dot(ob_ref[0, rows], wob_ref[...])
        x1 = x_ref[0, rows] + gt1_ref[0, 0] * mix
        y = x1 * lax.rsqrt(jnp.mean(x1 * x1, axis=-1, keepdims=True) + RMS_EPS) * gf_ref[...]
        h2 = (y * (1.0 + sc2_ref[0, 0]) + sh2_ref[0, 0]).astype(BF16)
        u = jnp.maximum(_dot(h2, w1_ref[...]), 0.0)
        o_ref[0, rows] = x1 + gt2_ref[0, 0] * _dot((u * u).astype(BF16), w2_ref[...])


def _ffn_call(x, oa, ob, mod4, g_ffn, w_out_a, w_out_b, w1, w2):
    bsz, s, d = x.shape
    dff = w1.shape[1]
    tm = FFN_TILE
    tok = lambda w: pl.BlockSpec((1, tm, w), lambda b, j: (b, j, 0))
    modk = lambda k: pl.BlockSpec((1, 1, 1, d), lambda b, j, k=k: (b, k, 0, 0))
    res = lambda shape: pl.BlockSpec(shape, lambda b, j: (0, 0), pipeline_mode=pl.Buffered(1))
    return pl.pallas_call(
        _ffn_kernel,
        grid=(bsz, s // tm),
        in_specs=[tok(d), tok(A_WIDTH), tok(B_WIDTH), modk(2), modk(3), modk(4), modk(5),
                  pl.BlockSpec((1, d), lambda b, j: (0, 0)),
                  res((A_WIDTH, d)), res((B_WIDTH, d)), res((d, dff)), res((dff, d))],
        out_specs=tok(d),
        out_shape=jax.ShapeDtypeStruct((bsz, s, d), F32),
        compiler_params=_params(("arbitrary", "arbitrary")),
    )(x, oa, ob, mod4, mod4, mod4, mod4, g_ffn, w_out_a, w_out_b, w1, w2)


def _block_ones(n, blk, dtype=BF16):
    i = jnp.arange(n)
    return ((i[:, None] // blk) == (i[None, :] // blk)).astype(dtype)


def kernel(x, c, w_ada, b_ada, g_mix, g_ffn, w_in, g_q, g_k, g_kv, w_uk, w_uv, mu_shift, w0, w2, a0, a2, g2,
           k_k, k_a, r_k, ln_w, ln_b, w_out, w_ff1, w_ff2):
    bsz, s, d = x.shape
    depth = w_ada.shape[0]
    assert s % Q_TILE == 0 and s % FRONT_TILE == 0 and s % FFN_TILE == 0
    topk = min(TOPK_MAX, s // 4)

    eb = _block_ones(SEG_K, B_HEAD_DIM)
    ex = (jnp.arange(2 * A_HEAD_DIM)[:, None] // A_HEAD_DIM == jnp.arange(2 * KV_LATENT)[None, :] // KV_LATENT
          ).astype(BF16)
    sel = (jnp.arange(LANES)[None, :] == IDX_DIM + jnp.arange(IDX_HEADS)[:, None]).astype(BF16)
    eye_l = jnp.eye(KV_LATENT, dtype=BF16)
    ti = jnp.arange(TOK_TILE)
    tri = (((ti[:, None] // CHUNK) == (ti[None, :] // CHUNK)) & (ti[:, None] >= ti[None, :])).astype(BF16)
    ki = jnp.arange(K_TILE)
    lstrict = (ki[None, :] < ki[:, None]).astype(BF16)
    hsel = (jnp.arange(A_HEADS * KV_LATENT)[:, None] // KV_LATENT == jnp.arange(LANES)[None, :]).astype(BF16)

    for l in range(depth):
        w_a = jnp.pad(w_in[l][:, :N_IN_A], ((0, 0), (0, N_A_PAD - N_IN_A)))
        w_in_p = jnp.concatenate([w_a, w_in[l][:, N_IN_A:]], axis=1).astype(BF16)
        wuk_flat = w_uk[l].reshape(KV_LATENT, A_WIDTH).astype(BF16)
        wuk_t = jnp.transpose(w_uk[l], (1, 2, 0)).reshape(A_HEADS // 2, 2, A_HEAD_DIM, KV_LATENT)
        wuk_bd = (jnp.eye(2, dtype=F32)[None, :, None, :, None] * wuk_t[:, :, :, None, :]).reshape(
            A_HEADS // 2, 2 * A_HEAD_DIM, 2 * KV_LATENT).astype(BF16)
        wuv_t = jnp.transpose(w_uv[l], (1, 0, 2)).reshape(A_HEADS // 2, 2, KV_LATENT, A_HEAD_DIM)
        wuv_pair = (jnp.eye(2, dtype=F32)[None, :, None, :, None] * wuv_t[:, :, :, None, :]).reshape(
            A_HEADS // 2, 2 * KV_LATENT, 2 * A_HEAD_DIM).astype(BF16)
        gqk = jnp.tile(g_q[l] * g_k[l], A_HEADS).reshape(1, A_WIDTH)
        r1 = lambda t: t.reshape(1, -1)

        mod = _mod_call(c, w_ada[l], b_ada[l])
        mod4 = mod.reshape(bsz, 6, 1, d)
        a_consts = (r1(g_kv[l]), gqk, wuk_flat, wuk_bd, eb, ex, sel, eye_l)
        b_consts = (r1(w0[l]), w2[l].astype(BF16), r1(a0[l]), a2[l].astype(BF16), g2[l].astype(BF16),
                    r1(k_k[l]), r1(k_a[l]), r1(r_k[l]), eb, tri)
        ckr, cvt, qabs, qidx, kidx, widx, rt, kt, bt, kl, v, g, bv, pc = _front_call(
            x, mod4, r1(g_mix[l]), w_in_p, r1(mu_shift[l]), a_consts, b_consts)
        o_a = _dsa_call(topk, qabs, qidx, widx, ckr, cvt, kidx, wuv_pair, lstrict, hsel)
        o_b = _rwkv_call(rt, kt, bt, kl, v, g, bv, pc, r1(ln_w[l]), r1(ln_b[l]), eb)
        x = _ffn_call(x, o_a, o_b, mod4, r1(g_ffn[l]), w_out[l][:A_WIDTH].astype(BF16),
                      w_out[l][A_WIDTH:].astype(BF16), w_ff1[l].astype(BF16), w_ff2[l].astype(BF16))
    return x
```

```python
import functools

import jax
import jax.numpy as jnp
from jax import lax
from jax.experimental import pallas as pl
from jax.experimental.pallas import tpu as pltpu

F32 = jnp.float32
BF16 = jnp.bfloat16

CHUNK = 64
A_HEADS = 8
A_HEAD_DIM = 64
A_WIDTH = A_HEADS * A_HEAD_DIM
KV_LATENT = 128
IDX_HEADS = 8
IDX_DIM = 64
TOPK_MAX = 256
B_HEADS = 8
B_HEAD_DIM = 64
B_WIDTH = B_HEADS * B_HEAD_DIM
W_LORA = 64
A_LORA = 64
G_LORA = 128
RMS_EPS = 1e-6
GN_EPS = 64e-5
N_IN_A = A_WIDTH + KV_LATENT + IDX_HEADS * IDX_DIM + IDX_DIM + IDX_HEADS
N_IN_B = 3 * B_WIDTH + W_LORA + A_LORA + G_LORA
N_A_PAD = 1280

LANES = 128
SEG_K = 256
TOK_TILE = 256
FRONT_TILE = 512
RWKV_TILE = 512
Q_TILE = 256
FFN_TILE = 512
FFN_ROWS = 256
K_TILE = 256
DIST_BIG = 1e30
ONES_ROWS = 16
LOG2E = 1.4426950408889634
EXP_RANGE = 90.0
BOUND_MARGIN = 1.02
SEARCH_PROBES = 14
VMEM_LIMIT = 56 * 1024 * 1024


def _dot(a, b):
    return jnp.dot(a, b, preferred_element_type=F32)


def _dot_nt(a, b):
    return lax.dot_general(a, b, (((1,), (1,)), ((), ())), preferred_element_type=F32)


def _dot_tn(a, b):
    return lax.dot_general(a, b, (((0,), (0,)), ((), ())), preferred_element_type=F32)


def _split(x):
    hi = x.astype(BF16)
    lo = (x - hi.astype(F32)).astype(BF16)
    return hi, lo


def _dot_hl(x, e):
    hi, lo = _split(x)
    return _dot(hi, e) + _dot(lo, e)


def _seg_dot_hl(x, e):
    k = e.shape[0]
    return jnp.concatenate([_dot_hl(x[:, j:j + k], e) for j in range(0, x.shape[1], k)], axis=1)


def _params(sem):
    return pltpu.CompilerParams(dimension_semantics=sem, vmem_limit_bytes=VMEM_LIMIT)


def _mod_kernel(c_ref, w_ref, b_ref, o_ref):
    c = c_ref[...]
    s = c * jax.nn.sigmoid(c)
    s_hi, s_lo = _split(s)
    w_hi, w_lo = _split(w_ref[...])
    o_ref[...] = _dot(s_hi, w_hi) + _dot(s_hi, w_lo) + _dot(s_lo, w_hi) + b_ref[...]


def _mod_call(c, w_ada, b_ada):
    bsz, d = c.shape
    n = w_ada.shape[1]
    tn = 1024
    return pl.pallas_call(
        _mod_kernel,
        grid=(n // tn,),
        in_specs=[pl.BlockSpec((bsz, d), lambda j: (0, 0)),
                  pl.BlockSpec((d, tn), lambda j: (0, j)),
                  pl.BlockSpec((1, tn), lambda j: (0, j))],
        out_specs=pl.BlockSpec((bsz, tn), lambda j: (0, j)),
        out_shape=jax.ShapeDtypeStruct((bsz, n), F32),
        compiler_params=_params(("arbitrary",)),
    )(c, w_ada, b_ada.reshape(1, n))


def _prep_a(pa, rows, blk, gkv_ref, gqk_ref, wuk_ref, wukbd_ref, eb_ref, ex_ref, sel_ref, eye_ref,
            ckr_ref, cvt_ref, qabs_ref, qidx_ref, kidx_ref, widx_ref):
    tm = pa.shape[0]
    q = pa[:, :A_WIDTH]
    cl = pa[:, A_WIDTH:A_WIDTH + KV_LATENT]
    o_qi = A_WIDTH + KV_LATENT
    qi = pa[:, o_qi:o_qi + IDX_HEADS * IDX_DIM]
    o_kw = o_qi + IDX_HEADS * IDX_DIM
    kw = pa[:, o_kw:o_kw + LANES]

    ckv = cl * lax.rsqrt(jnp.mean(cl * cl, axis=-1, keepdims=True) + RMS_EPS) * gkv_ref[...]
    ckv_b = ckv.astype(BF16)
    cvt_ref[0, blk, :KV_LATENT, :] = _dot_nt(eye_ref[...], ckv_b).astype(BF16)
    cvt_ref[0, blk, KV_LATENT:, :] = jnp.ones((ONES_ROWS, tm), BF16)
    kf = _dot(ckv_b, wuk_ref[...])
    ss = _seg_dot_hl(kf * kf, ex_ref[...])
    inv_rms = lax.rsqrt(ss * (1.0 / A_HEAD_DIM) + RMS_EPS)
    ckr_ref[0, rows] = (jnp.concatenate([ckv] * A_HEADS, axis=1) * inv_rms).astype(BF16)

    ssq = _seg_dot_hl(q * q, eb_ref[...])
    qh = q * lax.rsqrt(ssq * (1.0 / A_HEAD_DIM) + RMS_EPS) * gqk_ref[...]
    qh_b = qh.astype(BF16)
    for j in range(A_HEADS // 2):
        qabs = _dot(qh_b[:, j * LANES:(j + 1) * LANES], wukbd_ref[j]) * (A_HEAD_DIM ** -0.5 * LOG2E)
        qabs_ref[0, rows, 2 * j * KV_LATENT:2 * (j + 1) * KV_LATENT] = qabs.astype(BF16)
    for h in range(IDX_HEADS):
        qidx_ref[0, h, rows] = qi[:, h * IDX_DIM:(h + 1) * IDX_DIM].astype(BF16)
    kidx_ref[0, rows] = kw[:, :IDX_DIM].astype(BF16)
    kw_hi, kw_lo = _split(kw)
    w_t = _dot_nt(sel_ref[...], kw_hi) + _dot_nt(sel_ref[...], kw_lo)
    widx_ref[0, blk] = w_t * (IDX_HEADS ** -0.5 * IDX_DIM ** -0.5)


def _prep_b(pb, rows, blk, w0_ref, w2_ref, a0_ref, a2_ref, g2_ref, kk_ref, ka_ref, rk_ref, eb_ref, tri_ref,
            rt_ref, kt_ref, bt_ref, kl_ref, v_ref, g_ref, bv_ref, pc_ref):
    r = pb[:, :B_WIDTH]
    k = pb[:, B_WIDTH:2 * B_WIDTH]
    v = pb[:, 2 * B_WIDTH:3 * B_WIDTH]
    o = 3 * B_WIDTH
    xw = pb[:, o:o + W_LORA]
    xa = pb[:, o + W_LORA:o + W_LORA + A_LORA]
    xg = pb[:, o + W_LORA + A_LORA:o + W_LORA + A_LORA + G_LORA]

    z = w0_ref[...] + _dot(jnp.tanh(xw).astype(BF16), w2_ref[...])
    nz = -z
    softplus = jnp.maximum(nz, 0.0) + jnp.log(1.0 + jnp.exp(-jnp.abs(nz)))
    lw = -jnp.exp(-softplus - 0.5)
    a = jax.nn.sigmoid(a0_ref[...] + _dot(xa.astype(BF16), a2_ref[...]))
    g = _dot(jax.nn.sigmoid(xg).astype(BF16), g2_ref[...])
    kk = k * kk_ref[...]
    kkn = kk / jnp.maximum(jnp.sqrt(_seg_dot_hl(kk * kk, eb_ref[...])), 1e-12)
    kp = k * (1.0 + (a - 1.0) * ka_ref[...])
    bonus = _seg_dot_hl(r * kp * rk_ref[...], eb_ref[...])

    lw_hi, lw_lo = _split(lw)
    cum = _dot(tri_ref[...], lw_hi) + _dot(tri_ref[...], lw_lo)
    e_pos = jnp.exp(cum)
    e_neg = jnp.exp(-cum)
    rt_ref[0, rows] = (r * e_pos).astype(BF16)
    kt_ref[0, rows] = (kkn * jnp.exp(cum - lw)).astype(BF16)
    bt_ref[0, rows] = (kkn * a * e_neg).astype(BF16)
    kl_ref[0, rows] = (kp * e_neg).astype(BF16)
    v_ref[0, rows] = v.astype(BF16)
    g_ref[0, rows] = g
    bv_ref[0, rows] = bonus * v
    for c in range(pb.shape[0] // CHUNK):
        pc_ref[0, blk, c:c + 1, :] = e_pos[(c + 1) * CHUNK - 1:(c + 1) * CHUNK, :]


N_FRONT_IN = 6
N_PREP_A_IN = 8
N_PREP_B_IN = 10
N_PREP_A_OUT = 6


def _front_kernel(*refs):
    x_ref, sh_ref, sc_ref, g_ref, w_ref, mu_ref = refs[:N_FRONT_IN]
    a_in = refs[N_FRONT_IN:N_FRONT_IN + N_PREP_A_IN]
    b_in = refs[N_FRONT_IN + N_PREP_A_IN:N_FRONT_IN + N_PREP_A_IN + N_PREP_B_IN]
    outs = refs[N_FRONT_IN + N_PREP_A_IN + N_PREP_B_IN:-1]
    carry_ref = refs[-1]
    j = pl.program_id(1)

    @pl.when(j == 0)
    def _():
        carry_ref[...] = jnp.zeros_like(carry_ref)

    x = x_ref[0]
    y = x * lax.rsqrt(jnp.mean(x * x, axis=-1, keepdims=True) + RMS_EPS) * g_ref[...]
    h = y * (1.0 + sc_ref[0, 0]) + sh_ref[0, 0]
    p = _dot(h.astype(BF16), w_ref[...])
    pb = p[:, N_A_PAD:]
    tm = pb.shape[0]
    row = lax.broadcasted_iota(jnp.int32, (tm, 1), 0)
    prev = jnp.where(row == 0, carry_ref[...], pltpu.roll(pb, 1, axis=0))
    carry_ref[...] = pb[tm - 1:tm, :]
    pb = pb + mu_ref[...] * (prev - pb)
    for blk in range(tm // TOK_TILE):
        rows = slice(blk * TOK_TILE, (blk + 1) * TOK_TILE)
        _prep_a(p[rows, :N_A_PAD], rows, blk, *a_in, *outs[:N_PREP_A_OUT])
        _prep_b(pb[rows], rows, blk, *b_in, *outs[N_PREP_A_OUT:])


def _front_call(x, mod4, g_mix, w_in_p, mu, a_consts, b_consts):
    bsz, s, d = x.shape
    n = w_in_p.shape[1]
    nb = n - N_A_PAD
    tm = FRONT_TILE
    tt = TOK_TILE
    full = lambda arr: pl.BlockSpec(arr.shape, lambda b, j, nd=arr.ndim: (0,) * nd)
    tok = lambda w: pl.BlockSpec((1, tm, w), lambda b, j: (b, j, 0))
    per_tile = lambda r, c: pl.BlockSpec((1, tm // tt, r, c), lambda b, j: (b, j, 0, 0))
    bf = lambda w: jax.ShapeDtypeStruct((bsz, s, w), BF16)
    ff = lambda w: jax.ShapeDtypeStruct((bsz, s, w), F32)
    nt = s // tt
    out_specs = [tok(A_HEADS * KV_LATENT), per_tile(KV_LATENT + ONES_ROWS, tt), tok(A_HEADS * KV_LATENT),
                 pl.BlockSpec((1, IDX_HEADS, tm, IDX_DIM), lambda b, j: (b, 0, j, 0)),
                 tok(IDX_DIM), per_tile(IDX_HEADS, tt)] + [tok(B_WIDTH)] * 7 + [per_tile(tt // CHUNK, B_WIDTH)]
    out_shape = [bf(A_HEADS * KV_LATENT),
                 jax.ShapeDtypeStruct((bsz, nt, KV_LATENT + ONES_ROWS, tt), BF16),
                 bf(A_HEADS * KV_LATENT),
                 jax.ShapeDtypeStruct((bsz, IDX_HEADS, s, IDX_DIM), BF16),
                 bf(IDX_DIM),
                 jax.ShapeDtypeStruct((bsz, nt, IDX_HEADS, tt), F32),
                 bf(B_WIDTH), bf(B_WIDTH), bf(B_WIDTH), bf(B_WIDTH), bf(B_WIDTH), ff(B_WIDTH), ff(B_WIDTH),
                 jax.ShapeDtypeStruct((bsz, nt, tt // CHUNK, B_WIDTH), F32)]
    assert len(a_consts) == N_PREP_A_IN and len(b_consts) == N_PREP_B_IN
    return pl.pallas_call(
        _front_kernel,
        grid=(bsz, s // tm),
        in_specs=[pl.BlockSpec((1, tm, d), lambda b, j: (b, j, 0)),
                  pl.BlockSpec((1, 1, 1, d), lambda b, j: (b, 0, 0, 0)),
                  pl.BlockSpec((1, 1, 1, d), lambda b, j: (b, 1, 0, 0)),
                  full(g_mix),
                  pl.BlockSpec(w_in_p.shape, lambda b, j: (0, 0), pipeline_mode=pl.Buffered(1)),
                  full(mu)] + [full(t) for t in a_consts] + [full(t) for t in b_consts],
        out_specs=out_specs,
        out_shape=out_shape,
        scratch_shapes=[pltpu.VMEM((1, nb), F32)],
        compiler_params=_params(("arbitrary", "arbitrary")),
    )(x, mod4, mod4, g_mix, w_in_p, mu, *a_consts, *b_consts)


def _colsum8(x):
    y = x.reshape(4, K_TILE // 32, 8, Q_TILE)
    return jnp.sum(jnp.sum(y, axis=1), axis=0)


def _colmin8(x):
    y = x.reshape(4, K_TILE // 32, 8, Q_TILE)
    return jnp.min(jnp.min(y, axis=1), axis=0)


def _colmax8(x):
    y = x.reshape(4, K_TILE // 32, 8, Q_TILE)
    return jnp.max(jnp.max(y, axis=1), axis=0)


def _for_key_tiles(nkc, body, init):
    def pair(j, c):
        return body(2 * j + 1, body(2 * j, c))
    c = lax.fori_loop(0, nkc // 2, pair, init)
    return lax.cond(nkc % 2 == 1, lambda c: body(nkc - 1, c), lambda c: c, c)


def _head_norm_max(x, hsel_ref):
    return jnp.sqrt(jnp.max(_dot(x * x, hsel_ref[...])))


def _dsa_kernel(topk, qabs_ref, qidx_ref, widx_ref, ckr_ref, cvt_ref, kidx_ref, wuv_ref, lstrict_ref, hsel_ref,
                o_ref, score_ref, dist_ref, logit_ref, p_ref, m_ref, acc_ref, kmax_ref):
    i = pl.program_id(1)
    nkc = i + 1
    t0 = i * Q_TILE
    krow = lax.broadcasted_iota(jnp.int32, (K_TILE, 1), 0)
    qcol = lax.broadcasted_iota(jnp.int32, (1, Q_TILE), 1)
    limit = ((t0 + qcol) // CHUNK + 1) * CHUNK
    kp = jnp.minimum(limit, topk).astype(F32)
    rel = (qcol - krow).astype(F32)

    def p1(kc, carry):
        rmin, rmax = carry
        k = kidx_ref[0, pl.ds(pl.multiple_of(kc * K_TILE, K_TILE), K_TILE), :]
        acc = jnp.zeros((K_TILE, Q_TILE), F32)
        for h in range(IDX_HEADS):
            s = _dot_nt(k, qidx_ref[0, h])
            acc = acc + widx_ref[0, 0, h:h + 1, :] * jnp.maximum(s, 0.0)
        adm = (kc * K_TILE + krow) < limit
        score_ref[kc] = jnp.where(adm, acc, -jnp.inf)
        rmin = jnp.minimum(rmin, _colmin8(jnp.where(adm, acc, jnp.inf)))
        rmax = jnp.maximum(rmax, _colmax8(jnp.where(adm, acc, -jnp.inf)))
        return rmin, rmax

    rmin, rmax = _for_key_tiles(
        nkc, p1, (jnp.full((8, Q_TILE), jnp.inf, F32), jnp.full((8, Q_TILE), -jnp.inf, F32)))
    lo = jnp.min(rmin, axis=0, keepdims=True)
    hi = jnp.max(rmax, axis=0, keepdims=True)

    def count(pred):
        def body(kc, acc):
            return acc + _colsum8(jnp.where(pred(score_ref[kc]), 1.0, 0.0))
        return jnp.sum(lax.fori_loop(0, nkc, body, jnp.zeros((8, Q_TILE), F32)), axis=0, keepdims=True)

    def probe(c):
        lo, hi, cnt_lo = c
        mid = lo + 0.5 * (hi - lo)
        cnt = count(lambda sc: sc >= mid)
        ge = cnt >= kp
        return jnp.where(ge, mid, lo), jnp.where(ge, hi, mid), jnp.where(ge, cnt, cnt_lo)

    def smallest(pred):
        def body(kc, acc):
            sc = score_ref[kc]
            return jnp.minimum(acc, _colmin8(jnp.where(pred(sc), sc, jnp.inf)))
        return jnp.min(lax.fori_loop(0, nkc, body, jnp.full((8, Q_TILE), jnp.inf, F32)), axis=0, keepdims=True)

    def any_true(x):
        return jnp.max(jnp.where(x, 1.0, 0.0)) > 0.0

    lo, _, cnt_lo = lax.fori_loop(0, SEARCH_PROBES, lambda _, c: probe(c), (lo, hi, limit.astype(F32)))

    def step_up(c):
        it, thr, cnt_gt, cnt_ge = c
        up = cnt_gt >= kp
        thr = jnp.where(up, smallest(lambda sc: sc > thr), thr)
        return it + 1, thr, count(lambda sc: sc > thr), jnp.where(up, cnt_gt, cnt_ge)

    thr = smallest(lambda sc: sc >= lo)
    search = lax.while_loop(
        lambda c: jnp.logical_and(c[0] < nkc * K_TILE, any_true(c[2] >= kp)),
        step_up, (jnp.int32(0), thr, count(lambda sc: sc > thr), cnt_lo))
    thr, need, cnt_ge = search[1], kp - search[2], search[3]

    def dist_tile(kc):
        return jnp.abs(rel + (t0 - kc * K_TILE).astype(F32))

    big8 = jnp.full((8, Q_TILE), DIST_BIG, F32)

    def sel_plain():
        def body(kc, near):
            d = jnp.where(score_ref[kc] >= thr, dist_tile(kc), DIST_BIG)
            dist_ref[kc] = d
            return jnp.minimum(near, _colmin8(d))
        return jnp.min(lax.fori_loop(0, nkc, body, big8), axis=0, keepdims=True)

    def sel_ties():
        def body(kc, c):
            run, near = c
            sc = score_ref[kc]
            eq = sc == thr
            eq_f = jnp.where(eq, 1.0, 0.0)
            pre = run + _dot(lstrict_ref[...], eq_f.astype(BF16))
            keep = (sc > thr) | (eq & (pre < need))
            d = jnp.where(keep, dist_tile(kc), DIST_BIG)
            dist_ref[kc] = d
            return run + jnp.sum(_colsum8(eq_f), axis=0, keepdims=True), jnp.minimum(near, _colmin8(d))
        _, near = _for_key_tiles(nkc, body, (jnp.zeros((1, Q_TILE), F32), big8))
        return jnp.min(near, axis=0, keepdims=True)

    near = lax.cond(any_true(cnt_ge != kp), sel_ties, sel_plain)

    acc_ref[...] = jnp.zeros(acc_ref.shape, F32)

    @pl.when(i == 0)
    def _():
        kmax = _head_norm_max(ckr_ref[0, :K_TILE], hsel_ref)
        for kc in range(1, ckr_ref.shape[1] // K_TILE):
            kmax = jnp.maximum(kmax, _head_norm_max(ckr_ref[0, kc * K_TILE:(kc + 1) * K_TILE], hsel_ref))
        kmax_ref[0] = kmax

    bound = _head_norm_max(qabs_ref[0], hsel_ref) * kmax_ref[0] * BOUND_MARGIN

    def att_shifted():
        def body(kc, _):
            d = dist_ref[kc] - near
            for h in range(A_HEADS):
                slope = 2.0 ** (-8.0 * (h + 1) / A_HEADS) * LOG2E
                ck = ckr_ref[0, pl.ds(pl.multiple_of(kc * K_TILE, K_TILE), K_TILE),
                             h * KV_LATENT:(h + 1) * KV_LATENT]
                logit = _dot_nt(ck, qabs_ref[0, :, h * KV_LATENT:(h + 1) * KV_LATENT]) - slope * d
                p_ref[h] = jnp.exp2(logit).astype(BF16)
            cv = cvt_ref[0, kc]
            for h in range(A_HEADS):
                acc_ref[h] = acc_ref[h] + _dot(cv, p_ref[h])
            return 0
        _for_key_tiles(nkc, body, 0)

    def att_online():
        m_ref[...] = jnp.full(m_ref.shape, -jnp.inf, F32)
        _for_key_tiles(nkc, att, 0)

    def att(kc, _):
        dist = dist_ref[kc]
        m_new = []
        for h in range(A_HEADS):
            slope = 2.0 ** (-8.0 * (h + 1) / A_HEADS) * LOG2E
            ck = ckr_ref[0, pl.ds(pl.multiple_of(kc * K_TILE, K_TILE), K_TILE), h * KV_LATENT:(h + 1) * KV_LATENT]
            logit = _dot_nt(ck, qabs_ref[0, :, h * KV_LATENT:(h + 1) * KV_LATENT]) - slope * dist
            logit_ref[h] = logit
            m_new.append(jnp.maximum(m_ref[h], jnp.max(_colmax8(logit), axis=0, keepdims=True)))
        cv = cvt_ref[0, kc]
        for h in range(A_HEADS):
            p = jnp.exp2(logit_ref[h] - m_new[h])
            acc_ref[h] = acc_ref[h] * jnp.exp2(m_ref[h] - m_new[h]) + _dot(cv, p.astype(BF16))
            m_ref[h] = m_new[h]
        return 0

    lax.cond(bound <= EXP_RANGE, att_shifted, att_online)

    for pair in range(A_HEADS // 2):
        o_pair = []
        for hh in range(2):
            a = acc_ref[2 * pair + hh]
            o_t = a[:KV_LATENT] * (1.0 / a[KV_LATENT:KV_LATENT + 1])
            o_pair.append(o_t.T.astype(BF16))
        o_lat = jnp.concatenate(o_pair, axis=1)
        o_ref[0, :, pair * LANES:(pair + 1) * LANES] = _dot(o_lat, wuv_ref[pair]).astype(o_ref.dtype)


def _dsa_call(topk, qabs, qidx, widx, ckr, cvt, kidx, wuv_pair, lstrict, hsel):
    bsz, s, _ = qabs.shape
    nq = s // Q_TILE
    nk = s // K_TILE
    qt = lambda w: pl.BlockSpec((1, Q_TILE, w), lambda b, i: (b, i, 0))
    return pl.pallas_call(
        functools.partial(_dsa_kernel, topk),
        grid=(bsz, nq),
        in_specs=[qt(A_HEADS * KV_LATENT),
                  pl.BlockSpec((1, IDX_HEADS, Q_TILE, IDX_DIM), lambda b, i: (b, 0, i, 0)),
                  pl.BlockSpec((1, 1, IDX_HEADS, Q_TILE), lambda b, i: (b, i, 0, 0)),
                  pl.BlockSpec((1, s, A_HEADS * KV_LATENT), lambda b, i: (b, 0, 0)),
                  pl.BlockSpec((1, nk, KV_LATENT + ONES_ROWS, K_TILE), lambda b, i: (b, 0, 0, 0)),
                  pl.BlockSpec((1, s, IDX_DIM), lambda b, i: (b, 0, 0)),
                  pl.BlockSpec((A_HEADS // 2, 2 * KV_LATENT, LANES), lambda b, i: (0, 0, 0)),
                  pl.BlockSpec((K_TILE, K_TILE), lambda b, i: (0, 0)),
                  pl.BlockSpec((A_HEADS * KV_LATENT, LANES), lambda b, i: (0, 0))],
        out_specs=qt(A_WIDTH),
        out_shape=jax.ShapeDtypeStruct((bsz, s, A_WIDTH), BF16),
        scratch_shapes=[pltpu.VMEM((nk, K_TILE, Q_TILE), F32),
                        pltpu.VMEM((nk, K_TILE, Q_TILE), F32),
                        pltpu.VMEM((A_HEADS, K_TILE, Q_TILE), F32),
                        pltpu.VMEM((A_HEADS, K_TILE, Q_TILE), BF16),
                        pltpu.VMEM((A_HEADS, 1, Q_TILE), F32),
                        pltpu.VMEM((A_HEADS, KV_LATENT + ONES_ROWS, Q_TILE), F32),
                        pltpu.SMEM((1,), F32)],
        compiler_params=_params(("arbitrary", "arbitrary")),
    )(qabs, qidx, widx, ckr, cvt, kidx, wuv_pair, lstrict, hsel)


def _rwkv_block(blk, tm, masks, rt_ref, kt_ref, bt_ref, kl_ref, v_ref, pc_ref):
    strict, incl, eye_t, blk_diag, diag, head0 = masks
    nch = tm // CHUNK
    npair = B_HEADS // 2
    rows_b = slice(blk * tm, (blk + 1) * tm)
    zero_b = jnp.zeros((), BF16)
    heads = [(p, hh) for p in range(npair) for hh in range(2)]
    rt, kt, bt, kl, v = [], [], [], [], []
    a_ab, a_ak, m_rb, m_rk = [], [], [], []
    for p in range(npair):
        sl = slice(p * LANES, (p + 1) * LANES)
        rt.append(rt_ref[0, rows_b, sl])
        kt.append(kt_ref[0, rows_b, sl])
        bt.append(bt_ref[0, rows_b, sl])
        kl.append(kl_ref[0, rows_b, sl])
        v.append(v_ref[0, rows_b, sl])
        lhs = jnp.concatenate([jnp.where(head0, kt[p], zero_b), jnp.where(head0, zero_b, kt[p]),
                               jnp.where(head0, rt[p], zero_b), jnp.where(head0, zero_b, rt[p])], axis=0)
        prod = _dot_nt(lhs, jnp.concatenate([bt[p], kl[p]], axis=0))
        for hh in range(2):
            a_ab.append(jnp.where(strict, prod[hh * tm:(hh + 1) * tm, :tm], 0.0))
            a_ak.append(jnp.where(strict, prod[hh * tm:(hh + 1) * tm, tm:], 0.0).astype(BF16))
            m_rb.append(jnp.where(incl, prod[(2 + hh) * tm:(3 + hh) * tm, :tm], 0.0).astype(BF16))
            m_rk.append(jnp.where(incl, prod[(2 + hh) * tm:(3 + hh) * tm, tm:], 0.0).astype(BF16))

    t_inv = [(eye_t - a).astype(BF16) for a in a_ab]
    a_pow = [a.astype(BF16) for a in a_ab]
    for _ in range(5):
        a_sq = [_dot(a, a) for a in a_pow]
        a_pow = [a.astype(BF16) for a in a_sq]
        t_inv = [_dot(t, (eye_t + a).astype(BF16)).astype(BF16) for t, a in zip(t_inv, a_sq)]

    avm = [_dot(jnp.concatenate([a_ak[i], m_rk[i]], axis=0), v[p]) for i, (p, _) in enumerate(heads)]
    x = [_dot(t_inv[i], jnp.concatenate([kt[p], avm[i][:tm].astype(BF16)], axis=1))
         for i, (p, _) in enumerate(heads)]
    y = [_dot(m_rb[i], x[i].astype(BF16)) for i in range(len(heads))]

    head0_2 = jnp.concatenate([head0, head0], axis=1)
    zeros_b = jnp.zeros((CHUNK, LANES), BF16)
    q_b, ol, g_mat, f_mat = [], [], [], []
    for p in range(npair):
        i0, i1 = 2 * p, 2 * p + 1
        sl = slice(p * LANES, (p + 1) * LANES)
        wu_b = (-jnp.where(head0_2, x[i0], x[i1])).astype(BF16)
        yy = jnp.where(head0_2, y[i0], y[i1])
        q_b.append((rt[p].astype(F32) - yy[:, :LANES]).astype(BF16))
        ol.append(jnp.where(head0, avm[i0][tm:], avm[i1][tm:]) - yy[:, LANES:])
        gp, fp = [], []
        for c in range(nch):
            rows = slice(c * CHUNK, (c + 1) * CHUNK)
            pc = pc_ref[0, blk, c:c + 1, sl]
            bh = (bt[p][rows].astype(F32) * pc).astype(BF16)
            kh = (kl[p][rows].astype(F32) * pc).astype(BF16)
            rhs = jnp.concatenate([wu_b[rows], jnp.concatenate([zeros_b, v[p][rows]], axis=1)], axis=0)
            bw = _dot_tn(jnp.concatenate([bh, kh], axis=0), rhs)
            gp.append((jnp.where(diag, pc, 0.0) + jnp.where(blk_diag, bw[:, :LANES], 0.0)).astype(BF16))
            fp.append(jnp.where(blk_diag, bw[:, LANES:], 0.0))
        g_mat.append(gp)
        f_mat.append(fp)
    return q_b, ol, g_mat, f_mat


def _rwkv_kernel(rt_ref, kt_ref, bt_ref, kl_ref, v_ref, g_ref, bv_ref, pc_ref, lnw_ref, lnb_ref, eb_ref,
                 o_ref, h_ref):
    j = pl.program_id(1)

    @pl.when(j == 0)
    def _():
        h_ref[...] = jnp.zeros_like(h_ref)

    tm = TOK_TILE
    nch = tm // CHUNK
    npair = B_HEADS // 2
    ri = lax.broadcasted_iota(jnp.int32, (tm, tm), 0)
    ci = lax.broadcasted_iota(jnp.int32, (tm, tm), 1)
    same = (ri // CHUNK) == (ci // CHUNK)
    r2 = lax.broadcasted_iota(jnp.int32, (LANES, LANES), 0)
    c2 = lax.broadcasted_iota(jnp.int32, (LANES, LANES), 1)
    lane = lax.broadcasted_iota(jnp.int32, (1, LANES), 1)
    masks = (same & (ri > ci), same & (ri >= ci), jnp.where(ri == ci, 1.0, 0.0),
             (r2 // B_HEAD_DIM) == (c2 // B_HEAD_DIM), r2 == c2, (lane // B_HEAD_DIM) == 0)

    nblk = rt_ref.shape[1] // tm
    blocks = [_rwkv_block(b, tm, masks, rt_ref, kt_ref, bt_ref, kl_ref, v_ref, pc_ref) for b in range(nblk)]

    h = [h_ref[p] for p in range(npair)]
    for b, (q_b, ol, g_mat, f_mat) in enumerate(blocks):
        o_chunks = [[] for _ in range(npair)]
        for c in range(nch):
            rows = slice(c * CHUNK, (c + 1) * CHUNK)
            for p in range(npair):
                h_b = h[p].astype(BF16)
                o_chunks[p].append(_dot(q_b[p][rows], h_b) + ol[p][rows])
                h[p] = _dot(g_mat[p][c], h_b) + f_mat[p][c]
        out = jnp.concatenate([jnp.concatenate(oc, axis=0) for oc in o_chunks], axis=1)

        rows_b = slice(b * tm, (b + 1) * tm)
        eb = eb_ref[...]
        mean = _seg_dot_hl(out, eb) * (1.0 / B_HEAD_DIM)
        d = out - mean
        var = _seg_dot_hl(d * d, eb) * (1.0 / B_HEAD_DIM)
        y = d * lax.rsqrt(var + GN_EPS) * lnw_ref[...] + lnb_ref[...] + bv_ref[0, rows_b]
        o_ref[0, rows_b] = (y * g_ref[0, rows_b]).astype(o_ref.dtype)
    for p in range(npair):
        h_ref[p] = h[p]


def _rwkv_call(rt, kt, bt, kl, v, g, bv, pc, ln_w, ln_b, eb):
    bsz, s, _ = rt.shape
    tm = RWKV_TILE
    tok = pl.BlockSpec((1, tm, B_WIDTH), lambda b, j: (b, j, 0))
    row = pl.BlockSpec((1, B_WIDTH), lambda b, j: (0, 0))
    return pl.pallas_call(
        _rwkv_kernel,
        grid=(bsz, s // tm),
        in_specs=[tok] * 7 + [pl.BlockSpec((1, tm // TOK_TILE, TOK_TILE // CHUNK, B_WIDTH), lambda b, j: (b, j, 0, 0)),
                              row, row, pl.BlockSpec((SEG_K, SEG_K), lambda b, j: (0, 0))],
        out_specs=tok,
        out_shape=jax.ShapeDtypeStruct((bsz, s, B_WIDTH), BF16),
        scratch_shapes=[pltpu.VMEM((B_HEADS // 2, LANES, LANES), F32)],
        compiler_params=_params(("arbitrary", "arbitrary")),
    )(rt, kt, bt, kl, v, g, bv, pc, ln_w, ln_b, eb)


def _ffn_kernel(x_ref, oa_ref, ob_ref, gt1_ref, sh2_ref, sc2_ref, gt2_ref, gf_ref, woa_ref, wob_ref,
                w1_ref, w2_ref, o_ref):
    for r0 in range(0, x_ref.shape[1], FFN_ROWS):
        rows = slice(r0, r0 + FFN_ROWS)
        mix = _dot(oa_ref[0, rows], woa_ref[...]) + _dot(ob_ref[0, rows], wob_ref[...])
        x1 = x_ref[0, rows] + gt1_ref[0, 0] * mix
        y = x1 * lax.rsqrt(jnp.mean(x1 * x1, axis=-1, keepdims=True) + RMS_EPS) * gf_ref[...]
        h2 = (y * (1.0 + sc2_ref[0, 0]) + sh2_ref[0, 0]).astype(BF16)
        u = jnp.maximum(_dot(h2, w1_ref[...]), 0.0)
        o_ref[0, rows] = x1 + gt2_ref[0, 0] * _dot((u * u).astype(BF16), w2_ref[...])


def _ffn_call(x, oa, ob, mod4, g_ffn, w_out_a, w_out_b, w1, w2):
    bsz, s, d = x.shape
    dff = w1.shape[1]
    tm = FFN_TILE
    tok = lambda w: pl.BlockSpec((1, tm, w), lambda b, j: (b, j, 0))
    modk = lambda k: pl.BlockSpec((1, 1, 1, d), lambda b, j, k=k: (b, k, 0, 0))
    res = lambda shape: pl.BlockSpec(shape, lambda b, j: (0, 0), pipeline_mode=pl.Buffered(1))
    return pl.pallas_call(
        _ffn_kernel,
        grid=(bsz, s // tm),
        in_specs=[tok(d), tok(A_WIDTH), tok(B_WIDTH), modk(2), modk(3), modk(4), modk(5),
                  pl.BlockSpec((1, d), lambda b, j: (0, 0)),
                  res((A_WIDTH, d)), res((B_WIDTH, d)), res((d, dff)), res((dff, d))],
        out_specs=tok(d),
        out_shape=jax.ShapeDtypeStruct((bsz, s, d), F32),
        compiler_params=_params(("arbitrary", "arbitrary")),
    )(x, oa, ob, mod4, mod4, mod4, mod4, g_ffn, w_out_a, w_out_b, w1, w2)


def _block_ones(n, blk, dtype=BF16):
    i = jnp.arange(n)
    return ((i[:, None] // blk) == (i[None, :] // blk)).astype(dtype)


def kernel(x, c, w_ada, b_ada, g_mix, g_ffn, w_in, g_q, g_k, g_kv, w_uk, w_uv, mu_shift, w0, w2, a0, a2, g2,
           k_k, k_a, r_k, ln_w, ln_b, w_out, w_ff1, w_ff2):
    bsz, s, d = x.shape
    depth = w_ada.shape[0]
    assert s % Q_TILE == 0 and s % FRONT_TILE == 0 and s % RWKV_TILE == 0 and s % FFN_TILE == 0
    topk = min(TOPK_MAX, s // 4)

    eb = _block_ones(SEG_K, B_HEAD_DIM)
    ex = (jnp.arange(2 * A_HEAD_DIM)[:, None] // A_HEAD_DIM == jnp.arange(2 * KV_LATENT)[None, :] // KV_LATENT
          ).astype(BF16)
    sel = (jnp.arange(LANES)[None, :] == IDX_DIM + jnp.arange(IDX_HEADS)[:, None]).astype(BF16)
    eye_l = jnp.eye(KV_LATENT, dtype=BF16)
    ti = jnp.arange(TOK_TILE)
    tri = (((ti[:, None] // CHUNK) == (ti[None, :] // CHUNK)) & (ti[:, None] >= ti[None, :])).astype(BF16)
    ki = jnp.arange(K_TILE)
    lstrict = (ki[None, :] < ki[:, None]).astype(BF16)
    hsel = (jnp.arange(A_HEADS * KV_LATENT)[:, None] // KV_LATENT == jnp.arange(LANES)[None, :]).astype(BF16)

    for l in range(depth):
        w_a = jnp.pad(w_in[l][:, :N_IN_A], ((0, 0), (0, N_A_PAD - N_IN_A)))
        w_in_p = jnp.concatenate([w_a, w_in[l][:, N_IN_A:]], axis=1).astype(BF16)
        wuk_flat = w_uk[l].reshape(KV_LATENT, A_WIDTH).astype(BF16)
        wuk_t = jnp.transpose(w_uk[l], (1, 2, 0)).reshape(A_HEADS // 2, 2, A_HEAD_DIM, KV_LATENT)
        wuk_bd = (jnp.eye(2, dtype=F32)[None, :, None, :, None] * wuk_t[:, :, :, None, :]).reshape(
            A_HEADS // 2, 2 * A_HEAD_DIM, 2 * KV_LATENT).astype(BF16)
        wuv_t = jnp.transpose(w_uv[l], (1, 0, 2)).reshape(A_HEADS // 2, 2, KV_LATENT, A_HEAD_DIM)
        wuv_pair = (jnp.eye(2, dtype=F32)[None, :, None, :, None] * wuv_t[:, :, :, None, :]).reshape(
            A_HEADS // 2, 2 * KV_LATENT, 2 * A_HEAD_DIM).astype(BF16)
        gqk = jnp.tile(g_q[l] * g_k[l], A_HEADS).reshape(1, A_WIDTH)
        r1 = lambda t: t.reshape(1, -1)

        mod = _mod_call(c, w_ada[l], b_ada[l])
        mod4 = mod.reshape(bsz, 6, 1, d)
        a_consts = (r1(g_kv[l]), gqk, wuk_flat, wuk_bd, eb, ex, sel, eye_l)
        b_consts = (r1(w0[l]), w2[l].astype(BF16), r1(a0[l]), a2[l].astype(BF16), g2[l].astype(BF16),
                    r1(k_k[l]), r1(k_a[l]), r1(r_k[l]), eb, tri)
        ckr, cvt, qabs, qidx, kidx, widx, rt, kt, bt, kl, v, g, bv, pc = _front_call(
            x, mod4, r1(g_mix[l]), w_in_p, r1(mu_shift[l]), a_consts, b_consts)
        o_a = _dsa_call(topk, qabs, qidx, widx, ckr, cvt, kidx, wuv_pair, lstrict, hsel)
        o_b = _rwkv_call(rt, kt, bt, kl, v, g, bv, pc, r1(ln_w[l]), r1(ln_b[l]), eb)
        x = _ffn_call(x, o_a, o_b, mod4, r1(g_ffn[l]), w_out[l][:A_WIDTH].astype(BF16),
                      w_out[l][A_WIDTH:].astype(BF16), w_ff1[l].astype(BF16), w_ff2[l].astype(BF16))
    return x
```

```python
import functools

import jax
import jax.numpy as jnp
from jax import lax
from jax.experimental import pallas as pl
from jax.experimental.pallas import tpu as pltpu

F32 = jnp.float32
BF16 = jnp.bfloat16

CHUNK = 64
A_HEADS = 8
A_HEAD_DIM = 64
A_WIDTH = A_HEADS * A_HEAD_DIM
KV_LATENT = 128
IDX_HEADS = 8
IDX_DIM = 64
TOPK_MAX = 256
B_HEADS = 8
B_HEAD_DIM = 64
B_WIDTH = B_HEADS * B_HEAD_DIM
W_LORA = 64
A_LORA = 64
G_LORA = 128
RMS_EPS = 1e-6
GN_EPS = 64e-5
N_IN_A = A_WIDTH + KV_LATENT + IDX_HEADS * IDX_DIM + IDX_DIM + IDX_HEADS
N_IN_B = 3 * B_WIDTH + W_LORA + A_LORA + G_LORA
N_A_PAD = 1280

LANES = 128
SEG_K = 256
TOK_TILE = 256
FRONT_TILE = 512
RWKV_TILE = 512
Q_TILE = 256
FFN_TILE = 512
FFN_ROWS = 256
K_TILE = 256
DIST_BIG = 1e30
ONES_ROWS = 16
LOG2E = 1.4426950408889634
EXP_RANGE = 90.0
BOUND_MARGIN = 1.02
SEARCH_PROBES = 14
VMEM_LIMIT = 56 * 1024 * 1024


def _dot(a, b):
    return jnp.dot(a, b, preferred_element_type=F32)


def _dot_nt(a, b):
    return lax.dot_general(a, b, (((1,), (1,)), ((), ())), preferred_element_type=F32)


def _dot_tn(a, b):
    return lax.dot_general(a, b, (((0,), (0,)), ((), ())), preferred_element_type=F32)


def _split(x):
    hi = x.astype(BF16)
    lo = (x - hi.astype(F32)).astype(BF16)
    return hi, lo


def _dot_hl(x, e):
    hi, lo = _split(x)
    return _dot(hi, e) + _dot(lo, e)


def _seg_dot_hl(x, e):
    k = e.shape[0]
    return jnp.concatenate([_dot_hl(x[:, j:j + k], e) for j in range(0, x.shape[1], k)], axis=1)


def _params(sem):
    return pltpu.CompilerParams(dimension_semantics=sem, vmem_limit_bytes=VMEM_LIMIT)


def _mod_kernel(c_ref, w_ref, b_ref, o_ref):
    c = c_ref[...]
    s = c * jax.nn.sigmoid(c)
    s_hi, s_lo = _split(s)
    w_hi, w_lo = _split(w_ref[...])
    o_ref[...] = _dot(s_hi, w_hi) + _dot(s_hi, w_lo) + _dot(s_lo, w_hi) + b_ref[...]


def _mod_call(c, w_ada, b_ada):
    bsz, d = c.shape
    n = w_ada.shape[1]
    tn = 1024
    return pl.pallas_call(
        _mod_kernel,
        grid=(n // tn,),
        in_specs=[pl.BlockSpec((bsz, d), lambda j: (0, 0)),
                  pl.BlockSpec((d, tn), lambda j: (0, j)),
                  pl.BlockSpec((1, tn), lambda j: (0, j))],
        out_specs=pl.BlockSpec((bsz, tn), lambda j: (0, j)),
        out_shape=jax.ShapeDtypeStruct((bsz, n), F32),
        compiler_params=_params(("arbitrary",)),
    )(c, w_ada, b_ada.reshape(1, n))


def _prep_a(pa, rows, blk, gkv_ref, gqk_ref, wuk_ref, wukbd_ref, eb_ref, ex_ref, sel_ref, eye_ref,
            ckr_ref, cvt_ref, qabs_ref, qidx_ref, kidx_ref, widx_ref):
    tm = pa.shape[0]
    q = pa[:, :A_WIDTH]
    cl = pa[:, A_WIDTH:A_WIDTH + KV_LATENT]
    o_qi = A_WIDTH + KV_LATENT
    qi = pa[:, o_qi:o_qi + IDX_HEADS * IDX_DIM]
    o_kw = o_qi + IDX_HEADS * IDX_DIM
    kw = pa[:, o_kw:o_kw + LANES]

    ckv = cl * lax.rsqrt(jnp.mean(cl * cl, axis=-1, keepdims=True) + RMS_EPS) * gkv_ref[...]
    ckv_b = ckv.astype(BF16)
    cvt_ref[0, blk, :KV_LATENT, :] = _dot_nt(eye_ref[...], ckv_b).astype(BF16)
    cvt_ref[0, blk, KV_LATENT:, :] = jnp.ones((ONES_ROWS, tm), BF16)
    kf = _dot(ckv_b, wuk_ref[...])
    ss = _seg_dot_hl(kf * kf, ex_ref[...])
    inv_rms = lax.rsqrt(ss * (1.0 / A_HEAD_DIM) + RMS_EPS)
    ckr_ref[0, rows] = (jnp.concatenate([ckv] * A_HEADS, axis=1) * inv_rms).astype(BF16)

    ssq = _seg_dot_hl(q * q, eb_ref[...])
    qh = q * lax.rsqrt(ssq * (1.0 / A_HEAD_DIM) + RMS_EPS) * gqk_ref[...]
    qh_b = qh.astype(BF16)
    for j in range(A_HEADS // 2):
        qabs = _dot(qh_b[:, j * LANES:(j + 1) * LANES], wukbd_ref[j]) * (A_HEAD_DIM ** -0.5 * LOG2E)
        qabs_ref[0, rows, 2 * j * KV_LATENT:2 * (j + 1) * KV_LATENT] = qabs.astype(BF16)
    for h in range(IDX_HEADS):
        qidx_ref[0, h, rows] = qi[:, h * IDX_DIM:(h + 1) * IDX_DIM].astype(BF16)
    kidx_ref[0, rows] = kw[:, :IDX_DIM].astype(BF16)
    kw_hi, kw_lo = _split(kw)
    w_t = _dot_nt(sel_ref[...], kw_hi) + _dot_nt(sel_ref[...], kw_lo)
    widx_ref[0, blk] = w_t * (IDX_HEADS ** -0.5 * IDX_DIM ** -0.5)


def _prep_b(pb, rows, blk, w0_ref, w2_ref, a0_ref, a2_ref, g2_ref, kk_ref, ka_ref, rk_ref, eb_ref, tri_ref,
            rt_ref, kt_ref, bt_ref, kl_ref, v_ref, g_ref, bv_ref, pc_ref):
    r = pb[:, :B_WIDTH]
    k = pb[:, B_WIDTH:2 * B_WIDTH]
    v = pb[:, 2 * B_WIDTH:3 * B_WIDTH]
    o = 3 * B_WIDTH
    xw = pb[:, o:o + W_LORA]
    xa = pb[:, o + W_LORA:o + W_LORA + A_LORA]
    xg = pb[:, o + W_LORA + A_LORA:o + W_LORA + A_LORA + G_LORA]

    z = w0_ref[...] + _dot(jnp.tanh(xw).astype(BF16), w2_ref[...])
    nz = -z
    softplus = jnp.maximum(nz, 0.0) + jnp.log(1.0 + jnp.exp(-jnp.abs(nz)))
    lw = -jnp.exp(-softplus - 0.5)
    a = jax.nn.sigmoid(a0_ref[...] + _dot(xa.astype(BF16), a2_ref[...]))
    g = _dot(jax.nn.sigmoid(xg).astype(BF16), g2_ref[...])
    kk = k * kk_ref[...]
    kkn = kk / jnp.maximum(jnp.sqrt(_seg_dot_hl(kk * kk, eb_ref[...])), 1e-12)
    kp = k * (1.0 + (a - 1.0) * ka_ref[...])
    bonus = _seg_dot_hl(r * kp * rk_ref[...], eb_ref[...])

    lw_hi, lw_lo = _split(lw)
    cum = _dot(tri_ref[...], lw_hi) + _dot(tri_ref[...], lw_lo)
    e_pos = jnp.exp(cum)
    e_neg = jnp.exp(-cum)
    rt_ref[0, rows] = (r * e_pos).astype(BF16)
    kt_ref[0, rows] = (kkn * jnp.exp(cum - lw)).astype(BF16)
    bt_ref[0, rows] = (kkn * a * e_neg).astype(BF16)
    kl_ref[0, rows] = (kp * e_neg).astype(BF16)
    v_ref[0, rows] = v.astype(BF16)
    g_ref[0, rows] = g
    bv_ref[0, rows] = bonus * v
    for c in range(pb.shape[0] // CHUNK):
        pc_ref[0, blk, c:c + 1, :] = e_pos[(c + 1) * CHUNK - 1:(c + 1) * CHUNK, :]


N_FRONT_IN = 6
N_PREP_A_IN = 8
N_PREP_B_IN = 10
N_PREP_A_OUT = 6


def _front_kernel(*refs):
    x_ref, sh_ref, sc_ref, g_ref, w_ref, mu_ref = refs[:N_FRONT_IN]
    a_in = refs[N_FRONT_IN:N_FRONT_IN + N_PREP_A_IN]
    b_in = refs[N_FRONT_IN + N_PREP_A_IN:N_FRONT_IN + N_PREP_A_IN + N_PREP_B_IN]
    outs = refs[N_FRONT_IN + N_PREP_A_IN + N_PREP_B_IN:-1]
    carry_ref = refs[-1]
    j = pl.program_id(1)

    @pl.when(j == 0)
    def _():
        carry_ref[...] = jnp.zeros_like(carry_ref)

    x = x_ref[0]
    y = x * lax.rsqrt(jnp.mean(x * x, axis=-1, keepdims=True) + RMS_EPS) * g_ref[...]
    h = y * (1.0 + sc_ref[0, 0]) + sh_ref[0, 0]
    p = _dot(h.astype(BF16), w_ref[...])
    pb = p[:, N_A_PAD:]
    tm = pb.shape[0]
    row = lax.broadcasted_iota(jnp.int32, (tm, 1), 0)
    prev = jnp.where(row == 0, carry_ref[...], pltpu.roll(pb, 1, axis=0))
    carry_ref[...] = pb[tm - 1:tm, :]
    pb = pb + mu_ref[...] * (prev - pb)
    for blk in range(tm // TOK_TILE):
        rows = slice(blk * TOK_TILE, (blk + 1) * TOK_TILE)
        _prep_a(p[rows, :N_A_PAD], rows, blk, *a_in, *outs[:N_PREP_A_OUT])
        _prep_b(pb[rows], rows, blk, *b_in, *outs[N_PREP_A_OUT:])


def _front_call(x, mod4, g_mix, w_in_p, mu, a_consts, b_consts):
    bsz, s, d = x.shape
    n = w_in_p.shape[1]
    nb = n - N_A_PAD
    tm = FRONT_TILE
    tt = TOK_TILE
    full = lambda arr: pl.BlockSpec(arr.shape, lambda b, j, nd=arr.ndim: (0,) * nd)
    tok = lambda w: pl.BlockSpec((1, tm, w), lambda b, j: (b, j, 0))
    per_tile = lambda r, c: pl.BlockSpec((1, tm // tt, r, c), lambda b, j: (b, j, 0, 0))
    bf = lambda w: jax.ShapeDtypeStruct((bsz, s, w), BF16)
    ff = lambda w: jax.ShapeDtypeStruct((bsz, s, w), F32)
    nt = s // tt
    out_specs = [tok(A_HEADS * KV_LATENT), per_tile(KV_LATENT + ONES_ROWS, tt), tok(A_HEADS * KV_LATENT),
                 pl.BlockSpec((1, IDX_HEADS, tm, IDX_DIM), lambda b, j: (b, 0, j, 0)),
                 tok(IDX_DIM), per_tile(IDX_HEADS, tt)] + [tok(B_WIDTH)] * 7 + [per_tile(tt // CHUNK, B_WIDTH)]
    out_shape = [bf(A_HEADS * KV_LATENT),
                 jax.ShapeDtypeStruct((bsz, nt, KV_LATENT + ONES_ROWS, tt), BF16),
                 bf(A_HEADS * KV_LATENT),
                 jax.ShapeDtypeStruct((bsz, IDX_HEADS, s, IDX_DIM), BF16),
                 bf(IDX_DIM),
                 jax.ShapeDtypeStruct((bsz, nt, IDX_HEADS, tt), F32),
                 bf(B_WIDTH), bf(B_WIDTH), bf(B_WIDTH), bf(B_WIDTH), bf(B_WIDTH), ff(B_WIDTH), ff(B_WIDTH),
                 jax.ShapeDtypeStruct((bsz, nt, tt // CHUNK, B_WIDTH), F32)]
    assert len(a_consts) == N_PREP_A_IN and len(b_consts) == N_PREP_B_IN
    return pl.pallas_call(
        _front_kernel,
        grid=(bsz, s // tm),
        in_specs=[pl.BlockSpec((1, tm, d), lambda b, j: (b, j, 0)),
                  pl.BlockSpec((1, 1, 1, d), lambda b, j: (b, 0, 0, 0)),
                  pl.BlockSpec((1, 1, 1, d), lambda b, j: (b, 1, 0, 0)),
                  full(g_mix),
                  pl.BlockSpec(w_in_p.shape, lambda b, j: (0, 0), pipeline_mode=pl.Buffered(1)),
                  full(mu)] + [full(t) for t in a_consts] + [full(t) for t in b_consts],
        out_specs=out_specs,
        out_shape=out_shape,
        scratch_shapes=[pltpu.VMEM((1, nb), F32)],
        compiler_params=_params(("arbitrary", "arbitrary")),
    )(x, mod4, mod4, g_mix, w_in_p, mu, *a_consts, *b_consts)


def _colsum8(x):
    y = x.reshape(4, K_TILE // 32, 8, Q_TILE)
    return jnp.sum(jnp.sum(y, axis=1), axis=0)


def _colmin8(x):
    y = x.reshape(4, K_TILE // 32, 8, Q_TILE)
    return jnp.min(jnp.min(y, axis=1), axis=0)


def _colmax8(x):
    y = x.reshape(4, K_TILE // 32, 8, Q_TILE)
    return jnp.max(jnp.max(y, axis=1), axis=0)


def _for_key_tiles(nkc, body, init):
    def quad(j, c):
        return body(4 * j + 3, body(4 * j + 2, body(4 * j + 1, body(4 * j, c))))
    c = lax.fori_loop(0, nkc // 4, quad, init)
    base = (nkc // 4) * 4
    c = lax.cond(nkc % 4 >= 2, lambda c: body(base + 1, body(base, c)), lambda c: c, c)
    return lax.cond(nkc % 2 == 1, lambda c: body(nkc - 1, c), lambda c: c, c)


def _head_norm_max(x, hsel_ref):
    return jnp.sqrt(jnp.max(_dot(x * x, hsel_ref[...])))


def _dsa_kernel(topk, qabs_ref, qidx_ref, widx_ref, ckr_ref, cvt_ref, kidx_ref, wuv_ref, lstrict_ref, hsel_ref,
                o_ref, score_ref, dist_ref, logit_ref, p_ref, m_ref, acc_ref, kmax_ref):
    i = pl.program_id(1)
    nkc = i + 1
    t0 = i * Q_TILE
    krow = lax.broadcasted_iota(jnp.int32, (K_TILE, 1), 0)
    qcol = lax.broadcasted_iota(jnp.int32, (1, Q_TILE), 1)
    limit = ((t0 + qcol) // CHUNK + 1) * CHUNK
    kp = jnp.minimum(limit, topk).astype(F32)
    rel = (qcol - krow).astype(F32)

    def p1(kc, carry):
        rmin, rmax = carry
        k = kidx_ref[0, pl.ds(pl.multiple_of(kc * K_TILE, K_TILE), K_TILE), :]
        acc = jnp.zeros((K_TILE, Q_TILE), F32)
        for h in range(IDX_HEADS):
            s = _dot_nt(k, qidx_ref[0, h])
            acc = acc + widx_ref[0, 0, h:h + 1, :] * jnp.maximum(s, 0.0)
        adm = (kc * K_TILE + krow) < limit
        score_ref[kc] = jnp.where(adm, acc, -jnp.inf)
        rmin = jnp.minimum(rmin, _colmin8(jnp.where(adm, acc, jnp.inf)))
        rmax = jnp.maximum(rmax, _colmax8(jnp.where(adm, acc, -jnp.inf)))
        return rmin, rmax

    rmin, rmax = _for_key_tiles(
        nkc, p1, (jnp.full((8, Q_TILE), jnp.inf, F32), jnp.full((8, Q_TILE), -jnp.inf, F32)))
    lo = jnp.min(rmin, axis=0, keepdims=True)
    hi = jnp.max(rmax, axis=0, keepdims=True)

    def count(pred):
        def body(kc, acc):
            return acc + _colsum8(jnp.where(pred(score_ref[kc]), 1.0, 0.0))
        return jnp.sum(lax.fori_loop(0, nkc, body, jnp.zeros((8, Q_TILE), F32)), axis=0, keepdims=True)

    def probe(c):
        lo, hi, cnt_lo = c
        mid = lo + 0.5 * (hi - lo)
        cnt = count(lambda sc: sc >= mid)
        ge = cnt >= kp
        return jnp.where(ge, mid, lo), jnp.where(ge, hi, mid), jnp.where(ge, cnt, cnt_lo)

    def smallest(pred):
        def body(kc, acc):
            sc = score_ref[kc]
            return jnp.minimum(acc, _colmin8(jnp.where(pred(sc), sc, jnp.inf)))
        return jnp.min(lax.fori_loop(0, nkc, body, jnp.full((8, Q_TILE), jnp.inf, F32)), axis=0, keepdims=True)

    def any_true(x):
        return jnp.max(jnp.where(x, 1.0, 0.0)) > 0.0

    lo, _, cnt_lo = lax.fori_loop(0, SEARCH_PROBES, lambda _, c: probe(c), (lo, hi, limit.astype(F32)))

    def step_up(c):
        it, thr, cnt_gt, cnt_ge = c
        up = cnt_gt >= kp
        thr = jnp.where(up, smallest(lambda sc: sc > thr), thr)
        return it + 1, thr, count(lambda sc: sc > thr), jnp.where(up, cnt_gt, cnt_ge)

    thr = smallest(lambda sc: sc >= lo)
    search = lax.while_loop(
        lambda c: jnp.logical_and(c[0] < nkc * K_TILE, any_true(c[2] >= kp)),
        step_up, (jnp.int32(0), thr, count(lambda sc: sc > thr), cnt_lo))
    thr, need, cnt_ge = search[1], kp - search[2], search[3]

    def dist_tile(kc):
        return jnp.abs(rel + (t0 - kc * K_TILE).astype(F32))

    big8 = jnp.full((8, Q_TILE), DIST_BIG, F32)

    def sel_plain():
        def body(kc, near):
            d = jnp.where(score_ref[kc] >= thr, dist_tile(kc), DIST_BIG)
            dist_ref[kc] = d
            return jnp.minimum(near, _colmin8(d))
        return jnp.min(lax.fori_loop(0, nkc, body, big8), axis=0, keepdims=True)

    def sel_ties():
        def body(kc, c):
            run, near = c
            sc = score_ref[kc]
            eq = sc == thr
            eq_f = jnp.where(eq, 1.0, 0.0)
            pre = run + _dot(lstrict_ref[...], eq_f.astype(BF16))
            keep = (sc > thr) | (eq & (pre < need))
            d = jnp.where(keep, dist_tile(kc), DIST_BIG)
            dist_ref[kc] = d
            return run + jnp.sum(_colsum8(eq_f), axis=0, keepdims=True), jnp.minimum(near, _colmin8(d))
        _, near = _for_key_tiles(nkc, body, (jnp.zeros((1, Q_TILE), F32), big8))
        return jnp.min(near, axis=0, keepdims=True)

    near = lax.cond(any_true(cnt_ge != kp), sel_ties, sel_plain)

    acc_ref[...] = jnp.zeros(acc_ref.shape, F32)

    @pl.when(i == 0)
    def _():
        kmax = _head_norm_max(ckr_ref[0, :K_TILE], hsel_ref)
        for kc in range(1, ckr_ref.shape[1] // K_TILE):
            kmax = jnp.maximum(kmax, _head_norm_max(ckr_ref[0, kc * K_TILE:(kc + 1) * K_TILE], hsel_ref))
        kmax_ref[0] = kmax

    bound = _head_norm_max(qabs_ref[0], hsel_ref) * kmax_ref[0] * BOUND_MARGIN

    def att_shifted():
        def body(kc, _):
            d = dist_ref[kc] - near
            for h in range(A_HEADS):
                slope = 2.0 ** (-8.0 * (h + 1) / A_HEADS) * LOG2E
                ck = ckr_ref[0, pl.ds(pl.multiple_of(kc * K_TILE, K_TILE), K_TILE),
                             h * KV_LATENT:(h + 1) * KV_LATENT]
                logit = _dot_nt(ck, qabs_ref[0, :, h * KV_LATENT:(h + 1) * KV_LATENT]) - slope * d
                p_ref[h] = jnp.exp2(logit).astype(BF16)
            cv = cvt_ref[0, kc]
            for h in range(A_HEADS):
                acc_ref[h] = acc_ref[h] + _dot(cv, p_ref[h])
            return 0
        _for_key_tiles(nkc, body, 0)

    def att_online():
        m_ref[...] = jnp.full(m_ref.shape, -jnp.inf, F32)
        _for_key_tiles(nkc, att, 0)

    def att(kc, _):
        dist = dist_ref[kc]
        m_new = []
        for h in range(A_HEADS):
            slope = 2.0 ** (-8.0 * (h + 1) / A_HEADS) * LOG2E
            ck = ckr_ref[0, pl.ds(pl.multiple_of(kc * K_TILE, K_TILE), K_TILE), h * KV_LATENT:(h + 1) * KV_LATENT]
            logit = _dot_nt(ck, qabs_ref[0, :, h * KV_LATENT:(h + 1) * KV_LATENT]) - slope * dist
            logit_ref[h] = logit
            m_new.append(jnp.maximum(m_ref[h], jnp.max(_colmax8(logit), axis=0, keepdims=True)))
        cv = cvt_ref[0, kc]
        for h in range(A_HEADS):
            p = jnp.exp2(logit_ref[h] - m_new[h])
            acc_ref[h] = acc_ref[h] * jnp.exp2(m_ref[h] - m_new[h]) + _dot(cv, p.astype(BF16))
            m_ref[h] = m_new[h]
        return 0

    lax.cond(bound <= EXP_RANGE, att_shifted, att_online)

    for pair in range(A_HEADS // 2):
        o_pair = []
        for hh in range(2):
            a = acc_ref[2 * pair + hh]
            o_t = a[:KV_LATENT] * (1.0 / a[KV_LATENT:KV_LATENT + 1])
            o_pair.append(o_t.T.astype(BF16))
        o_lat = jnp.concatenate(o_pair, axis=1)
        o_ref[0, :, pair * LANES:(pair + 1) * LANES] = _dot(o_lat, wuv_ref[pair]).astype(o_ref.dtype)


def _dsa_call(topk, qabs, qidx, widx, ckr, cvt, kidx, wuv_pair, lstrict, hsel):
    bsz, s, _ = qabs.shape
    nq = s // Q_TILE
    nk = s // K_TILE
    qt = lambda w: pl.BlockSpec((1, Q_TILE, w), lambda b, i: (b, i, 0))
    return pl.pallas_call(
        functools.partial(_dsa_kernel, topk),
        grid=(bsz, nq),
        in_specs=[qt(A_HEADS * KV_LATENT),
                  pl.BlockSpec((1, IDX_HEADS, Q_TILE, IDX_DIM), lambda b, i: (b, 0, i, 0)),
                  pl.BlockSpec((1, 1, IDX_HEADS, Q_TILE), lambda b, i: (b, i, 0, 0)),
                  pl.BlockSpec((1, s, A_HEADS * KV_LATENT), lambda b, i: (b, 0, 0)),
                  pl.BlockSpec((1, nk, KV_LATENT + ONES_ROWS, K_TILE), lambda b, i: (b, 0, 0, 0)),
                  pl.BlockSpec((1, s, IDX_DIM), lambda b, i: (b, 0, 0)),
                  pl.BlockSpec((A_HEADS // 2, 2 * KV_LATENT, LANES), lambda b, i: (0, 0, 0)),
                  pl.BlockSpec((K_TILE, K_TILE), lambda b, i: (0, 0)),
                  pl.BlockSpec((A_HEADS * KV_LATENT, LANES), lambda b, i: (0, 0))],
        out_specs=qt(A_WIDTH),
        out_shape=jax.ShapeDtypeStruct((bsz, s, A_WIDTH), BF16),
        scratch_shapes=[pltpu.VMEM((nk, K_TILE, Q_TILE), F32),
                        pltpu.VMEM((nk, K_TILE, Q_TILE), F32),
                        pltpu.VMEM((A_HEADS, K_TILE, Q_TILE), F32),
                        pltpu.VMEM((A_HEADS, K_TILE, Q_TILE), BF16),
                        pltpu.VMEM((A_HEADS, 1, Q_TILE), F32),
                        pltpu.VMEM((A_HEADS, KV_LATENT + ONES_ROWS, Q_TILE), F32),
                        pltpu.SMEM((1,), F32)],
        compiler_params=_params(("arbitrary", "arbitrary")),
    )(qabs, qidx, widx, ckr, cvt, kidx, wuv_pair, lstrict, hsel)


def _rwkv_block(blk, tm, masks, rt_ref, kt_ref, bt_ref, kl_ref, v_ref, pc_ref):
    strict, incl, eye_t, blk_diag, diag, head0 = masks
    nch = tm // CHUNK
    npair = B_HEADS // 2
    rows_b = slice(blk * tm, (blk + 1) * tm)
    zero_b = jnp.zeros((), BF16)
    heads = [(p, hh) for p in range(npair) for hh in range(2)]
    rt, kt, bt, kl, v = [], [], [], [], []
    a_ab, a_ak, m_rb, m_rk = [], [], [], []
    for p in range(npair):
        sl = slice(p * LANES, (p + 1) * LANES)
        rt.append(rt_ref[0, rows_b, sl])
        kt.append(kt_ref[0, rows_b, sl])
        bt.append(bt_ref[0, rows_b, sl])
        kl.append(kl_ref[0, rows_b, sl])
        v.append(v_ref[0, rows_b, sl])
        lhs = jnp.concatenate([jnp.where(head0, kt[p], zero_b), jnp.where(head0, zero_b, kt[p]),
                               jnp.where(head0, rt[p], zero_b), jnp.where(head0, zero_b, rt[p])], axis=0)
        prod = _dot_nt(lhs, jnp.concatenate([bt[p], kl[p]], axis=0))
        for hh in range(2):
            a_ab.append(jnp.where(strict, prod[hh * tm:(hh + 1) * tm, :tm], 0.0))
            a_ak.append(jnp.where(strict, prod[hh * tm:(hh + 1) * tm, tm:], 0.0).astype(BF16))
            m_rb.append(jnp.where(incl, prod[(2 + hh) * tm:(3 + hh) * tm, :tm], 0.0).astype(BF16))
            m_rk.append(jnp.where(incl, prod[(2 + hh) * tm:(3 + hh) * tm, tm:], 0.0).astype(BF16))

    t_inv = [(eye_t - a).astype(BF16) for a in a_ab]
    a_pow = [a.astype(BF16) for a in a_ab]
    for _ in range(5):
        a_sq = [_dot(a, a) for a in a_pow]
        a_pow = [a.astype(BF16) for a in a_sq]
        t_inv = [_dot(t, (eye_t + a).astype(BF16)).astype(BF16) for t, a in zip(t_inv, a_sq)]

    avm = [_dot(jnp.concatenate([a_ak[i], m_rk[i]], axis=0), v[p]) for i, (p, _) in enumerate(heads)]
    x = [_dot(t_inv[i], jnp.concatenate([kt[p], avm[i][:tm].astype(BF16)], axis=1))
         for i, (p, _) in enumerate(heads)]
    y = [_dot(m_rb[i], x[i].astype(BF16)) for i in range(len(heads))]

    head0_2 = jnp.concatenate([head0, head0], axis=1)
    zeros_b = jnp.zeros((CHUNK, LANES), BF16)
    q_b, ol, g_mat, f_mat = [], [], [], []
    for p in range(npair):
        i0, i1 = 2 * p, 2 * p + 1
        sl = slice(p * LANES, (p + 1) * LANES)
        wu_b = (-jnp.where(head0_2, x[i0], x[i1])).astype(BF16)
        yy = jnp.where(head0_2, y[i0], y[i1])
        q_b.append((rt[p].astype(F32) - yy[:, :LANES]).astype(BF16))
        ol.append(jnp.where(head0, avm[i0][tm:], avm[i1][tm:]) - yy[:, LANES:])
        gp, fp = [], []
        for c in range(nch):
            rows = slice(c * CHUNK, (c + 1) * CHUNK)
            pc = pc_ref[0, blk, c:c + 1, sl]
            bh = (bt[p][rows].astype(F32) * pc).astype(BF16)
            kh = (kl[p][rows].astype(F32) * pc).astype(BF16)
            rhs = jnp.concatenate([wu_b[rows], jnp.concatenate([zeros_b, v[p][rows]], axis=1)], axis=0)
            bw = _dot_tn(jnp.concatenate([bh, kh], axis=0), rhs)
            gp.append((jnp.where(diag, pc, 0.0) + jnp.where(blk_diag, bw[:, :LANES], 0.0)).astype(BF16))
            fp.append(jnp.where(blk_diag, bw[:, LANES:], 0.0))
        g_mat.append(gp)
        f_mat.append(fp)
    return q_b, ol, g_mat, f_mat


def _rwkv_kernel(rt_ref, kt_ref, bt_ref, kl_ref, v_ref, g_ref, bv_ref, pc_ref, lnw_ref, lnb_ref, eb_ref,
                 o_ref, h_ref):
    j = pl.program_id(1)

    @pl.when(j == 0)
    def _():
        h_ref[...] = jnp.zeros_like(h_ref)

    tm = TOK_TILE
    nch = tm // CHUNK
    npair = B_HEADS // 2
    ri = lax.broadcasted_iota(jnp.int32, (tm, tm), 0)
    ci = lax.broadcasted_iota(jnp.int32, (tm, tm), 1)
    same = (ri // CHUNK) == (ci // CHUNK)
    r2 = lax.broadcasted_iota(jnp.int32, (LANES, LANES), 0)
    c2 = lax.broadcasted_iota(jnp.int32, (LANES, LANES), 1)
    lane = lax.broadcasted_iota(jnp.int32, (1, LANES), 1)
    masks = (same & (ri > ci), same & (ri >= ci), jnp.where(ri == ci, 1.0, 0.0),
             (r2 // B_HEAD_DIM) == (c2 // B_HEAD_DIM), r2 == c2, (lane // B_HEAD_DIM) == 0)

    nblk = rt_ref.shape[1] // tm
    blocks = [_rwkv_block(b, tm, masks, rt_ref, kt_ref, bt_ref, kl_ref, v_ref, pc_ref) for b in range(nblk)]

    h = [h_ref[p] for p in range(npair)]
    for b, (q_b, ol, g_mat, f_mat) in enumerate(blocks):
        o_chunks = [[] for _ in range(npair)]
        for c in range(nch):
            rows = slice(c * CHUNK, (c + 1) * CHUNK)
            for p in range(npair):
                h_b = h[p].astype(BF16)
                o_chunks[p].append(_dot(q_b[p][rows], h_b) + ol[p][rows])
                h[p] = _dot(g_mat[p][c], h_b) + f_mat[p][c]
        out = jnp.concatenate([jnp.concatenate(oc, axis=0) for oc in o_chunks], axis=1)

        rows_b = slice(b * tm, (b + 1) * tm)
        eb = eb_ref[...]
        mean = _seg_dot_hl(out, eb) * (1.0 / B_HEAD_DIM)
        d = out - mean
        var = _seg_dot_hl(d * d, eb) * (1.0 / B_HEAD_DIM)
        y = d * lax.rsqrt(var + GN_EPS) * lnw_ref[...] + lnb_ref[...] + bv_ref[0, rows_b]
        o_ref[0, rows_b] = (y * g_ref[0, rows_b]).astype(o_ref.dtype)
    for p in range(npair):
        h_ref[p] = h[p]


def _rwkv_call(rt, kt, bt, kl, v, g, bv, pc, ln_w, ln_b, eb):
    bsz, s, _ = rt.shape
    tm = RWKV_TILE
    tok = pl.BlockSpec((1, tm, B_WIDTH), lambda b, j: (b, j, 0))
    row = pl.BlockSpec((1, B_WIDTH), lambda b, j: (0, 0))
    return pl.pallas_call(
        _rwkv_kernel,
        grid=(bsz, s // tm),
        in_specs=[tok] * 7 + [pl.BlockSpec((1, tm // TOK_TILE, TOK_TILE // CHUNK, B_WIDTH), lambda b, j: (b, j, 0, 0)),
                              row, row, pl.BlockSpec((SEG_K, SEG_K), lambda b, j: (0, 0))],
        out_specs=tok,
        out_shape=jax.ShapeDtypeStruct((bsz, s, B_WIDTH), BF16),
        scratch_shapes=[pltpu.VMEM((B_HEADS // 2, LANES, LANES), F32)],
        compiler_params=_params(("arbitrary", "arbitrary")),
    )(rt, kt, bt, kl, v, g, bv, pc, ln_w, ln_b, eb)


def _ffn_kernel(x_ref, oa_ref, ob_ref, gt1_ref, sh2_ref, sc2_ref, gt2_ref, gf_ref, woa_ref, wob_ref,
                w1_ref, w2_ref, o_ref):
    for r0 in range(0, x_ref.shape[1], FFN_ROWS):
        rows = slice(r0, r0 + FFN_ROWS)
        mix = _dot(oa_ref[0, rows], woa_ref[...]) + _dot(ob_ref[0, rows], wob_ref[...])
        x1 = x_ref[0, rows] + gt1_ref[0, 0] * mix
        y = x1 * lax.rsqrt(jnp.mean(x1 * x1, axis=-1, keepdims=True) + RMS_EPS) * gf_ref[...]
        h2 = (y * (1.0 + sc2_ref[0, 0]) + sh2_ref[0, 0]).astype(BF16)
        u = jnp.maximum(_dot(h2, w1_ref[...]), 0.0)
        o_ref[0, rows] = x1 + gt2_ref[0, 0] * _dot((u * u).astype(BF16), w2_ref[...])


def _ffn_call(x, oa, ob, mod4, g_ffn, w_out_a, w_out_b, w1, w2):
    bsz, s, d = x.shape
    dff = w1.shape[1]
    tm = FFN_TILE
    tok = lambda w: pl.BlockSpec((1, tm, w), lambda b, j: (b, j, 0))
    modk = lambda k: pl.BlockSpec((1, 1, 1, d), lambda b, j, k=k: (b, k, 0, 0))
    res = lambda shape: pl.BlockSpec(shape, lambda b, j: (0, 0), pipeline_mode=pl.Buffered(1))
    return pl.pallas_call(
        _ffn_kernel,
        grid=(bsz, s // tm),
        in_specs=[tok(d), tok(A_WIDTH), tok(B_WIDTH), modk(2), modk(3), modk(4), modk(5),
                  pl.BlockSpec((1, d), lambda b, j: (0, 0)),
                  res((A_WIDTH, d)), res((B_WIDTH, d)), res((d, dff)), res((dff, d))],
        out_specs=tok(d),
        out_shape=jax.ShapeDtypeStruct((bsz, s, d), F32),
        compiler_params=_params(("arbitrary", "arbitrary")),
    )(x, oa, ob, mod4, mod4, mod4, mod4, g_ffn, w_out_a, w_out_b, w1, w2)


def _block_ones(n, blk, dtype=BF16):
    i = jnp.arange(n)
    return ((i[:, None] // blk) == (i[None, :] // blk)).astype(dtype)


def kernel(x, c, w_ada, b_ada, g_mix, g_ffn, w_in, g_q, g_k, g_kv, w_uk, w_uv, mu_shift, w0, w2, a0, a2, g2,
           k_k, k_a, r_k, ln_w, ln_b, w_out, w_ff1, w_ff2):
    bsz, s, d = x.shape
    depth = w_ada.shape[0]
    assert s % Q_TILE == 0 and s % FRONT_TILE == 0 and s % RWKV_TILE == 0 and s % FFN_TILE == 0
    topk = min(TOPK_MAX, s // 4)

    eb = _block_ones(SEG_K, B_HEAD_DIM)
    ex = (jnp.arange(2 * A_HEAD_DIM)[:, None] // A_HEAD_DIM == jnp.arange(2 * KV_LATENT)[None, :] // KV_LATENT
          ).astype(BF16)
    sel = (jnp.arange(LANES)[None, :] == IDX_DIM + jnp.arange(IDX_HEADS)[:, None]).astype(BF16)
    eye_l = jnp.eye(KV_LATENT, dtype=BF16)
    ti = jnp.arange(TOK_TILE)
    tri = (((ti[:, None] // CHUNK) == (ti[None, :] // CHUNK)) & (ti[:, None] >= ti[None, :])).astype(BF16)
    ki = jnp.arange(K_TILE)
    lstrict = (ki[None, :] < ki[:, None]).astype(BF16)
    hsel = (jnp.arange(A_HEADS * KV_LATENT)[:, None] // KV_LATENT == jnp.arange(LANES)[None, :]).astype(BF16)

    for l in range(depth):
        w_a = jnp.pad(w_in[l][:, :N_IN_A], ((0, 0), (0, N_A_PAD - N_IN_A)))
        w_in_p = jnp.concatenate([w_a, w_in[l][:, N_IN_A:]], axis=1).astype(BF16)
        wuk_flat = w_uk[l].reshape(KV_LATENT, A_WIDTH).astype(BF16)
        wuk_t = jnp.transpose(w_uk[l], (1, 2, 0)).reshape(A_HEADS // 2, 2, A_HEAD_DIM, KV_LATENT)
        wuk_bd = (jnp.eye(2, dtype=F32)[None, :, None, :, None] * wuk_t[:, :, :, None, :]).reshape(
            A_HEADS // 2, 2 * A_HEAD_DIM, 2 * KV_LATENT).astype(BF16)
        wuv_t = jnp.transpose(w_uv[l], (1, 0, 2)).reshape(A_HEADS // 2, 2, KV_LATENT, A_HEAD_DIM)
        wuv_pair = (jnp.eye(2, dtype=F32)[None, :, None, :, None] * wuv_t[:, :, :, None, :]).reshape(
            A_HEADS // 2, 2 * KV_LATENT, 2 * A_HEAD_DIM).astype(BF16)
        gqk = jnp.tile(g_q[l] * g_k[l], A_HEADS).reshape(1, A_WIDTH)
        r1 = lambda t: t.reshape(1, -1)

        mod = _mod_call(c, w_ada[l], b_ada[l])
        mod4 = mod.reshape(bsz, 6, 1, d)
        a_consts = (r1(g_kv[l]), gqk, wuk_flat, wuk_bd, eb, ex, sel, eye_l)
        b_consts = (r1(w0[l]), w2[l].astype(BF16), r1(a0[l]), a2[l].astype(BF16), g2[l].astype(BF16),
                    r1(k_k[l]), r1(k_a[l]), r1(r_k[l]), eb, tri)
        ckr, cvt, qabs, qidx, kidx, widx, rt, kt, bt, kl, v, g, bv, pc = _front_call(
            x, mod4, r1(g_mix[l]), w_in_p, r1(mu_shift[l]), a_consts, b_consts)
        o_a = _dsa_call(topk, qabs, qidx, widx, ckr, cvt, kidx, wuv_pair, lstrict, hsel)
        o_b = _rwkv_call(rt, kt, bt, kl, v, g, bv, pc, r1(ln_w[l]), r1(ln_b[l]), eb)
        x = _ffn_call(x, o_a, o_b, mod4, r1(g_ffn[l]), w_out[l][:A_WIDTH].astype(BF16),
                      w_out[l][A_WIDTH:].astype(BF16), w_ff1[l].astype(BF16), w_ff2[l].astype(BF16))
    return x
```

```python
import functools

import jax
import jax.numpy as jnp
from jax import lax
from jax.experimental import pallas as pl
from jax.experimental.pallas import tpu as pltpu

F32 = jnp.float32
BF16 = jnp.bfloat16

CHUNK = 64
A_HEADS = 8
A_HEAD_DIM = 64
A_WIDTH = A_HEADS * A_HEAD_DIM
KV_LATENT = 128
IDX_HEADS = 8
IDX_DIM = 64
TOPK_MAX = 256
B_HEADS = 8
B_HEAD_DIM = 64
B_WIDTH = B_HEADS * B_HEAD_DIM
W_LORA = 64
A_LORA = 64
G_LORA = 128
RMS_EPS = 1e-6
GN_EPS = 64e-5
N_IN_A = A_WIDTH + KV_LATENT + IDX_HEADS * IDX_DIM + IDX_DIM + IDX_HEADS
N_IN_B = 3 * B_WIDTH + W_LORA + A_LORA + G_LORA
N_A_PAD = 1280

LANES = 128
SEG_K = 256
TOK_TILE = 256
FRONT_TILE = 512
RWKV_TILE = 512
Q_TILE = 256
FFN_TILE = 512
FFN_ROWS = 256
K_TILE = 256
DIST_BIG = 1e30
ONES_ROWS = 16
LOG2E = 1.4426950408889634
EXP_NEG_HALF = 0.6065306597126334
EXP_RANGE = 90.0
BOUND_MARGIN = 1.02
SEARCH_PROBES = 14
VMEM_LIMIT = 56 * 1024 * 1024


def _dot(a, b):
    return jnp.dot(a, b, preferred_element_type=F32)


def _dot_nt(a, b):
    return lax.dot_general(a, b, (((1,), (1,)), ((), ())), preferred_element_type=F32)


def _dot_tn(a, b):
    return lax.dot_general(a, b, (((0,), (0,)), ((), ())), preferred_element_type=F32)


def _split(x):
    hi = x.astype(BF16)
    lo = (x - hi.astype(F32)).astype(BF16)
    return hi, lo


def _dot_hl(x, e):
    hi, lo = _split(x)
    return _dot(hi, e) + _dot(lo, e)


def _seg_dot_hl(x, e):
    k = e.shape[0]
    return jnp.concatenate([_dot_hl(x[:, j:j + k], e) for j in range(0, x.shape[1], k)], axis=1)


def _params(sem):
    return pltpu.CompilerParams(dimension_semantics=sem, vmem_limit_bytes=VMEM_LIMIT)


def _mod_kernel(c_ref, w_ref, b_ref, o_ref):
    c = c_ref[...]
    s = c * jax.nn.sigmoid(c)
    s_hi, s_lo = _split(s)
    w_hi, w_lo = _split(w_ref[...])
    o_ref[...] = _dot(s_hi, w_hi) + _dot(s_hi, w_lo) + _dot(s_lo, w_hi) + b_ref[...]


def _mod_call(c, w_ada, b_ada):
    bsz, d = c.shape
    n = w_ada.shape[1]
    tn = 1024
    return pl.pallas_call(
        _mod_kernel,
        grid=(n // tn,),
        in_specs=[pl.BlockSpec((bsz, d), lambda j: (0, 0)),
                  pl.BlockSpec((d, tn), lambda j: (0, j)),
                  pl.BlockSpec((1, tn), lambda j: (0, j))],
        out_specs=pl.BlockSpec((bsz, tn), lambda j: (0, j)),
        out_shape=jax.ShapeDtypeStruct((bsz, n), F32),
        compiler_params=_params(("arbitrary",)),
    )(c, w_ada, b_ada.reshape(1, n))


def _prep_a(pa, rows, blk, gkv_ref, gqk_ref, wuk_ref, wukbd_ref, eb_ref, ex_ref, sel_ref, eye_ref,
            ckr_ref, cvt_ref, qabs_ref, qidx_ref, kidx_ref, widx_ref):
    tm = pa.shape[0]
    q = pa[:, :A_WIDTH]
    cl = pa[:, A_WIDTH:A_WIDTH + KV_LATENT]
    o_qi = A_WIDTH + KV_LATENT
    qi = pa[:, o_qi:o_qi + IDX_HEADS * IDX_DIM]
    o_kw = o_qi + IDX_HEADS * IDX_DIM
    kw = pa[:, o_kw:o_kw + LANES]

    ckv = cl * lax.rsqrt(jnp.mean(cl * cl, axis=-1, keepdims=True) + RMS_EPS) * gkv_ref[...]
    ckv_b = ckv.astype(BF16)
    cvt_ref[0, blk, :KV_LATENT, :] = _dot_nt(eye_ref[...], ckv_b).astype(BF16)
    cvt_ref[0, blk, KV_LATENT:, :] = jnp.ones((ONES_ROWS, tm), BF16)
    kf = _dot(ckv_b, wuk_ref[...])
    ss = _seg_dot_hl(kf * kf, ex_ref[...])
    inv_rms = lax.rsqrt(ss * (1.0 / A_HEAD_DIM) + RMS_EPS)
    ckr_ref[0, rows] = (jnp.concatenate([ckv] * A_HEADS, axis=1) * inv_rms).astype(BF16)

    ssq = _seg_dot_hl(q * q, eb_ref[...])
    qh = q * lax.rsqrt(ssq * (1.0 / A_HEAD_DIM) + RMS_EPS) * gqk_ref[...]
    qh_b = qh.astype(BF16)
    for j in range(A_HEADS // 2):
        qabs = _dot(qh_b[:, j * LANES:(j + 1) * LANES], wukbd_ref[j]) * (A_HEAD_DIM ** -0.5 * LOG2E)
        qabs_ref[0, rows, 2 * j * KV_LATENT:2 * (j + 1) * KV_LATENT] = qabs.astype(BF16)
    for h in range(IDX_HEADS):
        qidx_ref[0, h, rows] = qi[:, h * IDX_DIM:(h + 1) * IDX_DIM].astype(BF16)
    kidx_ref[0, rows] = kw[:, :IDX_DIM].astype(BF16)
    kw_hi, kw_lo = _split(kw)
    w_t = _dot_nt(sel_ref[...], kw_hi) + _dot_nt(sel_ref[...], kw_lo)
    widx_ref[0, blk] = w_t * (IDX_HEADS ** -0.5 * IDX_DIM ** -0.5)


def _prep_b(pb, rows, blk, w0_ref, w2_ref, a0_ref, a2_ref, g2_ref, kk_ref, ka_ref, rk_ref, eb_ref, tri_ref,
            rt_ref, kt_ref, bt_ref, kl_ref, v_ref, g_ref, bv_ref, pc_ref):
    r = pb[:, :B_WIDTH]
    k = pb[:, B_WIDTH:2 * B_WIDTH]
    v = pb[:, 2 * B_WIDTH:3 * B_WIDTH]
    o = 3 * B_WIDTH
    xw = pb[:, o:o + W_LORA]
    xa = pb[:, o + W_LORA:o + W_LORA + A_LORA]
    xg = pb[:, o + W_LORA + A_LORA:o + W_LORA + A_LORA + G_LORA]

    z = w0_ref[...] + _dot(jnp.tanh(xw).astype(BF16), w2_ref[...])
    lw = -EXP_NEG_HALF * jax.nn.sigmoid(z)
    a = jax.nn.sigmoid(a0_ref[...] + _dot(xa.astype(BF16), a2_ref[...]))
    g = _dot(jax.nn.sigmoid(xg).astype(BF16), g2_ref[...])
    kk = k * kk_ref[...]
    kkn = kk * lax.rsqrt(jnp.maximum(_seg_dot_hl(kk * kk, eb_ref[...]), 1e-24))
    kp = k * (1.0 + (a - 1.0) * ka_ref[...])
    bonus = _seg_dot_hl(r * kp * rk_ref[...], eb_ref[...])

    lw_hi, lw_lo = _split(lw)
    cum = _dot(tri_ref[...], lw_hi) + _dot(tri_ref[...], lw_lo)
    e_pos = jnp.exp(cum)
    e_neg = jnp.exp(-cum)
    rt_ref[0, rows] = (r * e_pos).astype(BF16)
    kt_ref[0, rows] = (kkn * jnp.exp(cum - lw)).astype(BF16)
    bt_ref[0, rows] = (kkn * a * e_neg).astype(BF16)
    kl_ref[0, rows] = (kp * e_neg).astype(BF16)
    v_ref[0, rows] = v.astype(BF16)
    g_ref[0, rows] = g
    bv_ref[0, rows] = bonus * v
    for c in range(pb.shape[0] // CHUNK):
        pc_ref[0, blk, c:c + 1, :] = e_pos[(c + 1) * CHUNK - 1:(c + 1) * CHUNK, :]


N_FRONT_IN = 6
N_PREP_A_IN = 8
N_PREP_B_IN = 10
N_PREP_A_OUT = 6


def _front_kernel(*refs):
    x_ref, sh_ref, sc_ref, g_ref, w_ref, mu_ref = refs[:N_FRONT_IN]
    a_in = refs[N_FRONT_IN:N_FRONT_IN + N_PREP_A_IN]
    b_in = refs[N_FRONT_IN + N_PREP_A_IN:N_FRONT_IN + N_PREP_A_IN + N_PREP_B_IN]
    outs = refs[N_FRONT_IN + N_PREP_A_IN + N_PREP_B_IN:-1]
    carry_ref = refs[-1]
    j = pl.program_id(1)

    @pl.when(j == 0)
    def _():
        carry_ref[...] = jnp.zeros_like(carry_ref)

    x = x_ref[0]
    y = x * lax.rsqrt(jnp.mean(x * x, axis=-1, keepdims=True) + RMS_EPS) * g_ref[...]
    h = y * (1.0 + sc_ref[0, 0]) + sh_ref[0, 0]
    p = _dot(h.astype(BF16), w_ref[...])
    pb = p[:, N_A_PAD:]
    tm = pb.shape[0]
    row = lax.broadcasted_iota(jnp.int32, (tm, 1), 0)
    prev = jnp.where(row == 0, carry_ref[...], pltpu.roll(pb, 1, axis=0))
    carry_ref[...] = pb[tm - 1:tm, :]
    pb = pb + mu_ref[...] * (prev - pb)
    for blk in range(tm // TOK_TILE):
        rows = slice(blk * TOK_TILE, (blk + 1) * TOK_TILE)
        _prep_a(p[rows, :N_A_PAD], rows, blk, *a_in, *outs[:N_PREP_A_OUT])
        _prep_b(pb[rows], rows, blk, *b_in, *outs[N_PREP_A_OUT:])


def _front_call(x, mod4, g_mix, w_in_p, mu, a_consts, b_consts):
    bsz, s, d = x.shape
    n = w_in_p.shape[1]
    nb = n - N_A_PAD
    tm = FRONT_TILE
    tt = TOK_TILE
    full = lambda arr: pl.BlockSpec(arr.shape, lambda b, j, nd=arr.ndim: (0,) * nd)
    tok = lambda w: pl.BlockSpec((1, tm, w), lambda b, j: (b, j, 0))
    per_tile = lambda r, c: pl.BlockSpec((1, tm // tt, r, c), lambda b, j: (b, j, 0, 0))
    bf = lambda w: jax.ShapeDtypeStruct((bsz, s, w), BF16)
    ff = lambda w: jax.ShapeDtypeStruct((bsz, s, w), F32)
    nt = s // tt
    out_specs = [tok(A_HEADS * KV_LATENT), per_tile(KV_LATENT + ONES_ROWS, tt), tok(A_HEADS * KV_LATENT),
                 pl.BlockSpec((1, IDX_HEADS, tm, IDX_DIM), lambda b, j: (b, 0, j, 0)),
                 tok(IDX_DIM), per_tile(IDX_HEADS, tt)] + [tok(B_WIDTH)] * 7 + [per_tile(tt // CHUNK, B_WIDTH)]
    out_shape = [bf(A_HEADS * KV_LATENT),
                 jax.ShapeDtypeStruct((bsz, nt, KV_LATENT + ONES_ROWS, tt), BF16),
                 bf(A_HEADS * KV_LATENT),
                 jax.ShapeDtypeStruct((bsz, IDX_HEADS, s, IDX_DIM), BF16),
                 bf(IDX_DIM),
                 jax.ShapeDtypeStruct((bsz, nt, IDX_HEADS, tt), F32),
                 bf(B_WIDTH), bf(B_WIDTH), bf(B_WIDTH), bf(B_WIDTH), bf(B_WIDTH), ff(B_WIDTH), ff(B_WIDTH),
                 jax.ShapeDtypeStruct((bsz, nt, tt // CHUNK, B_WIDTH), F32)]
    assert len(a_consts) == N_PREP_A_IN and len(b_consts) == N_PREP_B_IN
    return pl.pallas_call(
        _front_kernel,
        grid=(bsz, s // tm),
        in_specs=[pl.BlockSpec((1, tm, d), lambda b, j: (b, j, 0)),
                  pl.BlockSpec((1, 1, 1, d), lambda b, j: (b, 0, 0, 0)),
                  pl.BlockSpec((1, 1, 1, d), lambda b, j: (b, 1, 0, 0)),
                  full(g_mix),
                  pl.BlockSpec(w_in_p.shape, lambda b, j: (0, 0), pipeline_mode=pl.Buffered(1)),
                  full(mu)] + [full(t) for t in a_consts] + [full(t) for t in b_consts],
        out_specs=out_specs,
        out_shape=out_shape,
        scratch_shapes=[pltpu.VMEM((1, nb), F32)],
        compiler_params=_params(("arbitrary", "arbitrary")),
    )(x, mod4, mod4, g_mix, w_in_p, mu, *a_consts, *b_consts)


def _colsum8(x):
    y = x.reshape(4, K_TILE // 32, 8, Q_TILE)
    return jnp.sum(jnp.sum(y, axis=1), axis=0)


def _colmin8(x):
    y = x.reshape(4, K_TILE // 32, 8, Q_TILE)
    return jnp.min(jnp.min(y, axis=1), axis=0)


def _colmax8(x):
    y = x.reshape(4, K_TILE // 32, 8, Q_TILE)
    return jnp.max(jnp.max(y, axis=1), axis=0)


def _for_key_tiles(nkc, body, init):
    def quad(j, c):
        return body(4 * j + 3, body(4 * j + 2, body(4 * j + 1, body(4 * j, c))))
    c = lax.fori_loop(0, nkc // 4, quad, init)
    base = (nkc // 4) * 4
    c = lax.cond(nkc % 4 >= 2, lambda c: body(base + 1, body(base, c)), lambda c: c, c)
    return lax.cond(nkc % 2 == 1, lambda c: body(nkc - 1, c), lambda c: c, c)


def _head_norm_max(x, hsel_ref):
    return jnp.sqrt(jnp.max(_dot(x * x, hsel_ref[...])))


def _dsa_kernel(topk, qabs_ref, qidx_ref, widx_ref, ckr_ref, cvt_ref, kidx_ref, wuv_ref, lstrict_ref, hsel_ref,
                o_ref, score_ref, dist_ref, logit_ref, p_ref, m_ref, acc_ref, kmax_ref):
    i = pl.program_id(1)
    nkc = i + 1
    t0 = i * Q_TILE
    krow = lax.broadcasted_iota(jnp.int32, (K_TILE, 1), 0)
    qcol = lax.broadcasted_iota(jnp.int32, (1, Q_TILE), 1)
    limit = ((t0 + qcol) // CHUNK + 1) * CHUNK
    kp = jnp.minimum(limit, topk).astype(F32)
    rel = (qcol - krow).astype(F32)

    def p1(kc, carry):
        rmin, rmax = carry
        k = kidx_ref[0, pl.ds(pl.multiple_of(kc * K_TILE, K_TILE), K_TILE), :]
        acc = jnp.zeros((K_TILE, Q_TILE), F32)
        for h in range(IDX_HEADS):
            s = _dot_nt(k, qidx_ref[0, h])
            acc = acc + widx_ref[0, 0, h:h + 1, :] * jnp.maximum(s, 0.0)
        adm = (kc * K_TILE + krow) < limit
        score_ref[kc] = jnp.where(adm, acc, -jnp.inf)
        rmin = jnp.minimum(rmin, _colmin8(jnp.where(adm, acc, jnp.inf)))
        rmax = jnp.maximum(rmax, _colmax8(jnp.where(adm, acc, -jnp.inf)))
        return rmin, rmax

    rmin, rmax = _for_key_tiles(
        nkc, p1, (jnp.full((8, Q_TILE), jnp.inf, F32), jnp.full((8, Q_TILE), -jnp.inf, F32)))
    lo = jnp.min(rmin, axis=0, keepdims=True)
    hi = jnp.max(rmax, axis=0, keepdims=True)

    def count(pred):
        def body(kc, acc):
            return acc + _colsum8(jnp.where(pred(score_ref[kc]), 1.0, 0.0))
        return jnp.sum(lax.fori_loop(0, nkc, body, jnp.zeros((8, Q_TILE), F32)), axis=0, keepdims=True)

    def probe(c):
        lo, hi, cnt_lo = c
        mid = lo + 0.5 * (hi - lo)
        cnt = count(lambda sc: sc >= mid)
        ge = cnt >= kp
        return jnp.where(ge, mid, lo), jnp.where(ge, hi, mid), jnp.where(ge, cnt, cnt_lo)

    def smallest(pred):
        def body(kc, acc):
            sc = score_ref[kc]
            return jnp.minimum(acc, _colmin8(jnp.where(pred(sc), sc, jnp.inf)))
        return jnp.min(lax.fori_loop(0, nkc, body, jnp.full((8, Q_TILE), jnp.inf, F32)), axis=0, keepdims=True)

    def any_true(x):
        return jnp.max(jnp.where(x, 1.0, 0.0)) > 0.0

    lo, _, cnt_lo = lax.fori_loop(0, SEARCH_PROBES, lambda _, c: probe(c), (lo, hi, limit.astype(F32)))

    def step_up(c):
        it, thr, cnt_gt, cnt_ge = c
        up = cnt_gt >= kp
        thr = jnp.where(up, smallest(lambda sc: sc > thr), thr)
        return it + 1, thr, count(lambda sc: sc > thr), jnp.where(up, cnt_gt, cnt_ge)

    thr = smallest(lambda sc: sc >= lo)
    search = lax.while_loop(
        lambda c: jnp.logical_and(c[0] < nkc * K_TILE, any_true(c[2] >= kp)),
        step_up, (jnp.int32(0), thr, count(lambda sc: sc > thr), cnt_lo))
    thr, need, cnt_ge = search[1], kp - search[2], search[3]

    def dist_tile(kc):
        return jnp.abs(rel + (t0 - kc * K_TILE).astype(F32))

    big8 = jnp.full((8, Q_TILE), DIST_BIG, F32)

    def sel_plain():
        def body(kc, near):
            d = jnp.where(score_ref[kc] >= thr, dist_tile(kc), DIST_BIG)
            dist_ref[kc] = d
            return jnp.minimum(near, _colmin8(d))
        return jnp.min(lax.fori_loop(0, nkc, body, big8), axis=0, keepdims=True)

    def sel_ties():
        def body(kc, c):
            run, near = c
            sc = score_ref[kc]
            eq = sc == thr
            eq_f = jnp.where(eq, 1.0, 0.0)
            pre = run + _dot(lstrict_ref[...], eq_f.astype(BF16))
            keep = (sc > thr) | (eq & (pre < need))
            d = jnp.where(keep, dist_tile(kc), DIST_BIG)
            dist_ref[kc] = d
            return run + jnp.sum(_colsum8(eq_f), axis=0, keepdims=True), jnp.minimum(near, _colmin8(d))
        _, near = _for_key_tiles(nkc, body, (jnp.zeros((1, Q_TILE), F32), big8))
        return jnp.min(near, axis=0, keepdims=True)

    near = lax.cond(any_true(cnt_ge != kp), sel_ties, sel_plain)

    acc_ref[...] = jnp.zeros(acc_ref.shape, F32)

    @pl.when(i == 0)
    def _():
        kmax = _head_norm_max(ckr_ref[0, :K_TILE], hsel_ref)
        for kc in range(1, ckr_ref.shape[1] // K_TILE):
            kmax = jnp.maximum(kmax, _head_norm_max(ckr_ref[0, kc * K_TILE:(kc + 1) * K_TILE], hsel_ref))
        kmax_ref[0] = kmax

    bound = _head_norm_max(qabs_ref[0], hsel_ref) * kmax_ref[0] * BOUND_MARGIN

    def att_shifted():
        def body(kc, _):
            d = dist_ref[kc] - near
            for h in range(A_HEADS):
                slope = 2.0 ** (-8.0 * (h + 1) / A_HEADS) * LOG2E
                ck = ckr_ref[0, pl.ds(pl.multiple_of(kc * K_TILE, K_TILE), K_TILE),
                             h * KV_LATENT:(h + 1) * KV_LATENT]
                logit = _dot_nt(ck, qabs_ref[0, :, h * KV_LATENT:(h + 1) * KV_LATENT]) - slope * d
                p_ref[h] = jnp.exp2(logit).astype(BF16)
            cv = cvt_ref[0, kc]
            for h in range(A_HEADS):
                acc_ref[h] = acc_ref[h] + _dot(cv, p_ref[h])
            return 0
        _for_key_tiles(nkc, body, 0)

    def att_online():
        m_ref[...] = jnp.full(m_ref.shape, -jnp.inf, F32)
        _for_key_tiles(nkc, att, 0)

    def att(kc, _):
        dist = dist_ref[kc]
        m_new = []
        for h in range(A_HEADS):
            slope = 2.0 ** (-8.0 * (h + 1) / A_HEADS) * LOG2E
            ck = ckr_ref[0, pl.ds(pl.multiple_of(kc * K_TILE, K_TILE), K_TILE), h * KV_LATENT:(h + 1) * KV_LATENT]
            logit = _dot_nt(ck, qabs_ref[0, :, h * KV_LATENT:(h + 1) * KV_LATENT]) - slope * dist
            logit_ref[h] = logit
            m_new.append(jnp.maximum(m_ref[h], jnp.max(_colmax8(logit), axis=0, keepdims=True)))
        cv = cvt_ref[0, kc]
        for h in range(A_HEADS):
            p = jnp.exp2(logit_ref[h] - m_new[h])
            acc_ref[h] = acc_ref[h] * jnp.exp2(m_ref[h] - m_new[h]) + _dot(cv, p.astype(BF16))
            m_ref[h] = m_new[h]
        return 0

    lax.cond(bound <= EXP_RANGE, att_shifted, att_online)

    for pair in range(A_HEADS // 2):
        o_pair = []
        for hh in range(2):
            a = acc_ref[2 * pair + hh]
            o_t = a[:KV_LATENT] * (1.0 / a[KV_LATENT:KV_LATENT + 1])
            o_pair.append(o_t.T.astype(BF16))
        o_lat = jnp.concatenate(o_pair, axis=1)
        o_ref[0, :, pair * LANES:(pair + 1) * LANES] = _dot(o_lat, wuv_ref[pair]).astype(o_ref.dtype)


def _dsa_call(topk, qabs, qidx, widx, ckr, cvt, kidx, wuv_pair, lstrict, hsel):
    bsz, s, _ = qabs.shape
    nq = s // Q_TILE
    nk = s // K_TILE
    qt = lambda w: pl.BlockSpec((1, Q_TILE, w), lambda b, i: (b, i, 0))
    return pl.pallas_call(
        functools.partial(_dsa_kernel, topk),
        grid=(bsz, nq),
        in_specs=[qt(A_HEADS * KV_LATENT),
                  pl.BlockSpec((1, IDX_HEADS, Q_TILE, IDX_DIM), lambda b, i: (b, 0, i, 0)),
                  pl.BlockSpec((1, 1, IDX_HEADS, Q_TILE), lambda b, i: (b, i, 0, 0)),
                  pl.BlockSpec((1, s, A_HEADS * KV_LATENT), lambda b, i: (b, 0, 0)),
                  pl.BlockSpec((1, nk, KV_LATENT + ONES_ROWS, K_TILE), lambda b, i: (b, 0, 0, 0)),
                  pl.BlockSpec((1, s, IDX_DIM), lambda b, i: (b, 0, 0)),
                  pl.BlockSpec((A_HEADS // 2, 2 * KV_LATENT, LANES), lambda b, i: (0, 0, 0)),
                  pl.BlockSpec((K_TILE, K_TILE), lambda b, i: (0, 0)),
                  pl.BlockSpec((A_HEADS * KV_LATENT, LANES), lambda b, i: (0, 0))],
        out_specs=qt(A_WIDTH),
        out_shape=jax.ShapeDtypeStruct((bsz, s, A_WIDTH), BF16),
        scratch_shapes=[pltpu.VMEM((nk, K_TILE, Q_TILE), F32),
                        pltpu.VMEM((nk, K_TILE, Q_TILE), F32),
                        pltpu.VMEM((A_HEADS, K_TILE, Q_TILE), F32),
                        pltpu.VMEM((A_HEADS, K_TILE, Q_TILE), BF16),
                        pltpu.VMEM((A_HEADS, 1, Q_TILE), F32),
                        pltpu.VMEM((A_HEADS, KV_LATENT + ONES_ROWS, Q_TILE), F32),
                        pltpu.SMEM((1,), F32)],
        compiler_params=_params(("arbitrary", "arbitrary")),
    )(qabs, qidx, widx, ckr, cvt, kidx, wuv_pair, lstrict, hsel)


def _rwkv_block(blk, tm, masks, rt_ref, kt_ref, bt_ref, kl_ref, v_ref, pc_ref):
    strict, incl, eye_t, blk_diag, diag, head0 = masks
    nch = tm // CHUNK
    npair = B_HEADS // 2
    rows_b = slice(blk * tm, (blk + 1) * tm)
    zero_b = jnp.zeros((), BF16)
    heads = [(p, hh) for p in range(npair) for hh in range(2)]
    rt, kt, bt, kl, v = [], [], [], [], []
    a_ab, a_ak, m_rb, m_rk = [], [], [], []
    for p in range(npair):
        sl = slice(p * LANES, (p + 1) * LANES)
        rt.append(rt_ref[0, rows_b, sl])
        kt.append(kt_ref[0, rows_b, sl])
        bt.append(bt_ref[0, rows_b, sl])
        kl.append(kl_ref[0, rows_b, sl])
        v.append(v_ref[0, rows_b, sl])
        lhs = jnp.concatenate([jnp.where(head0, kt[p], zero_b), jnp.where(head0, zero_b, kt[p]),
                               jnp.where(head0, rt[p], zero_b), jnp.where(head0, zero_b, rt[p])], axis=0)
        prod = _dot_nt(lhs, jnp.concatenate([bt[p], kl[p]], axis=0))
        for hh in range(2):
            a_ab.append(jnp.where(strict, prod[hh * tm:(hh + 1) * tm, :tm], 0.0))
            a_ak.append(jnp.where(strict, prod[hh * tm:(hh + 1) * tm, tm:], 0.0).astype(BF16))
            m_rb.append(jnp.where(incl, prod[(2 + hh) * tm:(3 + hh) * tm, :tm], 0.0).astype(BF16))
            m_rk.append(jnp.where(incl, prod[(2 + hh) * tm:(3 + hh) * tm, tm:], 0.0).astype(BF16))

    t_inv = [(eye_t - a).astype(BF16) for a in a_ab]
    a_pow = [a.astype(BF16) for a in a_ab]
    for _ in range(5):
        a_sq = [_dot(a, a) for a in a_pow]
        a_pow = [a.astype(BF16) for a in a_sq]
        t_inv = [_dot(t, (eye_t + a).astype(BF16)).astype(BF16) for t, a in zip(t_inv, a_sq)]

    avm = [_dot(jnp.concatenate([a_ak[i], m_rk[i]], axis=0), v[p]) for i, (p, _) in enumerate(heads)]
    x = [_dot(t_inv[i], jnp.concatenate([kt[p], avm[i][:tm].astype(BF16)], axis=1))
         for i, (p, _) in enumerate(heads)]
    y = [_dot(m_rb[i], x[i].astype(BF16)) for i in range(len(heads))]

    head0_2 = jnp.concatenate([head0, head0], axis=1)
    zeros_b = jnp.zeros((CHUNK, LANES), BF16)
    q_b, ol, g_mat, f_mat = [], [], [], []
    for p in range(npair):
        i0, i1 = 2 * p, 2 * p + 1
        sl = slice(p * LANES, (p + 1) * LANES)
        wu_b = (-jnp.where(head0_2, x[i0], x[i1])).astype(BF16)
        yy = jnp.where(head0_2, y[i0], y[i1])
        q_b.append((rt[p].astype(F32) - yy[:, :LANES]).astype(BF16))
        ol.append(jnp.where(head0, avm[i0][tm:], avm[i1][tm:]) - yy[:, LANES:])
        gp, fp = [], []
        for c in range(nch):
            rows = slice(c * CHUNK, (c + 1) * CHUNK)
            pc = pc_ref[0, blk, c:c + 1, sl]
            bh = (bt[p][rows].astype(F32) * pc).astype(BF16)
            kh = (kl[p][rows].astype(F32) * pc).astype(BF16)
            rhs = jnp.concatenate([wu_b[rows], jnp.concatenate([zeros_b, v[p][rows]], axis=1)], axis=0)
            bw = _dot_tn(jnp.concatenate([bh, kh], axis=0), rhs)
            gp.append((jnp.where(diag, pc, 0.0) + jnp.where(blk_diag, bw[:, :LANES], 0.0)).astype(BF16))
            fp.append(jnp.where(blk_diag, bw[:, LANES:], 0.0))
        g_mat.append(gp)
        f_mat.append(fp)
    return q_b, ol, g_mat, f_mat


def _rwkv_kernel(rt_ref, kt_ref, bt_ref, kl_ref, v_ref, g_ref, bv_ref, pc_ref, lnw_ref, lnb_ref, eb_ref,
                 o_ref, h_ref):
    j = pl.program_id(1)

    @pl.when(j == 0)
    def _():
        h_ref[...] = jnp.zeros_like(h_ref)

    tm = TOK_TILE
    nch = tm // CHUNK
    npair = B_HEADS // 2
    ri = lax.broadcasted_iota(jnp.int32, (tm, tm), 0)
    ci = lax.broadcasted_iota(jnp.int32, (tm, tm), 1)
    same = (ri // CHUNK) == (ci // CHUNK)
    r2 = lax.broadcasted_iota(jnp.int32, (LANES, LANES), 0)
    c2 = lax.broadcasted_iota(jnp.int32, (LANES, LANES), 1)
    lane = lax.broadcasted_iota(jnp.int32, (1, LANES), 1)
    masks = (same & (ri > ci), same & (ri >= ci), jnp.where(ri == ci, 1.0, 0.0),
             (r2 // B_HEAD_DIM) == (c2 // B_HEAD_DIM), r2 == c2, (lane // B_HEAD_DIM) == 0)

    nblk = rt_ref.shape[1] // tm
    blocks = [_rwkv_block(b, tm, masks, rt_ref, kt_ref, bt_ref, kl_ref, v_ref, pc_ref) for b in range(nblk)]

    h = [h_ref[p] for p in range(npair)]
    for b, (q_b, ol, g_mat, f_mat) in enumerate(blocks):
        o_chunks = [[] for _ in range(npair)]
        for c in range(nch):
            rows = slice(c * CHUNK, (c + 1) * CHUNK)
            for p in range(npair):
                h_b = h[p].astype(BF16)
                o_chunks[p].append(_dot(q_b[p][rows], h_b) + ol[p][rows])
                h[p] = _dot(g_mat[p][c], h_b) + f_mat[p][c]
        out = jnp.concatenate([jnp.concatenate(oc, axis=0) for oc in o_chunks], axis=1)

        rows_b = slice(b * tm, (b + 1) * tm)
        eb = eb_ref[...]
        mean = _seg_dot_hl(out, eb) * (1.0 / B_HEAD_DIM)
        d = out - mean
        var = _seg_dot_hl(d * d, eb) * (1.0 / B_HEAD_DIM)
        y = d * lax.rsqrt(var + GN_EPS) * lnw_ref[...] + lnb_ref[...] + bv_ref[0, rows_b]
        o_ref[0, rows_b] = (y * g_ref[0, rows_b]).astype(o_ref.dtype)
    for p in range(npair):
        h_ref[p] = h[p]


def _rwkv_call(rt, kt, bt, kl, v, g, bv, pc, ln_w, ln_b, eb):
    bsz, s, _ = rt.shape
    tm = RWKV_TILE
    tok = pl.BlockSpec((1, tm, B_WIDTH), lambda b, j: (b, j, 0))
    row = pl.BlockSpec((1, B_WIDTH), lambda b, j: (0, 0))
    return pl.pallas_call(
        _rwkv_kernel,
        grid=(bsz, s // tm),
        in_specs=[tok] * 7 + [pl.BlockSpec((1, tm // TOK_TILE, TOK_TILE // CHUNK, B_WIDTH), lambda b, j: (b, j, 0, 0)),
                              row, row, pl.BlockSpec((SEG_K, SEG_K), lambda b, j: (0, 0))],
        out_specs=tok,
        out_shape=jax.ShapeDtypeStruct((bsz, s, B_WIDTH), BF16),
        scratch_shapes=[pltpu.VMEM((B_HEADS // 2, LANES, LANES), F32)],
        compiler_params=_params(("arbitrary", "arbitrary")),
    )(rt, kt, bt, kl, v, g, bv, pc, ln_w, ln_b, eb)


def _ffn_kernel(x_ref, oa_ref, ob_ref, gt1_ref, sh2_ref, sc2_ref, gt2_ref, gf_ref, woa_ref, wob_ref,
                w1_ref, w2_ref, o_ref):
    for r0 in range(0, x_ref.shape[1], FFN_ROWS):
        rows = slice(r0, r0 + FFN_ROWS)
        mix = _dot(oa_ref[0, rows], woa_ref[...]) + _dot(ob_ref[0, rows], wob_ref[...])
        x1 = x_ref[0, rows] + gt1_ref[0, 0] * mix
        y = x1 * lax.rsqrt(jnp.mean(x1 * x1, axis=-1, keepdims=True) + RMS_EPS) * gf_ref[...]
        h2 = (y * (1.0 + sc2_ref[0, 0]) + sh2_ref[0, 0]).astype(BF16)
        u = jnp.maximum(_dot(h2, w1_ref[...]), 0.0)
        o_ref[0, rows] = x1 + gt2_ref[0, 0] * _dot((u * u).astype(BF16), w2_ref[...])


def _ffn_call(x, oa, ob, mod4, g_ffn, w_out_a, w_out_b, w1, w2):
    bsz, s, d = x.shape
    dff = w1.shape[1]
    tm = FFN_TILE
    tok = lambda w: pl.BlockSpec((1, tm, w), lambda b, j: (b, j, 0))
    modk = lambda k: pl.BlockSpec((1, 1, 1, d), lambda b, j, k=k: (b, k, 0, 0))
    res = lambda shape: pl.BlockSpec(shape, lambda b, j: (0, 0), pipeline_mode=pl.Buffered(1))
    return pl.pallas_call(
        _ffn_kernel,
        grid=(bsz, s // tm),
        in_specs=[tok(d), tok(A_WIDTH), tok(B_WIDTH), modk(2), modk(3), modk(4), modk(5),
                  pl.BlockSpec((1, d), lambda b, j: (0, 0)),
                  res((A_WIDTH, d)), res((B_WIDTH, d)), res((d, dff)), res((dff, d))],
        out_specs=tok(d),
        out_shape=jax.ShapeDtypeStruct((bsz, s, d), F32),
        compiler_params=_params(("arbitrary", "arbitrary")),
    )(x, oa, ob, mod4, mod4, mod4, mod4, g_ffn, w_out_a, w_out_b, w1, w2)


def _block_ones(n, blk, dtype=BF16):
    i = jnp.arange(n)
    return ((i[:, None] // blk) == (i[None, :] // blk)).astype(dtype)


def kernel(x, c, w_ada, b_ada, g_mix, g_ffn, w_in, g_q, g_k, g_kv, w_uk, w_uv, mu_shift, w0, w2, a0, a2, g2,
           k_k, k_a, r_k, ln_w, ln_b, w_out, w_ff1, w_ff2):
    bsz, s, d = x.shape
    depth = w_ada.shape[0]
    assert s % Q_TILE == 0 and s % FRONT_TILE == 0 and s % RWKV_TILE == 0 and s % FFN_TILE == 0
    topk = min(TOPK_MAX, s // 4)

    eb = _block_ones(SEG_K, B_HEAD_DIM)
    ex = (jnp.arange(2 * A_HEAD_DIM)[:, None] // A_HEAD_DIM == jnp.arange(2 * KV_LATENT)[None, :] // KV_LATENT
          ).astype(BF16)
    sel = (jnp.arange(LANES)[None, :] == IDX_DIM + jnp.arange(IDX_HEADS)[:, None]).astype(BF16)
    eye_l = jnp.eye(KV_LATENT, dtype=BF16)
    ti = jnp.arange(TOK_TILE)
    tri = (((ti[:, None] // CHUNK) == (ti[None, :] // CHUNK)) & (ti[:, None] >= ti[None, :])).astype(BF16)
    ki = jnp.arange(K_TILE)
    lstrict = (ki[None, :] < ki[:, None]).astype(BF16)
    hsel = (jnp.arange(A_HEADS * KV_LATENT)[:, None] // KV_LATENT == jnp.arange(LANES)[None, :]).astype(BF16)

    for l in range(depth):
        w_a = jnp.pad(w_in[l][:, :N_IN_A], ((0, 0), (0, N_A_PAD - N_IN_A)))
        w_in_p = jnp.concatenate([w_a, w_in[l][:, N_IN_A:]], axis=1).astype(BF16)
        wuk_flat = w_uk[l].reshape(KV_LATENT, A_WIDTH).astype(BF16)
        wuk_t = jnp.transpose(w_uk[l], (1, 2, 0)).reshape(A_HEADS // 2, 2, A_HEAD_DIM, KV_LATENT)
        wuk_bd = (jnp.eye(2, dtype=F32)[None, :, None, :, None] * wuk_t[:, :, :, None, :]).reshape(
            A_HEADS // 2, 2 * A_HEAD_DIM, 2 * KV_LATENT).astype(BF16)
        wuv_t = jnp.transpose(w_uv[l], (1, 0, 2)).reshape(A_HEADS // 2, 2, KV_LATENT, A_HEAD_DIM)
        wuv_pair = (jnp.eye(2, dtype=F32)[None, :, None, :, None] * wuv_t[:, :, :, None, :]).reshape(
            A_HEADS // 2, 2 * KV_LATENT, 2 * A_HEAD_DIM).astype(BF16)
        gqk = jnp.tile(g_q[l] * g_k[l], A_HEADS).reshape(1, A_WIDTH)
        r1 = lambda t: t.reshape(1, -1)

        mod = _mod_call(c, w_ada[l], b_ada[l])
        mod4 = mod.reshape(bsz, 6, 1, d)
        a_consts = (r1(g_kv[l]), gqk, wuk_flat, wuk_bd, eb, ex, sel, eye_l)
        b_consts = (r1(w0[l]), w2[l].astype(BF16), r1(a0[l]), a2[l].astype(BF16), g2[l].astype(BF16),
                    r1(k_k[l]), r1(k_a[l]), r1(r_k[l]), eb, tri)
        ckr, cvt, qabs, qidx, kidx, widx, rt, kt, bt, kl, v, g, bv, pc = _front_call(
            x, mod4, r1(g_mix[l]), w_in_p, r1(mu_shift[l]), a_consts, b_consts)
        o_a = _dsa_call(topk, qabs, qidx, widx, ckr, cvt, kidx, wuv_pair, lstrict, hsel)
        o_b = _rwkv_call(rt, kt, bt, kl, v, g, bv, pc, r1(ln_w[l]), r1(ln_b[l]), eb)
        x = _ffn_call(x, o_a, o_b, mod4, r1(g_ffn[l]), w_out[l][:A_WIDTH].astype(BF16),
                      w_out[l][A_WIDTH:].astype(BF16), w_ff1[l].astype(BF16), w_ff2[l].astype(BF16))
    return x
```

```python
import functools

import jax
import jax.numpy as jnp
from jax import lax
from jax.experimental import pallas as pl
from jax.experimental.pallas import tpu as pltpu

F32 = jnp.float32
BF16 = jnp.bfloat16

CHUNK = 64
A_HEADS = 8
A_HEAD_DIM = 64
A_WIDTH = A_HEADS * A_HEAD_DIM
KV_LATENT = 128
IDX_HEADS = 8
IDX_DIM = 64
TOPK_MAX = 256
B_HEADS = 8
B_HEAD_DIM = 64
B_WIDTH = B_HEADS * B_HEAD_DIM
W_LORA = 64
A_LORA = 64
G_LORA = 128
RMS_EPS = 1e-6
GN_EPS = 64e-5
N_IN_A = A_WIDTH + KV_LATENT + IDX_HEADS * IDX_DIM + IDX_DIM + IDX_HEADS
N_IN_B = 3 * B_WIDTH + W_LORA + A_LORA + G_LORA
N_A_PAD = 1280

LANES = 128
SUBLANES = 8
ADD_CHAINS = 4
MOD_COLS = 1024
SEG_K = 256
TOK_TILE = 256
FRONT_TILE = 512
RWKV_TILE = 512
Q_TILE = 256
FFN_TILE = 512
FFN_ROWS = 256
K_TILE = 256
DIST_BIG = 1e30
ONES_ROWS = 16
LOG2E = 1.4426950408889634
EXP_NEG_HALF = 0.6065306597126334
EXP_RANGE = 90.0
BOUND_MARGIN = 1.02
SEARCH_PROBES = 14
VMEM_LIMIT = 56 * 1024 * 1024


def _dot(a, b):
    return jnp.dot(a, b, preferred_element_type=F32)


def _dot_nt(a, b):
    return lax.dot_general(a, b, (((1,), (1,)), ((), ())), preferred_element_type=F32)


def _dot_tn(a, b):
    return lax.dot_general(a, b, (((0,), (0,)), ((), ())), preferred_element_type=F32)


def _split(x):
    hi = x.astype(BF16)
    lo = (x - hi.astype(F32)).astype(BF16)
    return hi, lo


def _dot_hl(x, e):
    hi, lo = _split(x)
    return _dot(hi, e) + _dot(lo, e)


def _seg_dot_hl(x, e):
    k = e.shape[0]
    return jnp.concatenate([_dot_hl(x[:, j:j + k], e) for j in range(0, x.shape[1], k)], axis=1)


def _seg_dot(x, e):
    k = e.shape[0]
    xb = x.astype(BF16)
    return jnp.concatenate([_dot(xb[:, j:j + k], e) for j in range(0, x.shape[1], k)], axis=1)


def _params(sem):
    return pltpu.CompilerParams(dimension_semantics=sem, vmem_limit_bytes=VMEM_LIMIT)


def _mod_kernel(c_ref, w_ref, b_ref, o_ref):
    c = c_ref[...]
    s = c * jax.nn.sigmoid(c)
    s_hi, s_lo = _split(s)
    w_hi, w_lo = _split(w_ref[...])
    o_ref[...] = _dot(s_hi, w_hi) + _dot(s_hi, w_lo) + _dot(s_lo, w_hi) + b_ref[...]


def _mod_call(c, w_ada, b_ada):
    bsz, d = c.shape
    n = w_ada.shape[1]
    tn = MOD_COLS
    return pl.pallas_call(
        _mod_kernel,
        grid=(n // tn,),
        in_specs=[pl.BlockSpec((bsz, d), lambda j: (0, 0)),
                  pl.BlockSpec((d, tn), lambda j: (0, j)),
                  pl.BlockSpec((1, tn), lambda j: (0, j))],
        out_specs=pl.BlockSpec((bsz, tn), lambda j: (0, j)),
        out_shape=jax.ShapeDtypeStruct((bsz, n), F32),
        compiler_params=_params(("arbitrary",)),
    )(c, w_ada, b_ada.reshape(1, n))


def _prep_a(pa, rows, blk, gkv_ref, gqk_ref, wuk_ref, wukbd_ref, eb_ref, ex_ref, sel_ref, eye_ref,
            ckr_ref, cvt_ref, qabs_ref, qidx_ref, kidx_ref, widx_ref):
    tm = pa.shape[0]
    q = pa[:, :A_WIDTH]
    cl = pa[:, A_WIDTH:A_WIDTH + KV_LATENT]
    o_qi = A_WIDTH + KV_LATENT
    qi = pa[:, o_qi:o_qi + IDX_HEADS * IDX_DIM]
    o_kw = o_qi + IDX_HEADS * IDX_DIM
    kw = pa[:, o_kw:o_kw + LANES]

    ckv = cl * lax.rsqrt(jnp.mean(cl * cl, axis=-1, keepdims=True) + RMS_EPS) * gkv_ref[...]
    ckv_b = ckv.astype(BF16)
    cvt_ref[0, blk, :KV_LATENT, :] = _dot_nt(eye_ref[...], ckv_b).astype(BF16)
    cvt_ref[0, blk, KV_LATENT:, :] = jnp.ones((ONES_ROWS, tm), BF16)
    kf = _dot(ckv_b, wuk_ref[...])
    ss = _seg_dot(kf * kf, ex_ref[...])
    inv_rms = lax.rsqrt(ss * (1.0 / A_HEAD_DIM) + RMS_EPS)
    ckr_ref[0, rows] = (jnp.concatenate([ckv] * A_HEADS, axis=1) * inv_rms).astype(BF16)

    ssq = _seg_dot(q * q, eb_ref[...])
    qh = q * lax.rsqrt(ssq * (1.0 / A_HEAD_DIM) + RMS_EPS) * gqk_ref[...]
    qh_b = qh.astype(BF16)
    for j in range(A_HEADS // 2):
        qabs = _dot(qh_b[:, j * LANES:(j + 1) * LANES], wukbd_ref[j]) * (A_HEAD_DIM ** -0.5 * LOG2E)
        qabs_ref[0, rows, 2 * j * KV_LATENT:2 * (j + 1) * KV_LATENT] = qabs.astype(BF16)
    for h in range(IDX_HEADS):
        qidx_ref[0, h, rows] = qi[:, h * IDX_DIM:(h + 1) * IDX_DIM].astype(BF16)
    kidx_ref[0, rows] = kw[:, :IDX_DIM].astype(BF16)
    kw_hi, kw_lo = _split(kw)
    w_t = _dot_nt(sel_ref[...], kw_hi) + _dot_nt(sel_ref[...], kw_lo)
    widx_ref[0, blk] = w_t * (IDX_HEADS ** -0.5 * IDX_DIM ** -0.5)


def _prep_b(pb, rows, blk, w0_ref, w2_ref, a0_ref, a2_ref, g2_ref, kk_ref, ka_ref, rk_ref, eb_ref, tri_ref,
            rt_ref, kt_ref, bt_ref, kl_ref, v_ref, g_ref, bv_ref, pc_ref):
    r = pb[:, :B_WIDTH]
    k = pb[:, B_WIDTH:2 * B_WIDTH]
    v = pb[:, 2 * B_WIDTH:3 * B_WIDTH]
    o = 3 * B_WIDTH
    xw = pb[:, o:o + W_LORA]
    xa = pb[:, o + W_LORA:o + W_LORA + A_LORA]
    xg = pb[:, o + W_LORA + A_LORA:o + W_LORA + A_LORA + G_LORA]

    z = w0_ref[...] + _dot(jnp.tanh(xw).astype(BF16), w2_ref[...])
    lw = -EXP_NEG_HALF * jax.nn.sigmoid(z)
    a = jax.nn.sigmoid(a0_ref[...] + _dot(xa.astype(BF16), a2_ref[...]))
    g = _dot(jax.nn.sigmoid(xg).astype(BF16), g2_ref[...])
    kk = k * kk_ref[...]
    kkn = kk * lax.rsqrt(jnp.maximum(_seg_dot(kk * kk, eb_ref[...]), 1e-24))
    kp = k * (1.0 + (a - 1.0) * ka_ref[...])
    bonus = _seg_dot(r * kp * rk_ref[...], eb_ref[...])

    lw_hi, lw_lo = _split(lw)
    cum = _dot(tri_ref[...], lw_hi) + _dot(tri_ref[...], lw_lo)
    e_pos = jnp.exp(cum)
    e_neg = jnp.exp(-cum)
    rt_ref[0, rows] = (r * e_pos).astype(BF16)
    kt_ref[0, rows] = (kkn * jnp.exp(cum - lw)).astype(BF16)
    bt_ref[0, rows] = (kkn * a * e_neg).astype(BF16)
    kl_ref[0, rows] = (kp * e_neg).astype(BF16)
    v_ref[0, rows] = v.astype(BF16)
    g_ref[0, rows] = g
    bv_ref[0, rows] = bonus * v
    for c in range(pb.shape[0] // CHUNK):
        pc_ref[0, blk, c:c + 1, :] = e_pos[(c + 1) * CHUNK - 1:(c + 1) * CHUNK, :]


N_FRONT_IN = 6
N_PREP_A_IN = 8
N_PREP_B_IN = 10
N_PREP_A_OUT = 6


def _front_kernel(*refs):
    x_ref, sh_ref, sc_ref, g_ref, w_ref, mu_ref = refs[:N_FRONT_IN]
    a_in = refs[N_FRONT_IN:N_FRONT_IN + N_PREP_A_IN]
    b_in = refs[N_FRONT_IN + N_PREP_A_IN:N_FRONT_IN + N_PREP_A_IN + N_PREP_B_IN]
    outs = refs[N_FRONT_IN + N_PREP_A_IN + N_PREP_B_IN:-1]
    carry_ref = refs[-1]
    j = pl.program_id(1)

    @pl.when(j == 0)
    def _():
        carry_ref[...] = jnp.zeros_like(carry_ref)

    x = x_ref[0]
    y = x * lax.rsqrt(jnp.mean(x * x, axis=-1, keepdims=True) + RMS_EPS) * g_ref[...]
    h = y * (1.0 + sc_ref[0, 0]) + sh_ref[0, 0]
    p = _dot(h.astype(BF16), w_ref[...])
    pb = p[:, N_A_PAD:]
    tm = pb.shape[0]
    row = lax.broadcasted_iota(jnp.int32, (tm, 1), 0)
    prev = jnp.where(row == 0, carry_ref[...], pltpu.roll(pb, 1, axis=0))
    carry_ref[...] = pb[tm - 1:tm, :]
    pb = pb + mu_ref[...] * (prev - pb)
    for blk in range(tm // TOK_TILE):
        rows = slice(blk * TOK_TILE, (blk + 1) * TOK_TILE)
        _prep_a(p[rows, :N_A_PAD], rows, blk, *a_in, *outs[:N_PREP_A_OUT])
        _prep_b(pb[rows], rows, blk, *b_in, *outs[N_PREP_A_OUT:])


def _front_call(x, mod4, g_mix, w_in_p, mu, a_consts, b_consts):
    bsz, s, d = x.shape
    n = w_in_p.shape[1]
    nb = n - N_A_PAD
    tm = FRONT_TILE
    tt = TOK_TILE
    full = lambda arr: pl.BlockSpec(arr.shape, lambda b, j, nd=arr.ndim: (0,) * nd)
    tok = lambda w: pl.BlockSpec((1, tm, w), lambda b, j: (b, j, 0))
    per_tile = lambda r, c: pl.BlockSpec((1, tm // tt, r, c), lambda b, j: (b, j, 0, 0))
    bf = lambda w: jax.ShapeDtypeStruct((bsz, s, w), BF16)
    ff = lambda w: jax.ShapeDtypeStruct((bsz, s, w), F32)
    nt = s // tt
    out_specs = [tok(A_HEADS * KV_LATENT), per_tile(KV_LATENT + ONES_ROWS, tt), tok(A_HEADS * KV_LATENT),
                 pl.BlockSpec((1, IDX_HEADS, tm, IDX_DIM), lambda b, j: (b, 0, j, 0)),
                 tok(IDX_DIM), per_tile(IDX_HEADS, tt)] + [tok(B_WIDTH)] * 7 + [per_tile(tt // CHUNK, B_WIDTH)]
    out_shape = [bf(A_HEADS * KV_LATENT),
                 jax.ShapeDtypeStruct((bsz, nt, KV_LATENT + ONES_ROWS, tt), BF16),
                 bf(A_HEADS * KV_LATENT),
                 jax.ShapeDtypeStruct((bsz, IDX_HEADS, s, IDX_DIM), BF16),
                 bf(IDX_DIM),
                 jax.ShapeDtypeStruct((bsz, nt, IDX_HEADS, tt), F32),
                 bf(B_WIDTH), bf(B_WIDTH), bf(B_WIDTH), bf(B_WIDTH), bf(B_WIDTH), ff(B_WIDTH), ff(B_WIDTH),
                 jax.ShapeDtypeStruct((bsz, nt, tt // CHUNK, B_WIDTH), F32)]
    assert len(a_consts) == N_PREP_A_IN and len(b_consts) == N_PREP_B_IN
    return pl.pallas_call(
        _front_kernel,
        grid=(bsz, s // tm),
        in_specs=[pl.BlockSpec((1, tm, d), lambda b, j: (b, j, 0)),
                  pl.BlockSpec((1, 1, 1, d), lambda b, j: (b, 0, 0, 0)),
                  pl.BlockSpec((1, 1, 1, d), lambda b, j: (b, 1, 0, 0)),
                  full(g_mix),
                  pl.BlockSpec(w_in_p.shape, lambda b, j: (0, 0), pipeline_mode=pl.Buffered(1)),
                  full(mu)] + [full(t) for t in a_consts] + [full(t) for t in b_consts],
        out_specs=out_specs,
        out_shape=out_shape,
        scratch_shapes=[pltpu.VMEM((1, nb), F32)],
        compiler_params=_params(("arbitrary", "arbitrary")),
    )(x, mod4, mod4, g_mix, w_in_p, mu, *a_consts, *b_consts)


def _colsum8(x):
    y = x.reshape(ADD_CHAINS, K_TILE // (ADD_CHAINS * SUBLANES), SUBLANES, Q_TILE)
    return jnp.sum(jnp.sum(y, axis=1), axis=0)


def _colmin8(x):
    y = x.reshape(ADD_CHAINS, K_TILE // (ADD_CHAINS * SUBLANES), SUBLANES, Q_TILE)
    return jnp.min(jnp.min(y, axis=1), axis=0)


def _colmax8(x):
    y = x.reshape(ADD_CHAINS, K_TILE // (ADD_CHAINS * SUBLANES), SUBLANES, Q_TILE)
    return jnp.max(jnp.max(y, axis=1), axis=0)


def _for_key_tiles(nkc, body, init):
    def quad(j, c):
        return body(4 * j + 3, body(4 * j + 2, body(4 * j + 1, body(4 * j, c))))
    c = lax.fori_loop(0, nkc // 4, quad, init)
    base = (nkc // 4) * 4
    c = lax.cond(nkc % 4 >= 2, lambda c: body(base + 1, body(base, c)), lambda c: c, c)
    return lax.cond(nkc % 2 == 1, lambda c: body(nkc - 1, c), lambda c: c, c)


def _head_norm_max(x, hsel_ref):
    return jnp.sqrt(jnp.max(_dot(x * x, hsel_ref[...])))


def _dsa_kernel(topk, qabs_ref, qidx_ref, widx_ref, ckr_ref, cvt_ref, kidx_ref, wuv_ref, lstrict_ref, hsel_ref,
                o_ref, score_ref, dist_ref, logit_ref, p_ref, m_ref, acc_ref, kmax_ref):
    i = pl.program_id(1)
    nkc = i + 1
    t0 = i * Q_TILE
    krow = lax.broadcasted_iota(jnp.int32, (K_TILE, 1), 0)
    qcol = lax.broadcasted_iota(jnp.int32, (1, Q_TILE), 1)
    limit = ((t0 + qcol) // CHUNK + 1) * CHUNK
    kp = jnp.minimum(limit, topk).astype(F32)
    rel = (qcol - krow).astype(F32)

    def p1(kc, carry):
        rmin, rmax = carry
        k = kidx_ref[0, pl.ds(pl.multiple_of(kc * K_TILE, K_TILE), K_TILE), :]
        acc = jnp.zeros((K_TILE, Q_TILE), F32)
        for h in range(IDX_HEADS):
            s = _dot_nt(k, qidx_ref[0, h])
            acc = acc + widx_ref[0, 0, h:h + 1, :] * jnp.maximum(s, 0.0)
        adm = (kc * K_TILE + krow) < limit
        score_ref[kc] = jnp.where(adm, acc, -jnp.inf)
        rmin = jnp.minimum(rmin, _colmin8(jnp.where(adm, acc, jnp.inf)))
        rmax = jnp.maximum(rmax, _colmax8(jnp.where(adm, acc, -jnp.inf)))
        return rmin, rmax

    rmin, rmax = _for_key_tiles(
        nkc, p1, (jnp.full((SUBLANES, Q_TILE), jnp.inf, F32), jnp.full((SUBLANES, Q_TILE), -jnp.inf, F32)))
    lo = jnp.min(rmin, axis=0, keepdims=True)
    hi = jnp.max(rmax, axis=0, keepdims=True)

    def count(pred):
        def body(kc, acc):
            return acc + _colsum8(jnp.where(pred(score_ref[kc]), 1.0, 0.0))
        return jnp.sum(lax.fori_loop(0, nkc, body, jnp.zeros((SUBLANES, Q_TILE), F32)), axis=0, keepdims=True)

    def probe(c):
        lo, hi, cnt_lo = c
        mid = lo + 0.5 * (hi - lo)
        cnt = count(lambda sc: sc >= mid)
        ge = cnt >= kp
        return jnp.where(ge, mid, lo), jnp.where(ge, hi, mid), jnp.where(ge, cnt, cnt_lo)

    def smallest(pred):
        def body(kc, acc):
            sc = score_ref[kc]
            return jnp.minimum(acc, _colmin8(jnp.where(pred(sc), sc, jnp.inf)))
        return jnp.min(lax.fori_loop(0, nkc, body, jnp.full((SUBLANES, Q_TILE), jnp.inf, F32)), axis=0, keepdims=True)

    def any_true(x):
        return jnp.max(jnp.where(x, 1.0, 0.0)) > 0.0

    lo, _, cnt_lo = lax.fori_loop(0, SEARCH_PROBES, lambda _, c: probe(c), (lo, hi, limit.astype(F32)))

    def step_up(c):
        it, thr, cnt_gt, cnt_ge = c
        up = cnt_gt >= kp
        thr = jnp.where(up, smallest(lambda sc: sc > thr), thr)
        return it + 1, thr, count(lambda sc: sc > thr), jnp.where(up, cnt_gt, cnt_ge)

    thr = smallest(lambda sc: sc >= lo)
    search = lax.while_loop(
        lambda c: jnp.logical_and(c[0] < nkc * K_TILE, any_true(c[2] >= kp)),
        step_up, (jnp.int32(0), thr, count(lambda sc: sc > thr), cnt_lo))
    thr, need, cnt_ge = search[1], kp - search[2], search[3]

    def dist_tile(kc):
        return jnp.abs(rel + (t0 - kc * K_TILE).astype(F32))

    big8 = jnp.full((SUBLANES, Q_TILE), DIST_BIG, F32)

    def sel_plain():
        def body(kc, near):
            d = jnp.where(score_ref[kc] >= thr, dist_tile(kc), DIST_BIG)
            dist_ref[kc] = d
            return jnp.minimum(near, _colmin8(d))
        return jnp.min(lax.fori_loop(0, nkc, body, big8), axis=0, keepdims=True)

    def sel_ties():
        def body(kc, c):
            run, near = c
            sc = score_ref[kc]
            eq = sc == thr
            eq_f = jnp.where(eq, 1.0, 0.0)
            pre = run + _dot(lstrict_ref[...], eq_f.astype(BF16))
            keep = (sc > thr) | (eq & (pre < need))
            d = jnp.where(keep, dist_tile(kc), DIST_BIG)
            dist_ref[kc] = d
            return run + jnp.sum(_colsum8(eq_f), axis=0, keepdims=True), jnp.minimum(near, _colmin8(d))
        _, near = _for_key_tiles(nkc, body, (jnp.zeros((1, Q_TILE), F32), big8))
        return jnp.min(near, axis=0, keepdims=True)

    near = lax.cond(any_true(cnt_ge != kp), sel_ties, sel_plain)

    acc_ref[...] = jnp.zeros(acc_ref.shape, F32)

    @pl.when(i == 0)
    def _():
        kmax = _head_norm_max(ckr_ref[0, :K_TILE], hsel_ref)
        for kc in range(1, ckr_ref.shape[1] // K_TILE):
            kmax = jnp.maximum(kmax, _head_norm_max(ckr_ref[0, kc * K_TILE:(kc + 1) * K_TILE], hsel_ref))
        kmax_ref[0] = kmax

    bound = _head_norm_max(qabs_ref[0], hsel_ref) * kmax_ref[0] * BOUND_MARGIN

    def att_shifted():
        def body(kc, _):
            d = dist_ref[kc] - near
            for h in range(A_HEADS):
                slope = 2.0 ** (-8.0 * (h + 1) / A_HEADS) * LOG2E
                ck = ckr_ref[0, pl.ds(pl.multiple_of(kc * K_TILE, K_TILE), K_TILE),
                             h * KV_LATENT:(h + 1) * KV_LATENT]
                logit = _dot_nt(ck, qabs_ref[0, :, h * KV_LATENT:(h + 1) * KV_LATENT]) - slope * d
                p_ref[h] = jnp.exp2(logit).astype(BF16)
            cv = cvt_ref[0, kc]
            for h in range(A_HEADS):
                acc_ref[h] = acc_ref[h] + _dot(cv, p_ref[h])
            return 0
        _for_key_tiles(nkc, body, 0)

    def att_online():
        m_ref[...] = jnp.full(m_ref.shape, -jnp.inf, F32)
        _for_key_tiles(nkc, att, 0)

    def att(kc, _):
        dist = dist_ref[kc]
        m_new = []
        for h in range(A_HEADS):
            slope = 2.0 ** (-8.0 * (h + 1) / A_HEADS) * LOG2E
            ck = ckr_ref[0, pl.ds(pl.multiple_of(kc * K_TILE, K_TILE), K_TILE), h * KV_LATENT:(h + 1) * KV_LATENT]
            logit = _dot_nt(ck, qabs_ref[0, :, h * KV_LATENT:(h + 1) * KV_LATENT]) - slope * dist
            logit_ref[h] = logit
            m_new.append(jnp.maximum(m_ref[h], jnp.max(_colmax8(logit), axis=0, keepdims=True)))
        cv = cvt_ref[0, kc]
        for h in range(A_HEADS):
            p = jnp.exp2(logit_ref[h] - m_new[h])
            acc_ref[h] = acc_ref[h] * jnp.exp2(m_ref[h] - m_new[h]) + _dot(cv, p.astype(BF16))
            m_ref[h] = m_new[h]
        return 0

    lax.cond(bound <= EXP_RANGE, att_shifted, att_online)

    for pair in range(A_HEADS // 2):
        o_pair = []
        for hh in range(2):
            a = acc_ref[2 * pair + hh]
            o_t = a[:KV_LATENT] * (1.0 / a[KV_LATENT:KV_LATENT + 1])
            o_pair.append(o_t.T.astype(BF16))
        o_lat = jnp.concatenate(o_pair, axis=1)
        o_ref[0, :, pair * LANES:(pair + 1) * LANES] = _dot(o_lat, wuv_ref[pair]).astype(o_ref.dtype)


def _dsa_call(topk, qabs, qidx, widx, ckr, cvt, kidx, wuv_pair, lstrict, hsel):
    bsz, s, _ = qabs.shape
    nq = s // Q_TILE
    nk = s // K_TILE
    qt = lambda w: pl.BlockSpec((1, Q_TILE, w), lambda b, i: (b, i, 0))
    return pl.pallas_call(
        functools.partial(_dsa_kernel, topk),
        grid=(bsz, nq),
        in_specs=[qt(A_HEADS * KV_LATENT),
                  pl.BlockSpec((1, IDX_HEADS, Q_TILE, IDX_DIM), lambda b, i: (b, 0, i, 0)),
                  pl.BlockSpec((1, 1, IDX_HEADS, Q_TILE), lambda b, i: (b, i, 0, 0)),
                  pl.BlockSpec((1, s, A_HEADS * KV_LATENT), lambda b, i: (b, 0, 0)),
                  pl.BlockSpec((1, nk, KV_LATENT + ONES_ROWS, K_TILE), lambda b, i: (b, 0, 0, 0)),
                  pl.BlockSpec((1, s, IDX_DIM), lambda b, i: (b, 0, 0)),
                  pl.BlockSpec((A_HEADS // 2, 2 * KV_LATENT, LANES), lambda b, i: (0, 0, 0)),
                  pl.BlockSpec((K_TILE, K_TILE), lambda b, i: (0, 0)),
                  pl.BlockSpec((A_HEADS * KV_LATENT, LANES), lambda b, i: (0, 0))],
        out_specs=qt(A_WIDTH),
        out_shape=jax.ShapeDtypeStruct((bsz, s, A_WIDTH), BF16),
        scratch_shapes=[pltpu.VMEM((nk, K_TILE, Q_TILE), F32),
                        pltpu.VMEM((nk, K_TILE, Q_TILE), F32),
                        pltpu.VMEM((A_HEADS, K_TILE, Q_TILE), F32),
                        pltpu.VMEM((A_HEADS, K_TILE, Q_TILE), BF16),
                        pltpu.VMEM((A_HEADS, 1, Q_TILE), F32),
                        pltpu.VMEM((A_HEADS, KV_LATENT + ONES_ROWS, Q_TILE), F32),
                        pltpu.SMEM((1,), F32)],
        compiler_params=_params(("arbitrary", "arbitrary")),
    )(qabs, qidx, widx, ckr, cvt, kidx, wuv_pair, lstrict, hsel)


def _rwkv_block(blk, tm, masks, rt_ref, kt_ref, bt_ref, kl_ref, v_ref, pc_ref):
    strict, incl, eye_t, blk_diag, diag, head0 = masks
    nch = tm // CHUNK
    npair = B_HEADS // 2
    rows_b = slice(blk * tm, (blk + 1) * tm)
    zero_b = jnp.zeros((), BF16)
    heads = [(p, hh) for p in range(npair) for hh in range(2)]
    rt, kt, bt, kl, v = [], [], [], [], []
    a_ab, a_ak, m_rb, m_rk = [], [], [], []
    for p in range(npair):
        sl = slice(p * LANES, (p + 1) * LANES)
        rt.append(rt_ref[0, rows_b, sl])
        kt.append(kt_ref[0, rows_b, sl])
        bt.append(bt_ref[0, rows_b, sl])
        kl.append(kl_ref[0, rows_b, sl])
        v.append(v_ref[0, rows_b, sl])
        lhs = jnp.concatenate([jnp.where(head0, kt[p], zero_b), jnp.where(head0, zero_b, kt[p]),
                               jnp.where(head0, rt[p], zero_b), jnp.where(head0, zero_b, rt[p])], axis=0)
        prod = _dot_nt(lhs, jnp.concatenate([bt[p], kl[p]], axis=0))
        for hh in range(2):
            a_ab.append(jnp.where(strict, prod[hh * tm:(hh + 1) * tm, :tm], 0.0))
            a_ak.append(jnp.where(strict, prod[hh * tm:(hh + 1) * tm, tm:], 0.0).astype(BF16))
            m_rb.append(jnp.where(incl, prod[(2 + hh) * tm:(3 + hh) * tm, :tm], 0.0).astype(BF16))
            m_rk.append(jnp.where(incl, prod[(2 + hh) * tm:(3 + hh) * tm, tm:], 0.0).astype(BF16))

    t_inv = [(eye_t - a).astype(BF16) for a in a_ab]
    a_pow = [a.astype(BF16) for a in a_ab]
    for _ in range(5):
        a_sq = [_dot(a, a) for a in a_pow]
        a_pow = [a.astype(BF16) for a in a_sq]
        t_inv = [_dot(t, (eye_t + a).astype(BF16)).astype(BF16) for t, a in zip(t_inv, a_sq)]

    avm = [_dot(jnp.concatenate([a_ak[i], m_rk[i]], axis=0), v[p]) for i, (p, _) in enumerate(heads)]
    x = [_dot(t_inv[i], jnp.concatenate([kt[p], avm[i][:tm].astype(BF16)], axis=1))
         for i, (p, _) in enumerate(heads)]
    y = [_dot(m_rb[i], x[i].astype(BF16)) for i in range(len(heads))]

    head0_2 = jnp.concatenate([head0, head0], axis=1)
    zeros_b = jnp.zeros((CHUNK, LANES), BF16)
    q_b, ol, g_mat, f_mat = [], [], [], []
    for p in range(npair):
        i0, i1 = 2 * p, 2 * p + 1
        sl = slice(p * LANES, (p + 1) * LANES)
        wu_b = (-jnp.where(head0_2, x[i0], x[i1])).astype(BF16)
        yy = jnp.where(head0_2, y[i0], y[i1])
        q_b.append((rt[p].astype(F32) - yy[:, :LANES]).astype(BF16))
        ol.append(jnp.where(head0, avm[i0][tm:], avm[i1][tm:]) - yy[:, LANES:])
        gp, fp = [], []
        for c in range(nch):
            rows = slice(c * CHUNK, (c + 1) * CHUNK)
            pc = pc_ref[0, blk, c:c + 1, sl]
            bh = (bt[p][rows].astype(F32) * pc).astype(BF16)
            kh = (kl[p][rows].astype(F32) * pc).astype(BF16)
            rhs = jnp.concatenate([wu_b[rows], jnp.concatenate([zeros_b, v[p][rows]], axis=1)], axis=0)
            bw = _dot_tn(jnp.concatenate([bh, kh], axis=0), rhs)
            gp.append((jnp.where(diag, pc, 0.0) + jnp.where(blk_diag, bw[:, :LANES], 0.0)).astype(BF16))
            fp.append(jnp.where(blk_diag, bw[:, LANES:], 0.0))
        g_mat.append(gp)
        f_mat.append(fp)
    return q_b, ol, g_mat, f_mat


def _rwkv_kernel(rt_ref, kt_ref, bt_ref, kl_ref, v_ref, g_ref, bv_ref, pc_ref, lnw_ref, lnb_ref, eb_ref,
                 o_ref, h_ref):
    j = pl.program_id(1)

    @pl.when(j == 0)
    def _():
        h_ref[...] = jnp.zeros_like(h_ref)

    tm = TOK_TILE
    nch = tm // CHUNK
    npair = B_HEADS // 2
    ri = lax.broadcasted_iota(jnp.int32, (tm, tm), 0)
    ci = lax.broadcasted_iota(jnp.int32, (tm, tm), 1)
    same = (ri // CHUNK) == (ci // CHUNK)
    r2 = lax.broadcasted_iota(jnp.int32, (LANES, LANES), 0)
    c2 = lax.broadcasted_iota(jnp.int32, (LANES, LANES), 1)
    lane = lax.broadcasted_iota(jnp.int32, (1, LANES), 1)
    masks = (same & (ri > ci), same & (ri >= ci), jnp.where(ri == ci, 1.0, 0.0),
             (r2 // B_HEAD_DIM) == (c2 // B_HEAD_DIM), r2 == c2, (lane // B_HEAD_DIM) == 0)

    nblk = rt_ref.shape[1] // tm
    blocks = [_rwkv_block(b, tm, masks, rt_ref, kt_ref, bt_ref, kl_ref, v_ref, pc_ref) for b in range(nblk)]

    h = [h_ref[p] for p in range(npair)]
    for b, (q_b, ol, g_mat, f_mat) in enumerate(blocks):
        o_chunks = [[] for _ in range(npair)]
        for c in range(nch):
            rows = slice(c * CHUNK, (c + 1) * CHUNK)
            for p in range(npair):
                h_b = h[p].astype(BF16)
                o_chunks[p].append(_dot(q_b[p][rows], h_b) + ol[p][rows])
                h[p] = _dot(g_mat[p][c], h_b) + f_mat[p][c]
        out = jnp.concatenate([jnp.concatenate(oc, axis=0) for oc in o_chunks], axis=1)

        rows_b = slice(b * tm, (b + 1) * tm)
        eb = eb_ref[...]
        mean = _seg_dot_hl(out, eb) * (1.0 / B_HEAD_DIM)
        d = out - mean
        var = _seg_dot(d * d, eb) * (1.0 / B_HEAD_DIM)
        y = d * lax.rsqrt(var + GN_EPS) * lnw_ref[...] + lnb_ref[...] + bv_ref[0, rows_b]
        o_ref[0, rows_b] = (y * g_ref[0, rows_b]).astype(o_ref.dtype)
    for p in range(npair):
        h_ref[p] = h[p]


def _rwkv_call(rt, kt, bt, kl, v, g, bv, pc, ln_w, ln_b, eb):
    bsz, s, _ = rt.shape
    tm = RWKV_TILE
    tok = pl.BlockSpec((1, tm, B_WIDTH), lambda b, j: (b, j, 0))
    row = pl.BlockSpec((1, B_WIDTH), lambda b, j: (0, 0))
    return pl.pallas_call(
        _rwkv_kernel,
        grid=(bsz, s // tm),
        in_specs=[tok] * 7 + [pl.BlockSpec((1, tm // TOK_TILE, TOK_TILE // CHUNK, B_WIDTH), lambda b, j: (b, j, 0, 0)),
                              row, row, pl.BlockSpec((SEG_K, SEG_K), lambda b, j: (0, 0))],
        out_specs=tok,
        out_shape=jax.ShapeDtypeStruct((bsz, s, B_WIDTH), BF16),
        scratch_shapes=[pltpu.VMEM((B_HEADS // 2, LANES, LANES), F32)],
        compiler_params=_params(("arbitrary", "arbitrary")),
    )(rt, kt, bt, kl, v, g, bv, pc, ln_w, ln_b, eb)


def _ffn_kernel(x_ref, oa_ref, ob_ref, gt1_ref, sh2_ref, sc2_ref, gt2_ref, gf_ref, woa_ref, wob_ref,
                w1_ref, w2_ref, o_ref):
    for r0 in range(0, x_ref.shape[1], FFN_ROWS):
        rows = slice(r0, r0 + FFN_ROWS)
        mix = _dot(oa_ref[0, rows], woa_ref[...]) + _dot(ob_ref[0, rows], wob_ref[...])
        x1 = x_ref[0, rows] + gt1_ref[0, 0] * mix
        y = x1 * lax.rsqrt(jnp.mean(x1 * x1, axis=-1, keepdims=True) + RMS_EPS) * gf_ref[...]
        h2 = (y * (1.0 + sc2_ref[0, 0]) + sh2_ref[0, 0]).astype(BF16)
        u = jnp.maximum(_dot(h2, w1_ref[...]), 0.0)
        o_ref[0, rows] = x1 + gt2_ref[0, 0] * _dot((u * u).astype(BF16), w2_ref[...])


def _ffn_call(x, oa, ob, mod4, g_ffn, w_out_a, w_out_b, w1, w2):
    bsz, s, d = x.shape
    dff = w1.shape[1]
    tm = FFN_TILE
    tok = lambda w: pl.BlockSpec((1, tm, w), lambda b, j: (b, j, 0))
    modk = lambda k: pl.BlockSpec((1, 1, 1, d), lambda b, j, k=k: (b, k, 0, 0))
    res = lambda shape: pl.BlockSpec(shape, lambda b, j: (0, 0), pipeline_mode=pl.Buffered(1))
    return pl.pallas_call(
        _ffn_kernel,
        grid=(bsz, s // tm),
        in_specs=[tok(d), tok(A_WIDTH), tok(B_WIDTH), modk(2), modk(3), modk(4), modk(5),
                  pl.BlockSpec((1, d), lambda b, j: (0, 0)),
                  res((A_WIDTH, d)), res((B_WIDTH, d)), res((d, dff)), res((dff, d))],
        out_specs=tok(d),
        out_shape=jax.ShapeDtypeStruct((bsz, s, d), F32),
        compiler_params=_params(("arbitrary", "arbitrary")),
    )(x, oa, ob, mod4, mod4, mod4, mod4, g_ffn, w_out_a, w_out_b, w1, w2)


def _block_ones(n, blk, dtype=BF16):
    i = jnp.arange(n)
    return ((i[:, None] // blk) == (i[None, :] // blk)).astype(dtype)


def kernel(x, c, w_ada, b_ada, g_mix, g_ffn, w_in, g_q, g_k, g_kv, w_uk, w_uv, mu_shift, w0, w2, a0, a2, g2,
           k_k, k_a, r_k, ln_w, ln_b, w_out, w_ff1, w_ff2):
    bsz, s, d = x.shape
    depth = w_ada.shape[0]
    assert s % Q_TILE == 0 and s % FRONT_TILE == 0 and s % RWKV_TILE == 0 and s % FFN_TILE == 0
    topk = min(TOPK_MAX, s // 4)

    eb = _block_ones(SEG_K, B_HEAD_DIM)
    ex = (jnp.arange(2 * A_HEAD_DIM)[:, None] // A_HEAD_DIM == jnp.arange(2 * KV_LATENT)[None, :] // KV_LATENT
          ).astype(BF16)
    sel = (jnp.arange(LANES)[None, :] == IDX_DIM + jnp.arange(IDX_HEADS)[:, None]).astype(BF16)
    eye_l = jnp.eye(KV_LATENT, dtype=BF16)
    ti = jnp.arange(TOK_TILE)
    tri = (((ti[:, None] // CHUNK) == (ti[None, :] // CHUNK)) & (ti[:, None] >= ti[None, :])).astype(BF16)
    ki = jnp.arange(K_TILE)
    lstrict = (ki[None, :] < ki[:, None]).astype(BF16)
    hsel = (jnp.arange(A_HEADS * KV_LATENT)[:, None] // KV_LATENT == jnp.arange(LANES)[None, :]).astype(BF16)

    for l in range(depth):
        w_a = jnp.pad(w_in[l][:, :N_IN_A], ((0, 0), (0, N_A_PAD - N_IN_A)))
        w_in_p = jnp.concatenate([w_a, w_in[l][:, N_IN_A:]], axis=1).astype(BF16)
        wuk_flat = w_uk[l].reshape(KV_LATENT, A_WIDTH).astype(BF16)
        wuk_t = jnp.transpose(w_uk[l], (1, 2, 0)).reshape(A_HEADS // 2, 2, A_HEAD_DIM, KV_LATENT)
        wuk_bd = (jnp.eye(2, dtype=F32)[None, :, None, :, None] * wuk_t[:, :, :, None, :]).reshape(
            A_HEADS // 2, 2 * A_HEAD_DIM, 2 * KV_LATENT).astype(BF16)
        wuv_t = jnp.transpose(w_uv[l], (1, 0, 2)).reshape(A_HEADS // 2, 2, KV_LATENT, A_HEAD_DIM)
        wuv_pair = (jnp.eye(2, dtype=F32)[None, :, None, :, None] * wuv_t[:, :, :, None, :]).reshape(
            A_HEADS // 2, 2 * KV_LATENT, 2 * A_HEAD_DIM).astype(BF16)
        gqk = jnp.tile(g_q[l] * g_k[l], A_HEADS).reshape(1, A_WIDTH)
        r1 = lambda t: t.reshape(1, -1)

        mod = _mod_call(c, w_ada[l], b_ada[l])
        mod4 = mod.reshape(bsz, 6, 1, d)
        a_consts = (r1(g_kv[l]), gqk, wuk_flat, wuk_bd, eb, ex, sel, eye_l)
        b_consts = (r1(w0[l]), w2[l].astype(BF16), r1(a0[l]), a2[l].astype(BF16), g2[l].astype(BF16),
                    r1(k_k[l]), r1(k_a[l]), r1(r_k[l]), eb, tri)
        ckr, cvt, qabs, qidx, kidx, widx, rt, kt, bt, kl, v, g, bv, pc = _front_call(
            x, mod4, r1(g_mix[l]), w_in_p, r1(mu_shift[l]), a_consts, b_consts)
        o_a = _dsa_call(topk, qabs, qidx, widx, ckr, cvt, kidx, wuv_pair, lstrict, hsel)
        o_b = _rwkv_call(rt, kt, bt, kl, v, g, bv, pc, r1(ln_w[l]), r1(ln_b[l]), eb)
        x = _ffn_call(x, o_a, o_b, mod4, r1(g_ffn[l]), w_out[l][:A_WIDTH].astype(BF16),
                      w_out[l][A_WIDTH:].astype(BF16), w_ff1[l].astype(BF16), w_ff2[l].astype(BF16))
    return x
```

```python
import functools

import jax
import jax.numpy as jnp
from jax import lax
from jax.experimental import pallas as pl
from jax.experimental.pallas import tpu as pltpu

F32 = jnp.float32
BF16 = jnp.bfloat16

CHUNK = 64
A_HEADS = 8
A_HEAD_DIM = 64
A_WIDTH = A_HEADS * A_HEAD_DIM
KV_LATENT = 128
IDX_HEADS = 8
IDX_DIM = 64
TOPK_MAX = 256
B_HEADS = 8
B_HEAD_DIM = 64
B_WIDTH = B_HEADS * B_HEAD_DIM
W_LORA = 64
A_LORA = 64
G_LORA = 128
RMS_EPS = 1e-6
GN_EPS = 64e-5
N_IN_A = A_WIDTH + KV_LATENT + IDX_HEADS * IDX_DIM + IDX_DIM + IDX_HEADS
N_IN_B = 3 * B_WIDTH + W_LORA + A_LORA + G_LORA
N_A_PAD = 1280

LANES = 128
SUBLANES = 8
ADD_CHAINS = 4
MOD_COLS = 1024
SEG_K = 256
TOK_TILE = 256
FRONT_TILE = 512
RWKV_TILE = 512
Q_TILE = 256
FFN_TILE = 512
FFN_ROWS = 256
K_TILE = 256
DIST_BIG = 1e30
ONES_ROWS = 16
LOG2E = 1.4426950408889634
EXP_NEG_HALF = 0.6065306597126334
EXP_RANGE = 90.0
BOUND_MARGIN = 1.02
SEARCH_PROBES = 14
VMEM_LIMIT = 56 * 1024 * 1024


def _dot(a, b):
    return jnp.dot(a, b, preferred_element_type=F32)


def _dot_nt(a, b):
    return lax.dot_general(a, b, (((1,), (1,)), ((), ())), preferred_element_type=F32)


def _dot_tn(a, b):
    return lax.dot_general(a, b, (((0,), (0,)), ((), ())), preferred_element_type=F32)


def _split(x):
    hi = x.astype(BF16)
    lo = (x - hi.astype(F32)).astype(BF16)
    return hi, lo


def _dot_hl(x, e):
    hi, lo = _split(x)
    return _dot(hi, e) + _dot(lo, e)


def _seg_dot_hl(x, e):
    k = e.shape[0]
    return jnp.concatenate([_dot_hl(x[:, j:j + k], e) for j in range(0, x.shape[1], k)], axis=1)


def _seg_dot(x, e):
    k = e.shape[0]
    xb = x.astype(BF16)
    return jnp.concatenate([_dot(xb[:, j:j + k], e) for j in range(0, x.shape[1], k)], axis=1)


def _params(sem):
    return pltpu.CompilerParams(dimension_semantics=sem, vmem_limit_bytes=VMEM_LIMIT)


def _mod_kernel(c_ref, w_ref, b_ref, o_ref):
    c = c_ref[...]
    s = c * jax.nn.sigmoid(c)
    s_hi, s_lo = _split(s)
    w_hi, w_lo = _split(w_ref[...])
    o_ref[...] = _dot(s_hi, w_hi) + _dot(s_hi, w_lo) + _dot(s_lo, w_hi) + b_ref[...]


def _mod_call(c, w_ada, b_ada):
    bsz, d = c.shape
    n = w_ada.shape[1]
    tn = MOD_COLS
    return pl.pallas_call(
        _mod_kernel,
        grid=(n // tn,),
        in_specs=[pl.BlockSpec((bsz, d), lambda j: (0, 0)),
                  pl.BlockSpec((d, tn), lambda j: (0, j)),
                  pl.BlockSpec((1, tn), lambda j: (0, j))],
        out_specs=pl.BlockSpec((bsz, tn), lambda j: (0, j)),
        out_shape=jax.ShapeDtypeStruct((bsz, n), F32),
        compiler_params=_params(("arbitrary",)),
    )(c, w_ada, b_ada.reshape(1, n))


def _prep_a(pa, rows, blk, gkv_ref, gqk_ref, wuk_ref, wukbd_ref, eb_ref, ex_ref, sel_ref, eye_ref,
            ckr_ref, cvt_ref, qabs_ref, qidx_ref, kidx_ref, widx_ref):
    tm = pa.shape[0]
    q = pa[:, :A_WIDTH]
    cl = pa[:, A_WIDTH:A_WIDTH + KV_LATENT]
    o_qi = A_WIDTH + KV_LATENT
    qi = pa[:, o_qi:o_qi + IDX_HEADS * IDX_DIM]
    o_kw = o_qi + IDX_HEADS * IDX_DIM
    kw = pa[:, o_kw:o_kw + LANES]

    ckv = cl * lax.rsqrt(jnp.mean(cl * cl, axis=-1, keepdims=True) + RMS_EPS) * gkv_ref[...]
    ckv_b = ckv.astype(BF16)
    cvt_ref[0, blk, :KV_LATENT, :] = _dot_nt(eye_ref[...], ckv_b).astype(BF16)
    cvt_ref[0, blk, KV_LATENT:, :] = jnp.ones((ONES_ROWS, tm), BF16)
    kf = _dot(ckv_b, wuk_ref[...])
    ss = _seg_dot(kf * kf, ex_ref[...])
    inv_rms = lax.rsqrt(ss * (1.0 / A_HEAD_DIM) + RMS_EPS)
    ckr_ref[0, rows] = (jnp.concatenate([ckv] * A_HEADS, axis=1) * inv_rms).astype(BF16)

    ssq = _seg_dot(q * q, eb_ref[...])
    qh = q * lax.rsqrt(ssq * (1.0 / A_HEAD_DIM) + RMS_EPS) * gqk_ref[...]
    qh_b = qh.astype(BF16)
    for j in range(A_HEADS // 2):
        qabs = _dot(qh_b[:, j * LANES:(j + 1) * LANES], wukbd_ref[j]) * (A_HEAD_DIM ** -0.5 * LOG2E)
        qabs_ref[0, rows, 2 * j * KV_LATENT:2 * (j + 1) * KV_LATENT] = qabs.astype(BF16)
    for h in range(IDX_HEADS):
        qidx_ref[0, h, rows] = qi[:, h * IDX_DIM:(h + 1) * IDX_DIM].astype(BF16)
    kidx_ref[0, rows] = kw[:, :IDX_DIM].astype(BF16)
    kw_hi, kw_lo = _split(kw)
    w_t = _dot_nt(sel_ref[...], kw_hi) + _dot_nt(sel_ref[...], kw_lo)
    widx_ref[0, blk] = w_t * (IDX_HEADS ** -0.5 * IDX_DIM ** -0.5)


def _prep_b(pb, rows, blk, w0_ref, w2_ref, a0_ref, a2_ref, g2_ref, kk_ref, ka_ref, rk_ref, eb_ref, tri_ref,
            rt_ref, kt_ref, bt_ref, kl_ref, v_ref, g_ref, bv_ref, pc_ref):
    r = pb[:, :B_WIDTH]
    k = pb[:, B_WIDTH:2 * B_WIDTH]
    v = pb[:, 2 * B_WIDTH:3 * B_WIDTH]
    o = 3 * B_WIDTH
    xw = pb[:, o:o + W_LORA]
    xa = pb[:, o + W_LORA:o + W_LORA + A_LORA]
    xg = pb[:, o + W_LORA + A_LORA:o + W_LORA + A_LORA + G_LORA]

    z = w0_ref[...] + _dot(jnp.tanh(xw).astype(BF16), w2_ref[...])
    lw = -EXP_NEG_HALF * jax.nn.sigmoid(z)
    a = jax.nn.sigmoid(a0_ref[...] + _dot(xa.astype(BF16), a2_ref[...]))
    g = _dot(jax.nn.sigmoid(xg).astype(BF16), g2_ref[...])
    kk = k * kk_ref[...]
    kkn = kk * lax.rsqrt(jnp.maximum(_seg_dot(kk * kk, eb_ref[...]), 1e-24))
    kp = k * (1.0 + (a - 1.0) * ka_ref[...])
    bonus = _seg_dot(r * kp * rk_ref[...], eb_ref[...])

    lw_hi, lw_lo = _split(lw)
    cum = _dot(tri_ref[...], lw_hi) + _dot(tri_ref[...], lw_lo)
    e_pos = jnp.exp(cum)
    e_neg = jnp.exp(-cum)
    rt_ref[0, rows] = (r * e_pos).astype(BF16)
    kt_ref[0, rows] = (kkn * jnp.exp(cum - lw)).astype(BF16)
    bt_ref[0, rows] = (kkn * a * e_neg).astype(BF16)
    kl_ref[0, rows] = (kp * e_neg).astype(BF16)
    v_ref[0, rows] = v.astype(BF16)
    g_ref[0, rows] = g
    bv_ref[0, rows] = bonus * v
    for c in range(pb.shape[0] // CHUNK):
        pc_ref[0, blk, c:c + 1, :] = e_pos[(c + 1) * CHUNK - 1:(c + 1) * CHUNK, :]


N_FRONT_IN = 6
N_PREP_A_IN = 8
N_PREP_B_IN = 10
N_PREP_A_OUT = 6


def _front_kernel(*refs):
    x_ref, sh_ref, sc_ref, g_ref, w_ref, mu_ref = refs[:N_FRONT_IN]
    a_in = refs[N_FRONT_IN:N_FRONT_IN + N_PREP_A_IN]
    b_in = refs[N_FRONT_IN + N_PREP_A_IN:N_FRONT_IN + N_PREP_A_IN + N_PREP_B_IN]
    outs = refs[N_FRONT_IN + N_PREP_A_IN + N_PREP_B_IN:-1]
    carry_ref = refs[-1]
    j = pl.program_id(1)

    @pl.when(j == 0)
    def _():
        carry_ref[...] = jnp.zeros_like(carry_ref)

    x = x_ref[0]
    y = x * lax.rsqrt(jnp.mean(x * x, axis=-1, keepdims=True) + RMS_EPS) * g_ref[...]
    h = y * (1.0 + sc_ref[0, 0]) + sh_ref[0, 0]
    p = _dot(h.astype(BF16), w_ref[...])
    pb = p[:, N_A_PAD:]
    tm = pb.shape[0]
    row = lax.broadcasted_iota(jnp.int32, (tm, 1), 0)
    prev = jnp.where(row == 0, carry_ref[...], pltpu.roll(pb, 1, axis=0))
    carry_ref[...] = pb[tm - 1:tm, :]
    pb = pb + mu_ref[...] * (prev - pb)
    for blk in range(tm // TOK_TILE):
        rows = slice(blk * TOK_TILE, (blk + 1) * TOK_TILE)
        _prep_a(p[rows, :N_A_PAD], rows, blk, *a_in, *outs[:N_PREP_A_OUT])
        _prep_b(pb[rows], rows, blk, *b_in, *outs[N_PREP_A_OUT:])


def _front_call(x, mod4, g_mix, w_in_p, mu, a_consts, b_consts):
    bsz, s, d = x.shape
    n = w_in_p.shape[1]
    nb = n - N_A_PAD
    tm = FRONT_TILE
    tt = TOK_TILE
    full = lambda arr: pl.BlockSpec(arr.shape, lambda b, j, nd=arr.ndim: (0,) * nd)
    tok = lambda w: pl.BlockSpec((1, tm, w), lambda b, j: (b, j, 0))
    per_tile = lambda r, c: pl.BlockSpec((1, tm // tt, r, c), lambda b, j: (b, j, 0, 0))
    bf = lambda w: jax.ShapeDtypeStruct((bsz, s, w), BF16)
    ff = lambda w: jax.ShapeDtypeStruct((bsz, s, w), F32)
    nt = s // tt
    out_specs = [tok(A_HEADS * KV_LATENT), per_tile(KV_LATENT + ONES_ROWS, tt), tok(A_HEADS * KV_LATENT),
                 pl.BlockSpec((1, IDX_HEADS, tm, IDX_DIM), lambda b, j: (b, 0, j, 0)),
                 tok(IDX_DIM), per_tile(IDX_HEADS, tt)] + [tok(B_WIDTH)] * 7 + [per_tile(tt // CHUNK, B_WIDTH)]
    out_shape = [bf(A_HEADS * KV_LATENT),
                 jax.ShapeDtypeStruct((bsz, nt, KV_LATENT + ONES_ROWS, tt), BF16),
                 bf(A_HEADS * KV_LATENT),
                 jax.ShapeDtypeStruct((bsz, IDX_HEADS, s, IDX_DIM), BF16),
                 bf(IDX_DIM),
                 jax.ShapeDtypeStruct((bsz, nt, IDX_HEADS, tt), F32),
                 bf(B_WIDTH), bf(B_WIDTH), bf(B_WIDTH), bf(B_WIDTH), bf(B_WIDTH), ff(B_WIDTH), ff(B_WIDTH),
                 jax.ShapeDtypeStruct((bsz, nt, tt // CHUNK, B_WIDTH), F32)]
    assert len(a_consts) == N_PREP_A_IN and len(b_consts) == N_PREP_B_IN
    return pl.pallas_call(
        _front_kernel,
        grid=(bsz, s // tm),
        in_specs=[pl.BlockSpec((1, tm, d), lambda b, j: (b, j, 0)),
                  pl.BlockSpec((1, 1, 1, d), lambda b, j: (b, 0, 0, 0)),
                  pl.BlockSpec((1, 1, 1, d), lambda b, j: (b, 1, 0, 0)),
                  full(g_mix),
                  pl.BlockSpec(w_in_p.shape, lambda b, j: (0, 0), pipeline_mode=pl.Buffered(1)),
                  full(mu)] + [full(t) for t in a_consts] + [full(t) for t in b_consts],
        out_specs=out_specs,
        out_shape=out_shape,
        scratch_shapes=[pltpu.VMEM((1, nb), F32)],
        compiler_params=_params(("arbitrary", "arbitrary")),
    )(x, mod4, mod4, g_mix, w_in_p, mu, *a_consts, *b_consts)


def _colsum8(x):
    y = x.reshape(ADD_CHAINS, K_TILE // (ADD_CHAINS * SUBLANES), SUBLANES, Q_TILE)
    return jnp.sum(jnp.sum(y, axis=1), axis=0)


def _colmin8(x):
    y = x.reshape(ADD_CHAINS, K_TILE // (ADD_CHAINS * SUBLANES), SUBLANES, Q_TILE)
    return jnp.min(jnp.min(y, axis=1), axis=0)


def _colmax8(x):
    y = x.reshape(ADD_CHAINS, K_TILE // (ADD_CHAINS * SUBLANES), SUBLANES, Q_TILE)
    return jnp.max(jnp.max(y, axis=1), axis=0)


def _for_key_tiles(nkc, body, init):
    def quad(j, c):
        return body(4 * j + 3, body(4 * j + 2, body(4 * j + 1, body(4 * j, c))))
    c = lax.fori_loop(0, nkc // 4, quad, init)
    base = (nkc // 4) * 4
    c = lax.cond(nkc % 4 >= 2, lambda c: body(base + 1, body(base, c)), lambda c: c, c)
    return lax.cond(nkc % 2 == 1, lambda c: body(nkc - 1, c), lambda c: c, c)


def _head_norm_max(x, hsel_ref):
    return jnp.sqrt(jnp.max(_dot(x * x, hsel_ref[...])))


def _dsa_kernel(topk, qabs_ref, qidx_ref, widx_ref, ckr_ref, cvt_ref, kidx_ref, wuv_ref, lstrict_ref, hsel_ref,
                o_ref, score_ref, dist_ref, logit_ref, p_ref, m_ref, acc_ref, kmax_ref):
    i = pl.program_id(1)
    nkc = i + 1
    t0 = i * Q_TILE
    krow = lax.broadcasted_iota(jnp.int32, (K_TILE, 1), 0)
    qcol = lax.broadcasted_iota(jnp.int32, (1, Q_TILE), 1)
    limit = ((t0 + qcol) // CHUNK + 1) * CHUNK
    kp = jnp.minimum(limit, topk).astype(F32)
    rel = (qcol - krow).astype(F32)

    def p1(kc, carry):
        rmin, rmax = carry
        k = kidx_ref[0, pl.ds(pl.multiple_of(kc * K_TILE, K_TILE), K_TILE), :]
        acc = jnp.zeros((K_TILE, Q_TILE), F32)
        for h in range(IDX_HEADS):
            s = _dot_nt(k, qidx_ref[0, h])
            acc = acc + widx_ref[0, 0, h:h + 1, :] * jnp.maximum(s, 0.0)
        adm = (kc * K_TILE + krow) < limit
        score_ref[kc] = jnp.where(adm, acc, -jnp.inf)
        rmin = jnp.minimum(rmin, _colmin8(jnp.where(adm, acc, jnp.inf)))
        rmax = jnp.maximum(rmax, _colmax8(jnp.where(adm, acc, -jnp.inf)))
        return rmin, rmax

    rmin, rmax = _for_key_tiles(
        nkc, p1, (jnp.full((SUBLANES, Q_TILE), jnp.inf, F32), jnp.full((SUBLANES, Q_TILE), -jnp.inf, F32)))
    lo = jnp.min(rmin, axis=0, keepdims=True)
    hi = jnp.max(rmax, axis=0, keepdims=True)

    def count(pred):
        def body(kc, acc):
            return acc + _colsum8(jnp.where(pred(score_ref[kc]), 1.0, 0.0))
        return jnp.sum(lax.fori_loop(0, nkc, body, jnp.zeros((SUBLANES, Q_TILE), F32)), axis=0, keepdims=True)

    def probe(c):
        lo, hi, cnt_lo = c
        mid = lo + 0.5 * (hi - lo)
        cnt = count(lambda sc: sc >= mid)
        ge = cnt >= kp
        return jnp.where(ge, mid, lo), jnp.where(ge, hi, mid), jnp.where(ge, cnt, cnt_lo)

    def smallest(pred):
        def body(kc, acc):
            sc = score_ref[kc]
            return jnp.minimum(acc, _colmin8(jnp.where(pred(sc), sc, jnp.inf)))
        return jnp.min(lax.fori_loop(0, nkc, body, jnp.full((SUBLANES, Q_TILE), jnp.inf, F32)), axis=0, keepdims=True)

    def any_true(x):
        return jnp.max(jnp.where(x, 1.0, 0.0)) > 0.0

    lo, _, cnt_lo = lax.fori_loop(0, SEARCH_PROBES, lambda _, c: probe(c), (lo, hi, limit.astype(F32)))

    def step_up(c):
        it, thr, cnt_gt, cnt_ge = c
        up = cnt_gt >= kp
        thr = jnp.where(up, smallest(lambda sc: sc > thr), thr)
        return it + 1, thr, count(lambda sc: sc > thr), jnp.where(up, cnt_gt, cnt_ge)

    thr = smallest(lambda sc: sc >= lo)
    search = lax.while_loop(
        lambda c: jnp.logical_and(c[0] < nkc * K_TILE, any_true(c[2] >= kp)),
        step_up, (jnp.int32(0), thr, count(lambda sc: sc > thr), cnt_lo))
    thr, need, cnt_ge = search[1], kp - search[2], search[3]

    def dist_tile(kc):
        return jnp.abs(rel + (t0 - kc * K_TILE).astype(F32))

    big8 = jnp.full((SUBLANES, Q_TILE), DIST_BIG, F32)

    def sel_plain():
        def body(kc, near):
            d = jnp.where(score_ref[kc] >= thr, dist_tile(kc), DIST_BIG)
            dist_ref[kc] = d
            return jnp.minimum(near, _colmin8(d))
        return jnp.min(lax.fori_loop(0, nkc, body, big8), axis=0, keepdims=True)

    def sel_ties():
        def body(kc, c):
            run, near = c
            sc = score_ref[kc]
            eq = sc == thr
            eq_f = jnp.where(eq, 1.0, 0.0)
            pre = run + _dot(lstrict_ref[...], eq_f.astype(BF16))
            keep = (sc > thr) | (eq & (pre < need))
            d = jnp.where(keep, dist_tile(kc), DIST_BIG)
            dist_ref[kc] = d
            return run + jnp.sum(_colsum8(eq_f), axis=0, keepdims=True), jnp.minimum(near, _colmin8(d))
        _, near = _for_key_tiles(nkc, body, (jnp.zeros((1, Q_TILE), F32), big8))
        return jnp.min(near, axis=0, keepdims=True)

    near = lax.cond(any_true(cnt_ge != kp), sel_ties, sel_plain)

    acc_ref[...] = jnp.zeros(acc_ref.shape, F32)

    @pl.when(i == 0)
    def _():
        kmax = _head_norm_max(ckr_ref[0, :K_TILE], hsel_ref)
        for kc in range(1, ckr_ref.shape[1] // K_TILE):
            kmax = jnp.maximum(kmax, _head_norm_max(ckr_ref[0, kc * K_TILE:(kc + 1) * K_TILE], hsel_ref))
        kmax_ref[0] = kmax

    bound = _head_norm_max(qabs_ref[0], hsel_ref) * kmax_ref[0] * BOUND_MARGIN

    def att_shifted():
        def body(kc, _):
            d = dist_ref[kc] - near
            for h in range(A_HEADS):
                slope = 2.0 ** (-8.0 * (h + 1) / A_HEADS) * LOG2E
                ck = ckr_ref[0, pl.ds(pl.multiple_of(kc * K_TILE, K_TILE), K_TILE),
                             h * KV_LATENT:(h + 1) * KV_LATENT]
                logit = _dot_nt(ck, qabs_ref[0, :, h * KV_LATENT:(h + 1) * KV_LATENT]) - slope * d
                p_ref[h] = jnp.exp2(logit).astype(BF16)
            cv = cvt_ref[0, kc]
            for h in range(A_HEADS):
                acc_ref[h] = acc_ref[h] + _dot(cv, p_ref[h])
            return 0
        _for_key_tiles(nkc, body, 0)

    def att_online():
        m_ref[...] = jnp.full(m_ref.shape, -jnp.inf, F32)
        _for_key_tiles(nkc, att, 0)

    def att(kc, _):
        dist = dist_ref[kc]
        m_new = []
        for h in range(A_HEADS):
            slope = 2.0 ** (-8.0 * (h + 1) / A_HEADS) * LOG2E
            ck = ckr_ref[0, pl.ds(pl.multiple_of(kc * K_TILE, K_TILE), K_TILE), h * KV_LATENT:(h + 1) * KV_LATENT]
            logit = _dot_nt(ck, qabs_ref[0, :, h * KV_LATENT:(h + 1) * KV_LATENT]) - slope * dist
            logit_ref[h] = logit
            m_new.append(jnp.maximum(m_ref[h], jnp.max(_colmax8(logit), axis=0, keepdims=True)))
        cv = cvt_ref[0, kc]
        for h in range(A_HEADS):
            p = jnp.exp2(logit_ref[h] - m_new[h])
            acc_ref[h] = acc_ref[h] * jnp.exp2(m_ref[h] - m_new[h]) + _dot(cv, p.astype(BF16))
            m_ref[h] = m_new[h]
        return 0

    lax.cond(bound <= EXP_RANGE, att_shifted, att_online)

    for pair in range(A_HEADS // 2):
        o_pair = []
        for hh in range(2):
            a = acc_ref[2 * pair + hh]
            o_t = a[:KV_LATENT] * (1.0 / a[KV_LATENT:KV_LATENT + 1])
            o_pair.append(o_t.T.astype(BF16))
        o_lat = jnp.concatenate(o_pair, axis=1)
        o_ref[0, :, pair * LANES:(pair + 1) * LANES] = _dot(o_lat, wuv_ref[pair]).astype(o_ref.dtype)


def _dsa_call(topk, qabs, qidx, widx, ckr, cvt, kidx, wuv_pair, lstrict, hsel):
    bsz, s, _ = qabs.shape
    nq = s // Q_TILE
    nk = s // K_TILE
    qt = lambda w: pl.BlockSpec((1, Q_TILE, w), lambda b, i: (b, i, 0))
    return pl.pallas_call(
        functools.partial(_dsa_kernel, topk),
        grid=(bsz, nq),
        in_specs=[qt(A_HEADS * KV_LATENT),
                  pl.BlockSpec((1, IDX_HEADS, Q_TILE, IDX_DIM), lambda b, i: (b, 0, i, 0)),
                  pl.BlockSpec((1, 1, IDX_HEADS, Q_TILE), lambda b, i: (b, i, 0, 0)),
                  pl.BlockSpec((1, s, A_HEADS * KV_LATENT), lambda b, i: (b, 0, 0)),
                  pl.BlockSpec((1, nk, KV_LATENT + ONES_ROWS, K_TILE), lambda b, i: (b, 0, 0, 0)),
                  pl.BlockSpec((1, s, IDX_DIM), lambda b, i: (b, 0, 0)),
                  pl.BlockSpec((A_HEADS // 2, 2 * KV_LATENT, LANES), lambda b, i: (0, 0, 0)),
                  pl.BlockSpec((K_TILE, K_TILE), lambda b, i: (0, 0)),
                  pl.BlockSpec((A_HEADS * KV_LATENT, LANES), lambda b, i: (0, 0))],
        out_specs=qt(A_WIDTH),
        out_shape=jax.ShapeDtypeStruct((bsz, s, A_WIDTH), BF16),
        scratch_shapes=[pltpu.VMEM((nk, K_TILE, Q_TILE), F32),
                        pltpu.VMEM((nk, K_TILE, Q_TILE), F32),
                        pltpu.VMEM((A_HEADS, K_TILE, Q_TILE), F32),
                        pltpu.VMEM((A_HEADS, K_TILE, Q_TILE), BF16),
                        pltpu.VMEM((A_HEADS, 1, Q_TILE), F32),
                        pltpu.VMEM((A_HEADS, KV_LATENT + ONES_ROWS, Q_TILE), F32),
                        pltpu.SMEM((1,), F32)],
        compiler_params=_params(("arbitrary", "arbitrary")),
    )(qabs, qidx, widx, ckr, cvt, kidx, wuv_pair, lstrict, hsel)


def _rwkv_block(blk, tm, masks, rt_ref, kt_ref, bt_ref, kl_ref, v_ref, pc_ref):
    strict, incl, eye_s, lane_chunk, row_chunk, blk_diag, diag, head0 = masks
    nch = tm // CHUNK
    npair = B_HEADS // 2
    rows_b = slice(blk * tm, (blk + 1) * tm)
    zero_b = jnp.zeros((), BF16)

    def side_by_side(m):
        return sum(m[c * CHUNK:(c + 1) * CHUNK] for c in range(1, nch)) + m[:CHUNK]

    def block_diag(m):
        return jnp.concatenate([jnp.where(lane_chunk == c, m, jnp.zeros((), m.dtype)) for c in range(nch)], axis=0)

    heads = [(p, hh) for p in range(npair) for hh in range(2)]
    rt, kt, bt, kl, v, v_bd = [], [], [], [], [], []
    a_ab, a_ak, m_rb, m_rk = [], [], [], []
    for p in range(npair):
        sl = slice(p * LANES, (p + 1) * LANES)
        rt.append(rt_ref[0, rows_b, sl])
        kt.append(kt_ref[0, rows_b, sl])
        bt.append(bt_ref[0, rows_b, sl])
        kl.append(kl_ref[0, rows_b, sl])
        v.append(v_ref[0, rows_b, sl])
        lhs = jnp.concatenate([jnp.where(head0, kt[p], zero_b), jnp.where(head0, zero_b, kt[p]),
                               jnp.where(head0, rt[p], zero_b), jnp.where(head0, zero_b, rt[p])], axis=0)
        prod = _dot_nt(lhs, jnp.concatenate([bt[p], kl[p]], axis=0))
        v_bd.append(jnp.concatenate([jnp.where(row_chunk == c, v[p], zero_b) for c in range(nch)], axis=1))
        for hh in range(2):
            a_ab.append(side_by_side(jnp.where(strict, prod[hh * tm:(hh + 1) * tm, :tm], 0.0)))
            a_ak.append(side_by_side(jnp.where(strict, prod[hh * tm:(hh + 1) * tm, tm:], 0.0)).astype(BF16))
            m_rb.append(jnp.where(incl, prod[(2 + hh) * tm:(3 + hh) * tm, :tm], 0.0).astype(BF16))
            m_rk.append(side_by_side(jnp.where(incl, prod[(2 + hh) * tm:(3 + hh) * tm, tm:], 0.0)).astype(BF16))

    t_inv = [(eye_s - a).astype(BF16) for a in a_ab]
    a_pow = [a.astype(BF16) for a in a_ab]
    for _ in range(5):
        a_sq = [_dot(a, block_diag(a)) for a in a_pow]
        a_pow = [a.astype(BF16) for a in a_sq]
        t_inv = [_dot(t, block_diag((eye_s + a).astype(BF16))).astype(BF16) for t, a in zip(t_inv, a_sq)]
    t_inv = [block_diag(t) for t in t_inv]

    avm = [_dot(jnp.concatenate([a_ak[i], m_rk[i]], axis=0), v_bd[p]) for i, (p, _) in enumerate(heads)]
    stack = lambda m: jnp.concatenate([m[:, c * LANES:(c + 1) * LANES] for c in range(nch)], axis=0)
    av = [stack(m[:CHUNK]) for m in avm]
    mv = [stack(m[CHUNK:]) for m in avm]
    x = [_dot(t_inv[i], jnp.concatenate([kt[p], av[i].astype(BF16)], axis=1))
         for i, (p, _) in enumerate(heads)]
    y = [_dot(m_rb[i], x[i].astype(BF16)) for i in range(len(heads))]

    head0_2 = jnp.concatenate([head0, head0], axis=1)
    zeros_b = jnp.zeros((CHUNK, LANES), BF16)
    q_b, ol, g_mat, f_mat = [], [], [], []
    for p in range(npair):
        i0, i1 = 2 * p, 2 * p + 1
        sl = slice(p * LANES, (p + 1) * LANES)
        wu_b = (-jnp.where(head0_2, x[i0], x[i1])).astype(BF16)
        yy = jnp.where(head0_2, y[i0], y[i1])
        q_b.append((rt[p].astype(F32) - yy[:, :LANES]).astype(BF16))
        ol.append(jnp.where(head0, mv[i0], mv[i1]) - yy[:, LANES:])
        gp, fp = [], []
        for c in range(nch):
            rows = slice(c * CHUNK, (c + 1) * CHUNK)
            pc = pc_ref[0, blk, c:c + 1, sl]
            bh = (bt[p][rows].astype(F32) * pc).astype(BF16)
            kh = (kl[p][rows].astype(F32) * pc).astype(BF16)
            rhs = jnp.concatenate([wu_b[rows], jnp.concatenate([zeros_b, v[p][rows]], axis=1)], axis=0)
            bw = _dot_tn(jnp.concatenate([bh, kh], axis=0), rhs)
            gp.append((jnp.where(diag, pc, 0.0) + jnp.where(blk_diag, bw[:, :LANES], 0.0)).astype(BF16))
            fp.append(jnp.where(blk_diag, bw[:, LANES:], 0.0))
        g_mat.append(gp)
        f_mat.append(fp)
    return q_b, ol, g_mat, f_mat


def _rwkv_kernel(rt_ref, kt_ref, bt_ref, kl_ref, v_ref, g_ref, bv_ref, pc_ref, lnw_ref, lnb_ref, eb_ref,
                 o_ref, h_ref):
    j = pl.program_id(1)

    @pl.when(j == 0)
    def _():
        h_ref[...] = jnp.zeros_like(h_ref)

    tm = TOK_TILE
    nch = tm // CHUNK
    npair = B_HEADS // 2
    ri = lax.broadcasted_iota(jnp.int32, (tm, tm), 0)
    ci = lax.broadcasted_iota(jnp.int32, (tm, tm), 1)
    same = (ri // CHUNK) == (ci // CHUNK)
    r2 = lax.broadcasted_iota(jnp.int32, (LANES, LANES), 0)
    c2 = lax.broadcasted_iota(jnp.int32, (LANES, LANES), 1)
    lane = lax.broadcasted_iota(jnp.int32, (1, LANES), 1)
    rs = lax.broadcasted_iota(jnp.int32, (CHUNK, tm), 0)
    cs = lax.broadcasted_iota(jnp.int32, (CHUNK, tm), 1)
    masks = (same & (ri > ci), same & (ri >= ci), jnp.where(cs % CHUNK == rs, 1.0, 0.0),
             lax.broadcasted_iota(jnp.int32, (1, tm), 1) // CHUNK, lax.broadcasted_iota(jnp.int32, (tm, 1), 0) // CHUNK,
             (r2 // B_HEAD_DIM) == (c2 // B_HEAD_DIM), r2 == c2, (lane // B_HEAD_DIM) == 0)

    nblk = rt_ref.shape[1] // tm
    blocks = [_rwkv_block(b, tm, masks, rt_ref, kt_ref, bt_ref, kl_ref, v_ref, pc_ref) for b in range(nblk)]

    h = [h_ref[p] for p in range(npair)]
    for b, (q_b, ol, g_mat, f_mat) in enumerate(blocks):
        o_chunks = [[] for _ in range(npair)]
        for c in range(nch):
            rows = slice(c * CHUNK, (c + 1) * CHUNK)
            for p in range(npair):
                h_b = h[p].astype(BF16)
                o_chunks[p].append(_dot(q_b[p][rows], h_b) + ol[p][rows])
                h[p] = _dot(g_mat[p][c], h_b) + f_mat[p][c]
        out = jnp.concatenate([jnp.concatenate(oc, axis=0) for oc in o_chunks], axis=1)

        rows_b = slice(b * tm, (b + 1) * tm)
        eb = eb_ref[...]
        mean = _seg_dot_hl(out, eb) * (1.0 / B_HEAD_DIM)
        d = out - mean
        var = _seg_dot(d * d, eb) * (1.0 / B_HEAD_DIM)
        y = d * lax.rsqrt(var + GN_EPS) * lnw_ref[...] + lnb_ref[...] + bv_ref[0, rows_b]
        o_ref[0, rows_b] = (y * g_ref[0, rows_b]).astype(o_ref.dtype)
    for p in range(npair):
        h_ref[p] = h[p]


def _rwkv_call(rt, kt, bt, kl, v, g, bv, pc, ln_w, ln_b, eb):
    bsz, s, _ = rt.shape
    tm = RWKV_TILE
    tok = pl.BlockSpec((1, tm, B_WIDTH), lambda b, j: (b, j, 0))
    row = pl.BlockSpec((1, B_WIDTH), lambda b, j: (0, 0))
    return pl.pallas_call(
        _rwkv_kernel,
        grid=(bsz, s // tm),
        in_specs=[tok] * 7 + [pl.BlockSpec((1, tm // TOK_TILE, TOK_TILE // CHUNK, B_WIDTH), lambda b, j: (b, j, 0, 0)),
                              row, row, pl.BlockSpec((SEG_K, SEG_K), lambda b, j: (0, 0))],
        out_specs=tok,
        out_shape=jax.ShapeDtypeStruct((bsz, s, B_WIDTH), BF16),
        scratch_shapes=[pltpu.VMEM((B_HEADS // 2, LANES, LANES), F32)],
        compiler_params=_params(("arbitrary", "arbitrary")),
    )(rt, kt, bt, kl, v, g, bv, pc, ln_w, ln_b, eb)


def _ffn_kernel(x_ref, oa_ref, ob_ref, gt1_ref, sh2_ref, sc2_ref, gt2_ref, gf_ref, woa_ref, wob_ref,
                w1_ref, w2_ref, o_ref):
    for r0 in range(0, x_ref.shape[1], FFN_ROWS):
        rows = slice(r0, r0 + FFN_ROWS)
        mix = _dot(oa_ref[0, rows], woa_ref[...]) + _dot(ob_ref[0, rows], wob_ref[...])
        x1 = x_ref[0, rows] + gt1_ref[0, 0] * mix
        y = x1 * lax.rsqrt(jnp.mean(x1 * x1, axis=-1, keepdims=True) + RMS_EPS) * gf_ref[...]
        h2 = (y * (1.0 + sc2_ref[0, 0]) + sh2_ref[0, 0]).astype(BF16)
        u = jnp.maximum(_dot(h2, w1_ref[...]), 0.0)
        o_ref[0, rows] = x1 + gt2_ref[0, 0] * _dot((u * u).astype(BF16), w2_ref[...])


def _ffn_call(x, oa, ob, mod4, g_ffn, w_out_a, w_out_b, w1, w2):
    bsz, s, d = x.shape
    dff = w1.shape[1]
    tm = FFN_TILE
    tok = lambda w: pl.BlockSpec((1, tm, w), lambda b, j: (b, j, 0))
    modk = lambda k: pl.BlockSpec((1, 1, 1, d), lambda b, j, k=k: (b, k, 0, 0))
    res = lambda shape: pl.BlockSpec(shape, lambda b, j: (0, 0), pipeline_mode=pl.Buffered(1))
    return pl.pallas_call(
        _ffn_kernel,
        grid=(bsz, s // tm),
        in_specs=[tok(d), tok(A_WIDTH), tok(B_WIDTH), modk(2), modk(3), modk(4), modk(5),
                  pl.BlockSpec((1, d), lambda b, j: (0, 0)),
                  res((A_WIDTH, d)), res((B_WIDTH, d)), res((d, dff)), res((dff, d))],
        out_specs=tok(d),
        out_shape=jax.ShapeDtypeStruct((bsz, s, d), F32),
        compiler_params=_params(("arbitrary", "arbitrary")),
    )(x, oa, ob, mod4, mod4, mod4, mod4, g_ffn, w_out_a, w_out_b, w1, w2)


def _block_ones(n, blk, dtype=BF16):
    i = jnp.arange(n)
    return ((i[:, None] // blk) == (i[None, :] // blk)).astype(dtype)


def kernel(x, c, w_ada, b_ada, g_mix, g_ffn, w_in, g_q, g_k, g_kv, w_uk, w_uv, mu_shift, w0, w2, a0, a2, g2,
           k_k, k_a, r_k, ln_w, ln_b, w_out, w_ff1, w_ff2):
    bsz, s, d = x.shape
    depth = w_ada.shape[0]
    assert s % Q_TILE == 0 and s % FRONT_TILE == 0 and s % RWKV_TILE == 0 and s % FFN_TILE == 0
    topk = min(TOPK_MAX, s // 4)

    eb = _block_ones(SEG_K, B_HEAD_DIM)
    ex = (jnp.arange(2 * A_HEAD_DIM)[:, None] // A_HEAD_DIM == jnp.arange(2 * KV_LATENT)[None, :] // KV_LATENT
          ).astype(BF16)
    sel = (jnp.arange(LANES)[None, :] == IDX_DIM + jnp.arange(IDX_HEADS)[:, None]).astype(BF16)
    eye_l = jnp.eye(KV_LATENT, dtype=BF16)
    ti = jnp.arange(TOK_TILE)
    tri = (((ti[:, None] // CHUNK) == (ti[None, :] // CHUNK)) & (ti[:, None] >= ti[None, :])).astype(BF16)
    ki = jnp.arange(K_TILE)
    lstrict = (ki[None, :] < ki[:, None]).astype(BF16)
    hsel = (jnp.arange(A_HEADS * KV_LATENT)[:, None] // KV_LATENT == jnp.arange(LANES)[None, :]).astype(BF16)

    for l in range(depth):
        w_a = jnp.pad(w_in[l][:, :N_IN_A], ((0, 0), (0, N_A_PAD - N_IN_A)))
        w_in_p = jnp.concatenate([w_a, w_in[l][:, N_IN_A:]], axis=1).astype(BF16)
        wuk_flat = w_uk[l].reshape(KV_LATENT, A_WIDTH).astype(BF16)
        wuk_t = jnp.transpose(w_uk[l], (1, 2, 0)).reshape(A_HEADS // 2, 2, A_HEAD_DIM, KV_LATENT)
        wuk_bd = (jnp.eye(2, dtype=F32)[None, :, None, :, None] * wuk_t[:, :, :, None, :]).reshape(
            A_HEADS // 2, 2 * A_HEAD_DIM, 2 * KV_LATENT).astype(BF16)
        wuv_t = jnp.transpose(w_uv[l], (1, 0, 2)).reshape(A_HEADS // 2, 2, KV_LATENT, A_HEAD_DIM)
        wuv_pair = (jnp.eye(2, dtype=F32)[None, :, None, :, None] * wuv_t[:, :, :, None, :]).reshape(
            A_HEADS // 2, 2 * KV_LATENT, 2 * A_HEAD_DIM).astype(BF16)
        gqk = jnp.tile(g_q[l] * g_k[l], A_HEADS).reshape(1, A_WIDTH)
        r1 = lambda t: t.reshape(1, -1)

        mod = _mod_call(c, w_ada[l], b_ada[l])
        mod4 = mod.reshape(bsz, 6, 1, d)
        a_consts = (r1(g_kv[l]), gqk, wuk_flat, wuk_bd, eb, ex, sel, eye_l)
        b_consts = (r1(w0[l]), w2[l].astype(BF16), r1(a0[l]), a2[l].astype(BF16), g2[l].astype(BF16),
                    r1(k_k[l]), r1(k_a[l]), r1(r_k[l]), eb, tri)
        ckr, cvt, qabs, qidx, kidx, widx, rt, kt, bt, kl, v, g, bv, pc = _front_call(
            x, mod4, r1(g_mix[l]), w_in_p, r1(mu_shift[l]), a_consts, b_consts)
        o_a = _dsa_call(topk, qabs, qidx, widx, ckr, cvt, kidx, wuv_pair, lstrict, hsel)
        o_b = _rwkv_call(rt, kt, bt, kl, v, g, bv, pc, r1(ln_w[l]), r1(ln_b[l]), eb)
        x = _ffn_call(x, o_a, o_b, mod4, r1(g_ffn[l]), w_out[l][:A_WIDTH].astype(BF16),
                      w_out[l][A_WIDTH:].astype(BF16), w_ff1[l].astype(BF16), w_ff2[l].astype(BF16))
    return x
```

```python
import functools

import jax
import jax.numpy as jnp
from jax import lax
from jax.experimental import pallas as pl
from jax.experimental.pallas import tpu as pltpu

F32 = jnp.float32
BF16 = jnp.bfloat16

CHUNK = 64
A_HEADS = 8
A_HEAD_DIM = 64
A_WIDTH = A_HEADS * A_HEAD_DIM
KV_LATENT = 128
IDX_HEADS = 8
IDX_DIM = 64
TOPK_MAX = 256
B_HEADS = 8
B_HEAD_DIM = 64
B_WIDTH = B_HEADS * B_HEAD_DIM
W_LORA = 64
A_LORA = 64
G_LORA = 128
RMS_EPS = 1e-6
GN_EPS = 64e-5
N_IN_A = A_WIDTH + KV_LATENT + IDX_HEADS * IDX_DIM + IDX_DIM + IDX_HEADS
N_IN_B = 3 * B_WIDTH + W_LORA + A_LORA + G_LORA
N_A_PAD = 1280

LANES = 128
SUBLANES = 8
ADD_CHAINS = 4
MOD_COLS = 1024
SEG_K = 256
TOK_TILE = 256
FRONT_TILE = 512
RWKV_TILE = 512
Q_TILE = 256
FFN_TILE = 512
FFN_ROWS = 256
K_TILE = 256
DIST_BIG = 1e30
ONES_ROWS = 16
LOG2E = 1.4426950408889634
EXP_NEG_HALF = 0.6065306597126334
EXP_RANGE = 90.0
BOUND_MARGIN = 1.02
SEARCH_PROBES = 14
VMEM_LIMIT = 56 * 1024 * 1024


def _dot(a, b):
    return jnp.dot(a, b, preferred_element_type=F32)


def _dot_nt(a, b):
    return lax.dot_general(a, b, (((1,), (1,)), ((), ())), preferred_element_type=F32)


def _dot_tn(a, b):
    return lax.dot_general(a, b, (((0,), (0,)), ((), ())), preferred_element_type=F32)


def _split(x):
    hi = x.astype(BF16)
    lo = (x - hi.astype(F32)).astype(BF16)
    return hi, lo


def _dot_hl(x, e):
    hi, lo = _split(x)
    return _dot(hi, e) + _dot(lo, e)


def _seg_dot_hl(x, e):
    k = e.shape[0]
    return jnp.concatenate([_dot_hl(x[:, j:j + k], e) for j in range(0, x.shape[1], k)], axis=1)


def _seg_dot(x, e):
    k = e.shape[0]
    xb = x.astype(BF16)
    return jnp.concatenate([_dot(xb[:, j:j + k], e) for j in range(0, x.shape[1], k)], axis=1)


def _params(sem):
    return pltpu.CompilerParams(dimension_semantics=sem, vmem_limit_bytes=VMEM_LIMIT)


def _mod_kernel(c_ref, w_ref, b_ref, o_ref):
    c = c_ref[...]
    s = c * jax.nn.sigmoid(c)
    s_hi, s_lo = _split(s)
    w_hi, w_lo = _split(w_ref[...])
    o_ref[...] = _dot(s_hi, w_hi) + _dot(s_hi, w_lo) + _dot(s_lo, w_hi) + b_ref[...]


def _mod_call(c, w_ada, b_ada):
    bsz, d = c.shape
    n = w_ada.shape[1]
    tn = MOD_COLS
    return pl.pallas_call(
        _mod_kernel,
        grid=(n // tn,),
        in_specs=[pl.BlockSpec((bsz, d), lambda j: (0, 0)),
                  pl.BlockSpec((d, tn), lambda j: (0, j)),
                  pl.BlockSpec((1, tn), lambda j: (0, j))],
        out_specs=pl.BlockSpec((bsz, tn), lambda j: (0, j)),
        out_shape=jax.ShapeDtypeStruct((bsz, n), F32),
        compiler_params=_params(("arbitrary",)),
    )(c, w_ada, b_ada.reshape(1, n))


def _prep_a(pa, rows, blk, gkv_ref, gqk_ref, wuk_ref, wukbd_ref, eb_ref, ex_ref, sel_ref, eye_ref,
            ckr_ref, cvt_ref, qabs_ref, qidx_ref, kidx_ref, widx_ref):
    tm = pa.shape[0]
    q = pa[:, :A_WIDTH]
    cl = pa[:, A_WIDTH:A_WIDTH + KV_LATENT]
    o_qi = A_WIDTH + KV_LATENT
    qi = pa[:, o_qi:o_qi + IDX_HEADS * IDX_DIM]
    o_kw = o_qi + IDX_HEADS * IDX_DIM
    kw = pa[:, o_kw:o_kw + LANES]

    ckv = cl * lax.rsqrt(jnp.mean(cl * cl, axis=-1, keepdims=True) + RMS_EPS) * gkv_ref[...]
    ckv_b = ckv.astype(BF16)
    cvt_ref[0, blk, :KV_LATENT, :] = _dot_nt(eye_ref[...], ckv_b).astype(BF16)
    cvt_ref[0, blk, KV_LATENT:, :] = jnp.ones((ONES_ROWS, tm), BF16)
    kf = _dot(ckv_b, wuk_ref[...])
    ss = _seg_dot(kf * kf, ex_ref[...])
    inv_rms = lax.rsqrt(ss * (1.0 / A_HEAD_DIM) + RMS_EPS)
    ckr_ref[0, rows] = (jnp.concatenate([ckv] * A_HEADS, axis=1) * inv_rms).astype(BF16)

    ssq = _seg_dot(q * q, eb_ref[...])
    qh = q * lax.rsqrt(ssq * (1.0 / A_HEAD_DIM) + RMS_EPS) * gqk_ref[...]
    qh_b = qh.astype(BF16)
    for j in range(A_HEADS // 2):
        qabs = _dot(qh_b[:, j * LANES:(j + 1) * LANES], wukbd_ref[j]) * (A_HEAD_DIM ** -0.5 * LOG2E)
        qabs_ref[0, rows, 2 * j * KV_LATENT:2 * (j + 1) * KV_LATENT] = qabs.astype(BF16)
    for h in range(IDX_HEADS):
        qidx_ref[0, h, rows] = qi[:, h * IDX_DIM:(h + 1) * IDX_DIM].astype(BF16)
    kidx_ref[0, rows] = kw[:, :IDX_DIM].astype(BF16)
    kw_hi, kw_lo = _split(kw)
    w_t = _dot_nt(sel_ref[...], kw_hi) + _dot_nt(sel_ref[...], kw_lo)
    widx_ref[0, blk] = w_t * (IDX_HEADS ** -0.5 * IDX_DIM ** -0.5)


def _prep_b(pb, rows, blk, w0_ref, w2_ref, a0_ref, a2_ref, g2_ref, kk_ref, ka_ref, rk_ref, eb_ref, tri_ref,
            rt_ref, kt_ref, bt_ref, kl_ref, v_ref, g_ref, bv_ref, pc_ref):
    r = pb[:, :B_WIDTH]
    k = pb[:, B_WIDTH:2 * B_WIDTH]
    v = pb[:, 2 * B_WIDTH:3 * B_WIDTH]
    o = 3 * B_WIDTH
    xw = pb[:, o:o + W_LORA]
    xa = pb[:, o + W_LORA:o + W_LORA + A_LORA]
    xg = pb[:, o + W_LORA + A_LORA:o + W_LORA + A_LORA + G_LORA]

    z = w0_ref[...] + _dot(jnp.tanh(xw).astype(BF16), w2_ref[...])
    lw = -EXP_NEG_HALF * jax.nn.sigmoid(z)
    a = jax.nn.sigmoid(a0_ref[...] + _dot(xa.astype(BF16), a2_ref[...]))
    g = _dot(jax.nn.sigmoid(xg).astype(BF16), g2_ref[...])
    kk = k * kk_ref[...]
    kkn = kk * lax.rsqrt(jnp.maximum(_seg_dot(kk * kk, eb_ref[...]), 1e-24))
    kp = k * (1.0 + (a - 1.0) * ka_ref[...])
    bonus = _seg_dot(r * kp * rk_ref[...], eb_ref[...])

    lw_hi, lw_lo = _split(lw)
    cum = _dot(tri_ref[...], lw_hi) + _dot(tri_ref[...], lw_lo)
    e_pos = jnp.exp(cum)
    e_neg = jnp.exp(-cum)
    rt_ref[0, rows] = (r * e_pos).astype(BF16)
    kt_ref[0, rows] = (kkn * jnp.exp(cum - lw)).astype(BF16)
    bt_ref[0, rows] = (kkn * a * e_neg).astype(BF16)
    kl_ref[0, rows] = (kp * e_neg).astype(BF16)
    v_ref[0, rows] = v.astype(BF16)
    g_ref[0, rows] = g
    bv_ref[0, rows] = bonus * v
    for c in range(pb.shape[0] // CHUNK):
        pc_ref[0, blk, c:c + 1, :] = e_pos[(c + 1) * CHUNK - 1:(c + 1) * CHUNK, :]


N_FRONT_IN = 6
N_PREP_A_IN = 8
N_PREP_B_IN = 10
N_PREP_A_OUT = 6


def _front_kernel(*refs):
    x_ref, sh_ref, sc_ref, g_ref, w_ref, mu_ref = refs[:N_FRONT_IN]
    a_in = refs[N_FRONT_IN:N_FRONT_IN + N_PREP_A_IN]
    b_in = refs[N_FRONT_IN + N_PREP_A_IN:N_FRONT_IN + N_PREP_A_IN + N_PREP_B_IN]
    outs = refs[N_FRONT_IN + N_PREP_A_IN + N_PREP_B_IN:-1]
    carry_ref = refs[-1]
    j = pl.program_id(1)

    @pl.when(j == 0)
    def _():
        carry_ref[...] = jnp.zeros_like(carry_ref)

    x = x_ref[0]
    y = x * lax.rsqrt(jnp.mean(x * x, axis=-1, keepdims=True) + RMS_EPS) * g_ref[...]
    h = y * (1.0 + sc_ref[0, 0]) + sh_ref[0, 0]
    p = _dot(h.astype(BF16), w_ref[...])
    pb = p[:, N_A_PAD:]
    tm = pb.shape[0]
    row = lax.broadcasted_iota(jnp.int32, (tm, 1), 0)
    prev = jnp.where(row == 0, carry_ref[...], pltpu.roll(pb, 1, axis=0))
    carry_ref[...] = pb[tm - 1:tm, :]
    pb = pb + mu_ref[...] * (prev - pb)
    for blk in range(tm // TOK_TILE):
        rows = slice(blk * TOK_TILE, (blk + 1) * TOK_TILE)
        _prep_a(p[rows, :N_A_PAD], rows, blk, *a_in, *outs[:N_PREP_A_OUT])
        _prep_b(pb[rows], rows, blk, *b_in, *outs[N_PREP_A_OUT:])


def _front_call(x, mod4, g_mix, w_in_p, mu, a_consts, b_consts):
    bsz, s, d = x.shape
    n = w_in_p.shape[1]
    nb = n - N_A_PAD
    tm = FRONT_TILE
    tt = TOK_TILE
    full = lambda arr: pl.BlockSpec(arr.shape, lambda b, j, nd=arr.ndim: (0,) * nd)
    tok = lambda w: pl.BlockSpec((1, tm, w), lambda b, j: (b, j, 0))
    per_tile = lambda r, c: pl.BlockSpec((1, tm // tt, r, c), lambda b, j: (b, j, 0, 0))
    bf = lambda w: jax.ShapeDtypeStruct((bsz, s, w), BF16)
    ff = lambda w: jax.ShapeDtypeStruct((bsz, s, w), F32)
    nt = s // tt
    out_specs = [tok(A_HEADS * KV_LATENT), per_tile(KV_LATENT + ONES_ROWS, tt), tok(A_HEADS * KV_LATENT),
                 pl.BlockSpec((1, IDX_HEADS, tm, IDX_DIM), lambda b, j: (b, 0, j, 0)),
                 tok(IDX_DIM), per_tile(IDX_HEADS, tt)] + [tok(B_WIDTH)] * 7 + [per_tile(tt // CHUNK, B_WIDTH)]
    out_shape = [bf(A_HEADS * KV_LATENT),
                 jax.ShapeDtypeStruct((bsz, nt, KV_LATENT + ONES_ROWS, tt), BF16),
                 bf(A_HEADS * KV_LATENT),
                 jax.ShapeDtypeStruct((bsz, IDX_HEADS, s, IDX_DIM), BF16),
                 bf(IDX_DIM),
                 jax.ShapeDtypeStruct((bsz, nt, IDX_HEADS, tt), F32),
                 bf(B_WIDTH), bf(B_WIDTH), bf(B_WIDTH), bf(B_WIDTH), bf(B_WIDTH), ff(B_WIDTH), ff(B_WIDTH),
                 jax.ShapeDtypeStruct((bsz, nt, tt // CHUNK, B_WIDTH), F32)]
    assert len(a_consts) == N_PREP_A_IN and len(b_consts) == N_PREP_B_IN
    return pl.pallas_call(
        _front_kernel,
        grid=(bsz, s // tm),
        in_specs=[pl.BlockSpec((1, tm, d), lambda b, j: (b, j, 0)),
                  pl.BlockSpec((1, 1, 1, d), lambda b, j: (b, 0, 0, 0)),
                  pl.BlockSpec((1, 1, 1, d), lambda b, j: (b, 1, 0, 0)),
                  full(g_mix),
                  pl.BlockSpec(w_in_p.shape, lambda b, j: (0, 0), pipeline_mode=pl.Buffered(1)),
                  full(mu)] + [full(t) for t in a_consts] + [full(t) for t in b_consts],
        out_specs=out_specs,
        out_shape=out_shape,
        scratch_shapes=[pltpu.VMEM((1, nb), F32)],
        compiler_params=_params(("arbitrary", "arbitrary")),
    )(x, mod4, mod4, g_mix, w_in_p, mu, *a_consts, *b_consts)


def _colsum8(x):
    y = x.reshape(ADD_CHAINS, K_TILE // (ADD_CHAINS * SUBLANES), SUBLANES, Q_TILE)
    return jnp.sum(jnp.sum(y, axis=1), axis=0)


def _colmin8(x):
    y = x.reshape(ADD_CHAINS, K_TILE // (ADD_CHAINS * SUBLANES), SUBLANES, Q_TILE)
    return jnp.min(jnp.min(y, axis=1), axis=0)


def _colmax8(x):
    y = x.reshape(ADD_CHAINS, K_TILE // (ADD_CHAINS * SUBLANES), SUBLANES, Q_TILE)
    return jnp.max(jnp.max(y, axis=1), axis=0)


def _for_key_tiles(nkc, body, init):
    def quad(j, c):
        return body(4 * j + 3, body(4 * j + 2, body(4 * j + 1, body(4 * j, c))))
    c = lax.fori_loop(0, nkc // 4, quad, init)
    base = (nkc // 4) * 4
    c = lax.cond(nkc % 4 >= 2, lambda c: body(base + 1, body(base, c)), lambda c: c, c)
    return lax.cond(nkc % 2 == 1, lambda c: body(nkc - 1, c), lambda c: c, c)


def _head_norm_max(x, hsel_ref):
    return jnp.sqrt(jnp.max(_dot(x * x, hsel_ref[...])))


def _dsa_kernel(topk, qabs_ref, qidx_ref, widx_ref, ckr_ref, cvt_ref, kidx_ref, wuv_ref, lstrict_ref, hsel_ref,
                o_ref, score_ref, dist_ref, logit_ref, p_ref, m_ref, acc_ref, kmax_ref):
    i = pl.program_id(1)
    nkc = i + 1
    t0 = i * Q_TILE
    krow = lax.broadcasted_iota(jnp.int32, (K_TILE, 1), 0)
    qcol = lax.broadcasted_iota(jnp.int32, (1, Q_TILE), 1)
    limit = ((t0 + qcol) // CHUNK + 1) * CHUNK
    kp = jnp.minimum(limit, topk).astype(F32)
    rel = (qcol - krow).astype(F32)

    def p1(kc, carry):
        rmin, rmax = carry
        k = kidx_ref[0, pl.ds(pl.multiple_of(kc * K_TILE, K_TILE), K_TILE), :]
        acc = jnp.zeros((K_TILE, Q_TILE), F32)
        for h in range(IDX_HEADS):
            s = _dot_nt(k, qidx_ref[0, h])
            acc = acc + widx_ref[0, 0, h:h + 1, :] * jnp.maximum(s, 0.0)
        adm = (kc * K_TILE + krow) < limit
        score_ref[kc] = jnp.where(adm, acc, -jnp.inf)
        rmin = jnp.minimum(rmin, _colmin8(jnp.where(adm, acc, jnp.inf)))
        rmax = jnp.maximum(rmax, _colmax8(jnp.where(adm, acc, -jnp.inf)))
        return rmin, rmax

    rmin, rmax = _for_key_tiles(
        nkc, p1, (jnp.full((SUBLANES, Q_TILE), jnp.inf, F32), jnp.full((SUBLANES, Q_TILE), -jnp.inf, F32)))
    lo = jnp.min(rmin, axis=0, keepdims=True)
    hi = jnp.max(rmax, axis=0, keepdims=True)

    def count(pred):
        def body(kc, acc):
            return acc + _colsum8(jnp.where(pred(score_ref[kc]), 1.0, 0.0))
        return jnp.sum(lax.fori_loop(0, nkc, body, jnp.zeros((SUBLANES, Q_TILE), F32)), axis=0, keepdims=True)

    def probe(c):
        lo, hi, cnt_lo = c
        mid = lo + 0.5 * (hi - lo)
        cnt = count(lambda sc: sc >= mid)
        ge = cnt >= kp
        return jnp.where(ge, mid, lo), jnp.where(ge, hi, mid), jnp.where(ge, cnt, cnt_lo)

    def smallest(pred):
        def body(kc, acc):
            sc = score_ref[kc]
            return jnp.minimum(acc, _colmin8(jnp.where(pred(sc), sc, jnp.inf)))
        return jnp.min(lax.fori_loop(0, nkc, body, jnp.full((SUBLANES, Q_TILE), jnp.inf, F32)), axis=0, keepdims=True)

    def any_true(x):
        return jnp.max(jnp.where(x, 1.0, 0.0)) > 0.0

    lo, _, cnt_lo = lax.fori_loop(0, SEARCH_PROBES, lambda _, c: probe(c), (lo, hi, limit.astype(F32)))

    def step_up(c):
        it, thr, cnt_gt, cnt_ge = c
        up = cnt_gt >= kp
        thr = jnp.where(up, smallest(lambda sc: sc > thr), thr)
        return it + 1, thr, count(lambda sc: sc > thr), jnp.where(up, cnt_gt, cnt_ge)

    thr = smallest(lambda sc: sc >= lo)
    search = lax.while_loop(
        lambda c: jnp.logical_and(c[0] < nkc * K_TILE, any_true(c[2] >= kp)),
        step_up, (jnp.int32(0), thr, count(lambda sc: sc > thr), cnt_lo))
    thr, need, cnt_ge = search[1], kp - search[2], search[3]

    def dist_tile(kc):
        return jnp.abs(rel + (t0 - kc * K_TILE).astype(F32))

    big8 = jnp.full((SUBLANES, Q_TILE), DIST_BIG, F32)

    def sel_plain():
        def body(kc, near):
            d = jnp.where(score_ref[kc] >= thr, dist_tile(kc), DIST_BIG)
            dist_ref[kc] = d
            return jnp.minimum(near, _colmin8(d))
        return jnp.min(lax.fori_loop(0, nkc, body, big8), axis=0, keepdims=True)

    def sel_ties():
        def body(kc, c):
            run, near = c
            sc = score_ref[kc]
            eq = sc == thr
            eq_f = jnp.where(eq, 1.0, 0.0)
            pre = run + _dot(lstrict_ref[...], eq_f.astype(BF16))
            keep = (sc > thr) | (eq & (pre < need))
            d = jnp.where(keep, dist_tile(kc), DIST_BIG)
            dist_ref[kc] = d
            return run + jnp.sum(_colsum8(eq_f), axis=0, keepdims=True), jnp.minimum(near, _colmin8(d))
        _, near = _for_key_tiles(nkc, body, (jnp.zeros((1, Q_TILE), F32), big8))
        return jnp.min(near, axis=0, keepdims=True)

    near = lax.cond(any_true(cnt_ge != kp), sel_ties, sel_plain)

    acc_ref[...] = jnp.zeros(acc_ref.shape, F32)

    @pl.when(i == 0)
    def _():
        kmax = _head_norm_max(ckr_ref[0, :K_TILE], hsel_ref)
        for kc in range(1, ckr_ref.shape[1] // K_TILE):
            kmax = jnp.maximum(kmax, _head_norm_max(ckr_ref[0, kc * K_TILE:(kc + 1) * K_TILE], hsel_ref))
        kmax_ref[0] = kmax

    bound = _head_norm_max(qabs_ref[0], hsel_ref) * kmax_ref[0] * BOUND_MARGIN

    def att_shifted():
        def body(kc, _):
            d = dist_ref[kc] - near
            for h in range(A_HEADS):
                slope = 2.0 ** (-8.0 * (h + 1) / A_HEADS) * LOG2E
                ck = ckr_ref[0, pl.ds(pl.multiple_of(kc * K_TILE, K_TILE), K_TILE),
                             h * KV_LATENT:(h + 1) * KV_LATENT]
                logit = _dot_nt(ck, qabs_ref[0, :, h * KV_LATENT:(h + 1) * KV_LATENT]) - slope * d
                p_ref[h] = jnp.exp2(logit).astype(BF16)
            cv = cvt_ref[0, kc]
            for h in range(A_HEADS):
                acc_ref[h] = acc_ref[h] + _dot(cv, p_ref[h])
            return 0
        _for_key_tiles(nkc, body, 0)

    def att_online():
        m_ref[...] = jnp.full(m_ref.shape, -jnp.inf, F32)
        _for_key_tiles(nkc, att, 0)

    def att(kc, _):
        dist = dist_ref[kc]
        m_new = []
        for h in range(A_HEADS):
            slope = 2.0 ** (-8.0 * (h + 1) / A_HEADS) * LOG2E
            ck = ckr_ref[0, pl.ds(pl.multiple_of(kc * K_TILE, K_TILE), K_TILE), h * KV_LATENT:(h + 1) * KV_LATENT]
            logit = _dot_nt(ck, qabs_ref[0, :, h * KV_LATENT:(h + 1) * KV_LATENT]) - slope * dist
            logit_ref[h] = logit
            m_new.append(jnp.maximum(m_ref[h], jnp.max(_colmax8(logit), axis=0, keepdims=True)))
        cv = cvt_ref[0, kc]
        for h in range(A_HEADS):
            p = jnp.exp2(logit_ref[h] - m_new[h])
            acc_ref[h] = acc_ref[h] * jnp.exp2(m_ref[h] - m_new[h]) + _dot(cv, p.astype(BF16))
            m_ref[h] = m_new[h]
        return 0

    lax.cond(bound <= EXP_RANGE, att_shifted, att_online)

    for pair in range(A_HEADS // 2):
        o_pair = []
        for hh in range(2):
            a = acc_ref[2 * pair + hh]
            o_t = a[:KV_LATENT] * (1.0 / a[KV_LATENT:KV_LATENT + 1])
            o_pair.append(o_t.T.astype(BF16))
        o_lat = jnp.concatenate(o_pair, axis=1)
        o_ref[0, :, pair * LANES:(pair + 1) * LANES] = _dot(o_lat, wuv_ref[pair]).astype(o_ref.dtype)


def _dsa_call(topk, qabs, qidx, widx, ckr, cvt, kidx, wuv_pair, lstrict, hsel):
    bsz, s, _ = qabs.shape
    nq = s // Q_TILE
    nk = s // K_TILE
    qt = lambda w: pl.BlockSpec((1, Q_TILE, w), lambda b, i: (b, i, 0))
    return pl.pallas_call(
        functools.partial(_dsa_kernel, topk),
        grid=(bsz, nq),
        in_specs=[qt(A_HEADS * KV_LATENT),
                  pl.BlockSpec((1, IDX_HEADS, Q_TILE, IDX_DIM), lambda b, i: (b, 0, i, 0)),
                  pl.BlockSpec((1, 1, IDX_HEADS, Q_TILE), lambda b, i: (b, i, 0, 0)),
                  pl.BlockSpec((1, s, A_HEADS * KV_LATENT), lambda b, i: (b, 0, 0)),
                  pl.BlockSpec((1, nk, KV_LATENT + ONES_ROWS, K_TILE), lambda b, i: (b, 0, 0, 0)),
                  pl.BlockSpec((1, s, IDX_DIM), lambda b, i: (b, 0, 0)),
                  pl.BlockSpec((A_HEADS // 2, 2 * KV_LATENT, LANES), lambda b, i: (0, 0, 0)),
                  pl.BlockSpec((K_TILE, K_TILE), lambda b, i: (0, 0)),
                  pl.BlockSpec((A_HEADS * KV_LATENT, LANES), lambda b, i: (0, 0))],
        out_specs=qt(A_WIDTH),
        out_shape=jax.ShapeDtypeStruct((bsz, s, A_WIDTH), BF16),
        scratch_shapes=[pltpu.VMEM((nk, K_TILE, Q_TILE), F32),
                        pltpu.VMEM((nk, K_TILE, Q_TILE), F32),
                        pltpu.VMEM((A_HEADS, K_TILE, Q_TILE), F32),
                        pltpu.VMEM((A_HEADS, K_TILE, Q_TILE), BF16),
                        pltpu.VMEM((A_HEADS, 1, Q_TILE), F32),
                        pltpu.VMEM((A_HEADS, KV_LATENT + ONES_ROWS, Q_TILE), F32),
                        pltpu.SMEM((1,), F32)],
        compiler_params=_params(("arbitrary", "arbitrary")),
    )(qabs, qidx, widx, ckr, cvt, kidx, wuv_pair, lstrict, hsel)


def _rwkv_block(blk, tm, masks, rt_ref, kt_ref, bt_ref, kl_ref, v_ref, pc_ref):
    strict, incl, eye_s, lane_chunk, row_chunk, blk_diag, diag, head0 = masks
    nch = tm // CHUNK
    npair = B_HEADS // 2
    rows_b = slice(blk * tm, (blk + 1) * tm)
    zero_b = jnp.zeros((), BF16)

    def block_diag(m):
        return jnp.concatenate([jnp.where(lane_chunk == c, m, jnp.zeros((), m.dtype)) for c in range(nch)], axis=0)

    heads = [(p, hh) for p in range(npair) for hh in range(2)]
    rt, kt, bt, kl, v, v_bd = [], [], [], [], [], []
    a_ab, a_ak, m_rb, m_rk = [], [], [], []
    for p in range(npair):
        sl = slice(p * LANES, (p + 1) * LANES)
        rt.append(rt_ref[0, rows_b, sl])
        kt.append(kt_ref[0, rows_b, sl])
        bt.append(bt_ref[0, rows_b, sl])
        kl.append(kl_ref[0, rows_b, sl])
        v.append(v_ref[0, rows_b, sl])
        tc = 2 * CHUNK
        folded = []
        for g in range(nch // 2):
            r2 = slice(g * tc, (g + 1) * tc)
            lhs = jnp.concatenate([jnp.where(head0, kt[p][r2], zero_b), jnp.where(head0, zero_b, kt[p][r2]),
                                   jnp.where(head0, rt[p][r2], zero_b), jnp.where(head0, zero_b, rt[p][r2])], axis=0)
            prod = _dot_nt(lhs, jnp.concatenate([bt[p][r2], kl[p][r2]], axis=0))
            fold = lambda m: m[:CHUNK] + m[CHUNK:]
            folded.append([fold(jnp.where(strict if j < 2 else incl, prod[j * tc:(j + 1) * tc], 0.0))
                           for j in range(4)])
        gather = lambda j, half: jnp.concatenate(
            [f[j][:, half * tc:(half + 1) * tc] for f in folded], axis=1)
        v_bd.append(jnp.concatenate([jnp.where(row_chunk == c, v[p], zero_b) for c in range(nch)], axis=1))
        for hh in range(2):
            a_ab.append(gather(hh, 0))
            a_ak.append(gather(hh, 1).astype(BF16))
            m_rb.append(block_diag(gather(2 + hh, 0).astype(BF16)))
            m_rk.append(gather(2 + hh, 1).astype(BF16))

    t_inv = [(eye_s - a).astype(BF16) for a in a_ab]
    a_pow = [a.astype(BF16) for a in a_ab]
    for _ in range(5):
        a_sq = [_dot(a, block_diag(a)) for a in a_pow]
        a_pow = [a.astype(BF16) for a in a_sq]
        t_inv = [_dot(t, block_diag((eye_s + a).astype(BF16))).astype(BF16) for t, a in zip(t_inv, a_sq)]
    t_inv = [block_diag(t) for t in t_inv]

    avm = [_dot(jnp.concatenate([a_ak[i], m_rk[i]], axis=0), v_bd[p]) for i, (p, _) in enumerate(heads)]
    stack = lambda m: jnp.concatenate([m[:, c * LANES:(c + 1) * LANES] for c in range(nch)], axis=0)
    av = [stack(m[:CHUNK]) for m in avm]
    mv = [stack(m[CHUNK:]) for m in avm]
    x = [_dot(t_inv[i], jnp.concatenate([kt[p], av[i].astype(BF16)], axis=1))
         for i, (p, _) in enumerate(heads)]
    y = [_dot(m_rb[i], x[i].astype(BF16)) for i in range(len(heads))]

    head0_2 = jnp.concatenate([head0, head0], axis=1)
    zeros_b = jnp.zeros((CHUNK, LANES), BF16)
    q_b, ol, g_mat, f_mat = [], [], [], []
    for p in range(npair):
        i0, i1 = 2 * p, 2 * p + 1
        sl = slice(p * LANES, (p + 1) * LANES)
        wu_b = (-jnp.where(head0_2, x[i0], x[i1])).astype(BF16)
        yy = jnp.where(head0_2, y[i0], y[i1])
        q_b.append((rt[p].astype(F32) - yy[:, :LANES]).astype(BF16))
        ol.append(jnp.where(head0, mv[i0], mv[i1]) - yy[:, LANES:])
        gp, fp = [], []
        for c in range(nch):
            rows = slice(c * CHUNK, (c + 1) * CHUNK)
            pc = pc_ref[0, blk, c:c + 1, sl]
            bh = (bt[p][rows].astype(F32) * pc).astype(BF16)
            kh = (kl[p][rows].astype(F32) * pc).astype(BF16)
            rhs = jnp.concatenate([wu_b[rows], jnp.concatenate([zeros_b, v[p][rows]], axis=1)], axis=0)
            bw = _dot_tn(jnp.concatenate([bh, kh], axis=0), rhs)
            gp.append((jnp.where(diag, pc, 0.0) + jnp.where(blk_diag, bw[:, :LANES], 0.0)).astype(BF16))
            fp.append(jnp.where(blk_diag, bw[:, LANES:], 0.0))
        g_mat.append(gp)
        f_mat.append(fp)
    return q_b, ol, g_mat, f_mat


def _rwkv_kernel(rt_ref, kt_ref, bt_ref, kl_ref, v_ref, g_ref, bv_ref, pc_ref, lnw_ref, lnb_ref, eb_ref,
                 o_ref, h_ref):
    j = pl.program_id(1)

    @pl.when(j == 0)
    def _():
        h_ref[...] = jnp.zeros_like(h_ref)

    tm = TOK_TILE
    nch = tm // CHUNK
    npair = B_HEADS // 2
    ri = lax.broadcasted_iota(jnp.int32, (2 * CHUNK, 4 * CHUNK), 0)
    ci = lax.broadcasted_iota(jnp.int32, (2 * CHUNK, 4 * CHUNK), 1) % (2 * CHUNK)
    same = (ri // CHUNK) == (ci // CHUNK)
    r2 = lax.broadcasted_iota(jnp.int32, (LANES, LANES), 0)
    c2 = lax.broadcasted_iota(jnp.int32, (LANES, LANES), 1)
    lane = lax.broadcasted_iota(jnp.int32, (1, LANES), 1)
    rs = lax.broadcasted_iota(jnp.int32, (CHUNK, tm), 0)
    cs = lax.broadcasted_iota(jnp.int32, (CHUNK, tm), 1)
    masks = (same & (ri > ci), same & (ri >= ci), jnp.where(cs % CHUNK == rs, 1.0, 0.0),
             lax.broadcasted_iota(jnp.int32, (1, tm), 1) // CHUNK, lax.broadcasted_iota(jnp.int32, (tm, 1), 0) // CHUNK,
             (r2 // B_HEAD_DIM) == (c2 // B_HEAD_DIM), r2 == c2, (lane // B_HEAD_DIM) == 0)

    nblk = rt_ref.shape[1] // tm
    blocks = [_rwkv_block(b, tm, masks, rt_ref, kt_ref, bt_ref, kl_ref, v_ref, pc_ref) for b in range(nblk)]

    h = [h_ref[p] for p in range(npair)]
    for b, (q_b, ol, g_mat, f_mat) in enumerate(blocks):
        o_chunks = [[] for _ in range(npair)]
        for c in range(nch):
            rows = slice(c * CHUNK, (c + 1) * CHUNK)
            for p in range(npair):
                h_b = h[p].astype(BF16)
                o_chunks[p].append(_dot(q_b[p][rows], h_b) + ol[p][rows])
                h[p] = _dot(g_mat[p][c], h_b) + f_mat[p][c]
        out = jnp.concatenate([jnp.concatenate(oc, axis=0) for oc in o_chunks], axis=1)

        rows_b = slice(b * tm, (b + 1) * tm)
        eb = eb_ref[...]
        mean = _seg_dot_hl(out, eb) * (1.0 / B_HEAD_DIM)
        d = out - mean
        var = _seg_dot(d * d, eb) * (1.0 / B_HEAD_DIM)
        y = d * lax.rsqrt(var + GN_EPS) * lnw_ref[...] + lnb_ref[...] + bv_ref[0, rows_b]
        o_ref[0, rows_b] = (y * g_ref[0, rows_b]).astype(o_ref.dtype)
    for p in range(npair):
        h_ref[p] = h[p]


def _rwkv_call(rt, kt, bt, kl, v, g, bv, pc, ln_w, ln_b, eb):
    bsz, s, _ = rt.shape
    tm = RWKV_TILE
    tok = pl.BlockSpec((1, tm, B_WIDTH), lambda b, j: (b, j, 0))
    row = pl.BlockSpec((1, B_WIDTH), lambda b, j: (0, 0))
    return pl.pallas_call(
        _rwkv_kernel,
        grid=(bsz, s // tm),
        in_specs=[tok] * 7 + [pl.BlockSpec((1, tm // TOK_TILE, TOK_TILE // CHUNK, B_WIDTH), lambda b, j: (b, j, 0, 0)),
                              row, row, pl.BlockSpec((SEG_K, SEG_K), lambda b, j: (0, 0))],
        out_specs=tok,
        out_shape=jax.ShapeDtypeStruct((bsz, s, B_WIDTH), BF16),
        scratch_shapes=[pltpu.VMEM((B_HEADS // 2, LANES, LANES), F32)],
        compiler_params=_params(("arbitrary", "arbitrary")),
    )(rt, kt, bt, kl, v, g, bv, pc, ln_w, ln_b, eb)


def _ffn_kernel(x_ref, oa_ref, ob_ref, gt1_ref, sh2_ref, sc2_ref, gt2_ref, gf_ref, woa_ref, wob_ref,
                w1_ref, w2_ref, o_ref):
    for r0 in range(0, x_ref.shape[1], FFN_ROWS):
        rows = slice(r0, r0 + FFN_ROWS)
        mix = _dot(oa_ref[0, rows], woa_ref[...]) + _dot(ob_ref[0, rows], wob_ref[...])
        x1 = x_ref[0, rows] + gt1_ref[0, 0] * mix
        y = x1 * lax.rsqrt(jnp.mean(x1 * x1, axis=-1, keepdims=True) + RMS_EPS) * gf_ref[...]
        h2 = (y * (1.0 + sc2_ref[0, 0]) + sh2_ref[0, 0]).astype(BF16)
        u = jnp.maximum(_dot(h2, w1_ref[...]), 0.0)
        o_ref[0, rows] = x1 + gt2_ref[0, 0] * _dot((u * u).astype(BF16), w2_ref[...])


def _ffn_call(x, oa, ob, mod4, g_ffn, w_out_a, w_out_b, w1, w2):
    bsz, s, d = x.shape
    dff = w1.shape[1]
    tm = FFN_TILE
    tok = lambda w: pl.BlockSpec((1, tm, w), lambda b, j: (b, j, 0))
    modk = lambda k: pl.BlockSpec((1, 1, 1, d), lambda b, j, k=k: (b, k, 0, 0))
    res = lambda shape: pl.BlockSpec(shape, lambda b, j: (0, 0), pipeline_mode=pl.Buffered(1))
    return pl.pallas_call(
        _ffn_kernel,
        grid=(bsz, s // tm),
        in_specs=[tok(d), tok(A_WIDTH), tok(B_WIDTH), modk(2), modk(3), modk(4), modk(5),
                  pl.BlockSpec((1, d), lambda b, j: (0, 0)),
                  res((A_WIDTH, d)), res((B_WIDTH, d)), res((d, dff)), res((dff, d))],
        out_specs=tok(d),
        out_shape=jax.ShapeDtypeStruct((bsz, s, d), F32),
        compiler_params=_params(("arbitrary", "arbitrary")),
    )(x, oa, ob, mod4, mod4, mod4, mod4, g_ffn, w_out_a, w_out_b, w1, w2)


def _block_ones(n, blk, dtype=BF16):
    i = jnp.arange(n)
    return ((i[:, None] // blk) == (i[None, :] // blk)).astype(dtype)


def kernel(x, c, w_ada, b_ada, g_mix, g_ffn, w_in, g_q, g_k, g_kv, w_uk, w_uv, mu_shift, w0, w2, a0, a2, g2,
           k_k, k_a, r_k, ln_w, ln_b, w_out, w_ff1, w_ff2):
    bsz, s, d = x.shape
    depth = w_ada.shape[0]
    assert s % Q_TILE == 0 and s % FRONT_TILE == 0 and s % RWKV_TILE == 0 and s % FFN_TILE == 0
    topk = min(TOPK_MAX, s // 4)

    eb = _block_ones(SEG_K, B_HEAD_DIM)
    ex = (jnp.arange(2 * A_HEAD_DIM)[:, None] // A_HEAD_DIM == jnp.arange(2 * KV_LATENT)[None, :] // KV_LATENT
          ).astype(BF16)
    sel = (jnp.arange(LANES)[None, :] == IDX_DIM + jnp.arange(IDX_HEADS)[:, None]).astype(BF16)
    eye_l = jnp.eye(KV_LATENT, dtype=BF16)
    ti = jnp.arange(TOK_TILE)
    tri = (((ti[:, None] // CHUNK) == (ti[None, :] // CHUNK)) & (ti[:, None] >= ti[None, :])).astype(BF16)
    ki = jnp.arange(K_TILE)
    lstrict = (ki[None, :] < ki[:, None]).astype(BF16)
    hsel = (jnp.arange(A_HEADS * KV_LATENT)[:, None] // KV_LATENT == jnp.arange(LANES)[None, :]).astype(BF16)

    for l in range(depth):
        w_a = jnp.pad(w_in[l][:, :N_IN_A], ((0, 0), (0, N_A_PAD - N_IN_A)))
        w_in_p = jnp.concatenate([w_a, w_in[l][:, N_IN_A:]], axis=1).astype(BF16)
        wuk_flat = w_uk[l].reshape(KV_LATENT, A_WIDTH).astype(BF16)
        wuk_t = jnp.transpose(w_uk[l], (1, 2, 0)).reshape(A_HEADS // 2, 2, A_HEAD_DIM, KV_LATENT)
        wuk_bd = (jnp.eye(2, dtype=F32)[None, :, None, :, None] * wuk_t[:, :, :, None, :]).reshape(
            A_HEADS // 2, 2 * A_HEAD_DIM, 2 * KV_LATENT).astype(BF16)
        wuv_t = jnp.transpose(w_uv[l], (1, 0, 2)).reshape(A_HEADS // 2, 2, KV_LATENT, A_HEAD_DIM)
        wuv_pair = (jnp.eye(2, dtype=F32)[None, :, None, :, None] * wuv_t[:, :, :, None, :]).reshape(
            A_HEADS // 2, 2 * KV_LATENT, 2 * A_HEAD_DIM).astype(BF16)
        gqk = jnp.tile(g_q[l] * g_k[l], A_HEADS).reshape(1, A_WIDTH)
        r1 = lambda t: t.reshape(1, -1)

        mod = _mod_call(c, w_ada[l], b_ada[l])
        mod4 = mod.reshape(bsz, 6, 1, d)
        a_consts = (r1(g_kv[l]), gqk, wuk_flat, wuk_bd, eb, ex, sel, eye_l)
        b_consts = (r1(w0[l]), w2[l].astype(BF16), r1(a0[l]), a2[l].astype(BF16), g2[l].astype(BF16),
                    r1(k_k[l]), r1(k_a[l]), r1(r_k[l]), eb, tri)
        ckr, cvt, qabs, qidx, kidx, widx, rt, kt, bt, kl, v, g, bv, pc = _front_call(
            x, mod4, r1(g_mix[l]), w_in_p, r1(mu_shift[l]), a_consts, b_consts)
        o_a = _dsa_call(topk, qabs, qidx, widx, ckr, cvt, kidx, wuv_pair, lstrict, hsel)
        o_b = _rwkv_call(rt, kt, bt, kl, v, g, bv, pc, r1(ln_w[l]), r1(ln_b[l]), eb)
        x = _ffn_call(x, o_a, o_b, mod4, r1(g_ffn[l]), w_out[l][:A_WIDTH].astype(BF16),
                      w_out[l][A_WIDTH:].astype(BF16), w_ff1[l].astype(BF16), w_ff2[l].astype(BF16))
    return x
```

```python
import functools

import jax
import jax.numpy as jnp
from jax import lax
from jax.experimental import pallas as pl
from jax.experimental.pallas import tpu as pltpu

F32 = jnp.float32
BF16 = jnp.bfloat16

CHUNK = 64
A_HEADS = 8
A_HEAD_DIM = 64
A_WIDTH = A_HEADS * A_HEAD_DIM
KV_LATENT = 128
IDX_HEADS = 8
IDX_DIM = 64
TOPK_MAX = 256
B_HEADS = 8
B_HEAD_DIM = 64
B_WIDTH = B_HEADS * B_HEAD_DIM
W_LORA = 64
A_LORA = 64
G_LORA = 128
RMS_EPS = 1e-6
GN_EPS = 64e-5
N_IN_A = A_WIDTH + KV_LATENT + IDX_HEADS * IDX_DIM + IDX_DIM + IDX_HEADS
N_IN_B = 3 * B_WIDTH + W_LORA + A_LORA + G_LORA
N_A_PAD = 1280

LANES = 128
SUBLANES = 8
ADD_CHAINS = 4
MOD_COLS = 1024
SEG_K = 256
TOK_TILE = 256
FRONT_TILE = 512
RWKV_TILE = 512
Q_TILE = 256
FFN_TILE = 512
FFN_ROWS = 256
K_TILE = 256
DIST_BIG = 1e30
ONES_ROWS = 16
LOG2E = 1.4426950408889634
EXP_NEG_HALF = 0.6065306597126334
EXP_RANGE = 90.0
BOUND_MARGIN = 1.02
SEARCH_PROBES = 14
VMEM_LIMIT = 56 * 1024 * 1024


def _dot(a, b):
    return jnp.dot(a, b, preferred_element_type=F32)


def _dot_nt(a, b):
    return lax.dot_general(a, b, (((1,), (1,)), ((), ())), preferred_element_type=F32)


def _dot_tn(a, b):
    return lax.dot_general(a, b, (((0,), (0,)), ((), ())), preferred_element_type=F32)


def _split(x):
    hi = x.astype(BF16)
    lo = (x - hi.astype(F32)).astype(BF16)
    return hi, lo


def _dot_hl(x, e):
    hi, lo = _split(x)
    return _dot(hi, e) + _dot(lo, e)


def _seg_dot_hl(x, e):
    k = e.shape[0]
    return jnp.concatenate([_dot_hl(x[:, j:j + k], e) for j in range(0, x.shape[1], k)], axis=1)


def _seg_dot(x, e):
    k = e.shape[0]
    xb = x.astype(BF16)
    return jnp.concatenate([_dot(xb[:, j:j + k], e) for j in range(0, x.shape[1], k)], axis=1)


def _params(sem):
    return pltpu.CompilerParams(dimension_semantics=sem, vmem_limit_bytes=VMEM_LIMIT)


def _mod_kernel(c_ref, w_ref, b_ref, o_ref):
    c = c_ref[...]
    s = c * jax.nn.sigmoid(c)
    s_hi, s_lo = _split(s)
    w_hi, w_lo = _split(w_ref[...])
    o_ref[...] = _dot(s_hi, w_hi) + _dot(s_hi, w_lo) + _dot(s_lo, w_hi) + b_ref[...]


def _mod_call(c, w_ada, b_ada):
    bsz, d = c.shape
    n = w_ada.shape[1]
    tn = MOD_COLS
    return pl.pallas_call(
        _mod_kernel,
        grid=(n // tn,),
        in_specs=[pl.BlockSpec((bsz, d), lambda j: (0, 0)),
                  pl.BlockSpec((d, tn), lambda j: (0, j)),
                  pl.BlockSpec((1, tn), lambda j: (0, j))],
        out_specs=pl.BlockSpec((bsz, tn), lambda j: (0, j)),
        out_shape=jax.ShapeDtypeStruct((bsz, n), F32),
        compiler_params=_params(("arbitrary",)),
    )(c, w_ada, b_ada.reshape(1, n))


def _head_norm2_max(x):
    best = None
    for h in range(x.shape[1] // KV_LATENT):
        xh = x[:, h * KV_LATENT:(h + 1) * KV_LATENT]
        n2 = jnp.max(jnp.sum(xh * xh, axis=-1, keepdims=True), axis=0, keepdims=True)
        best = n2 if best is None else jnp.maximum(best, n2)
    return best


def _prep_a(pa, rows, blk, gkv_ref, gqk_ref, wuk_ref, wukbd_ref, eb_ref, ex_ref, sel_ref, eye_ref,
            ckr_ref, cvt_ref, qabs_ref, qidx_ref, kidx_ref, widx_ref, qn_ref, kn_ref):
    tm = pa.shape[0]
    q = pa[:, :A_WIDTH]
    cl = pa[:, A_WIDTH:A_WIDTH + KV_LATENT]
    o_qi = A_WIDTH + KV_LATENT
    qi = pa[:, o_qi:o_qi + IDX_HEADS * IDX_DIM]
    o_kw = o_qi + IDX_HEADS * IDX_DIM
    kw = pa[:, o_kw:o_kw + LANES]

    ckv = cl * lax.rsqrt(jnp.mean(cl * cl, axis=-1, keepdims=True) + RMS_EPS) * gkv_ref[...]
    ckv_b = ckv.astype(BF16)
    cvt_ref[0, blk, :KV_LATENT, :] = _dot_nt(eye_ref[...], ckv_b).astype(BF16)
    cvt_ref[0, blk, KV_LATENT:, :] = jnp.ones((ONES_ROWS, tm), BF16)
    kf = _dot(ckv_b, wuk_ref[...])
    ss = _seg_dot(kf * kf, ex_ref[...])
    inv_rms = lax.rsqrt(ss * (1.0 / A_HEAD_DIM) + RMS_EPS)
    ckr = jnp.concatenate([ckv] * A_HEADS, axis=1) * inv_rms
    ckr_ref[0, rows] = ckr.astype(BF16)
    kn_ref[0, blk] = jnp.broadcast_to(_head_norm2_max(ckr), kn_ref.shape[2:])

    ssq = _seg_dot(q * q, eb_ref[...])
    qh = q * lax.rsqrt(ssq * (1.0 / A_HEAD_DIM) + RMS_EPS) * gqk_ref[...]
    qh_b = qh.astype(BF16)
    qn = None
    for j in range(A_HEADS // 2):
        qabs = _dot(qh_b[:, j * LANES:(j + 1) * LANES], wukbd_ref[j]) * (A_HEAD_DIM ** -0.5 * LOG2E)
        qabs_ref[0, rows, 2 * j * KV_LATENT:2 * (j + 1) * KV_LATENT] = qabs.astype(BF16)
        qn = _head_norm2_max(qabs) if qn is None else jnp.maximum(qn, _head_norm2_max(qabs))
    qn_ref[0, blk] = jnp.broadcast_to(qn, qn_ref.shape[2:])
    for h in range(IDX_HEADS):
        qidx_ref[0, h, rows] = qi[:, h * IDX_DIM:(h + 1) * IDX_DIM].astype(BF16)
    kidx_ref[0, rows] = kw[:, :IDX_DIM].astype(BF16)
    kw_hi, kw_lo = _split(kw)
    w_t = _dot_nt(sel_ref[...], kw_hi) + _dot_nt(sel_ref[...], kw_lo)
    widx_ref[0, blk] = w_t * (IDX_HEADS ** -0.5 * IDX_DIM ** -0.5)


def _prep_b(pb, rows, blk, w0_ref, w2_ref, a0_ref, a2_ref, g2_ref, kk_ref, ka_ref, rk_ref, eb_ref, tri_ref,
            rt_ref, kt_ref, bt_ref, kl_ref, v_ref, g_ref, bv_ref, pc_ref):
    r = pb[:, :B_WIDTH]
    k = pb[:, B_WIDTH:2 * B_WIDTH]
    v = pb[:, 2 * B_WIDTH:3 * B_WIDTH]
    o = 3 * B_WIDTH
    xw = pb[:, o:o + W_LORA]
    xa = pb[:, o + W_LORA:o + W_LORA + A_LORA]
    xg = pb[:, o + W_LORA + A_LORA:o + W_LORA + A_LORA + G_LORA]

    z = w0_ref[...] + _dot(jnp.tanh(xw).astype(BF16), w2_ref[...])
    lw = -EXP_NEG_HALF * jax.nn.sigmoid(z)
    a = jax.nn.sigmoid(a0_ref[...] + _dot(xa.astype(BF16), a2_ref[...]))
    g = _dot(jax.nn.sigmoid(xg).astype(BF16), g2_ref[...])
    kk = k * kk_ref[...]
    kkn = kk * lax.rsqrt(jnp.maximum(_seg_dot(kk * kk, eb_ref[...]), 1e-24))
    kp = k * (1.0 + (a - 1.0) * ka_ref[...])
    bonus = _seg_dot(r * kp * rk_ref[...], eb_ref[...])

    lw_hi, lw_lo = _split(lw)
    cum = _dot(tri_ref[...], lw_hi) + _dot(tri_ref[...], lw_lo)
    e_pos = jnp.exp(cum)
    e_neg = jnp.exp(-cum)
    rt_ref[0, rows] = (r * e_pos).astype(BF16)
    kt_ref[0, rows] = (kkn * jnp.exp(cum - lw)).astype(BF16)
    bt_ref[0, rows] = (kkn * a * e_neg).astype(BF16)
    kl_ref[0, rows] = (kp * e_neg).astype(BF16)
    v_ref[0, rows] = v.astype(BF16)
    g_ref[0, rows] = g
    bv_ref[0, rows] = bonus * v
    for c in range(pb.shape[0] // CHUNK):
        pc_ref[0, blk, c:c + 1, :] = e_pos[(c + 1) * CHUNK - 1:(c + 1) * CHUNK, :]


N_FRONT_IN = 6
N_PREP_A_IN = 8
N_PREP_B_IN = 10
N_PREP_A_OUT = 8


def _front_kernel(*refs):
    x_ref, sh_ref, sc_ref, g_ref, w_ref, mu_ref = refs[:N_FRONT_IN]
    a_in = refs[N_FRONT_IN:N_FRONT_IN + N_PREP_A_IN]
    b_in = refs[N_FRONT_IN + N_PREP_A_IN:N_FRONT_IN + N_PREP_A_IN + N_PREP_B_IN]
    outs = refs[N_FRONT_IN + N_PREP_A_IN + N_PREP_B_IN:-1]
    carry_ref = refs[-1]
    j = pl.program_id(1)

    @pl.when(j == 0)
    def _():
        carry_ref[...] = jnp.zeros_like(carry_ref)

    x = x_ref[0]
    y = x * lax.rsqrt(jnp.mean(x * x, axis=-1, keepdims=True) + RMS_EPS) * g_ref[...]
    h = y * (1.0 + sc_ref[0, 0]) + sh_ref[0, 0]
    p = _dot(h.astype(BF16), w_ref[...])
    pb = p[:, N_A_PAD:]
    tm = pb.shape[0]
    row = lax.broadcasted_iota(jnp.int32, (tm, 1), 0)
    prev = jnp.where(row == 0, carry_ref[...], pltpu.roll(pb, 1, axis=0))
    carry_ref[...] = pb[tm - 1:tm, :]
    pb = pb + mu_ref[...] * (prev - pb)
    for blk in range(tm // TOK_TILE):
        rows = slice(blk * TOK_TILE, (blk + 1) * TOK_TILE)
        _prep_a(p[rows, :N_A_PAD], rows, blk, *a_in, *outs[:N_PREP_A_OUT])
        _prep_b(pb[rows], rows, blk, *b_in, *outs[N_PREP_A_OUT:])


def _front_call(x, mod4, g_mix, w_in_p, mu, a_consts, b_consts):
    bsz, s, d = x.shape
    n = w_in_p.shape[1]
    nb = n - N_A_PAD
    tm = FRONT_TILE
    tt = TOK_TILE
    full = lambda arr: pl.BlockSpec(arr.shape, lambda b, j, nd=arr.ndim: (0,) * nd)
    tok = lambda w: pl.BlockSpec((1, tm, w), lambda b, j: (b, j, 0))
    per_tile = lambda r, c: pl.BlockSpec((1, tm // tt, r, c), lambda b, j: (b, j, 0, 0))
    bf = lambda w: jax.ShapeDtypeStruct((bsz, s, w), BF16)
    ff = lambda w: jax.ShapeDtypeStruct((bsz, s, w), F32)
    nt = s // tt
    out_specs = [tok(A_HEADS * KV_LATENT), per_tile(KV_LATENT + ONES_ROWS, tt), tok(A_HEADS * KV_LATENT),
                 pl.BlockSpec((1, IDX_HEADS, tm, IDX_DIM), lambda b, j: (b, 0, j, 0)),
                 tok(IDX_DIM), per_tile(IDX_HEADS, tt), per_tile(SUBLANES, LANES), per_tile(SUBLANES, LANES)
                 ] + [tok(B_WIDTH)] * 7 + [per_tile(tt // CHUNK, B_WIDTH)]
    out_shape = [bf(A_HEADS * KV_LATENT),
                 jax.ShapeDtypeStruct((bsz, nt, KV_LATENT + ONES_ROWS, tt), BF16),
                 bf(A_HEADS * KV_LATENT),
                 jax.ShapeDtypeStruct((bsz, IDX_HEADS, s, IDX_DIM), BF16),
                 bf(IDX_DIM),
                 jax.ShapeDtypeStruct((bsz, nt, IDX_HEADS, tt), F32),
                 jax.ShapeDtypeStruct((bsz, nt, SUBLANES, LANES), F32),
                 jax.ShapeDtypeStruct((bsz, nt, SUBLANES, LANES), F32),
                 bf(B_WIDTH), bf(B_WIDTH), bf(B_WIDTH), bf(B_WIDTH), bf(B_WIDTH), ff(B_WIDTH), ff(B_WIDTH),
                 jax.ShapeDtypeStruct((bsz, nt, tt // CHUNK, B_WIDTH), F32)]
    assert len(a_consts) == N_PREP_A_IN and len(b_consts) == N_PREP_B_IN
    return pl.pallas_call(
        _front_kernel,
        grid=(bsz, s // tm),
        in_specs=[pl.BlockSpec((1, tm, d), lambda b, j: (b, j, 0)),
                  pl.BlockSpec((1, 1, 1, d), lambda b, j: (b, 0, 0, 0)),
                  pl.BlockSpec((1, 1, 1, d), lambda b, j: (b, 1, 0, 0)),
                  full(g_mix),
                  pl.BlockSpec(w_in_p.shape, lambda b, j: (0, 0), pipeline_mode=pl.Buffered(1)),
                  full(mu)] + [full(t) for t in a_consts] + [full(t) for t in b_consts],
        out_specs=out_specs,
        out_shape=out_shape,
        scratch_shapes=[pltpu.VMEM((1, nb), F32)],
        compiler_params=_params(("arbitrary", "arbitrary")),
    )(x, mod4, mod4, g_mix, w_in_p, mu, *a_consts, *b_consts)


def _colsum8(x):
    y = x.reshape(ADD_CHAINS, K_TILE // (ADD_CHAINS * SUBLANES), SUBLANES, Q_TILE)
    return jnp.sum(jnp.sum(y, axis=1), axis=0)


def _colmin8(x):
    y = x.reshape(ADD_CHAINS, K_TILE // (ADD_CHAINS * SUBLANES), SUBLANES, Q_TILE)
    return jnp.min(jnp.min(y, axis=1), axis=0)


def _colmax8(x):
    y = x.reshape(ADD_CHAINS, K_TILE // (ADD_CHAINS * SUBLANES), SUBLANES, Q_TILE)
    return jnp.max(jnp.max(y, axis=1), axis=0)


def _for_key_tiles(nkc, body, init):
    def quad(j, c):
        return body(4 * j + 3, body(4 * j + 2, body(4 * j + 1, body(4 * j, c))))
    c = lax.fori_loop(0, nkc // 4, quad, init)
    base = (nkc // 4) * 4
    c = lax.cond(nkc % 4 >= 2, lambda c: body(base + 1, body(base, c)), lambda c: c, c)
    return lax.cond(nkc % 2 == 1, lambda c: body(nkc - 1, c), lambda c: c, c)


def _dsa_kernel(topk, qabs_ref, qidx_ref, widx_ref, ckr_ref, cvt_ref, kidx_ref, wuv_ref, lstrict_ref, qn_ref, kn_ref,
                o_ref, score_ref, dist_ref, logit_ref, p_ref, m_ref, acc_ref):
    i = pl.program_id(1)
    nkc = i + 1
    t0 = i * Q_TILE
    krow = lax.broadcasted_iota(jnp.int32, (K_TILE, 1), 0)
    qcol = lax.broadcasted_iota(jnp.int32, (1, Q_TILE), 1)
    limit = ((t0 + qcol) // CHUNK + 1) * CHUNK
    kp = jnp.minimum(limit, topk).astype(F32)
    rel = (qcol - krow).astype(F32)

    def p1(kc, carry):
        rmin, rmax = carry
        k = kidx_ref[0, pl.ds(pl.multiple_of(kc * K_TILE, K_TILE), K_TILE), :]
        acc = jnp.zeros((K_TILE, Q_TILE), F32)
        for h in range(IDX_HEADS):
            s = _dot_nt(k, qidx_ref[0, h])
            acc = acc + widx_ref[0, 0, h:h + 1, :] * jnp.maximum(s, 0.0)
        adm = (kc * K_TILE + krow) < limit
        score_ref[kc] = jnp.where(adm, acc, -jnp.inf)
        rmin = jnp.minimum(rmin, _colmin8(jnp.where(adm, acc, jnp.inf)))
        rmax = jnp.maximum(rmax, _colmax8(jnp.where(adm, acc, -jnp.inf)))
        return rmin, rmax

    rmin, rmax = _for_key_tiles(
        nkc, p1, (jnp.full((SUBLANES, Q_TILE), jnp.inf, F32), jnp.full((SUBLANES, Q_TILE), -jnp.inf, F32)))
    lo = jnp.min(rmin, axis=0, keepdims=True)
    hi = jnp.max(rmax, axis=0, keepdims=True)

    def count(pred):
        def body(kc, acc):
            return acc + _colsum8(jnp.where(pred(score_ref[kc]), 1.0, 0.0))
        return jnp.sum(lax.fori_loop(0, nkc, body, jnp.zeros((SUBLANES, Q_TILE), F32)), axis=0, keepdims=True)

    def probe(c):
        lo, hi, cnt_lo = c
        mid = lo + 0.5 * (hi - lo)
        cnt = count(lambda sc: sc >= mid)
        ge = cnt >= kp
        return jnp.where(ge, mid, lo), jnp.where(ge, hi, mid), jnp.where(ge, cnt, cnt_lo)

    def smallest(pred):
        def body(kc, acc):
            sc = score_ref[kc]
            return jnp.minimum(acc, _colmin8(jnp.where(pred(sc), sc, jnp.inf)))
        return jnp.min(lax.fori_loop(0, nkc, body, jnp.full((SUBLANES, Q_TILE), jnp.inf, F32)), axis=0, keepdims=True)

    def any_true(x):
        return jnp.max(jnp.where(x, 1.0, 0.0)) > 0.0

    lo, _, cnt_lo = lax.fori_loop(0, SEARCH_PROBES, lambda _, c: probe(c), (lo, hi, limit.astype(F32)))

    def step_up(c):
        it, thr, cnt_gt, cnt_ge = c
        up = cnt_gt >= kp
        thr = jnp.where(up, smallest(lambda sc: sc > thr), thr)
        return it + 1, thr, count(lambda sc: sc > thr), jnp.where(up, cnt_gt, cnt_ge)

    thr = smallest(lambda sc: sc >= lo)
    search = lax.while_loop(
        lambda c: jnp.logical_and(c[0] < nkc * K_TILE, any_true(c[2] >= kp)),
        step_up, (jnp.int32(0), thr, count(lambda sc: sc > thr), cnt_lo))
    thr, need, cnt_ge = search[1], kp - search[2], search[3]

    def dist_tile(kc):
        return jnp.abs(rel + (t0 - kc * K_TILE).astype(F32))

    big8 = jnp.full((SUBLANES, Q_TILE), DIST_BIG, F32)

    def sel_plain():
        def body(kc, near):
            d = jnp.where(score_ref[kc] >= thr, dist_tile(kc), DIST_BIG)
            dist_ref[kc] = d
            return jnp.minimum(near, _colmin8(d))
        return jnp.min(lax.fori_loop(0, nkc, body, big8), axis=0, keepdims=True)

    def sel_ties():
        def body(kc, c):
            run, near = c
            sc = score_ref[kc]
            eq = sc == thr
            eq_f = jnp.where(eq, 1.0, 0.0)
            pre = run + _dot(lstrict_ref[...], eq_f.astype(BF16))
            keep = (sc > thr) | (eq & (pre < need))
            d = jnp.where(keep, dist_tile(kc), DIST_BIG)
            dist_ref[kc] = d
            return run + jnp.sum(_colsum8(eq_f), axis=0, keepdims=True), jnp.minimum(near, _colmin8(d))
        _, near = _for_key_tiles(nkc, body, (jnp.zeros((1, Q_TILE), F32), big8))
        return jnp.min(near, axis=0, keepdims=True)

    near = lax.cond(any_true(cnt_ge != kp), sel_ties, sel_plain)

    acc_ref[...] = jnp.zeros(acc_ref.shape, F32)

    bound = jnp.sqrt(jnp.max(qn_ref[0, 0]) * jnp.max(kn_ref[0])) * BOUND_MARGIN

    def att_shifted():
        def body(kc, _):
            d = dist_ref[kc] - near
            for h in range(A_HEADS):
                slope = 2.0 ** (-8.0 * (h + 1) / A_HEADS) * LOG2E
                ck = ckr_ref[0, pl.ds(pl.multiple_of(kc * K_TILE, K_TILE), K_TILE),
                             h * KV_LATENT:(h + 1) * KV_LATENT]
                logit = _dot_nt(ck, qabs_ref[0, :, h * KV_LATENT:(h + 1) * KV_LATENT]) - slope * d
                p_ref[h] = jnp.exp2(logit).astype(BF16)
            cv = cvt_ref[0, kc]
            for h in range(A_HEADS):
                acc_ref[h] = acc_ref[h] + _dot(cv, p_ref[h])
            return 0
        _for_key_tiles(nkc, body, 0)

    def att_online():
        m_ref[...] = jnp.full(m_ref.shape, -jnp.inf, F32)
        _for_key_tiles(nkc, att, 0)

    def att(kc, _):
        dist = dist_ref[kc]
        m_new = []
        for h in range(A_HEADS):
            slope = 2.0 ** (-8.0 * (h + 1) / A_HEADS) * LOG2E
            ck = ckr_ref[0, pl.ds(pl.multiple_of(kc * K_TILE, K_TILE), K_TILE), h * KV_LATENT:(h + 1) * KV_LATENT]
            logit = _dot_nt(ck, qabs_ref[0, :, h * KV_LATENT:(h + 1) * KV_LATENT]) - slope * dist
            logit_ref[h] = logit
            m_new.append(jnp.maximum(m_ref[h], jnp.max(_colmax8(logit), axis=0, keepdims=True)))
        cv = cvt_ref[0, kc]
        for h in range(A_HEADS):
            p = jnp.exp2(logit_ref[h] - m_new[h])
            acc_ref[h] = acc_ref[h] * jnp.exp2(m_ref[h] - m_new[h]) + _dot(cv, p.astype(BF16))
            m_ref[h] = m_new[h]
        return 0

    lax.cond(bound <= EXP_RANGE, att_shifted, att_online)

    for pair in range(A_HEADS // 2):
        o_pair = []
        for hh in range(2):
            a = acc_ref[2 * pair + hh]
            o_t = a[:KV_LATENT] * (1.0 / a[KV_LATENT:KV_LATENT + 1])
            o_pair.append(o_t.T.astype(BF16))
        o_lat = jnp.concatenate(o_pair, axis=1)
        o_ref[0, :, pair * LANES:(pair + 1) * LANES] = _dot(o_lat, wuv_ref[pair]).astype(o_ref.dtype)


def _dsa_call(topk, qabs, qidx, widx, ckr, cvt, kidx, wuv_pair, lstrict, qn, kn):
    bsz, s, _ = qabs.shape
    nq = s // Q_TILE
    nk = s // K_TILE
    qt = lambda w: pl.BlockSpec((1, Q_TILE, w), lambda b, i: (b, i, 0))
    return pl.pallas_call(
        functools.partial(_dsa_kernel, topk),
        grid=(bsz, nq),
        in_specs=[qt(A_HEADS * KV_LATENT),
                  pl.BlockSpec((1, IDX_HEADS, Q_TILE, IDX_DIM), lambda b, i: (b, 0, i, 0)),
                  pl.BlockSpec((1, 1, IDX_HEADS, Q_TILE), lambda b, i: (b, i, 0, 0)),
                  pl.BlockSpec((1, s, A_HEADS * KV_LATENT), lambda b, i: (b, 0, 0)),
                  pl.BlockSpec((1, nk, KV_LATENT + ONES_ROWS, K_TILE), lambda b, i: (b, 0, 0, 0)),
                  pl.BlockSpec((1, s, IDX_DIM), lambda b, i: (b, 0, 0)),
                  pl.BlockSpec((A_HEADS // 2, 2 * KV_LATENT, LANES), lambda b, i: (0, 0, 0)),
                  pl.BlockSpec((K_TILE, K_TILE), lambda b, i: (0, 0)),
                  pl.BlockSpec((1, 1, SUBLANES, LANES), lambda b, i: (b, i, 0, 0)),
                  pl.BlockSpec((1, nk, SUBLANES, LANES), lambda b, i: (b, 0, 0, 0))],
        out_specs=qt(A_WIDTH),
        out_shape=jax.ShapeDtypeStruct((bsz, s, A_WIDTH), BF16),
        scratch_shapes=[pltpu.VMEM((nk, K_TILE, Q_TILE), F32),
                        pltpu.VMEM((nk, K_TILE, Q_TILE), F32),
                        pltpu.VMEM((A_HEADS, K_TILE, Q_TILE), F32),
                        pltpu.VMEM((A_HEADS, K_TILE, Q_TILE), BF16),
                        pltpu.VMEM((A_HEADS, 1, Q_TILE), F32),
                        pltpu.VMEM((A_HEADS, KV_LATENT + ONES_ROWS, Q_TILE), F32)],
        compiler_params=_params(("arbitrary", "arbitrary")),
    )(qabs, qidx, widx, ckr, cvt, kidx, wuv_pair, lstrict, qn, kn)


def _rwkv_block(blk, tm, masks, rt_ref, kt_ref, bt_ref, kl_ref, v_ref, pc_ref):
    strict, incl, eye_s, lane_chunk, row_chunk, blk_diag, diag, head0 = masks
    nch = tm // CHUNK
    npair = B_HEADS // 2
    rows_b = slice(blk * tm, (blk + 1) * tm)
    zero_b = jnp.zeros((), BF16)

    def block_diag(m):
        return jnp.concatenate([jnp.where(lane_chunk == c, m, jnp.zeros((), m.dtype)) for c in range(nch)], axis=0)

    heads = [(p, hh) for p in range(npair) for hh in range(2)]
    rt, kt, bt, kl, v, v_bd = [], [], [], [], [], []
    a_ab, a_ak, m_rb, m_rk = [], [], [], []
    for p in range(npair):
        sl = slice(p * LANES, (p + 1) * LANES)
        rt.append(rt_ref[0, rows_b, sl])
        kt.append(kt_ref[0, rows_b, sl])
        bt.append(bt_ref[0, rows_b, sl])
        kl.append(kl_ref[0, rows_b, sl])
        v.append(v_ref[0, rows_b, sl])
        tc = 2 * CHUNK
        folded = []
        for g in range(nch // 2):
            r2 = slice(g * tc, (g + 1) * tc)
            lhs = jnp.concatenate([jnp.where(head0, kt[p][r2], zero_b), jnp.where(head0, zero_b, kt[p][r2]),
                                   jnp.where(head0, rt[p][r2], zero_b), jnp.where(head0, zero_b, rt[p][r2])], axis=0)
            prod = _dot_nt(lhs, jnp.concatenate([bt[p][r2], kl[p][r2]], axis=0))
            fold = lambda m: m[:CHUNK] + m[CHUNK:]
            folded.append([fold(jnp.where(strict if j < 2 else incl, prod[j * tc:(j + 1) * tc], 0.0))
                           for j in range(4)])
        gather = lambda j, half: jnp.concatenate(
            [f[j][:, half * tc:(half + 1) * tc] for f in folded], axis=1)
        v_bd.append(jnp.concatenate([jnp.where(row_chunk == c, v[p], zero_b) for c in range(nch)], axis=1))
        for hh in range(2):
            a_ab.append(gather(hh, 0))
            a_ak.append(gather(hh, 1).astype(BF16))
            m_rb.append(block_diag(gather(2 + hh, 0).astype(BF16)))
            m_rk.append(gather(2 + hh, 1).astype(BF16))

    t_inv = [(eye_s - a).astype(BF16) for a in a_ab]
    a_pow = [a.astype(BF16) for a in a_ab]
    for _ in range(5):
        a_sq = [_dot(a, block_diag(a)) for a in a_pow]
        a_pow = [a.astype(BF16) for a in a_sq]
        t_inv = [_dot(t, block_diag((eye_s + a).astype(BF16))).astype(BF16) for t, a in zip(t_inv, a_sq)]
    t_inv = [block_diag(t) for t in t_inv]

    avm = [_dot(jnp.concatenate([a_ak[i], m_rk[i]], axis=0), v_bd[p]) for i, (p, _) in enumerate(heads)]
    stack = lambda m: jnp.concatenate([m[:, c * LANES:(c + 1) * LANES] for c in range(nch)], axis=0)
    av = [stack(m[:CHUNK]) for m in avm]
    mv = [stack(m[CHUNK:]) for m in avm]
    x = [_dot(t_inv[i], jnp.concatenate([kt[p], av[i].astype(BF16)], axis=1))
         for i, (p, _) in enumerate(heads)]
    y = [_dot(m_rb[i], x[i].astype(BF16)) for i in range(len(heads))]

    head0_2 = jnp.concatenate([head0, head0], axis=1)
    zeros_b = jnp.zeros((CHUNK, LANES), BF16)
    q_b, ol, g_mat, f_mat = [], [], [], []
    for p in range(npair):
        i0, i1 = 2 * p, 2 * p + 1
        sl = slice(p * LANES, (p + 1) * LANES)
        wu_b = (-jnp.where(head0_2, x[i0], x[i1])).astype(BF16)
        yy = jnp.where(head0_2, y[i0], y[i1])
        q_b.append((rt[p].astype(F32) - yy[:, :LANES]).astype(BF16))
        ol.append(jnp.where(head0, mv[i0], mv[i1]) - yy[:, LANES:])
        gp, fp = [], []
        for c in range(nch):
            rows = slice(c * CHUNK, (c + 1) * CHUNK)
            pc = pc_ref[0, blk, c:c + 1, sl]
            bh = (bt[p][rows].astype(F32) * pc).astype(BF16)
            kh = (kl[p][rows].astype(F32) * pc).astype(BF16)
            rhs = jnp.concatenate([wu_b[rows], jnp.concatenate([zeros_b, v[p][rows]], axis=1)], axis=0)
            bw = _dot_tn(jnp.concatenate([bh, kh], axis=0), rhs)
            gp.append((jnp.where(diag, pc, 0.0) + jnp.where(blk_diag, bw[:, :LANES], 0.0)).astype(BF16))
            fp.append(jnp.where(blk_diag, bw[:, LANES:], 0.0))
        g_mat.append(gp)
        f_mat.append(fp)
    return q_b, ol, g_mat, f_mat


def _rwkv_kernel(rt_ref, kt_ref, bt_ref, kl_ref, v_ref, g_ref, bv_ref, pc_ref, lnw_ref, lnb_ref, eb_ref,
                 o_ref, h_ref):
    j = pl.program_id(1)

    @pl.when(j == 0)
    def _():
        h_ref[...] = jnp.zeros_like(h_ref)

    tm = TOK_TILE
    nch = tm // CHUNK
    npair = B_HEADS // 2
    ri = lax.broadcasted_iota(jnp.int32, (2 * CHUNK, 4 * CHUNK), 0)
    ci = lax.broadcasted_iota(jnp.int32, (2 * CHUNK, 4 * CHUNK), 1) % (2 * CHUNK)
    same = (ri // CHUNK) == (ci // CHUNK)
    r2 = lax.broadcasted_iota(jnp.int32, (LANES, LANES), 0)
    c2 = lax.broadcasted_iota(jnp.int32, (LANES, LANES), 1)
    lane = lax.broadcasted_iota(jnp.int32, (1, LANES), 1)
    rs = lax.broadcasted_iota(jnp.int32, (CHUNK, tm), 0)
    cs = lax.broadcasted_iota(jnp.int32, (CHUNK, tm), 1)
    masks = (same & (ri > ci), same & (ri >= ci), jnp.where(cs % CHUNK == rs, 1.0, 0.0),
             lax.broadcasted_iota(jnp.int32, (1, tm), 1) // CHUNK, lax.broadcasted_iota(jnp.int32, (tm, 1), 0) // CHUNK,
             (r2 // B_HEAD_DIM) == (c2 // B_HEAD_DIM), r2 == c2, (lane // B_HEAD_DIM) == 0)

    nblk = rt_ref.shape[1] // tm
    blocks = [_rwkv_block(b, tm, masks, rt_ref, kt_ref, bt_ref, kl_ref, v_ref, pc_ref) for b in range(nblk)]

    h = [h_ref[p] for p in range(npair)]
    for b, (q_b, ol, g_mat, f_mat) in enumerate(blocks):
        o_chunks = [[] for _ in range(npair)]
        for c in range(nch):
            rows = slice(c * CHUNK, (c + 1) * CHUNK)
            for p in range(npair):
                h_b = h[p].astype(BF16)
                o_chunks[p].append(_dot(q_b[p][rows], h_b) + ol[p][rows])
                h[p] = _dot(g_mat[p][c], h_b) + f_mat[p][c]
        out = jnp.concatenate([jnp.concatenate(oc, axis=0) for oc in o_chunks], axis=1)

        rows_b = slice(b * tm, (b + 1) * tm)
        eb = eb_ref[...]
        mean = _seg_dot_hl(out, eb) * (1.0 / B_HEAD_DIM)
        d = out - mean
        var = _seg_dot(d * d, eb) * (1.0 / B_HEAD_DIM)
        y = d * lax.rsqrt(var + GN_EPS) * lnw_ref[...] + lnb_ref[...] + bv_ref[0, rows_b]
        o_ref[0, rows_b] = (y * g_ref[0, rows_b]).astype(o_ref.dtype)
    for p in range(npair):
        h_ref[p] = h[p]


def _rwkv_call(rt, kt, bt, kl, v, g, bv, pc, ln_w, ln_b, eb):
    bsz, s, _ = rt.shape
    tm = RWKV_TILE
    tok = pl.BlockSpec((1, tm, B_WIDTH), lambda b, j: (b, j, 0))
    row = pl.BlockSpec((1, B_WIDTH), lambda b, j: (0, 0))
    return pl.pallas_call(
        _rwkv_kernel,
        grid=(bsz, s // tm),
        in_specs=[tok] * 7 + [pl.BlockSpec((1, tm // TOK_TILE, TOK_TILE // CHUNK, B_WIDTH), lambda b, j: (b, j, 0, 0)),
                              row, row, pl.BlockSpec((SEG_K, SEG_K), lambda b, j: (0, 0))],
        out_specs=tok,
        out_shape=jax.ShapeDtypeStruct((bsz, s, B_WIDTH), BF16),
        scratch_shapes=[pltpu.VMEM((B_HEADS // 2, LANES, LANES), F32)],
        compiler_params=_params(("arbitrary", "arbitrary")),
    )(rt, kt, bt, kl, v, g, bv, pc, ln_w, ln_b, eb)


def _ffn_kernel(x_ref, oa_ref, ob_ref, gt1_ref, sh2_ref, sc2_ref, gt2_ref, gf_ref, woa_ref, wob_ref,
                w1_ref, w2_ref, o_ref):
    for r0 in range(0, x_ref.shape[1], FFN_ROWS):
        rows = slice(r0, r0 + FFN_ROWS)
        mix = _dot(oa_ref[0, rows], woa_ref[...]) + _dot(ob_ref[0, rows], wob_ref[...])
        x1 = x_ref[0, rows] + gt1_ref[0, 0] * mix
        y = x1 * lax.rsqrt(jnp.mean(x1 * x1, axis=-1, keepdims=True) + RMS_EPS) * gf_ref[...]
        h2 = (y * (1.0 + sc2_ref[0, 0]) + sh2_ref[0, 0]).astype(BF16)
        u = jnp.maximum(_dot(h2, w1_ref[...]), 0.0)
        o_ref[0, rows] = x1 + gt2_ref[0, 0] * _dot((u * u).astype(BF16), w2_ref[...])


def _ffn_call(x, oa, ob, mod4, g_ffn, w_out_a, w_out_b, w1, w2):
    bsz, s, d = x.shape
    dff = w1.shape[1]
    tm = FFN_TILE
    tok = lambda w: pl.BlockSpec((1, tm, w), lambda b, j: (b, j, 0))
    modk = lambda k: pl.BlockSpec((1, 1, 1, d), lambda b, j, k=k: (b, k, 0, 0))
    res = lambda shape: pl.BlockSpec(shape, lambda b, j: (0, 0), pipeline_mode=pl.Buffered(1))
    return pl.pallas_call(
        _ffn_kernel,
        grid=(bsz, s // tm),
        in_specs=[tok(d), tok(A_WIDTH), tok(B_WIDTH), modk(2), modk(3), modk(4), modk(5),
                  pl.BlockSpec((1, d), lambda b, j: (0, 0)),
                  res((A_WIDTH, d)), res((B_WIDTH, d)), res((d, dff)), res((dff, d))],
        out_specs=tok(d),
        out_shape=jax.ShapeDtypeStruct((bsz, s, d), F32),
        compiler_params=_params(("arbitrary", "arbitrary")),
    )(x, oa, ob, mod4, mod4, mod4, mod4, g_ffn, w_out_a, w_out_b, w1, w2)


def _block_ones(n, blk, dtype=BF16):
    i = jnp.arange(n)
    return ((i[:, None] // blk) == (i[None, :] // blk)).astype(dtype)


def kernel(x, c, w_ada, b_ada, g_mix, g_ffn, w_in, g_q, g_k, g_kv, w_uk, w_uv, mu_shift, w0, w2, a0, a2, g2,
           k_k, k_a, r_k, ln_w, ln_b, w_out, w_ff1, w_ff2):
    bsz, s, d = x.shape
    depth = w_ada.shape[0]
    assert s % Q_TILE == 0 and s % FRONT_TILE == 0 and s % RWKV_TILE == 0 and s % FFN_TILE == 0
    assert Q_TILE == TOK_TILE and K_TILE == TOK_TILE
    topk = min(TOPK_MAX, s // 4)

    eb = _block_ones(SEG_K, B_HEAD_DIM)
    ex = (jnp.arange(2 * A_HEAD_DIM)[:, None] // A_HEAD_DIM == jnp.arange(2 * KV_LATENT)[None, :] // KV_LATENT
          ).astype(BF16)
    sel = (jnp.arange(LANES)[None, :] == IDX_DIM + jnp.arange(IDX_HEADS)[:, None]).astype(BF16)
    eye_l = jnp.eye(KV_LATENT, dtype=BF16)
    ti = jnp.arange(TOK_TILE)
    tri = (((ti[:, None] // CHUNK) == (ti[None, :] // CHUNK)) & (ti[:, None] >= ti[None, :])).astype(BF16)
    ki = jnp.arange(K_TILE)
    lstrict = (ki[None, :] < ki[:, None]).astype(BF16)

    for l in range(depth):
        w_a = jnp.pad(w_in[l][:, :N_IN_A], ((0, 0), (0, N_A_PAD - N_IN_A)))
        w_in_p = jnp.concatenate([w_a, w_in[l][:, N_IN_A:]], axis=1).astype(BF16)
        wuk_flat = w_uk[l].reshape(KV_LATENT, A_WIDTH).astype(BF16)
        wuk_t = jnp.transpose(w_uk[l], (1, 2, 0)).reshape(A_HEADS // 2, 2, A_HEAD_DIM, KV_LATENT)
        wuk_bd = (jnp.eye(2, dtype=F32)[None, :, None, :, None] * wuk_t[:, :, :, None, :]).reshape(
            A_HEADS // 2, 2 * A_HEAD_DIM, 2 * KV_LATENT).astype(BF16)
        wuv_t = jnp.transpose(w_uv[l], (1, 0, 2)).reshape(A_HEADS // 2, 2, KV_LATENT, A_HEAD_DIM)
        wuv_pair = (jnp.eye(2, dtype=F32)[None, :, None, :, None] * wuv_t[:, :, :, None, :]).reshape(
            A_HEADS // 2, 2 * KV_LATENT, 2 * A_HEAD_DIM).astype(BF16)
        gqk = jnp.tile(g_q[l] * g_k[l], A_HEADS).reshape(1, A_WIDTH)
        r1 = lambda t: t.reshape(1, -1)

        mod = _mod_call(c, w_ada[l], b_ada[l])
        mod4 = mod.reshape(bsz, 6, 1, d)
        a_consts = (r1(g_kv[l]), gqk, wuk_flat, wuk_bd, eb, ex, sel, eye_l)
        b_consts = (r1(w0[l]), w2[l].astype(BF16), r1(a0[l]), a2[l].astype(BF16), g2[l].astype(BF16),
                    r1(k_k[l]), r1(k_a[l]), r1(r_k[l]), eb, tri)
        ckr, cvt, qabs, qidx, kidx, widx, qn, kn, rt, kt, bt, kl, v, g, bv, pc = _front_call(
            x, mod4, r1(g_mix[l]), w_in_p, r1(mu_shift[l]), a_consts, b_consts)
        o_a = _dsa_call(topk, qabs, qidx, widx, ckr, cvt, kidx, wuv_pair, lstrict, qn, kn)
        o_b = _rwkv_call(rt, kt, bt, kl, v, g, bv, pc, r1(ln_w[l]), r1(ln_b[l]), eb)
        x = _ffn_call(x, o_a, o_b, mod4, r1(g_ffn[l]), w_out[l][:A_WIDTH].astype(BF16),
                      w_out[l][A_WIDTH:].astype(BF16), w_ff1[l].astype(BF16), w_ff2[l].astype(BF16))
    return x
```

```python
import functools

import jax
import jax.numpy as jnp
from jax import lax
from jax.experimental import pallas as pl
from jax.experimental.pallas import tpu as pltpu

F32 = jnp.float32
BF16 = jnp.bfloat16

CHUNK = 64
A_HEADS = 8
A_HEAD_DIM = 64
A_WIDTH = A_HEADS * A_HEAD_DIM
KV_LATENT = 128
IDX_HEADS = 8
IDX_DIM = 64
TOPK_MAX = 256
B_HEADS = 8
B_HEAD_DIM = 64
B_WIDTH = B_HEADS * B_HEAD_DIM
W_LORA = 64
A_LORA = 64
G_LORA = 128
RMS_EPS = 1e-6
GN_EPS = 64e-5
N_IN_A = A_WIDTH + KV_LATENT + IDX_HEADS * IDX_DIM + IDX_DIM + IDX_HEADS
N_IN_B = 3 * B_WIDTH + W_LORA + A_LORA + G_LORA
N_A_PAD = 1280

LANES = 128
SUBLANES = 8
ADD_CHAINS = 4
MOD_COLS = 1024
SEG_K = 256
TOK_TILE = 256
FRONT_TILE = 512
RWKV_TILE = 512
Q_TILE = 256
FFN_TILE = 1024
FFN_ROWS = 512
K_TILE = 256
DIST_BIG = 1e30
ONES_ROWS = 16
LOG2E = 1.4426950408889634
EXP_NEG_HALF = 0.6065306597126334
EXP_RANGE = 90.0
BOUND_MARGIN = 1.02
SEARCH_PROBES = 14
VMEM_LIMIT = 56 * 1024 * 1024


def _dot(a, b):
    return jnp.dot(a, b, preferred_element_type=F32)


def _dot_nt(a, b):
    return lax.dot_general(a, b, (((1,), (1,)), ((), ())), preferred_element_type=F32)


def _dot_tn(a, b):
    return lax.dot_general(a, b, (((0,), (0,)), ((), ())), preferred_element_type=F32)


def _split(x):
    hi = x.astype(BF16)
    lo = (x - hi.astype(F32)).astype(BF16)
    return hi, lo


def _dot_hl(x, e):
    hi, lo = _split(x)
    return _dot(hi, e) + _dot(lo, e)


def _seg_dot_hl(x, e):
    k = e.shape[0]
    return jnp.concatenate([_dot_hl(x[:, j:j + k], e) for j in range(0, x.shape[1], k)], axis=1)


def _seg_dot(x, e):
    k = e.shape[0]
    xb = x.astype(BF16)
    return jnp.concatenate([_dot(xb[:, j:j + k], e) for j in range(0, x.shape[1], k)], axis=1)


def _params(sem):
    return pltpu.CompilerParams(dimension_semantics=sem, vmem_limit_bytes=VMEM_LIMIT)


def _mod_kernel(c_ref, w_ref, b_ref, o_ref):
    c = c_ref[...]
    s = c * jax.nn.sigmoid(c)
    s_hi, s_lo = _split(s)
    w_hi, w_lo = _split(w_ref[...])
    o_ref[...] = _dot(s_hi, w_hi) + _dot(s_hi, w_lo) + _dot(s_lo, w_hi) + b_ref[...]


def _mod_call(c, w_ada, b_ada):
    bsz, d = c.shape
    n = w_ada.shape[1]
    tn = MOD_COLS
    return pl.pallas_call(
        _mod_kernel,
        grid=(n // tn,),
        in_specs=[pl.BlockSpec((bsz, d), lambda j: (0, 0)),
                  pl.BlockSpec((d, tn), lambda j: (0, j)),
                  pl.BlockSpec((1, tn), lambda j: (0, j))],
        out_specs=pl.BlockSpec((bsz, tn), lambda j: (0, j)),
        out_shape=jax.ShapeDtypeStruct((bsz, n), F32),
        compiler_params=_params(("arbitrary",)),
    )(c, w_ada, b_ada.reshape(1, n))


def _head_norm2_max(x):
    best = None
    for h in range(x.shape[1] // KV_LATENT):
        xh = x[:, h * KV_LATENT:(h + 1) * KV_LATENT]
        n2 = jnp.max(jnp.sum(xh * xh, axis=-1, keepdims=True), axis=0, keepdims=True)
        best = n2 if best is None else jnp.maximum(best, n2)
    return best


def _prep_a(pa, rows, blk, gkv_ref, gqk_ref, wuk_ref, wukbd_ref, eb_ref, ex_ref, sel_ref, eye_ref,
            ckr_ref, cvt_ref, qabs_ref, qidx_ref, kidx_ref, widx_ref, qn_ref, kn_ref):
    tm = pa.shape[0]
    q = pa[:, :A_WIDTH]
    cl = pa[:, A_WIDTH:A_WIDTH + KV_LATENT]
    o_qi = A_WIDTH + KV_LATENT
    qi = pa[:, o_qi:o_qi + IDX_HEADS * IDX_DIM]
    o_kw = o_qi + IDX_HEADS * IDX_DIM
    kw = pa[:, o_kw:o_kw + LANES]

    ckv = cl * lax.rsqrt(jnp.mean(cl * cl, axis=-1, keepdims=True) + RMS_EPS) * gkv_ref[...]
    ckv_b = ckv.astype(BF16)
    cvt_ref[0, blk, :KV_LATENT, :] = _dot_nt(eye_ref[...], ckv_b).astype(BF16)
    cvt_ref[0, blk, KV_LATENT:, :] = jnp.ones((ONES_ROWS, tm), BF16)
    kf = _dot(ckv_b, wuk_ref[...])
    ss = _seg_dot(kf * kf, ex_ref[...])
    inv_rms = lax.rsqrt(ss * (1.0 / A_HEAD_DIM) + RMS_EPS)
    ckr = jnp.concatenate([ckv] * A_HEADS, axis=1) * inv_rms
    ckr_ref[0, rows] = ckr.astype(BF16)
    kn_ref[0, blk] = jnp.broadcast_to(_head_norm2_max(ckr), kn_ref.shape[2:])

    ssq = _seg_dot(q * q, eb_ref[...])
    qh = q * lax.rsqrt(ssq * (1.0 / A_HEAD_DIM) + RMS_EPS) * gqk_ref[...]
    qh_b = qh.astype(BF16)
    qn = None
    for j in range(A_HEADS // 2):
        qabs = _dot(qh_b[:, j * LANES:(j + 1) * LANES], wukbd_ref[j]) * (A_HEAD_DIM ** -0.5 * LOG2E)
        qabs_ref[0, rows, 2 * j * KV_LATENT:2 * (j + 1) * KV_LATENT] = qabs.astype(BF16)
        qn = _head_norm2_max(qabs) if qn is None else jnp.maximum(qn, _head_norm2_max(qabs))
    qn_ref[0, blk] = jnp.broadcast_to(qn, qn_ref.shape[2:])
    for h in range(IDX_HEADS):
        qidx_ref[0, h, rows] = qi[:, h * IDX_DIM:(h + 1) * IDX_DIM].astype(BF16)
    kidx_ref[0, rows] = kw[:, :IDX_DIM].astype(BF16)
    kw_hi, kw_lo = _split(kw)
    w_t = _dot_nt(sel_ref[...], kw_hi) + _dot_nt(sel_ref[...], kw_lo)
    widx_ref[0, blk] = w_t * (IDX_HEADS ** -0.5 * IDX_DIM ** -0.5)


def _prep_b(pb, rows, blk, w0_ref, w2_ref, a0_ref, a2_ref, g2_ref, kk_ref, ka_ref, rk_ref, eb_ref, tri_ref,
            rt_ref, kt_ref, bt_ref, kl_ref, v_ref, g_ref, bv_ref, pc_ref):
    r = pb[:, :B_WIDTH]
    k = pb[:, B_WIDTH:2 * B_WIDTH]
    v = pb[:, 2 * B_WIDTH:3 * B_WIDTH]
    o = 3 * B_WIDTH
    xw = pb[:, o:o + W_LORA]
    xa = pb[:, o + W_LORA:o + W_LORA + A_LORA]
    xg = pb[:, o + W_LORA + A_LORA:o + W_LORA + A_LORA + G_LORA]

    z = w0_ref[...] + _dot(jnp.tanh(xw).astype(BF16), w2_ref[...])
    lw = -EXP_NEG_HALF * jax.nn.sigmoid(z)
    a = jax.nn.sigmoid(a0_ref[...] + _dot(xa.astype(BF16), a2_ref[...]))
    g = _dot(jax.nn.sigmoid(xg).astype(BF16), g2_ref[...])
    kk = k * kk_ref[...]
    kkn = kk * lax.rsqrt(jnp.maximum(_seg_dot(kk * kk, eb_ref[...]), 1e-24))
    kp = k * (1.0 + (a - 1.0) * ka_ref[...])
    bonus = _seg_dot(r * kp * rk_ref[...], eb_ref[...])

    lw_hi, lw_lo = _split(lw)
    cum = _dot(tri_ref[...], lw_hi) + _dot(tri_ref[...], lw_lo)
    e_pos = jnp.exp(cum)
    e_neg = jnp.exp(-cum)
    rt_ref[0, rows] = (r * e_pos).astype(BF16)
    kt_ref[0, rows] = (kkn * jnp.exp(cum - lw)).astype(BF16)
    bt_ref[0, rows] = (kkn * a * e_neg).astype(BF16)
    kl_ref[0, rows] = (kp * e_neg).astype(BF16)
    v_ref[0, rows] = v.astype(BF16)
    g_ref[0, rows] = g
    bv_ref[0, rows] = bonus * v
    for c in range(pb.shape[0] // CHUNK):
        pc_ref[0, blk, c:c + 1, :] = e_pos[(c + 1) * CHUNK - 1:(c + 1) * CHUNK, :]


N_FRONT_IN = 6
N_PREP_A_IN = 8
N_PREP_B_IN = 10
N_PREP_A_OUT = 8


def _front_kernel(*refs):
    x_ref, sh_ref, sc_ref, g_ref, w_ref, mu_ref = refs[:N_FRONT_IN]
    a_in = refs[N_FRONT_IN:N_FRONT_IN + N_PREP_A_IN]
    b_in = refs[N_FRONT_IN + N_PREP_A_IN:N_FRONT_IN + N_PREP_A_IN + N_PREP_B_IN]
    outs = refs[N_FRONT_IN + N_PREP_A_IN + N_PREP_B_IN:-1]
    carry_ref = refs[-1]
    j = pl.program_id(1)

    @pl.when(j == 0)
    def _():
        carry_ref[...] = jnp.zeros_like(carry_ref)

    x = x_ref[0]
    y = x * lax.rsqrt(jnp.mean(x * x, axis=-1, keepdims=True) + RMS_EPS) * g_ref[...]
    h = y * (1.0 + sc_ref[0, 0]) + sh_ref[0, 0]
    p = _dot(h.astype(BF16), w_ref[...])
    pb = p[:, N_A_PAD:]
    tm = pb.shape[0]
    row = lax.broadcasted_iota(jnp.int32, (tm, 1), 0)
    prev = jnp.where(row == 0, carry_ref[...], pltpu.roll(pb, 1, axis=0))
    carry_ref[...] = pb[tm - 1:tm, :]
    pb = pb + mu_ref[...] * (prev - pb)
    for blk in range(tm // TOK_TILE):
        rows = slice(blk * TOK_TILE, (blk + 1) * TOK_TILE)
        _prep_a(p[rows, :N_A_PAD], rows, blk, *a_in, *outs[:N_PREP_A_OUT])
        _prep_b(pb[rows], rows, blk, *b_in, *outs[N_PREP_A_OUT:])


def _front_call(x, mod4, g_mix, w_in_p, mu, a_consts, b_consts):
    bsz, s, d = x.shape
    n = w_in_p.shape[1]
    nb = n - N_A_PAD
    tm = FRONT_TILE
    tt = TOK_TILE
    full = lambda arr: pl.BlockSpec(arr.shape, lambda b, j, nd=arr.ndim: (0,) * nd)
    tok = lambda w: pl.BlockSpec((1, tm, w), lambda b, j: (b, j, 0))
    per_tile = lambda r, c: pl.BlockSpec((1, tm // tt, r, c), lambda b, j: (b, j, 0, 0))
    bf = lambda w: jax.ShapeDtypeStruct((bsz, s, w), BF16)
    ff = lambda w: jax.ShapeDtypeStruct((bsz, s, w), F32)
    nt = s // tt
    out_specs = [tok(A_HEADS * KV_LATENT), per_tile(KV_LATENT + ONES_ROWS, tt), tok(A_HEADS * KV_LATENT),
                 pl.BlockSpec((1, IDX_HEADS, tm, IDX_DIM), lambda b, j: (b, 0, j, 0)),
                 tok(IDX_DIM), per_tile(IDX_HEADS, tt), per_tile(SUBLANES, LANES), per_tile(SUBLANES, LANES)
                 ] + [tok(B_WIDTH)] * 7 + [per_tile(tt // CHUNK, B_WIDTH)]
    out_shape = [bf(A_HEADS * KV_LATENT),
                 jax.ShapeDtypeStruct((bsz, nt, KV_LATENT + ONES_ROWS, tt), BF16),
                 bf(A_HEADS * KV_LATENT),
                 jax.ShapeDtypeStruct((bsz, IDX_HEADS, s, IDX_DIM), BF16),
                 bf(IDX_DIM),
                 jax.ShapeDtypeStruct((bsz, nt, IDX_HEADS, tt), F32),
                 jax.ShapeDtypeStruct((bsz, nt, SUBLANES, LANES), F32),
                 jax.ShapeDtypeStruct((bsz, nt, SUBLANES, LANES), F32),
                 bf(B_WIDTH), bf(B_WIDTH), bf(B_WIDTH), bf(B_WIDTH), bf(B_WIDTH), ff(B_WIDTH), ff(B_WIDTH),
                 jax.ShapeDtypeStruct((bsz, nt, tt // CHUNK, B_WIDTH), F32)]
    assert len(a_consts) == N_PREP_A_IN and len(b_consts) == N_PREP_B_IN
    return pl.pallas_call(
        _front_kernel,
        grid=(bsz, s // tm),
        in_specs=[pl.BlockSpec((1, tm, d), lambda b, j: (b, j, 0)),
                  pl.BlockSpec((1, 1, 1, d), lambda b, j: (b, 0, 0, 0)),
                  pl.BlockSpec((1, 1, 1, d), lambda b, j: (b, 1, 0, 0)),
                  full(g_mix),
                  pl.BlockSpec(w_in_p.shape, lambda b, j: (0, 0), pipeline_mode=pl.Buffered(1)),
                  full(mu)] + [full(t) for t in a_consts] + [full(t) for t in b_consts],
        out_specs=out_specs,
        out_shape=out_shape,
        scratch_shapes=[pltpu.VMEM((1, nb), F32)],
        compiler_params=_params(("arbitrary", "arbitrary")),
    )(x, mod4, mod4, g_mix, w_in_p, mu, *a_consts, *b_consts)


def _colsum8(x):
    y = x.reshape(ADD_CHAINS, K_TILE // (ADD_CHAINS * SUBLANES), SUBLANES, Q_TILE)
    return jnp.sum(jnp.sum(y, axis=1), axis=0)


def _colmin8(x):
    y = x.reshape(ADD_CHAINS, K_TILE // (ADD_CHAINS * SUBLANES), SUBLANES, Q_TILE)
    return jnp.min(jnp.min(y, axis=1), axis=0)


def _colmax8(x):
    y = x.reshape(ADD_CHAINS, K_TILE // (ADD_CHAINS * SUBLANES), SUBLANES, Q_TILE)
    return jnp.max(jnp.max(y, axis=1), axis=0)


def _for_key_tiles(nkc, body, init):
    def quad(j, c):
        return body(4 * j + 3, body(4 * j + 2, body(4 * j + 1, body(4 * j, c))))
    c = lax.fori_loop(0, nkc // 4, quad, init)
    base = (nkc // 4) * 4
    c = lax.cond(nkc % 4 >= 2, lambda c: body(base + 1, body(base, c)), lambda c: c, c)
    return lax.cond(nkc % 2 == 1, lambda c: body(nkc - 1, c), lambda c: c, c)


def _dsa_kernel(topk, qabs_ref, qidx_ref, widx_ref, ckr_ref, cvt_ref, kidx_ref, wuv_ref, lstrict_ref, qn_ref, kn_ref,
                o_ref, score_ref, dist_ref, logit_ref, p_ref, m_ref, acc_ref):
    i = pl.program_id(1)
    nkc = i + 1
    t0 = i * Q_TILE
    krow = lax.broadcasted_iota(jnp.int32, (K_TILE, 1), 0)
    qcol = lax.broadcasted_iota(jnp.int32, (1, Q_TILE), 1)
    limit = ((t0 + qcol) // CHUNK + 1) * CHUNK
    kp = jnp.minimum(limit, topk).astype(F32)
    rel = (qcol - krow).astype(F32)

    def p1(kc, carry):
        rmin, rmax = carry
        k = kidx_ref[0, pl.ds(pl.multiple_of(kc * K_TILE, K_TILE), K_TILE), :]
        acc = jnp.zeros((K_TILE, Q_TILE), F32)
        for h in range(IDX_HEADS):
            s = _dot_nt(k, qidx_ref[0, h])
            acc = acc + widx_ref[0, 0, h:h + 1, :] * jnp.maximum(s, 0.0)
        adm = (kc * K_TILE + krow) < limit
        score_ref[kc] = jnp.where(adm, acc, -jnp.inf)
        rmin = jnp.minimum(rmin, _colmin8(jnp.where(adm, acc, jnp.inf)))
        rmax = jnp.maximum(rmax, _colmax8(jnp.where(adm, acc, -jnp.inf)))
        return rmin, rmax

    rmin, rmax = _for_key_tiles(
        nkc, p1, (jnp.full((SUBLANES, Q_TILE), jnp.inf, F32), jnp.full((SUBLANES, Q_TILE), -jnp.inf, F32)))
    lo = jnp.min(rmin, axis=0, keepdims=True)
    hi = jnp.max(rmax, axis=0, keepdims=True)

    def count(pred):
        def body(kc, acc):
            return acc + _colsum8(jnp.where(pred(score_ref[kc]), 1.0, 0.0))
        return jnp.sum(lax.fori_loop(0, nkc, body, jnp.zeros((SUBLANES, Q_TILE), F32)), axis=0, keepdims=True)

    def probe(c):
        lo, hi, cnt_lo = c
        mid = lo + 0.5 * (hi - lo)
        cnt = count(lambda sc: sc >= mid)
        ge = cnt >= kp
        return jnp.where(ge, mid, lo), jnp.where(ge, hi, mid), jnp.where(ge, cnt, cnt_lo)

    def smallest(pred):
        def body(kc, acc):
            sc = score_ref[kc]
            return jnp.minimum(acc, _colmin8(jnp.where(pred(sc), sc, jnp.inf)))
        return jnp.min(lax.fori_loop(0, nkc, body, jnp.full((SUBLANES, Q_TILE), jnp.inf, F32)), axis=0, keepdims=True)

    def any_true(x):
        return jnp.max(jnp.where(x, 1.0, 0.0)) > 0.0

    lo, _, cnt_lo = lax.fori_loop(0, SEARCH_PROBES, lambda _, c: probe(c), (lo, hi, limit.astype(F32)))

    def step_up(c):
        it, thr, cnt_gt, cnt_ge = c
        up = cnt_gt >= kp
        thr = jnp.where(up, smallest(lambda sc: sc > thr), thr)
        return it + 1, thr, count(lambda sc: sc > thr), jnp.where(up, cnt_gt, cnt_ge)

    thr = smallest(lambda sc: sc >= lo)
    search = lax.while_loop(
        lambda c: jnp.logical_and(c[0] < nkc * K_TILE, any_true(c[2] >= kp)),
        step_up, (jnp.int32(0), thr, count(lambda sc: sc > thr), cnt_lo))
    thr, need, cnt_ge = search[1], kp - search[2], search[3]

    def dist_tile(kc):
        return jnp.abs(rel + (t0 - kc * K_TILE).astype(F32))

    big8 = jnp.full((SUBLANES, Q_TILE), DIST_BIG, F32)

    def sel_plain():
        def body(kc, near):
            d = jnp.where(score_ref[kc] >= thr, dist_tile(kc), DIST_BIG)
            dist_ref[kc] = d
            return jnp.minimum(near, _colmin8(d))
        return jnp.min(lax.fori_loop(0, nkc, body, big8), axis=0, keepdims=True)

    def sel_ties():
        def body(kc, c):
            run, near = c
            sc = score_ref[kc]
            eq = sc == thr
            eq_f = jnp.where(eq, 1.0, 0.0)
            pre = run + _dot(lstrict_ref[...], eq_f.astype(BF16))
            keep = (sc > thr) | (eq & (pre < need))
            d = jnp.where(keep, dist_tile(kc), DIST_BIG)
            dist_ref[kc] = d
            return run + jnp.sum(_colsum8(eq_f), axis=0, keepdims=True), jnp.minimum(near, _colmin8(d))
        _, near = _for_key_tiles(nkc, body, (jnp.zeros((1, Q_TILE), F32), big8))
        return jnp.min(near, axis=0, keepdims=True)

    near = lax.cond(any_true(cnt_ge != kp), sel_ties, sel_plain)

    acc_ref[...] = jnp.zeros(acc_ref.shape, F32)

    bound = jnp.sqrt(jnp.max(qn_ref[0, 0]) * jnp.max(kn_ref[0])) * BOUND_MARGIN

    def att_shifted():
        def body(kc, _):
            d = dist_ref[kc] - near
            for h in range(A_HEADS):
                slope = 2.0 ** (-8.0 * (h + 1) / A_HEADS) * LOG2E
                ck = ckr_ref[0, pl.ds(pl.multiple_of(kc * K_TILE, K_TILE), K_TILE),
                             h * KV_LATENT:(h + 1) * KV_LATENT]
                logit = _dot_nt(ck, qabs_ref[0, :, h * KV_LATENT:(h + 1) * KV_LATENT]) - slope * d
                p_ref[h] = jnp.exp2(logit).astype(BF16)
            cv = cvt_ref[0, kc]
            for h in range(A_HEADS):
                acc_ref[h] = acc_ref[h] + _dot(cv, p_ref[h])
            return 0
        _for_key_tiles(nkc, body, 0)

    def att_online():
        m_ref[...] = jnp.full(m_ref.shape, -jnp.inf, F32)
        _for_key_tiles(nkc, att, 0)

    def att(kc, _):
        dist = dist_ref[kc]
        m_new = []
        for h in range(A_HEADS):
            slope = 2.0 ** (-8.0 * (h + 1) / A_HEADS) * LOG2E
            ck = ckr_ref[0, pl.ds(pl.multiple_of(kc * K_TILE, K_TILE), K_TILE), h * KV_LATENT:(h + 1) * KV_LATENT]
            logit = _dot_nt(ck, qabs_ref[0, :, h * KV_LATENT:(h + 1) * KV_LATENT]) - slope * dist
            logit_ref[h] = logit
            m_new.append(jnp.maximum(m_ref[h], jnp.max(_colmax8(logit), axis=0, keepdims=True)))
        cv = cvt_ref[0, kc]
        for h in range(A_HEADS):
            p = jnp.exp2(logit_ref[h] - m_new[h])
            acc_ref[h] = acc_ref[h] * jnp.exp2(m_ref[h] - m_new[h]) + _dot(cv, p.astype(BF16))
            m_ref[h] = m_new[h]
        return 0

    lax.cond(bound <= EXP_RANGE, att_shifted, att_online)

    for pair in range(A_HEADS // 2):
        o_pair = []
        for hh in range(2):
            a = acc_ref[2 * pair + hh]
            o_t = a[:KV_LATENT] * (1.0 / a[KV_LATENT:KV_LATENT + 1])
            o_pair.append(o_t.T.astype(BF16))
        o_lat = jnp.concatenate(o_pair, axis=1)
        o_ref[0, :, pair * LANES:(pair + 1) * LANES] = _dot(o_lat, wuv_ref[pair]).astype(o_ref.dtype)


def _dsa_call(topk, qabs, qidx, widx, ckr, cvt, kidx, wuv_pair, lstrict, qn, kn):
    bsz, s, _ = qabs.shape
    nq = s // Q_TILE
    nk = s // K_TILE
    qt = lambda w: pl.BlockSpec((1, Q_TILE, w), lambda b, i: (b, i, 0))
    return pl.pallas_call(
        functools.partial(_dsa_kernel, topk),
        grid=(bsz, nq),
        in_specs=[qt(A_HEADS * KV_LATENT),
                  pl.BlockSpec((1, IDX_HEADS, Q_TILE, IDX_DIM), lambda b, i: (b, 0, i, 0)),
                  pl.BlockSpec((1, 1, IDX_HEADS, Q_TILE), lambda b, i: (b, i, 0, 0)),
                  pl.BlockSpec((1, s, A_HEADS * KV_LATENT), lambda b, i: (b, 0, 0)),
                  pl.BlockSpec((1, nk, KV_LATENT + ONES_ROWS, K_TILE), lambda b, i: (b, 0, 0, 0)),
                  pl.BlockSpec((1, s, IDX_DIM), lambda b, i: (b, 0, 0)),
                  pl.BlockSpec((A_HEADS // 2, 2 * KV_LATENT, LANES), lambda b, i: (0, 0, 0)),
                  pl.BlockSpec((K_TILE, K_TILE), lambda b, i: (0, 0)),
                  pl.BlockSpec((1, 1, SUBLANES, LANES), lambda b, i: (b, i, 0, 0)),
                  pl.BlockSpec((1, nk, SUBLANES, LANES), lambda b, i: (b, 0, 0, 0))],
        out_specs=qt(A_WIDTH),
        out_shape=jax.ShapeDtypeStruct((bsz, s, A_WIDTH), BF16),
        scratch_shapes=[pltpu.VMEM((nk, K_TILE, Q_TILE), F32),
                        pltpu.VMEM((nk, K_TILE, Q_TILE), F32),
                        pltpu.VMEM((A_HEADS, K_TILE, Q_TILE), F32),
                        pltpu.VMEM((A_HEADS, K_TILE, Q_TILE), BF16),
                        pltpu.VMEM((A_HEADS, 1, Q_TILE), F32),
                        pltpu.VMEM((A_HEADS, KV_LATENT + ONES_ROWS, Q_TILE), F32)],
        compiler_params=_params(("arbitrary", "arbitrary")),
    )(qabs, qidx, widx, ckr, cvt, kidx, wuv_pair, lstrict, qn, kn)


def _rwkv_block(blk, tm, masks, rt_ref, kt_ref, bt_ref, kl_ref, v_ref, pc_ref):
    strict, incl, eye_s, lane_chunk, row_chunk, blk_diag, diag, head0 = masks
    nch = tm // CHUNK
    npair = B_HEADS // 2
    rows_b = slice(blk * tm, (blk + 1) * tm)
    zero_b = jnp.zeros((), BF16)

    def block_diag(m):
        return jnp.concatenate([jnp.where(lane_chunk == c, m, jnp.zeros((), m.dtype)) for c in range(nch)], axis=0)

    heads = [(p, hh) for p in range(npair) for hh in range(2)]
    rt, kt, bt, kl, v, v_bd = [], [], [], [], [], []
    a_ab, a_ak, m_rb, m_rk = [], [], [], []
    for p in range(npair):
        sl = slice(p * LANES, (p + 1) * LANES)
        rt.append(rt_ref[0, rows_b, sl])
        kt.append(kt_ref[0, rows_b, sl])
        bt.append(bt_ref[0, rows_b, sl])
        kl.append(kl_ref[0, rows_b, sl])
        v.append(v_ref[0, rows_b, sl])
        tc = 2 * CHUNK
        folded = []
        for g in range(nch // 2):
            r2 = slice(g * tc, (g + 1) * tc)
            lhs = jnp.concatenate([jnp.where(head0, kt[p][r2], zero_b), jnp.where(head0, zero_b, kt[p][r2]),
                                   jnp.where(head0, rt[p][r2], zero_b), jnp.where(head0, zero_b, rt[p][r2])], axis=0)
            prod = _dot_nt(lhs, jnp.concatenate([bt[p][r2], kl[p][r2]], axis=0))
            fold = lambda m: m[:CHUNK] + m[CHUNK:]
            folded.append([fold(jnp.where(strict if j < 2 else incl, prod[j * tc:(j + 1) * tc], 0.0))
                           for j in range(4)])
        gather = lambda j, half: jnp.concatenate(
            [f[j][:, half * tc:(half + 1) * tc] for f in folded], axis=1)
        v_bd.append(jnp.concatenate([jnp.where(row_chunk == c, v[p], zero_b) for c in range(nch)], axis=1))
        for hh in range(2):
            a_ab.append(gather(hh, 0))
            a_ak.append(gather(hh, 1).astype(BF16))
            m_rb.append(block_diag(gather(2 + hh, 0).astype(BF16)))
            m_rk.append(gather(2 + hh, 1).astype(BF16))

    t_inv = [(eye_s - a).astype(BF16) for a in a_ab]
    a_pow = [a.astype(BF16) for a in a_ab]
    for _ in range(5):
        a_sq = [_dot(a, block_diag(a)) for a in a_pow]
        a_pow = [a.astype(BF16) for a in a_sq]
        t_inv = [_dot(t, block_diag((eye_s + a).astype(BF16))).astype(BF16) for t, a in zip(t_inv, a_sq)]
    t_inv = [block_diag(t) for t in t_inv]

    avm = [_dot(jnp.concatenate([a_ak[i], m_rk[i]], axis=0), v_bd[p]) for i, (p, _) in enumerate(heads)]
    stack = lambda m: jnp.concatenate([m[:, c * LANES:(c + 1) * LANES] for c in range(nch)], axis=0)
    av = [stack(m[:CHUNK]) for m in avm]
    mv = [stack(m[CHUNK:]) for m in avm]
    x = [_dot(t_inv[i], jnp.concatenate([kt[p], av[i].astype(BF16)], axis=1))
         for i, (p, _) in enumerate(heads)]
    y = [_dot(m_rb[i], x[i].astype(BF16)) for i in range(len(heads))]

    head0_2 = jnp.concatenate([head0, head0], axis=1)
    zeros_b = jnp.zeros((CHUNK, LANES), BF16)
    q_b, ol, g_mat, f_mat = [], [], [], []
    for p in range(npair):
        i0, i1 = 2 * p, 2 * p + 1
        sl = slice(p * LANES, (p + 1) * LANES)
        wu_b = (-jnp.where(head0_2, x[i0], x[i1])).astype(BF16)
        yy = jnp.where(head0_2, y[i0], y[i1])
        q_b.append((rt[p].astype(F32) - yy[:, :LANES]).astype(BF16))
        ol.append(jnp.where(head0, mv[i0], mv[i1]) - yy[:, LANES:])
        gp, fp = [], []
        for c in range(nch):
            rows = slice(c * CHUNK, (c + 1) * CHUNK)
            pc = pc_ref[0, blk, c:c + 1, sl]
            bh = (bt[p][rows].astype(F32) * pc).astype(BF16)
            kh = (kl[p][rows].astype(F32) * pc).astype(BF16)
            rhs = jnp.concatenate([wu_b[rows], jnp.concatenate([zeros_b, v[p][rows]], axis=1)], axis=0)
            bw = _dot_tn(jnp.concatenate([bh, kh], axis=0), rhs)
            gp.append((jnp.where(diag, pc, 0.0) + jnp.where(blk_diag, bw[:, :LANES], 0.0)).astype(BF16))
            fp.append(jnp.where(blk_diag, bw[:, LANES:], 0.0))
        g_mat.append(gp)
        f_mat.append(fp)
    return q_b, ol, g_mat, f_mat


def _rwkv_kernel(rt_ref, kt_ref, bt_ref, kl_ref, v_ref, g_ref, bv_ref, pc_ref, lnw_ref, lnb_ref, eb_ref,
                 o_ref, h_ref):
    j = pl.program_id(1)

    @pl.when(j == 0)
    def _():
        h_ref[...] = jnp.zeros_like(h_ref)

    tm = TOK_TILE
    nch = tm // CHUNK
    npair = B_HEADS // 2
    ri = lax.broadcasted_iota(jnp.int32, (2 * CHUNK, 4 * CHUNK), 0)
    ci = lax.broadcasted_iota(jnp.int32, (2 * CHUNK, 4 * CHUNK), 1) % (2 * CHUNK)
    same = (ri // CHUNK) == (ci // CHUNK)
    r2 = lax.broadcasted_iota(jnp.int32, (LANES, LANES), 0)
    c2 = lax.broadcasted_iota(jnp.int32, (LANES, LANES), 1)
    lane = lax.broadcasted_iota(jnp.int32, (1, LANES), 1)
    rs = lax.broadcasted_iota(jnp.int32, (CHUNK, tm), 0)
    cs = lax.broadcasted_iota(jnp.int32, (CHUNK, tm), 1)
    masks = (same & (ri > ci), same & (ri >= ci), jnp.where(cs % CHUNK == rs, 1.0, 0.0),
             lax.broadcasted_iota(jnp.int32, (1, tm), 1) // CHUNK, lax.broadcasted_iota(jnp.int32, (tm, 1), 0) // CHUNK,
             (r2 // B_HEAD_DIM) == (c2 // B_HEAD_DIM), r2 == c2, (lane // B_HEAD_DIM) == 0)

    nblk = rt_ref.shape[1] // tm
    blocks = [_rwkv_block(b, tm, masks, rt_ref, kt_ref, bt_ref, kl_ref, v_ref, pc_ref) for b in range(nblk)]

    h = [h_ref[p] for p in range(npair)]
    for b, (q_b, ol, g_mat, f_mat) in enumerate(blocks):
        o_chunks = [[] for _ in range(npair)]
        for c in range(nch):
            rows = slice(c * CHUNK, (c + 1) * CHUNK)
            for p in range(npair):
                h_b = h[p].astype(BF16)
                o_chunks[p].append(_dot(q_b[p][rows], h_b) + ol[p][rows])
                h[p] = _dot(g_mat[p][c], h_b) + f_mat[p][c]
        out = jnp.concatenate([jnp.concatenate(oc, axis=0) for oc in o_chunks], axis=1)

        rows_b = slice(b * tm, (b + 1) * tm)
        eb = eb_ref[...]
        mean = _seg_dot_hl(out, eb) * (1.0 / B_HEAD_DIM)
        d = out - mean
        var = _seg_dot(d * d, eb) * (1.0 / B_HEAD_DIM)
        y = d * lax.rsqrt(var + GN_EPS) * lnw_ref[...] + lnb_ref[...] + bv_ref[0, rows_b]
        o_ref[0, rows_b] = (y * g_ref[0, rows_b]).astype(o_ref.dtype)
    for p in range(npair):
        h_ref[p] = h[p]


def _rwkv_call(rt, kt, bt, kl, v, g, bv, pc, ln_w, ln_b, eb):
    bsz, s, _ = rt.shape
    tm = RWKV_TILE
    tok = pl.BlockSpec((1, tm, B_WIDTH), lambda b, j: (b, j, 0))
    row = pl.BlockSpec((1, B_WIDTH), lambda b, j: (0, 0))
    return pl.pallas_call(
        _rwkv_kernel,
        grid=(bsz, s // tm),
        in_specs=[tok] * 7 + [pl.BlockSpec((1, tm // TOK_TILE, TOK_TILE // CHUNK, B_WIDTH), lambda b, j: (b, j, 0, 0)),
                              row, row, pl.BlockSpec((SEG_K, SEG_K), lambda b, j: (0, 0))],
        out_specs=tok,
        out_shape=jax.ShapeDtypeStruct((bsz, s, B_WIDTH), BF16),
        scratch_shapes=[pltpu.VMEM((B_HEADS // 2, LANES, LANES), F32)],
        compiler_params=_params(("arbitrary", "arbitrary")),
    )(rt, kt, bt, kl, v, g, bv, pc, ln_w, ln_b, eb)


def _ffn_kernel(x_ref, oa_ref, ob_ref, gt1_ref, sh2_ref, sc2_ref, gt2_ref, gf_ref, woa_ref, wob_ref,
                w1_ref, w2_ref, o_ref):
    for r0 in range(0, x_ref.shape[1], FFN_ROWS):
        rows = slice(r0, r0 + FFN_ROWS)
        mix = _dot(oa_ref[0, rows], woa_ref[...]) + _dot(ob_ref[0, rows], wob_ref[...])
        x1 = x_ref[0, rows] + gt1_ref[0, 0] * mix
        y = x1 * lax.rsqrt(jnp.mean(x1 * x1, axis=-1, keepdims=True) + RMS_EPS) * gf_ref[...]
        h2 = (y * (1.0 + sc2_ref[0, 0]) + sh2_ref[0, 0]).astype(BF16)
        u = jnp.maximum(_dot(h2, w1_ref[...]), 0.0)
        o_ref[0, rows] = x1 + gt2_ref[0, 0] * _dot((u * u).astype(BF16), w2_ref[...])


def _ffn_call(x, oa, ob, mod4, g_ffn, w_out_a, w_out_b, w1, w2):
    bsz, s, d = x.shape
    dff = w1.shape[1]
    tm = FFN_TILE
    tok = lambda w: pl.BlockSpec((1, tm, w), lambda b, j: (b, j, 0))
    modk = lambda k: pl.BlockSpec((1, 1, 1, d), lambda b, j, k=k: (b, k, 0, 0))
    res = lambda shape: pl.BlockSpec(shape, lambda b, j: (0, 0), pipeline_mode=pl.Buffered(1))
    return pl.pallas_call(
        _ffn_kernel,
        grid=(bsz, s // tm),
        in_specs=[tok(d), tok(A_WIDTH), tok(B_WIDTH), modk(2), modk(3), modk(4), modk(5),
                  pl.BlockSpec((1, d), lambda b, j: (0, 0)),
                  res((A_WIDTH, d)), res((B_WIDTH, d)), res((d, dff)), res((dff, d))],
        out_specs=tok(d),
        out_shape=jax.ShapeDtypeStruct((bsz, s, d), F32),
        compiler_params=_params(("arbitrary", "arbitrary")),
    )(x, oa, ob, mod4, mod4, mod4, mod4, g_ffn, w_out_a, w_out_b, w1, w2)


def _block_ones(n, blk, dtype=BF16):
    i = jnp.arange(n)
    return ((i[:, None] // blk) == (i[None, :] // blk)).astype(dtype)


def kernel(x, c, w_ada, b_ada, g_mix, g_ffn, w_in, g_q, g_k, g_kv, w_uk, w_uv, mu_shift, w0, w2, a0, a2, g2,
           k_k, k_a, r_k, ln_w, ln_b, w_out, w_ff1, w_ff2):
    bsz, s, d = x.shape
    depth = w_ada.shape[0]
    assert s % Q_TILE == 0 and s % FRONT_TILE == 0 and s % RWKV_TILE == 0 and s % FFN_TILE == 0
    assert Q_TILE == TOK_TILE and K_TILE == TOK_TILE
    topk = min(TOPK_MAX, s // 4)

    eb = _block_ones(SEG_K, B_HEAD_DIM)
    ex = (jnp.arange(2 * A_HEAD_DIM)[:, None] // A_HEAD_DIM == jnp.arange(2 * KV_LATENT)[None, :] // KV_LATENT
          ).astype(BF16)
    sel = (jnp.arange(LANES)[None, :] == IDX_DIM + jnp.arange(IDX_HEADS)[:, None]).astype(BF16)
    eye_l = jnp.eye(KV_LATENT, dtype=BF16)
    ti = jnp.arange(TOK_TILE)
    tri = (((ti[:, None] // CHUNK) == (ti[None, :] // CHUNK)) & (ti[:, None] >= ti[None, :])).astype(BF16)
    ki = jnp.arange(K_TILE)
    lstrict = (ki[None, :] < ki[:, None]).astype(BF16)

    for l in range(depth):
        w_a = jnp.pad(w_in[l][:, :N_IN_A], ((0, 0), (0, N_A_PAD - N_IN_A)))
        w_in_p = jnp.concatenate([w_a, w_in[l][:, N_IN_A:]], axis=1).astype(BF16)
        wuk_flat = w_uk[l].reshape(KV_LATENT, A_WIDTH).astype(BF16)
        wuk_t = jnp.transpose(w_uk[l], (1, 2, 0)).reshape(A_HEADS // 2, 2, A_HEAD_DIM, KV_LATENT)
        wuk_bd = (jnp.eye(2, dtype=F32)[None, :, None, :, None] * wuk_t[:, :, :, None, :]).reshape(
            A_HEADS // 2, 2 * A_HEAD_DIM, 2 * KV_LATENT).astype(BF16)
        wuv_t = jnp.transpose(w_uv[l], (1, 0, 2)).reshape(A_HEADS // 2, 2, KV_LATENT, A_HEAD_DIM)
        wuv_pair = (jnp.eye(2, dtype=F32)[None, :, None, :, None] * wuv_t[:, :, :, None, :]).reshape(
            A_HEADS // 2, 2 * KV_LATENT, 2 * A_HEAD_DIM).astype(BF16)
        gqk = jnp.tile(g_q[l] * g_k[l], A_HEADS).reshape(1, A_WIDTH)
        r1 = lambda t: t.reshape(1, -1)

        mod = _mod_call(c, w_ada[l], b_ada[l])
        mod4 = mod.reshape(bsz, 6, 1, d)
        a_consts = (r1(g_kv[l]), gqk, wuk_flat, wuk_bd, eb, ex, sel, eye_l)
        b_consts = (r1(w0[l]), w2[l].astype(BF16), r1(a0[l]), a2[l].astype(BF16), g2[l].astype(BF16),
                    r1(k_k[l]), r1(k_a[l]), r1(r_k[l]), eb, tri)
        ckr, cvt, qabs, qidx, kidx, widx, qn, kn, rt, kt, bt, kl, v, g, bv, pc = _front_call(
            x, mod4, r1(g_mix[l]), w_in_p, r1(mu_shift[l]), a_consts, b_consts)
        o_a = _dsa_call(topk, qabs, qidx, widx, ckr, cvt, kidx, wuv_pair, lstrict, qn, kn)
        o_b = _rwkv_call(rt, kt, bt, kl, v, g, bv, pc, r1(ln_w[l]), r1(ln_b[l]), eb)
        x = _ffn_call(x, o_a, o_b, mod4, r1(g_ffn[l]), w_out[l][:A_WIDTH].astype(BF16),
                      w_out[l][A_WIDTH:].astype(BF16), w_ff1[l].astype(BF16), w_ff2[l].astype(BF16))
    return x
```

```python
import functools

import jax
import jax.numpy as jnp
from jax import lax
from jax.experimental import pallas as pl
from jax.experimental.pallas import tpu as pltpu

F32 = jnp.float32
BF16 = jnp.bfloat16

CHUNK = 64
A_HEADS = 8
A_HEAD_DIM = 64
A_WIDTH = A_HEADS * A_HEAD_DIM
KV_LATENT = 128
IDX_HEADS = 8
IDX_DIM = 64
TOPK_MAX = 256
B_HEADS = 8
B_HEAD_DIM = 64
B_WIDTH = B_HEADS * B_HEAD_DIM
W_LORA = 64
A_LORA = 64
G_LORA = 128
RMS_EPS = 1e-6
GN_EPS = 64e-5
N_IN_A = A_WIDTH + KV_LATENT + IDX_HEADS * IDX_DIM + IDX_DIM + IDX_HEADS
N_IN_B = 3 * B_WIDTH + W_LORA + A_LORA + G_LORA
N_A_PAD = 1280

LANES = 128
SUBLANES = 8
ADD_CHAINS = 4
MOD_COLS = 1024
SEG_K = 256
TOK_TILE = 256
FRONT_TILE = 512
RWKV_TILE = 512
Q_TILE = 256
FFN_TILE = 1024
FFN_ROWS = 512
K_TILE = 256
DIST_BIG = 1e30
ONES_ROWS = 16
LOG2E = 1.4426950408889634
EXP_NEG_HALF = 0.6065306597126334
EXP_RANGE = 90.0
BOUND_MARGIN = 1.02
SEARCH_PROBES = 14
VMEM_LIMIT = 56 * 1024 * 1024


def _dot(a, b):
    return jnp.dot(a, b, preferred_element_type=F32)


def _dot_nt(a, b):
    return lax.dot_general(a, b, (((1,), (1,)), ((), ())), preferred_element_type=F32)


def _dot_tn(a, b):
    return lax.dot_general(a, b, (((0,), (0,)), ((), ())), preferred_element_type=F32)


def _split(x):
    hi = x.astype(BF16)
    lo = (x - hi.astype(F32)).astype(BF16)
    return hi, lo


def _dot_hl(x, e):
    hi, lo = _split(x)
    return _dot(hi, e) + _dot(lo, e)


def _seg_dot_hl(x, e):
    k = e.shape[0]
    return jnp.concatenate([_dot_hl(x[:, j:j + k], e) for j in range(0, x.shape[1], k)], axis=1)


def _seg_dot(x, e):
    k = e.shape[0]
    xb = x.astype(BF16)
    return jnp.concatenate([_dot(xb[:, j:j + k], e) for j in range(0, x.shape[1], k)], axis=1)


def _params(sem):
    return pltpu.CompilerParams(dimension_semantics=sem, vmem_limit_bytes=VMEM_LIMIT)


def _mod_kernel(c_ref, w_ref, b_ref, o_ref):
    c = c_ref[...]
    s = c * jax.nn.sigmoid(c)
    s_hi, s_lo = _split(s)
    w_hi, w_lo = _split(w_ref[...])
    o_ref[...] = _dot(s_hi, w_hi) + _dot(s_hi, w_lo) + _dot(s_lo, w_hi) + b_ref[...]


def _mod_call(c, w_ada, b_ada):
    bsz, d = c.shape
    n = w_ada.shape[1]
    tn = MOD_COLS
    return pl.pallas_call(
        _mod_kernel,
        grid=(n // tn,),
        in_specs=[pl.BlockSpec((bsz, d), lambda j: (0, 0)),
                  pl.BlockSpec((d, tn), lambda j: (0, j)),
                  pl.BlockSpec((1, tn), lambda j: (0, j))],
        out_specs=pl.BlockSpec((bsz, tn), lambda j: (0, j)),
        out_shape=jax.ShapeDtypeStruct((bsz, n), F32),
        compiler_params=_params(("arbitrary",)),
    )(c, w_ada, b_ada.reshape(1, n))


def _head_norm2_max(x):
    best = None
    for h in range(x.shape[1] // KV_LATENT):
        xh = x[:, h * KV_LATENT:(h + 1) * KV_LATENT]
        n2 = jnp.max(jnp.sum(xh * xh, axis=-1, keepdims=True), axis=0, keepdims=True)
        best = n2 if best is None else jnp.maximum(best, n2)
    return best


def _prep_a(pa, rows, blk, gkv_ref, gqk_ref, wuk_ref, wukbd_ref, eb_ref, ex_ref, sel_ref, eye_ref,
            ckr_ref, cvt_ref, qabs_ref, qidx_ref, kidx_ref, widx_ref, qn_ref, kn_ref):
    tm = pa.shape[0]
    q = pa[:, :A_WIDTH]
    cl = pa[:, A_WIDTH:A_WIDTH + KV_LATENT]
    o_qi = A_WIDTH + KV_LATENT
    qi = pa[:, o_qi:o_qi + IDX_HEADS * IDX_DIM]
    o_kw = o_qi + IDX_HEADS * IDX_DIM
    kw = pa[:, o_kw:o_kw + LANES]

    ckv = cl * lax.rsqrt(jnp.mean(cl * cl, axis=-1, keepdims=True) + RMS_EPS) * gkv_ref[...]
    ckv_b = ckv.astype(BF16)
    cvt_ref[0, blk, :KV_LATENT, :] = _dot_nt(eye_ref[...], ckv_b).astype(BF16)
    cvt_ref[0, blk, KV_LATENT:, :] = jnp.ones((ONES_ROWS, tm), BF16)
    kf = _dot(ckv_b, wuk_ref[...])
    ss = _seg_dot(kf * kf, ex_ref[...])
    inv_rms = lax.rsqrt(ss * (1.0 / A_HEAD_DIM) + RMS_EPS)
    ckr = jnp.concatenate([ckv] * A_HEADS, axis=1) * inv_rms
    ckr_ref[0, rows] = ckr.astype(BF16)
    kn_ref[0, blk] = jnp.broadcast_to(_head_norm2_max(ckr), kn_ref.shape[2:])

    ssq = _seg_dot(q * q, eb_ref[...])
    qh = q * lax.rsqrt(ssq * (1.0 / A_HEAD_DIM) + RMS_EPS) * gqk_ref[...]
    qh_b = qh.astype(BF16)
    qn = None
    for j in range(A_HEADS // 2):
        qabs = _dot(qh_b[:, j * LANES:(j + 1) * LANES], wukbd_ref[j]) * (A_HEAD_DIM ** -0.5 * LOG2E)
        qabs_ref[0, rows, 2 * j * KV_LATENT:2 * (j + 1) * KV_LATENT] = qabs.astype(BF16)
        qn = _head_norm2_max(qabs) if qn is None else jnp.maximum(qn, _head_norm2_max(qabs))
    qn_ref[0, blk] = jnp.broadcast_to(qn, qn_ref.shape[2:])
    for h in range(IDX_HEADS):
        qidx_ref[0, h, rows] = qi[:, h * IDX_DIM:(h + 1) * IDX_DIM].astype(BF16)
    kidx_ref[0, rows] = kw[:, :IDX_DIM].astype(BF16)
    kw_hi, kw_lo = _split(kw)
    w_t = _dot_nt(sel_ref[...], kw_hi) + _dot_nt(sel_ref[...], kw_lo)
    widx_ref[0, blk] = w_t * (IDX_HEADS ** -0.5 * IDX_DIM ** -0.5)


def _prep_b(pb, rows, blk, w0_ref, w2_ref, a0_ref, a2_ref, g2_ref, kk_ref, ka_ref, rk_ref, eb_ref, tri_ref,
            rt_ref, kt_ref, bt_ref, kl_ref, v_ref, g_ref, bv_ref, pc_ref):
    r = pb[:, :B_WIDTH]
    k = pb[:, B_WIDTH:2 * B_WIDTH]
    v = pb[:, 2 * B_WIDTH:3 * B_WIDTH]
    o = 3 * B_WIDTH
    xw = pb[:, o:o + W_LORA]
    xa = pb[:, o + W_LORA:o + W_LORA + A_LORA]
    xg = pb[:, o + W_LORA + A_LORA:o + W_LORA + A_LORA + G_LORA]

    z = w0_ref[...] + _dot(jnp.tanh(xw).astype(BF16), w2_ref[...])
    lw = -EXP_NEG_HALF * jax.nn.sigmoid(z)
    a = jax.nn.sigmoid(a0_ref[...] + _dot(xa.astype(BF16), a2_ref[...]))
    g = _dot(jax.nn.sigmoid(xg).astype(BF16), g2_ref[...])
    kk = k * kk_ref[...]
    kkn = kk * lax.rsqrt(jnp.maximum(_seg_dot(kk * kk, eb_ref[...]), 1e-24))
    kp = k * (1.0 + (a - 1.0) * ka_ref[...])
    bonus = _seg_dot(r * kp * rk_ref[...], eb_ref[...])

    lw_hi, lw_lo = _split(lw)
    cum = _dot(tri_ref[...], lw_hi) + _dot(tri_ref[...], lw_lo)
    e_pos = jnp.exp(cum)
    e_neg = jnp.exp(-cum)
    rt_ref[0, rows] = (r * e_pos).astype(BF16)
    kt_ref[0, rows] = (kkn * jnp.exp(cum - lw)).astype(BF16)
    bt_ref[0, rows] = (kkn * a * e_neg).astype(BF16)
    kl_ref[0, rows] = (kp * e_neg).astype(BF16)
    v_ref[0, rows] = v.astype(BF16)
    g_ref[0, rows] = g
    bv_ref[0, rows] = bonus * v
    for c in range(pb.shape[0] // CHUNK):
        pc_ref[0, blk, c:c + 1, :] = e_pos[(c + 1) * CHUNK - 1:(c + 1) * CHUNK, :]


N_FRONT_IN = 6
N_PREP_A_IN = 8
N_PREP_B_IN = 10
N_PREP_A_OUT = 8


def _front_kernel(*refs):
    x_ref, sh_ref, sc_ref, g_ref, w_ref, mu_ref = refs[:N_FRONT_IN]
    a_in = refs[N_FRONT_IN:N_FRONT_IN + N_PREP_A_IN]
    b_in = refs[N_FRONT_IN + N_PREP_A_IN:N_FRONT_IN + N_PREP_A_IN + N_PREP_B_IN]
    outs = refs[N_FRONT_IN + N_PREP_A_IN + N_PREP_B_IN:-1]
    carry_ref = refs[-1]
    j = pl.program_id(1)

    @pl.when(j == 0)
    def _():
        carry_ref[...] = jnp.zeros_like(carry_ref)

    x = x_ref[0]
    y = x * lax.rsqrt(jnp.mean(x * x, axis=-1, keepdims=True) + RMS_EPS) * g_ref[...]
    h = y * (1.0 + sc_ref[0, 0]) + sh_ref[0, 0]
    p = _dot(h.astype(BF16), w_ref[...])
    pb = p[:, N_A_PAD:]
    tm = pb.shape[0]
    row = lax.broadcasted_iota(jnp.int32, (tm, 1), 0)
    prev = jnp.where(row == 0, carry_ref[...], pltpu.roll(pb, 1, axis=0))
    carry_ref[...] = pb[tm - 1:tm, :]
    pb = pb + mu_ref[...] * (prev - pb)
    for blk in range(tm // TOK_TILE):
        rows = slice(blk * TOK_TILE, (blk + 1) * TOK_TILE)
        _prep_a(p[rows, :N_A_PAD], rows, blk, *a_in, *outs[:N_PREP_A_OUT])
        _prep_b(pb[rows], rows, blk, *b_in, *outs[N_PREP_A_OUT:])


def _front_call(x, mod4, g_mix, w_in_p, mu, a_consts, b_consts):
    bsz, s, d = x.shape
    n = w_in_p.shape[1]
    nb = n - N_A_PAD
    tm = FRONT_TILE
    tt = TOK_TILE
    full = lambda arr: pl.BlockSpec(arr.shape, lambda b, j, nd=arr.ndim: (0,) * nd)
    tok = lambda w: pl.BlockSpec((1, tm, w), lambda b, j: (b, j, 0))
    per_tile = lambda r, c: pl.BlockSpec((1, tm // tt, r, c), lambda b, j: (b, j, 0, 0))
    bf = lambda w: jax.ShapeDtypeStruct((bsz, s, w), BF16)
    ff = lambda w: jax.ShapeDtypeStruct((bsz, s, w), F32)
    nt = s // tt
    out_specs = [tok(A_HEADS * KV_LATENT), per_tile(KV_LATENT + ONES_ROWS, tt), tok(A_HEADS * KV_LATENT),
                 pl.BlockSpec((1, IDX_HEADS, tm, IDX_DIM), lambda b, j: (b, 0, j, 0)),
                 tok(IDX_DIM), per_tile(IDX_HEADS, tt), per_tile(SUBLANES, LANES), per_tile(SUBLANES, LANES)
                 ] + [tok(B_WIDTH)] * 7 + [per_tile(tt // CHUNK, B_WIDTH)]
    out_shape = [bf(A_HEADS * KV_LATENT),
                 jax.ShapeDtypeStruct((bsz, nt, KV_LATENT + ONES_ROWS, tt), BF16),
                 bf(A_HEADS * KV_LATENT),
                 jax.ShapeDtypeStruct((bsz, IDX_HEADS, s, IDX_DIM), BF16),
                 bf(IDX_DIM),
                 jax.ShapeDtypeStruct((bsz, nt, IDX_HEADS, tt), F32),
                 jax.ShapeDtypeStruct((bsz, nt, SUBLANES, LANES), F32),
                 jax.ShapeDtypeStruct((bsz, nt, SUBLANES, LANES), F32),
                 bf(B_WIDTH), bf(B_WIDTH), bf(B_WIDTH), bf(B_WIDTH), bf(B_WIDTH), ff(B_WIDTH), ff(B_WIDTH),
                 jax.ShapeDtypeStruct((bsz, nt, tt // CHUNK, B_WIDTH), F32)]
    assert len(a_consts) == N_PREP_A_IN and len(b_consts) == N_PREP_B_IN
    return pl.pallas_call(
        _front_kernel,
        grid=(bsz, s // tm),
        in_specs=[pl.BlockSpec((1, tm, d), lambda b, j: (b, j, 0)),
                  pl.BlockSpec((1, 1, 1, d), lambda b, j: (b, 0, 0, 0)),
                  pl.BlockSpec((1, 1, 1, d), lambda b, j: (b, 1, 0, 0)),
                  full(g_mix),
                  pl.BlockSpec(w_in_p.shape, lambda b, j: (0, 0), pipeline_mode=pl.Buffered(1)),
                  full(mu)] + [full(t) for t in a_consts] + [full(t) for t in b_consts],
        out_specs=out_specs,
        out_shape=out_shape,
        scratch_shapes=[pltpu.VMEM((1, nb), F32)],
        compiler_params=_params(("arbitrary", "arbitrary")),
    )(x, mod4, mod4, g_mix, w_in_p, mu, *a_consts, *b_consts)


def _colsum8(x):
    y = x.reshape(ADD_CHAINS, K_TILE // (ADD_CHAINS * SUBLANES), SUBLANES, Q_TILE)
    return jnp.sum(jnp.sum(y, axis=1), axis=0)


def _colmin8(x):
    y = x.reshape(ADD_CHAINS, K_TILE // (ADD_CHAINS * SUBLANES), SUBLANES, Q_TILE)
    return jnp.min(jnp.min(y, axis=1), axis=0)


def _colmax8(x):
    y = x.reshape(ADD_CHAINS, K_TILE // (ADD_CHAINS * SUBLANES), SUBLANES, Q_TILE)
    return jnp.max(jnp.max(y, axis=1), axis=0)


def _for_key_tiles(nkc, body, init):
    def quad(j, c):
        return body(4 * j + 3, body(4 * j + 2, body(4 * j + 1, body(4 * j, c))))
    c = lax.fori_loop(0, nkc // 4, quad, init)
    base = (nkc // 4) * 4
    c = lax.cond(nkc % 4 >= 2, lambda c: body(base + 1, body(base, c)), lambda c: c, c)
    return lax.cond(nkc % 2 == 1, lambda c: body(nkc - 1, c), lambda c: c, c)


def _dsa_kernel(topk, qabs_ref, qidx_ref, widx_ref, ckr_ref, cvt_ref, kidx_ref, wuv_ref, lstrict_ref, qn_ref, kn_ref,
                o_ref, score_ref, dist_ref, logit_ref, p_ref, m_ref, acc_ref):
    i = pl.program_id(1)
    nkc = i + 1
    t0 = i * Q_TILE
    krow = lax.broadcasted_iota(jnp.int32, (K_TILE, 1), 0)
    qcol = lax.broadcasted_iota(jnp.int32, (1, Q_TILE), 1)
    limit = ((t0 + qcol) // CHUNK + 1) * CHUNK
    kp = jnp.minimum(limit, topk).astype(F32)
    rel = (qcol - krow).astype(F32)

    def p1(kc, carry):
        rmin, rmax = carry
        k = kidx_ref[0, pl.ds(pl.multiple_of(kc * K_TILE, K_TILE), K_TILE), :]
        acc = jnp.zeros((K_TILE, Q_TILE), F32)
        for h in range(IDX_HEADS):
            s = _dot_nt(k, qidx_ref[0, h])
            acc = acc + widx_ref[0, 0, h:h + 1, :] * jnp.maximum(s, 0.0)
        adm = (kc * K_TILE + krow) < limit
        score_ref[kc] = jnp.where(adm, acc, -jnp.inf)
        rmin = jnp.minimum(rmin, _colmin8(jnp.where(adm, acc, jnp.inf)))
        rmax = jnp.maximum(rmax, _colmax8(jnp.where(adm, acc, -jnp.inf)))
        return rmin, rmax

    rmin, rmax = _for_key_tiles(
        nkc, p1, (jnp.full((SUBLANES, Q_TILE), jnp.inf, F32), jnp.full((SUBLANES, Q_TILE), -jnp.inf, F32)))
    lo = jnp.min(rmin, axis=0, keepdims=True)
    hi = jnp.max(rmax, axis=0, keepdims=True)

    def count(pred):
        def body(kc, acc):
            return acc + _colsum8(jnp.where(pred(score_ref[kc]), 1.0, 0.0))
        return jnp.sum(lax.fori_loop(0, nkc, body, jnp.zeros((SUBLANES, Q_TILE), F32)), axis=0, keepdims=True)

    def probe(c):
        lo, hi, cnt_lo = c
        mid = lo + 0.5 * (hi - lo)
        cnt = count(lambda sc: sc >= mid)
        ge = cnt >= kp
        return jnp.where(ge, mid, lo), jnp.where(ge, hi, mid), jnp.where(ge, cnt, cnt_lo)

    def smallest(pred):
        def body(kc, acc):
            sc = score_ref[kc]
            return jnp.minimum(acc, _colmin8(jnp.where(pred(sc), sc, jnp.inf)))
        return jnp.min(lax.fori_loop(0, nkc, body, jnp.full((SUBLANES, Q_TILE), jnp.inf, F32)), axis=0, keepdims=True)

    def any_true(x):
        return jnp.max(jnp.where(x, 1.0, 0.0)) > 0.0

    def search():
        lo_b, _, cnt_lo = lax.fori_loop(0, SEARCH_PROBES, lambda _, c: probe(c), (lo, hi, limit.astype(F32)))

        def step_up(c):
            it, thr, cnt_gt, cnt_ge = c
            up = cnt_gt >= kp
            thr = jnp.where(up, smallest(lambda sc: sc > thr), thr)
            return it + 1, thr, count(lambda sc: sc > thr), jnp.where(up, cnt_gt, cnt_ge)

        thr = smallest(lambda sc: sc >= lo_b)
        found = lax.while_loop(
            lambda c: jnp.logical_and(c[0] < nkc * K_TILE, any_true(c[2] >= kp)),
            step_up, (jnp.int32(0), thr, count(lambda sc: sc > thr), cnt_lo))
        return found[1], kp - found[2], found[3]

    thr, need, cnt_ge = lax.cond((i + 1) * Q_TILE <= topk, lambda: (lo, kp, kp), search)

    def dist_tile(kc):
        return jnp.abs(rel + (t0 - kc * K_TILE).astype(F32))

    big8 = jnp.full((SUBLANES, Q_TILE), DIST_BIG, F32)

    def sel_plain():
        def body(kc, near):
            d = jnp.where(score_ref[kc] >= thr, dist_tile(kc), DIST_BIG)
            dist_ref[kc] = d
            return jnp.minimum(near, _colmin8(d))
        return jnp.min(lax.fori_loop(0, nkc, body, big8), axis=0, keepdims=True)

    def sel_ties():
        def body(kc, c):
            run, near = c
            sc = score_ref[kc]
            eq = sc == thr
            eq_f = jnp.where(eq, 1.0, 0.0)
            pre = run + _dot(lstrict_ref[...], eq_f.astype(BF16))
            keep = (sc > thr) | (eq & (pre < need))
            d = jnp.where(keep, dist_tile(kc), DIST_BIG)
            dist_ref[kc] = d
            return run + jnp.sum(_colsum8(eq_f), axis=0, keepdims=True), jnp.minimum(near, _colmin8(d))
        _, near = _for_key_tiles(nkc, body, (jnp.zeros((1, Q_TILE), F32), big8))
        return jnp.min(near, axis=0, keepdims=True)

    near = lax.cond(any_true(cnt_ge != kp), sel_ties, sel_plain)

    acc_ref[...] = jnp.zeros(acc_ref.shape, F32)

    bound = jnp.sqrt(jnp.max(qn_ref[0, 0]) * jnp.max(kn_ref[0])) * BOUND_MARGIN

    def att_shifted():
        def body(kc, _):
            d = dist_ref[kc] - near
            for h in range(A_HEADS):
                slope = 2.0 ** (-8.0 * (h + 1) / A_HEADS) * LOG2E
                ck = ckr_ref[0, pl.ds(pl.multiple_of(kc * K_TILE, K_TILE), K_TILE),
                             h * KV_LATENT:(h + 1) * KV_LATENT]
                logit = _dot_nt(ck, qabs_ref[0, :, h * KV_LATENT:(h + 1) * KV_LATENT]) - slope * d
                p_ref[h] = jnp.exp2(logit).astype(BF16)
            cv = cvt_ref[0, kc]
            for h in range(A_HEADS):
                acc_ref[h] = acc_ref[h] + _dot(cv, p_ref[h])
            return 0
        _for_key_tiles(nkc, body, 0)

    def att_online():
        m_ref[...] = jnp.full(m_ref.shape, -jnp.inf, F32)
        _for_key_tiles(nkc, att, 0)

    def att(kc, _):
        dist = dist_ref[kc]
        m_new = []
        for h in range(A_HEADS):
            slope = 2.0 ** (-8.0 * (h + 1) / A_HEADS) * LOG2E
            ck = ckr_ref[0, pl.ds(pl.multiple_of(kc * K_TILE, K_TILE), K_TILE), h * KV_LATENT:(h + 1) * KV_LATENT]
            logit = _dot_nt(ck, qabs_ref[0, :, h * KV_LATENT:(h + 1) * KV_LATENT]) - slope * dist
            logit_ref[h] = logit
            m_new.append(jnp.maximum(m_ref[h], jnp.max(_colmax8(logit), axis=0, keepdims=True)))
        cv = cvt_ref[0, kc]
        for h in range(A_HEADS):
            p = jnp.exp2(logit_ref[h] - m_new[h])
            acc_ref[h] = acc_ref[h] * jnp.exp2(m_ref[h] - m_new[h]) + _dot(cv, p.astype(BF16))
            m_ref[h] = m_new[h]
        return 0

    lax.cond(bound <= EXP_RANGE, att_shifted, att_online)

    for pair in range(A_HEADS // 2):
        o_pair = []
        for hh in range(2):
            a = acc_ref[2 * pair + hh]
            o_t = a[:KV_LATENT] * (1.0 / a[KV_LATENT:KV_LATENT + 1])
            o_pair.append(o_t.T.astype(BF16))
        o_lat = jnp.concatenate(o_pair, axis=1)
        o_ref[0, :, pair * LANES:(pair + 1) * LANES] = _dot(o_lat, wuv_ref[pair]).astype(o_ref.dtype)


def _dsa_call(topk, qabs, qidx, widx, ckr, cvt, kidx, wuv_pair, lstrict, qn, kn):
    bsz, s, _ = qabs.shape
    nq = s // Q_TILE
    nk = s // K_TILE
    qt = lambda w: pl.BlockSpec((1, Q_TILE, w), lambda b, i: (b, i, 0))
    return pl.pallas_call(
        functools.partial(_dsa_kernel, topk),
        grid=(bsz, nq),
        in_specs=[qt(A_HEADS * KV_LATENT),
                  pl.BlockSpec((1, IDX_HEADS, Q_TILE, IDX_DIM), lambda b, i: (b, 0, i, 0)),
                  pl.BlockSpec((1, 1, IDX_HEADS, Q_TILE), lambda b, i: (b, i, 0, 0)),
                  pl.BlockSpec((1, s, A_HEADS * KV_LATENT), lambda b, i: (b, 0, 0)),
                  pl.BlockSpec((1, nk, KV_LATENT + ONES_ROWS, K_TILE), lambda b, i: (b, 0, 0, 0)),
                  pl.BlockSpec((1, s, IDX_DIM), lambda b, i: (b, 0, 0)),
                  pl.BlockSpec((A_HEADS // 2, 2 * KV_LATENT, LANES), lambda b, i: (0, 0, 0)),
                  pl.BlockSpec((K_TILE, K_TILE), lambda b, i: (0, 0)),
                  pl.BlockSpec((1, 1, SUBLANES, LANES), lambda b, i: (b, i, 0, 0)),
                  pl.BlockSpec((1, nk, SUBLANES, LANES), lambda b, i: (b, 0, 0, 0))],
        out_specs=qt(A_WIDTH),
        out_shape=jax.ShapeDtypeStruct((bsz, s, A_WIDTH), BF16),
        scratch_shapes=[pltpu.VMEM((nk, K_TILE, Q_TILE), F32),
                        pltpu.VMEM((nk, K_TILE, Q_TILE), F32),
                        pltpu.VMEM((A_HEADS, K_TILE, Q_TILE), F32),
                        pltpu.VMEM((A_HEADS, K_TILE, Q_TILE), BF16),
                        pltpu.VMEM((A_HEADS, 1, Q_TILE), F32),
                        pltpu.VMEM((A_HEADS, KV_LATENT + ONES_ROWS, Q_TILE), F32)],
        compiler_params=_params(("arbitrary", "arbitrary")),
    )(qabs, qidx, widx, ckr, cvt, kidx, wuv_pair, lstrict, qn, kn)


def _rwkv_block(blk, tm, masks, rt_ref, kt_ref, bt_ref, kl_ref, v_ref, pc_ref):
    strict, incl, eye_s, lane_chunk, row_chunk, blk_diag, diag, head0 = masks
    nch = tm // CHUNK
    npair = B_HEADS // 2
    rows_b = slice(blk * tm, (blk + 1) * tm)
    zero_b = jnp.zeros((), BF16)

    def block_diag(m):
        return jnp.concatenate([jnp.where(lane_chunk == c, m, jnp.zeros((), m.dtype)) for c in range(nch)], axis=0)

    heads = [(p, hh) for p in range(npair) for hh in range(2)]
    rt, kt, bt, kl, v, v_bd = [], [], [], [], [], []
    a_ab, a_ak, m_rb, m_rk = [], [], [], []
    for p in range(npair):
        sl = slice(p * LANES, (p + 1) * LANES)
        rt.append(rt_ref[0, rows_b, sl])
        kt.append(kt_ref[0, rows_b, sl])
        bt.append(bt_ref[0, rows_b, sl])
        kl.append(kl_ref[0, rows_b, sl])
        v.append(v_ref[0, rows_b, sl])
        tc = 2 * CHUNK
        folded = []
        for g in range(nch // 2):
            r2 = slice(g * tc, (g + 1) * tc)
            lhs = jnp.concatenate([jnp.where(head0, kt[p][r2], zero_b), jnp.where(head0, zero_b, kt[p][r2]),
                                   jnp.where(head0, rt[p][r2], zero_b), jnp.where(head0, zero_b, rt[p][r2])], axis=0)
            prod = _dot_nt(lhs, jnp.concatenate([bt[p][r2], kl[p][r2]], axis=0))
            fold = lambda m: m[:CHUNK] + m[CHUNK:]
            folded.append([fold(jnp.where(strict if j < 2 else incl, prod[j * tc:(j + 1) * tc], 0.0))
                           for j in range(4)])
        gather = lambda j, half: jnp.concatenate(
            [f[j][:, half * tc:(half + 1) * tc] for f in folded], axis=1)
        v_bd.append(jnp.concatenate([jnp.where(row_chunk == c, v[p], zero_b) for c in range(nch)], axis=1))
        for hh in range(2):
            a_ab.append(gather(hh, 0))
            a_ak.append(gather(hh, 1).astype(BF16))
            m_rb.append(block_diag(gather(2 + hh, 0).astype(BF16)))
            m_rk.append(gather(2 + hh, 1).astype(BF16))

    t_inv = [(eye_s - a).astype(BF16) for a in a_ab]
    a_pow = [a.astype(BF16) for a in a_ab]
    for _ in range(5):
        a_sq = [_dot(a, block_diag(a)) for a in a_pow]
        a_pow = [a.astype(BF16) for a in a_sq]
        t_inv = [_dot(t, block_diag((eye_s + a).astype(BF16))).astype(BF16) for t, a in zip(t_inv, a_sq)]
    t_inv = [block_diag(t) for t in t_inv]

    avm = [_dot(jnp.concatenate([a_ak[i], m_rk[i]], axis=0), v_bd[p]) for i, (p, _) in enumerate(heads)]
    stack = lambda m: jnp.concatenate([m[:, c * LANES:(c + 1) * LANES] for c in range(nch)], axis=0)
    av = [stack(m[:CHUNK]) for m in avm]
    mv = [stack(m[CHUNK:]) for m in avm]
    x = [_dot(t_inv[i], jnp.concatenate([kt[p], av[i].astype(BF16)], axis=1))
         for i, (p, _) in enumerate(heads)]
    y = [_dot(m_rb[i], x[i].astype(BF16)) for i in range(len(heads))]

    head0_2 = jnp.concatenate([head0, head0], axis=1)
    zeros_b = jnp.zeros((CHUNK, LANES), BF16)
    q_b, ol, g_mat, f_mat = [], [], [], []
    for p in range(npair):
        i0, i1 = 2 * p, 2 * p + 1
        sl = slice(p * LANES, (p + 1) * LANES)
        wu_b = (-jnp.where(head0_2, x[i0], x[i1])).astype(BF16)
        yy = jnp.where(head0_2, y[i0], y[i1])
        q_b.append((rt[p].astype(F32) - yy[:, :LANES]).astype(BF16))
        ol.append(jnp.where(head0, mv[i0], mv[i1]) - yy[:, LANES:])
        gp, fp = [], []
        for c in range(nch):
            rows = slice(c * CHUNK, (c + 1) * CHUNK)
            pc = pc_ref[0, blk, c:c + 1, sl]
            bh = (bt[p][rows].astype(F32) * pc).astype(BF16)
            kh = (kl[p][rows].astype(F32) * pc).astype(BF16)
            rhs = jnp.concatenate([wu_b[rows], jnp.concatenate([zeros_b, v[p][rows]], axis=1)], axis=0)
            bw = _dot_tn(jnp.concatenate([bh, kh], axis=0), rhs)
            gp.append((jnp.where(diag, pc, 0.0) + jnp.where(blk_diag, bw[:, :LANES], 0.0)).astype(BF16))
            fp.append(jnp.where(blk_diag, bw[:, LANES:], 0.0))
        g_mat.append(gp)
        f_mat.append(fp)
    return q_b, ol, g_mat, f_mat


def _rwkv_kernel(rt_ref, kt_ref, bt_ref, kl_ref, v_ref, g_ref, bv_ref, pc_ref, lnw_ref, lnb_ref, eb_ref,
                 o_ref, h_ref):
    j = pl.program_id(1)

    @pl.when(j == 0)
    def _():
        h_ref[...] = jnp.zeros_like(h_ref)

    tm = TOK_TILE
    nch = tm // CHUNK
    npair = B_HEADS // 2
    ri = lax.broadcasted_iota(jnp.int32, (2 * CHUNK, 4 * CHUNK), 0)
    ci = lax.broadcasted_iota(jnp.int32, (2 * CHUNK, 4 * CHUNK), 1) % (2 * CHUNK)
    same = (ri // CHUNK) == (ci // CHUNK)
    r2 = lax.broadcasted_iota(jnp.int32, (LANES, LANES), 0)
    c2 = lax.broadcasted_iota(jnp.int32, (LANES, LANES), 1)
    lane = lax.broadcasted_iota(jnp.int32, (1, LANES), 1)
    rs = lax.broadcasted_iota(jnp.int32, (CHUNK, tm), 0)
    cs = lax.broadcasted_iota(jnp.int32, (CHUNK, tm), 1)
    masks = (same & (ri > ci), same & (ri >= ci), jnp.where(cs % CHUNK == rs, 1.0, 0.0),
             lax.broadcasted_iota(jnp.int32, (1, tm), 1) // CHUNK, lax.broadcasted_iota(jnp.int32, (tm, 1), 0) // CHUNK,
             (r2 // B_HEAD_DIM) == (c2 // B_HEAD_DIM), r2 == c2, (lane // B_HEAD_DIM) == 0)

    nblk = rt_ref.shape[1] // tm
    blocks = [_rwkv_block(b, tm, masks, rt_ref, kt_ref, bt_ref, kl_ref, v_ref, pc_ref) for b in range(nblk)]

    h = [h_ref[p] for p in range(npair)]
    for b, (q_b, ol, g_mat, f_mat) in enumerate(blocks):
        o_chunks = [[] for _ in range(npair)]
        for c in range(nch):
            rows = slice(c * CHUNK, (c + 1) * CHUNK)
            for p in range(npair):
                h_b = h[p].astype(BF16)
                o_chunks[p].append(_dot(q_b[p][rows], h_b) + ol[p][rows])
                h[p] = _dot(g_mat[p][c], h_b) + f_mat[p][c]
        out = jnp.concatenate([jnp.concatenate(oc, axis=0) for oc in o_chunks], axis=1)

        rows_b = slice(b * tm, (b + 1) * tm)
        eb = eb_ref[...]
        mean = _seg_dot_hl(out, eb) * (1.0 / B_HEAD_DIM)
        d = out - mean
        var = _seg_dot(d * d, eb) * (1.0 / B_HEAD_DIM)
        y = d * lax.rsqrt(var + GN_EPS) * lnw_ref[...] + lnb_ref[...] + bv_ref[0, rows_b]
        o_ref[0, rows_b] = (y * g_ref[0, rows_b]).astype(o_ref.dtype)
    for p in range(npair):
        h_ref[p] = h[p]


def _rwkv_call(rt, kt, bt, kl, v, g, bv, pc, ln_w, ln_b, eb):
    bsz, s, _ = rt.shape
    tm = RWKV_TILE
    tok = pl.BlockSpec((1, tm, B_WIDTH), lambda b, j: (b, j, 0))
    row = pl.BlockSpec((1, B_WIDTH), lambda b, j: (0, 0))
    return pl.pallas_call(
        _rwkv_kernel,
        grid=(bsz, s // tm),
        in_specs=[tok] * 7 + [pl.BlockSpec((1, tm // TOK_TILE, TOK_TILE // CHUNK, B_WIDTH), lambda b, j: (b, j, 0, 0)),
                              row, row, pl.BlockSpec((SEG_K, SEG_K), lambda b, j: (0, 0))],
        out_specs=tok,
        out_shape=jax.ShapeDtypeStruct((bsz, s, B_WIDTH), BF16),
        scratch_shapes=[pltpu.VMEM((B_HEADS // 2, LANES, LANES), F32)],
        compiler_params=_params(("arbitrary", "arbitrary")),
    )(rt, kt, bt, kl, v, g, bv, pc, ln_w, ln_b, eb)


def _ffn_kernel(x_ref, oa_ref, ob_ref, gt1_ref, sh2_ref, sc2_ref, gt2_ref, gf_ref, woa_ref, wob_ref,
                w1_ref, w2_ref, o_ref):
    for r0 in range(0, x_ref.shape[1], FFN_ROWS):
        rows = slice(r0, r0 + FFN_ROWS)
        mix = _dot(oa_ref[0, rows], woa_ref[...]) + _dot(ob_ref[0, rows], wob_ref[...])
        x1 = x_ref[0, rows] + gt1_ref[0, 0] * mix
        y = x1 * lax.rsqrt(jnp.mean(x1 * x1, axis=-1, keepdims=True) + RMS_EPS) * gf_ref[...]
        h2 = (y * (1.0 + sc2_ref[0, 0]) + sh2_ref[0, 0]).astype(BF16)
        u = jnp.maximum(_dot(h2, w1_ref[...]), 0.0)
        o_ref[0, rows] = x1 + gt2_ref[0, 0] * _dot((u * u).astype(BF16), w2_ref[...])


def _ffn_call(x, oa, ob, mod4, g_ffn, w_out_a, w_out_b, w1, w2):
    bsz, s, d = x.shape
    dff = w1.shape[1]
    tm = FFN_TILE
    tok = lambda w: pl.BlockSpec((1, tm, w), lambda b, j: (b, j, 0))
    modk = lambda k: pl.BlockSpec((1, 1, 1, d), lambda b, j, k=k: (b, k, 0, 0))
    res = lambda shape: pl.BlockSpec(shape, lambda b, j: (0, 0), pipeline_mode=pl.Buffered(1))
    return pl.pallas_call(
        _ffn_kernel,
        grid=(bsz, s // tm),
        in_specs=[tok(d), tok(A_WIDTH), tok(B_WIDTH), modk(2), modk(3), modk(4), modk(5),
                  pl.BlockSpec((1, d), lambda b, j: (0, 0)),
                  res((A_WIDTH, d)), res((B_WIDTH, d)), res((d, dff)), res((dff, d))],
        out_specs=tok(d),
        out_shape=jax.ShapeDtypeStruct((bsz, s, d), F32),
        compiler_params=_params(("arbitrary", "arbitrary")),
    )(x, oa, ob, mod4, mod4, mod4, mod4, g_ffn, w_out_a, w_out_b, w1, w2)


def _block_ones(n, blk, dtype=BF16):
    i = jnp.arange(n)
    return ((i[:, None] // blk) == (i[None, :] // blk)).astype(dtype)


def kernel(x, c, w_ada, b_ada, g_mix, g_ffn, w_in, g_q, g_k, g_kv, w_uk, w_uv, mu_shift, w0, w2, a0, a2, g2,
           k_k, k_a, r_k, ln_w, ln_b, w_out, w_ff1, w_ff2):
    bsz, s, d = x.shape
    depth = w_ada.shape[0]
    assert s % Q_TILE == 0 and s % FRONT_TILE == 0 and s % RWKV_TILE == 0 and s % FFN_TILE == 0
    assert Q_TILE == TOK_TILE and K_TILE == TOK_TILE
    topk = min(TOPK_MAX, s // 4)

    eb = _block_ones(SEG_K, B_HEAD_DIM)
    ex = (jnp.arange(2 * A_HEAD_DIM)[:, None] // A_HEAD_DIM == jnp.arange(2 * KV_LATENT)[None, :] // KV_LATENT
          ).astype(BF16)
    sel = (jnp.arange(LANES)[None, :] == IDX_DIM + jnp.arange(IDX_HEADS)[:, None]).astype(BF16)
    eye_l = jnp.eye(KV_LATENT, dtype=BF16)
    ti = jnp.arange(TOK_TILE)
    tri = (((ti[:, None] // CHUNK) == (ti[None, :] // CHUNK)) & (ti[:, None] >= ti[None, :])).astype(BF16)
    ki = jnp.arange(K_TILE)
    lstrict = (ki[None, :] < ki[:, None]).astype(BF16)

    for l in range(depth):
        w_a = jnp.pad(w_in[l][:, :N_IN_A], ((0, 0), (0, N_A_PAD - N_IN_A)))
        w_in_p = jnp.concatenate([w_a, w_in[l][:, N_IN_A:]], axis=1).astype(BF16)
        wuk_flat = w_uk[l].reshape(KV_LATENT, A_WIDTH).astype(BF16)
        wuk_t = jnp.transpose(w_uk[l], (1, 2, 0)).reshape(A_HEADS // 2, 2, A_HEAD_DIM, KV_LATENT)
        wuk_bd = (jnp.eye(2, dtype=F32)[None, :, None, :, None] * wuk_t[:, :, :, None, :]).reshape(
            A_HEADS // 2, 2 * A_HEAD_DIM, 2 * KV_LATENT).astype(BF16)
        wuv_t = jnp.transpose(w_uv[l], (1, 0, 2)).reshape(A_HEADS // 2, 2, KV_LATENT, A_HEAD_DIM)
        wuv_pair = (jnp.eye(2, dtype=F32)[None, :, None, :, None] * wuv_t[:, :, :, None, :]).reshape(
            A_HEADS // 2, 2 * KV_LATENT, 2 * A_HEAD_DIM).astype(BF16)
        gqk = jnp.tile(g_q[l] * g_k[l], A_HEADS).reshape(1, A_WIDTH)
        r1 = lambda t: t.reshape(1, -1)

        mod = _mod_call(c, w_ada[l], b_ada[l])
        mod4 = mod.reshape(bsz, 6, 1, d)
        a_consts = (r1(g_kv[l]), gqk, wuk_flat, wuk_bd, eb, ex, sel, eye_l)
        b_consts = (r1(w0[l]), w2[l].astype(BF16), r1(a0[l]), a2[l].astype(BF16), g2[l].astype(BF16),
                    r1(k_k[l]), r1(k_a[l]), r1(r_k[l]), eb, tri)
        ckr, cvt, qabs, qidx, kidx, widx, qn, kn, rt, kt, bt, kl, v, g, bv, pc = _front_call(
            x, mod4, r1(g_mix[l]), w_in_p, r1(mu_shift[l]), a_consts, b_consts)
        o_a = _dsa_call(topk, qabs, qidx, widx, ckr, cvt, kidx, wuv_pair, lstrict, qn, kn)
        o_b = _rwkv_call(rt, kt, bt, kl, v, g, bv, pc, r1(ln_w[l]), r1(ln_b[l]), eb)
        x = _ffn_call(x, o_a, o_b, mod4, r1(g_ffn[l]), w_out[l][:A_WIDTH].astype(BF16),
                      w_out[l][A_WIDTH:].astype(BF16), w_ff1[l].astype(BF16), w_ff2[l].astype(BF16))
    return x
```

```python
import functools

import jax
import jax.numpy as jnp
from jax import lax
from jax.experimental import pallas as pl
from jax.experimental.pallas import tpu as pltpu

F32 = jnp.float32
BF16 = jnp.bfloat16

CHUNK = 64
A_HEADS = 8
A_HEAD_DIM = 64
A_WIDTH = A_HEADS * A_HEAD_DIM
KV_LATENT = 128
IDX_HEADS = 8
IDX_DIM = 64
TOPK_MAX = 256
B_HEADS = 8
B_HEAD_DIM = 64
B_WIDTH = B_HEADS * B_HEAD_DIM
W_LORA = 64
A_LORA = 64
G_LORA = 128
RMS_EPS = 1e-6
GN_EPS = 64e-5
N_IN_A = A_WIDTH + KV_LATENT + IDX_HEADS * IDX_DIM + IDX_DIM + IDX_HEADS
N_IN_B = 3 * B_WIDTH + W_LORA + A_LORA + G_LORA
N_A_PAD = 1280

LANES = 128
SUBLANES = 8
ADD_CHAINS = 4
MOD_COLS = 1024
SEG_K = 256
TOK_TILE = 256
FRONT_TILE = 512
RWKV_TILE = 512
Q_TILE = 256
FFN_TILE = 1024
FFN_ROWS = 512
K_TILE = 256
DIST_BIG = 1e30
ONES_ROWS = 16
LOG2E = 1.4426950408889634
EXP_NEG_HALF = 0.6065306597126334
EXP_RANGE = 90.0
BOUND_MARGIN = 1.02
SEARCH_PROBES = 14
VMEM_LIMIT = 56 * 1024 * 1024


def _dot(a, b):
    return jnp.dot(a, b, preferred_element_type=F32)


def _dot_nt(a, b):
    return lax.dot_general(a, b, (((1,), (1,)), ((), ())), preferred_element_type=F32)


def _dot_tn(a, b):
    return lax.dot_general(a, b, (((0,), (0,)), ((), ())), preferred_element_type=F32)


def _split(x):
    hi = x.astype(BF16)
    lo = (x - hi.astype(F32)).astype(BF16)
    return hi, lo


def _dot_hl(x, e):
    hi, lo = _split(x)
    return _dot(hi, e) + _dot(lo, e)


def _seg_dot_hl(x, e):
    k = e.shape[0]
    return jnp.concatenate([_dot_hl(x[:, j:j + k], e) for j in range(0, x.shape[1], k)], axis=1)


def _seg_dot(x, e):
    k = e.shape[0]
    xb = x.astype(BF16)
    return jnp.concatenate([_dot(xb[:, j:j + k], e) for j in range(0, x.shape[1], k)], axis=1)


def _params(sem):
    return pltpu.CompilerParams(dimension_semantics=sem, vmem_limit_bytes=VMEM_LIMIT)


def _mod_kernel(c_ref, w_ref, b_ref, o_ref):
    c = c_ref[...]
    s = c * jax.nn.sigmoid(c)
    s_hi, s_lo = _split(s)
    w_hi, w_lo = _split(w_ref[...])
    o_ref[...] = _dot(s_hi, w_hi) + _dot(s_hi, w_lo) + _dot(s_lo, w_hi) + b_ref[...]


def _mod_call(c, w_ada, b_ada):
    bsz, d = c.shape
    n = w_ada.shape[1]
    tn = MOD_COLS
    return pl.pallas_call(
        _mod_kernel,
        grid=(n // tn,),
        in_specs=[pl.BlockSpec((bsz, d), lambda j: (0, 0)),
                  pl.BlockSpec((d, tn), lambda j: (0, j)),
                  pl.BlockSpec((1, tn), lambda j: (0, j))],
        out_specs=pl.BlockSpec((bsz, tn), lambda j: (0, j)),
        out_shape=jax.ShapeDtypeStruct((bsz, n), F32),
        compiler_params=_params(("arbitrary",)),
    )(c, w_ada, b_ada.reshape(1, n))


def _head_norm2_max(x):
    best = None
    for h in range(x.shape[1] // KV_LATENT):
        xh = x[:, h * KV_LATENT:(h + 1) * KV_LATENT]
        n2 = jnp.max(jnp.sum(xh * xh, axis=-1, keepdims=True), axis=0, keepdims=True)
        best = n2 if best is None else jnp.maximum(best, n2)
    return best


def _prep_a(pa, rows, blk, gkv_ref, gqk_ref, wuk_ref, wukbd_ref, eb_ref, ex_ref, sel_ref, eye_ref,
            ckr_ref, cvt_ref, qabs_ref, qidx_ref, kidx_ref, widx_ref, qn_ref, kn_ref):
    tm = pa.shape[0]
    q = pa[:, :A_WIDTH]
    cl = pa[:, A_WIDTH:A_WIDTH + KV_LATENT]
    o_qi = A_WIDTH + KV_LATENT
    qi = pa[:, o_qi:o_qi + IDX_HEADS * IDX_DIM]
    o_kw = o_qi + IDX_HEADS * IDX_DIM
    kw = pa[:, o_kw:o_kw + LANES]

    ckv = cl * lax.rsqrt(jnp.mean(cl * cl, axis=-1, keepdims=True) + RMS_EPS) * gkv_ref[...]
    ckv_b = ckv.astype(BF16)
    cvt_ref[0, blk, :KV_LATENT, :] = _dot_nt(eye_ref[...], ckv_b).astype(BF16)
    cvt_ref[0, blk, KV_LATENT:, :] = jnp.ones((ONES_ROWS, tm), BF16)
    kf = _dot(ckv_b, wuk_ref[...])
    ss = _seg_dot(kf * kf, ex_ref[...])
    inv_rms = lax.rsqrt(ss * (1.0 / A_HEAD_DIM) + RMS_EPS)
    ckr_ref[0, rows] = (jnp.concatenate([ckv] * A_HEADS, axis=1) * inv_rms).astype(BF16)
    inv_max = inv_rms[:, :KV_LATENT]
    for h in range(1, A_HEADS):
        inv_max = jnp.maximum(inv_max, inv_rms[:, h * KV_LATENT:(h + 1) * KV_LATENT])
    kn2 = jnp.sum(ckv * ckv, axis=-1, keepdims=True) * (inv_max * inv_max)
    kn_ref[0, blk] = jnp.broadcast_to(jnp.max(jnp.max(kn2, axis=1, keepdims=True), axis=0, keepdims=True),
                                      kn_ref.shape[2:])

    ssq = _seg_dot(q * q, eb_ref[...])
    qh = q * lax.rsqrt(ssq * (1.0 / A_HEAD_DIM) + RMS_EPS) * gqk_ref[...]
    qh_b = qh.astype(BF16)
    qn = None
    for j in range(A_HEADS // 2):
        qabs = _dot(qh_b[:, j * LANES:(j + 1) * LANES], wukbd_ref[j]) * (A_HEAD_DIM ** -0.5 * LOG2E)
        qabs_ref[0, rows, 2 * j * KV_LATENT:2 * (j + 1) * KV_LATENT] = qabs.astype(BF16)
        qn = _head_norm2_max(qabs) if qn is None else jnp.maximum(qn, _head_norm2_max(qabs))
    qn_ref[0, blk] = jnp.broadcast_to(qn, qn_ref.shape[2:])
    for h in range(IDX_HEADS):
        qidx_ref[0, h, rows] = qi[:, h * IDX_DIM:(h + 1) * IDX_DIM].astype(BF16)
    kidx_ref[0, rows] = kw[:, :IDX_DIM].astype(BF16)
    kw_hi, kw_lo = _split(kw)
    w_t = _dot_nt(sel_ref[...], kw_hi) + _dot_nt(sel_ref[...], kw_lo)
    widx_ref[0, blk] = w_t * (IDX_HEADS ** -0.5 * IDX_DIM ** -0.5)


def _prep_b(pb, rows, blk, w0_ref, w2_ref, a0_ref, a2_ref, g2_ref, kk_ref, ka_ref, rk_ref, eb_ref, tri_ref,
            rt_ref, kt_ref, bt_ref, kl_ref, v_ref, g_ref, bv_ref, pc_ref):
    r = pb[:, :B_WIDTH]
    k = pb[:, B_WIDTH:2 * B_WIDTH]
    v = pb[:, 2 * B_WIDTH:3 * B_WIDTH]
    o = 3 * B_WIDTH
    xw = pb[:, o:o + W_LORA]
    xa = pb[:, o + W_LORA:o + W_LORA + A_LORA]
    xg = pb[:, o + W_LORA + A_LORA:o + W_LORA + A_LORA + G_LORA]

    z = w0_ref[...] + _dot(jnp.tanh(xw).astype(BF16), w2_ref[...])
    lw = -EXP_NEG_HALF * jax.nn.sigmoid(z)
    a = jax.nn.sigmoid(a0_ref[...] + _dot(xa.astype(BF16), a2_ref[...]))
    g = _dot(jax.nn.sigmoid(xg).astype(BF16), g2_ref[...])
    kk = k * kk_ref[...]
    kkn = kk * lax.rsqrt(jnp.maximum(_seg_dot(kk * kk, eb_ref[...]), 1e-24))
    kp = k * (1.0 + (a - 1.0) * ka_ref[...])
    bonus = _seg_dot(r * kp * rk_ref[...], eb_ref[...])

    lw_hi, lw_lo = _split(lw)
    cum = _dot(tri_ref[...], lw_hi) + _dot(tri_ref[...], lw_lo)
    e_pos = jnp.exp(cum)
    e_neg = jnp.exp(-cum)
    rt_ref[0, rows] = (r * e_pos).astype(BF16)
    kt_ref[0, rows] = (kkn * jnp.exp(cum - lw)).astype(BF16)
    bt_ref[0, rows] = (kkn * a * e_neg).astype(BF16)
    kl_ref[0, rows] = (kp * e_neg).astype(BF16)
    v_ref[0, rows] = v.astype(BF16)
    g_ref[0, rows] = g
    bv_ref[0, rows] = bonus * v
    for c in range(pb.shape[0] // CHUNK):
        pc_ref[0, blk, c:c + 1, :] = e_pos[(c + 1) * CHUNK - 1:(c + 1) * CHUNK, :]


N_FRONT_IN = 6
N_PREP_A_IN = 8
N_PREP_B_IN = 10
N_PREP_A_OUT = 8


def _front_kernel(*refs):
    x_ref, sh_ref, sc_ref, g_ref, w_ref, mu_ref = refs[:N_FRONT_IN]
    a_in = refs[N_FRONT_IN:N_FRONT_IN + N_PREP_A_IN]
    b_in = refs[N_FRONT_IN + N_PREP_A_IN:N_FRONT_IN + N_PREP_A_IN + N_PREP_B_IN]
    outs = refs[N_FRONT_IN + N_PREP_A_IN + N_PREP_B_IN:-1]
    carry_ref = refs[-1]
    j = pl.program_id(1)

    @pl.when(j == 0)
    def _():
        carry_ref[...] = jnp.zeros_like(carry_ref)

    x = x_ref[0]
    y = x * lax.rsqrt(jnp.mean(x * x, axis=-1, keepdims=True) + RMS_EPS) * g_ref[...]
    h = y * (1.0 + sc_ref[0, 0]) + sh_ref[0, 0]
    p = _dot(h.astype(BF16), w_ref[...])
    pb = p[:, N_A_PAD:]
    tm = pb.shape[0]
    row = lax.broadcasted_iota(jnp.int32, (tm, 1), 0)
    prev = jnp.where(row == 0, carry_ref[...], pltpu.roll(pb, 1, axis=0))
    carry_ref[...] = pb[tm - 1:tm, :]
    pb = pb + mu_ref[...] * (prev - pb)
    for blk in range(tm // TOK_TILE):
        rows = slice(blk * TOK_TILE, (blk + 1) * TOK_TILE)
        _prep_a(p[rows, :N_A_PAD], rows, blk, *a_in, *outs[:N_PREP_A_OUT])
        _prep_b(pb[rows], rows, blk, *b_in, *outs[N_PREP_A_OUT:])


def _front_call(x, mod4, g_mix, w_in_p, mu, a_consts, b_consts):
    bsz, s, d = x.shape
    n = w_in_p.shape[1]
    nb = n - N_A_PAD
    tm = FRONT_TILE
    tt = TOK_TILE
    full = lambda arr: pl.BlockSpec(arr.shape, lambda b, j, nd=arr.ndim: (0,) * nd)
    tok = lambda w: pl.BlockSpec((1, tm, w), lambda b, j: (b, j, 0))
    per_tile = lambda r, c: pl.BlockSpec((1, tm // tt, r, c), lambda b, j: (b, j, 0, 0))
    bf = lambda w: jax.ShapeDtypeStruct((bsz, s, w), BF16)
    ff = lambda w: jax.ShapeDtypeStruct((bsz, s, w), F32)
    nt = s // tt
    out_specs = [tok(A_HEADS * KV_LATENT), per_tile(KV_LATENT + ONES_ROWS, tt), tok(A_HEADS * KV_LATENT),
                 pl.BlockSpec((1, IDX_HEADS, tm, IDX_DIM), lambda b, j: (b, 0, j, 0)),
                 tok(IDX_DIM), per_tile(IDX_HEADS, tt), per_tile(SUBLANES, LANES), per_tile(SUBLANES, LANES)
                 ] + [tok(B_WIDTH)] * 7 + [per_tile(tt // CHUNK, B_WIDTH)]
    out_shape = [bf(A_HEADS * KV_LATENT),
                 jax.ShapeDtypeStruct((bsz, nt, KV_LATENT + ONES_ROWS, tt), BF16),
                 bf(A_HEADS * KV_LATENT),
                 jax.ShapeDtypeStruct((bsz, IDX_HEADS, s, IDX_DIM), BF16),
                 bf(IDX_DIM),
                 jax.ShapeDtypeStruct((bsz, nt, IDX_HEADS, tt), F32),
                 jax.ShapeDtypeStruct((bsz, nt, SUBLANES, LANES), F32),
                 jax.ShapeDtypeStruct((bsz, nt, SUBLANES, LANES), F32),
                 bf(B_WIDTH), bf(B_WIDTH), bf(B_WIDTH), bf(B_WIDTH), bf(B_WIDTH), ff(B_WIDTH), ff(B_WIDTH),
                 jax.ShapeDtypeStruct((bsz, nt, tt // CHUNK, B_WIDTH), F32)]
    assert len(a_consts) == N_PREP_A_IN and len(b_consts) == N_PREP_B_IN
    return pl.pallas_call(
        _front_kernel,
        grid=(bsz, s // tm),
        in_specs=[pl.BlockSpec((1, tm, d), lambda b, j: (b, j, 0)),
                  pl.BlockSpec((1, 1, 1, d), lambda b, j: (b, 0, 0, 0)),
                  pl.BlockSpec((1, 1, 1, d), lambda b, j: (b, 1, 0, 0)),
                  full(g_mix),
                  pl.BlockSpec(w_in_p.shape, lambda b, j: (0, 0), pipeline_mode=pl.Buffered(1)),
                  full(mu)] + [full(t) for t in a_consts] + [full(t) for t in b_consts],
        out_specs=out_specs,
        out_shape=out_shape,
        scratch_shapes=[pltpu.VMEM((1, nb), F32)],
        compiler_params=_params(("arbitrary", "arbitrary")),
    )(x, mod4, mod4, g_mix, w_in_p, mu, *a_consts, *b_consts)


def _colsum8(x):
    y = x.reshape(ADD_CHAINS, K_TILE // (ADD_CHAINS * SUBLANES), SUBLANES, Q_TILE)
    return jnp.sum(jnp.sum(y, axis=1), axis=0)


def _colmin8(x):
    y = x.reshape(ADD_CHAINS, K_TILE // (ADD_CHAINS * SUBLANES), SUBLANES, Q_TILE)
    return jnp.min(jnp.min(y, axis=1), axis=0)


def _colmax8(x):
    y = x.reshape(ADD_CHAINS, K_TILE // (ADD_CHAINS * SUBLANES), SUBLANES, Q_TILE)
    return jnp.max(jnp.max(y, axis=1), axis=0)


def _for_key_tiles(nkc, body, init):
    def quad(j, c):
        return body(4 * j + 3, body(4 * j + 2, body(4 * j + 1, body(4 * j, c))))
    c = lax.fori_loop(0, nkc // 4, quad, init)
    base = (nkc // 4) * 4
    c = lax.cond(nkc % 4 >= 2, lambda c: body(base + 1, body(base, c)), lambda c: c, c)
    return lax.cond(nkc % 2 == 1, lambda c: body(nkc - 1, c), lambda c: c, c)


def _dsa_kernel(topk, qabs_ref, qidx_ref, widx_ref, ckr_ref, cvt_ref, kidx_ref, wuv_ref, lstrict_ref, qn_ref, kn_ref,
                o_ref, score_ref, dist_ref, logit_ref, p_ref, m_ref, acc_ref):
    i = pl.program_id(1)
    nkc = i + 1
    t0 = i * Q_TILE
    krow = lax.broadcasted_iota(jnp.int32, (K_TILE, 1), 0)
    qcol = lax.broadcasted_iota(jnp.int32, (1, Q_TILE), 1)
    limit = ((t0 + qcol) // CHUNK + 1) * CHUNK
    kp = jnp.minimum(limit, topk).astype(F32)
    rel = (qcol - krow).astype(F32)

    def p1(kc, carry):
        rmin, rmax = carry
        k = kidx_ref[0, pl.ds(pl.multiple_of(kc * K_TILE, K_TILE), K_TILE), :]
        acc = jnp.zeros((K_TILE, Q_TILE), F32)
        for h in range(IDX_HEADS):
            s = _dot_nt(k, qidx_ref[0, h])
            acc = acc + widx_ref[0, 0, h:h + 1, :] * jnp.maximum(s, 0.0)
        adm = (kc * K_TILE + krow) < limit
        score_ref[kc] = jnp.where(adm, acc, -jnp.inf)
        rmin = jnp.minimum(rmin, _colmin8(jnp.where(adm, acc, jnp.inf)))
        rmax = jnp.maximum(rmax, _colmax8(jnp.where(adm, acc, -jnp.inf)))
        return rmin, rmax

    rmin, rmax = _for_key_tiles(
        nkc, p1, (jnp.full((SUBLANES, Q_TILE), jnp.inf, F32), jnp.full((SUBLANES, Q_TILE), -jnp.inf, F32)))
    lo = jnp.min(rmin, axis=0, keepdims=True)
    hi = jnp.max(rmax, axis=0, keepdims=True)

    def count(pred):
        def body(kc, acc):
            return acc + _colsum8(jnp.where(pred(score_ref[kc]), 1.0, 0.0))
        return jnp.sum(lax.fori_loop(0, nkc, body, jnp.zeros((SUBLANES, Q_TILE), F32)), axis=0, keepdims=True)

    def probe(c):
        lo, hi, cnt_lo = c
        mid = lo + 0.5 * (hi - lo)
        cnt = count(lambda sc: sc >= mid)
        ge = cnt >= kp
        return jnp.where(ge, mid, lo), jnp.where(ge, hi, mid), jnp.where(ge, cnt, cnt_lo)

    def smallest(pred):
        def body(kc, acc):
            sc = score_ref[kc]
            return jnp.minimum(acc, _colmin8(jnp.where(pred(sc), sc, jnp.inf)))
        return jnp.min(lax.fori_loop(0, nkc, body, jnp.full((SUBLANES, Q_TILE), jnp.inf, F32)), axis=0, keepdims=True)

    def any_true(x):
        return jnp.max(jnp.where(x, 1.0, 0.0)) > 0.0

    def search():
        lo_b, _, cnt_lo = lax.fori_loop(0, SEARCH_PROBES, lambda _, c: probe(c), (lo, hi, limit.astype(F32)))

        def step_up(c):
            it, thr, cnt_gt, cnt_ge = c
            up = cnt_gt >= kp
            thr = jnp.where(up, smallest(lambda sc: sc > thr), thr)
            return it + 1, thr, count(lambda sc: sc > thr), jnp.where(up, cnt_gt, cnt_ge)

        thr = smallest(lambda sc: sc >= lo_b)
        found = lax.while_loop(
            lambda c: jnp.logical_and(c[0] < nkc * K_TILE, any_true(c[2] >= kp)),
            step_up, (jnp.int32(0), thr, count(lambda sc: sc > thr), cnt_lo))
        return found[1], kp - found[2], found[3]

    thr, need, cnt_ge = lax.cond((i + 1) * Q_TILE <= topk, lambda: (lo, kp, kp), search)

    def dist_tile(kc):
        return jnp.abs(rel + (t0 - kc * K_TILE).astype(F32))

    big8 = jnp.full((SUBLANES, Q_TILE), DIST_BIG, F32)

    def sel_plain():
        def body(kc, near):
            d = jnp.where(score_ref[kc] >= thr, dist_tile(kc), DIST_BIG)
            dist_ref[kc] = d
            return jnp.minimum(near, _colmin8(d))
        return jnp.min(lax.fori_loop(0, nkc, body, big8), axis=0, keepdims=True)

    def sel_ties():
        def body(kc, c):
            run, near = c
            sc = score_ref[kc]
            eq = sc == thr
            eq_f = jnp.where(eq, 1.0, 0.0)
            pre = run + _dot(lstrict_ref[...], eq_f.astype(BF16))
            keep = (sc > thr) | (eq & (pre < need))
            d = jnp.where(keep, dist_tile(kc), DIST_BIG)
            dist_ref[kc] = d
            return run + jnp.sum(_colsum8(eq_f), axis=0, keepdims=True), jnp.minimum(near, _colmin8(d))
        _, near = _for_key_tiles(nkc, body, (jnp.zeros((1, Q_TILE), F32), big8))
        return jnp.min(near, axis=0, keepdims=True)

    near = lax.cond(any_true(cnt_ge != kp), sel_ties, sel_plain)

    acc_ref[...] = jnp.zeros(acc_ref.shape, F32)

    bound = jnp.sqrt(jnp.max(qn_ref[0, 0]) * jnp.max(kn_ref[0])) * BOUND_MARGIN

    def att_shifted():
        def body(kc, _):
            d = dist_ref[kc] - near
            for h in range(A_HEADS):
                slope = 2.0 ** (-8.0 * (h + 1) / A_HEADS) * LOG2E
                ck = ckr_ref[0, pl.ds(pl.multiple_of(kc * K_TILE, K_TILE), K_TILE),
                             h * KV_LATENT:(h + 1) * KV_LATENT]
                logit = _dot_nt(ck, qabs_ref[0, :, h * KV_LATENT:(h + 1) * KV_LATENT]) - slope * d
                p_ref[h] = jnp.exp2(logit).astype(BF16)
            cv = cvt_ref[0, kc]
            for h in range(A_HEADS):
                acc_ref[h] = acc_ref[h] + _dot(cv, p_ref[h])
            return 0
        _for_key_tiles(nkc, body, 0)

    def att_online():
        m_ref[...] = jnp.full(m_ref.shape, -jnp.inf, F32)
        _for_key_tiles(nkc, att, 0)

    def att(kc, _):
        dist = dist_ref[kc]
        m_new = []
        for h in range(A_HEADS):
            slope = 2.0 ** (-8.0 * (h + 1) / A_HEADS) * LOG2E
            ck = ckr_ref[0, pl.ds(pl.multiple_of(kc * K_TILE, K_TILE), K_TILE), h * KV_LATENT:(h + 1) * KV_LATENT]
            logit = _dot_nt(ck, qabs_ref[0, :, h * KV_LATENT:(h + 1) * KV_LATENT]) - slope * dist
            logit_ref[h] = logit
            m_new.append(jnp.maximum(m_ref[h], jnp.max(_colmax8(logit), axis=0, keepdims=True)))
        cv = cvt_ref[0, kc]
        for h in range(A_HEADS):
            p = jnp.exp2(logit_ref[h] - m_new[h])
            acc_ref[h] = acc_ref[h] * jnp.exp2(m_ref[h] - m_new[h]) + _dot(cv, p.astype(BF16))
            m_ref[h] = m_new[h]
        return 0

    lax.cond(bound <= EXP_RANGE, att_shifted, att_online)

    for pair in range(A_HEADS // 2):
        o_pair = []
        for hh in range(2):
            a = acc_ref[2 * pair + hh]
            o_t = a[:KV_LATENT] * (1.0 / a[KV_LATENT:KV_LATENT + 1])
            o_pair.append(o_t.T.astype(BF16))
        o_lat = jnp.concatenate(o_pair, axis=1)
        o_ref[0, :, pair * LANES:(pair + 1) * LANES] = _dot(o_lat, wuv_ref[pair]).astype(o_ref.dtype)


def _dsa_call(topk, qabs, qidx, widx, ckr, cvt, kidx, wuv_pair, lstrict, qn, kn):
    bsz, s, _ = qabs.shape
    nq = s // Q_TILE
    nk = s // K_TILE
    qt = lambda w: pl.BlockSpec((1, Q_TILE, w), lambda b, i: (b, i, 0))
    return pl.pallas_call(
        functools.partial(_dsa_kernel, topk),
        grid=(bsz, nq),
        in_specs=[qt(A_HEADS * KV_LATENT),
                  pl.BlockSpec((1, IDX_HEADS, Q_TILE, IDX_DIM), lambda b, i: (b, 0, i, 0)),
                  pl.BlockSpec((1, 1, IDX_HEADS, Q_TILE), lambda b, i: (b, i, 0, 0)),
                  pl.BlockSpec((1, s, A_HEADS * KV_LATENT), lambda b, i: (b, 0, 0)),
                  pl.BlockSpec((1, nk, KV_LATENT + ONES_ROWS, K_TILE), lambda b, i: (b, 0, 0, 0)),
                  pl.BlockSpec((1, s, IDX_DIM), lambda b, i: (b, 0, 0)),
                  pl.BlockSpec((A_HEADS // 2, 2 * KV_LATENT, LANES), lambda b, i: (0, 0, 0)),
                  pl.BlockSpec((K_TILE, K_TILE), lambda b, i: (0, 0)),
                  pl.BlockSpec((1, 1, SUBLANES, LANES), lambda b, i: (b, i, 0, 0)),
                  pl.BlockSpec((1, nk, SUBLANES, LANES), lambda b, i: (b, 0, 0, 0))],
        out_specs=qt(A_WIDTH),
        out_shape=jax.ShapeDtypeStruct((bsz, s, A_WIDTH), BF16),
        scratch_shapes=[pltpu.VMEM((nk, K_TILE, Q_TILE), F32),
                        pltpu.VMEM((nk, K_TILE, Q_TILE), F32),
                        pltpu.VMEM((A_HEADS, K_TILE, Q_TILE), F32),
                        pltpu.VMEM((A_HEADS, K_TILE, Q_TILE), BF16),
                        pltpu.VMEM((A_HEADS, 1, Q_TILE), F32),
                        pltpu.VMEM((A_HEADS, KV_LATENT + ONES_ROWS, Q_TILE), F32)],
        compiler_params=_params(("arbitrary", "arbitrary")),
    )(qabs, qidx, widx, ckr, cvt, kidx, wuv_pair, lstrict, qn, kn)


def _rwkv_block(blk, tm, masks, rt_ref, kt_ref, bt_ref, kl_ref, v_ref, pc_ref):
    strict, incl, eye_s, lane_chunk, row_chunk, blk_diag, diag, head0 = masks
    nch = tm // CHUNK
    npair = B_HEADS // 2
    rows_b = slice(blk * tm, (blk + 1) * tm)
    zero_b = jnp.zeros((), BF16)

    def block_diag(m):
        return jnp.concatenate([jnp.where(lane_chunk == c, m, jnp.zeros((), m.dtype)) for c in range(nch)], axis=0)

    heads = [(p, hh) for p in range(npair) for hh in range(2)]
    rt, kt, bt, kl, v, v_bd = [], [], [], [], [], []
    a_ab, a_ak, m_rb, m_rk = [], [], [], []
    for p in range(npair):
        sl = slice(p * LANES, (p + 1) * LANES)
        rt.append(rt_ref[0, rows_b, sl])
        kt.append(kt_ref[0, rows_b, sl])
        bt.append(bt_ref[0, rows_b, sl])
        kl.append(kl_ref[0, rows_b, sl])
        v.append(v_ref[0, rows_b, sl])
        tc = 2 * CHUNK
        folded = []
        for g in range(nch // 2):
            r2 = slice(g * tc, (g + 1) * tc)
            lhs = jnp.concatenate([jnp.where(head0, kt[p][r2], zero_b), jnp.where(head0, zero_b, kt[p][r2]),
                                   jnp.where(head0, rt[p][r2], zero_b), jnp.where(head0, zero_b, rt[p][r2])], axis=0)
            prod = _dot_nt(lhs, jnp.concatenate([bt[p][r2], kl[p][r2]], axis=0))
            fold = lambda m: m[:CHUNK] + m[CHUNK:]
            folded.append([fold(jnp.where(strict if j < 2 else incl, prod[j * tc:(j + 1) * tc], 0.0))
                           for j in range(4)])
        gather = lambda j, half: jnp.concatenate(
            [f[j][:, half * tc:(half + 1) * tc] for f in folded], axis=1)
        v_bd.append(jnp.concatenate([jnp.where(row_chunk == c, v[p], zero_b) for c in range(nch)], axis=1))
        for hh in range(2):
            a_ab.append(gather(hh, 0))
            a_ak.append(gather(hh, 1).astype(BF16))
            m_rb.append(block_diag(gather(2 + hh, 0).astype(BF16)))
            m_rk.append(gather(2 + hh, 1).astype(BF16))

    t_inv = [(eye_s - a).astype(BF16) for a in a_ab]
    a_pow = [a.astype(BF16) for a in a_ab]
    for _ in range(5):
        a_sq = [_dot(a, block_diag(a)) for a in a_pow]
        a_pow = [a.astype(BF16) for a in a_sq]
        t_inv = [_dot(t, block_diag((eye_s + a).astype(BF16))).astype(BF16) for t, a in zip(t_inv, a_sq)]
    t_inv = [block_diag(t) for t in t_inv]

    avm = [_dot(jnp.concatenate([a_ak[i], m_rk[i]], axis=0), v_bd[p]) for i, (p, _) in enumerate(heads)]
    stack = lambda m: jnp.concatenate([m[:, c * LANES:(c + 1) * LANES] for c in range(nch)], axis=0)
    av = [stack(m[:CHUNK]) for m in avm]
    mv = [stack(m[CHUNK:]) for m in avm]
    x = [_dot(t_inv[i], jnp.concatenate([kt[p], av[i].astype(BF16)], axis=1))
         for i, (p, _) in enumerate(heads)]
    y = [_dot(m_rb[i], x[i].astype(BF16)) for i in range(len(heads))]

    head0_2 = jnp.concatenate([head0, head0], axis=1)
    zeros_b = jnp.zeros((CHUNK, LANES), BF16)
    q_b, ol, g_mat, f_mat = [], [], [], []
    for p in range(npair):
        i0, i1 = 2 * p, 2 * p + 1
        sl = slice(p * LANES, (p + 1) * LANES)
        wu_b = (-jnp.where(head0_2, x[i0], x[i1])).astype(BF16)
        yy = jnp.where(head0_2, y[i0], y[i1])
        q_b.append((rt[p].astype(F32) - yy[:, :LANES]).astype(BF16))
        ol.append(jnp.where(head0, mv[i0], mv[i1]) - yy[:, LANES:])
        gp, fp = [], []
        for c in range(nch):
            rows = slice(c * CHUNK, (c + 1) * CHUNK)
            pc = pc_ref[0, blk, c:c + 1, sl]
            bh = (bt[p][rows].astype(F32) * pc).astype(BF16)
            kh = (kl[p][rows].astype(F32) * pc).astype(BF16)
            rhs = jnp.concatenate([wu_b[rows], jnp.concatenate([zeros_b, v[p][rows]], axis=1)], axis=0)
            bw = _dot_tn(jnp.concatenate([bh, kh], axis=0), rhs)
            gp.append((jnp.where(diag, pc, 0.0) + jnp.where(blk_diag, bw[:, :LANES], 0.0)).astype(BF16))
            fp.append(jnp.where(blk_diag, bw[:, LANES:], 0.0))
        g_mat.append(gp)
        f_mat.append(fp)
    return q_b, ol, g_mat, f_mat


def _rwkv_kernel(rt_ref, kt_ref, bt_ref, kl_ref, v_ref, g_ref, bv_ref, pc_ref, lnw_ref, lnb_ref, eb_ref,
                 o_ref, h_ref):
    j = pl.program_id(1)

    @pl.when(j == 0)
    def _():
        h_ref[...] = jnp.zeros_like(h_ref)

    tm = TOK_TILE
    nch = tm // CHUNK
    npair = B_HEADS // 2
    ri = lax.broadcasted_iota(jnp.int32, (2 * CHUNK, 4 * CHUNK), 0)
    ci = lax.broadcasted_iota(jnp.int32, (2 * CHUNK, 4 * CHUNK), 1) % (2 * CHUNK)
    same = (ri // CHUNK) == (ci // CHUNK)
    r2 = lax.broadcasted_iota(jnp.int32, (LANES, LANES), 0)
    c2 = lax.broadcasted_iota(jnp.int32, (LANES, LANES), 1)
    lane = lax.broadcasted_iota(jnp.int32, (1, LANES), 1)
    rs = lax.broadcasted_iota(jnp.int32, (CHUNK, tm), 0)
    cs = lax.broadcasted_iota(jnp.int32, (CHUNK, tm), 1)
    masks = (same & (ri > ci), same & (ri >= ci), jnp.where(cs % CHUNK == rs, 1.0, 0.0),
             lax.broadcasted_iota(jnp.int32, (1, tm), 1) // CHUNK, lax.broadcasted_iota(jnp.int32, (tm, 1), 0) // CHUNK,
             (r2 // B_HEAD_DIM) == (c2 // B_HEAD_DIM), r2 == c2, (lane // B_HEAD_DIM) == 0)

    nblk = rt_ref.shape[1] // tm
    blocks = [_rwkv_block(b, tm, masks, rt_ref, kt_ref, bt_ref, kl_ref, v_ref, pc_ref) for b in range(nblk)]

    h = [h_ref[p] for p in range(npair)]
    for b, (q_b, ol, g_mat, f_mat) in enumerate(blocks):
        o_chunks = [[] for _ in range(npair)]
        for c in range(nch):
            rows = slice(c * CHUNK, (c + 1) * CHUNK)
            for p in range(npair):
                h_b = h[p].astype(BF16)
                o_chunks[p].append(_dot(q_b[p][rows], h_b) + ol[p][rows])
                h[p] = _dot(g_mat[p][c], h_b) + f_mat[p][c]
        out = jnp.concatenate([jnp.concatenate(oc, axis=0) for oc in o_chunks], axis=1)

        rows_b = slice(b * tm, (b + 1) * tm)
        eb = eb_ref[...]
        mean = _seg_dot_hl(out, eb) * (1.0 / B_HEAD_DIM)
        d = out - mean
        var = _seg_dot(d * d, eb) * (1.0 / B_HEAD_DIM)
        y = d * lax.rsqrt(var + GN_EPS) * lnw_ref[...] + lnb_ref[...] + bv_ref[0, rows_b]
        o_ref[0, rows_b] = (y * g_ref[0, rows_b]).astype(o_ref.dtype)
    for p in range(npair):
        h_ref[p] = h[p]


def _rwkv_call(rt, kt, bt, kl, v, g, bv, pc, ln_w, ln_b, eb):
    bsz, s, _ = rt.shape
    tm = RWKV_TILE
    tok = pl.BlockSpec((1, tm, B_WIDTH), lambda b, j: (b, j, 0))
    row = pl.BlockSpec((1, B_WIDTH), lambda b, j: (0, 0))
    return pl.pallas_call(
        _rwkv_kernel,
        grid=(bsz, s // tm),
        in_specs=[tok] * 7 + [pl.BlockSpec((1, tm // TOK_TILE, TOK_TILE // CHUNK, B_WIDTH), lambda b, j: (b, j, 0, 0)),
                              row, row, pl.BlockSpec((SEG_K, SEG_K), lambda b, j: (0, 0))],
        out_specs=tok,
        out_shape=jax.ShapeDtypeStruct((bsz, s, B_WIDTH), BF16),
        scratch_shapes=[pltpu.VMEM((B_HEADS // 2, LANES, LANES), F32)],
        compiler_params=_params(("arbitrary", "arbitrary")),
    )(rt, kt, bt, kl, v, g, bv, pc, ln_w, ln_b, eb)


def _ffn_kernel(x_ref, oa_ref, ob_ref, gt1_ref, sh2_ref, sc2_ref, gt2_ref, gf_ref, woa_ref, wob_ref,
                w1_ref, w2_ref, o_ref):
    for r0 in range(0, x_ref.shape[1], FFN_ROWS):
        rows = slice(r0, r0 + FFN_ROWS)
        mix = _dot(oa_ref[0, rows], woa_ref[...]) + _dot(ob_ref[0, rows], wob_ref[...])
        x1 = x_ref[0, rows] + gt1_ref[0, 0] * mix
        y = x1 * lax.rsqrt(jnp.mean(x1 * x1, axis=-1, keepdims=True) + RMS_EPS) * gf_ref[...]
        h2 = (y * (1.0 + sc2_ref[0, 0]) + sh2_ref[0, 0]).astype(BF16)
        u = jnp.maximum(_dot(h2, w1_ref[...]), 0.0)
        o_ref[0, rows] = x1 + gt2_ref[0, 0] * _dot((u * u).astype(BF16), w2_ref[...])


def _ffn_call(x, oa, ob, mod4, g_ffn, w_out_a, w_out_b, w1, w2):
    bsz, s, d = x.shape
    dff = w1.shape[1]
    tm = FFN_TILE
    tok = lambda w: pl.BlockSpec((1, tm, w), lambda b, j: (b, j, 0))
    modk = lambda k: pl.BlockSpec((1, 1, 1, d), lambda b, j, k=k: (b, k, 0, 0))
    res = lambda shape: pl.BlockSpec(shape, lambda b, j: (0, 0), pipeline_mode=pl.Buffered(1))
    return pl.pallas_call(
        _ffn_kernel,
        grid=(bsz, s // tm),
        in_specs=[tok(d), tok(A_WIDTH), tok(B_WIDTH), modk(2), modk(3), modk(4), modk(5),
                  pl.BlockSpec((1, d), lambda b, j: (0, 0)),
                  res((A_WIDTH, d)), res((B_WIDTH, d)), res((d, dff)), res((dff, d))],
        out_specs=tok(d),
        out_shape=jax.ShapeDtypeStruct((bsz, s, d), F32),
        compiler_params=_params(("arbitrary", "arbitrary")),
    )(x, oa, ob, mod4, mod4, mod4, mod4, g_ffn, w_out_a, w_out_b, w1, w2)


def _block_ones(n, blk, dtype=BF16):
    i = jnp.arange(n)
    return ((i[:, None] // blk) == (i[None, :] // blk)).astype(dtype)


def kernel(x, c, w_ada, b_ada, g_mix, g_ffn, w_in, g_q, g_k, g_kv, w_uk, w_uv, mu_shift, w0, w2, a0, a2, g2,
           k_k, k_a, r_k, ln_w, ln_b, w_out, w_ff1, w_ff2):
    bsz, s, d = x.shape
    depth = w_ada.shape[0]
    assert s % Q_TILE == 0 and s % FRONT_TILE == 0 and s % RWKV_TILE == 0 and s % FFN_TILE == 0
    assert Q_TILE == TOK_TILE and K_TILE == TOK_TILE
    topk = min(TOPK_MAX, s // 4)

    eb = _block_ones(SEG_K, B_HEAD_DIM)
    ex = (jnp.arange(2 * A_HEAD_DIM)[:, None] // A_HEAD_DIM == jnp.arange(2 * KV_LATENT)[None, :] // KV_LATENT
          ).astype(BF16)
    sel = (jnp.arange(LANES)[None, :] == IDX_DIM + jnp.arange(IDX_HEADS)[:, None]).astype(BF16)
    eye_l = jnp.eye(KV_LATENT, dtype=BF16)
    ti = jnp.arange(TOK_TILE)
    tri = (((ti[:, None] // CHUNK) == (ti[None, :] // CHUNK)) & (ti[:, None] >= ti[None, :])).astype(BF16)
    ki = jnp.arange(K_TILE)
    lstrict = (ki[None, :] < ki[:, None]).astype(BF16)

    for l in range(depth):
        w_a = jnp.pad(w_in[l][:, :N_IN_A], ((0, 0), (0, N_A_PAD - N_IN_A)))
        w_in_p = jnp.concatenate([w_a, w_in[l][:, N_IN_A:]], axis=1).astype(BF16)
        wuk_flat = w_uk[l].reshape(KV_LATENT, A_WIDTH).astype(BF16)
        wuk_t = jnp.transpose(w_uk[l], (1, 2, 0)).reshape(A_HEADS // 2, 2, A_HEAD_DIM, KV_LATENT)
        wuk_bd = (jnp.eye(2, dtype=F32)[None, :, None, :, None] * wuk_t[:, :, :, None, :]).reshape(
            A_HEADS // 2, 2 * A_HEAD_DIM, 2 * KV_LATENT).astype(BF16)
        wuv_t = jnp.transpose(w_uv[l], (1, 0, 2)).reshape(A_HEADS // 2, 2, KV_LATENT, A_HEAD_DIM)
        wuv_pair = (jnp.eye(2, dtype=F32)[None, :, None, :, None] * wuv_t[:, :, :, None, :]).reshape(
            A_HEADS // 2, 2 * KV_LATENT, 2 * A_HEAD_DIM).astype(BF16)
        gqk = jnp.tile(g_q[l] * g_k[l], A_HEADS).reshape(1, A_WIDTH)
        r1 = lambda t: t.reshape(1, -1)

        mod = _mod_call(c, w_ada[l], b_ada[l])
        mod4 = mod.reshape(bsz, 6, 1, d)
        a_consts = (r1(g_kv[l]), gqk, wuk_flat, wuk_bd, eb, ex, sel, eye_l)
        b_consts = (r1(w0[l]), w2[l].astype(BF16), r1(a0[l]), a2[l].astype(BF16), g2[l].astype(BF16),
                    r1(k_k[l]), r1(k_a[l]), r1(r_k[l]), eb, tri)
        ckr, cvt, qabs, qidx, kidx, widx, qn, kn, rt, kt, bt, kl, v, g, bv, pc = _front_call(
            x, mod4, r1(g_mix[l]), w_in_p, r1(mu_shift[l]), a_consts, b_consts)
        o_a = _dsa_call(topk, qabs, qidx, widx, ckr, cvt, kidx, wuv_pair, lstrict, qn, kn)
        o_b = _rwkv_call(rt, kt, bt, kl, v, g, bv, pc, r1(ln_w[l]), r1(ln_b[l]), eb)
        x = _ffn_call(x, o_a, o_b, mod4, r1(g_ffn[l]), w_out[l][:A_WIDTH].astype(BF16),
                      w_out[l][A_WIDTH:].astype(BF16), w_ff1[l].astype(BF16), w_ff2[l].astype(BF16))
    return x
```

```python
import functools

import jax
import jax.numpy as jnp
from jax import lax
from jax.experimental import pallas as pl
from jax.experimental.pallas import tpu as pltpu

F32 = jnp.float32
BF16 = jnp.bfloat16

CHUNK = 64
A_HEADS = 8
A_HEAD_DIM = 64
A_WIDTH = A_HEADS * A_HEAD_DIM
KV_LATENT = 128
IDX_HEADS = 8
IDX_DIM = 64
TOPK_MAX = 256
B_HEADS = 8
B_HEAD_DIM = 64
B_WIDTH = B_HEADS * B_HEAD_DIM
W_LORA = 64
A_LORA = 64
G_LORA = 128
RMS_EPS = 1e-6
GN_EPS = 64e-5
N_IN_A = A_WIDTH + KV_LATENT + IDX_HEADS * IDX_DIM + IDX_DIM + IDX_HEADS
N_IN_B = 3 * B_WIDTH + W_LORA + A_LORA + G_LORA
N_A_PAD = 1280

LANES = 128
SUBLANES = 8
ADD_CHAINS = 4
MOD_COLS = 1024
SEG_K = 256
TOK_TILE = 256
FRONT_TILE = 512
RWKV_TILE = 512
Q_TILE = 256
FFN_TILE = 1024
FFN_ROWS = 512
K_TILE = 256
DIST_BIG = 1e30
ONES_ROWS = 16
LOG2E = 1.4426950408889634
EXP_NEG_HALF = 0.6065306597126334
EXP_RANGE = 90.0
BOUND_MARGIN = 1.02
SEARCH_PROBES = 14
VMEM_LIMIT = 56 * 1024 * 1024


def _dot(a, b):
    return jnp.dot(a, b, preferred_element_type=F32)


def _dot_nt(a, b):
    return lax.dot_general(a, b, (((1,), (1,)), ((), ())), preferred_element_type=F32)


def _dot_tn(a, b):
    return lax.dot_general(a, b, (((0,), (0,)), ((), ())), preferred_element_type=F32)


def _split(x):
    hi = x.astype(BF16)
    lo = (x - hi.astype(F32)).astype(BF16)
    return hi, lo


def _dot_hl(x, e):
    hi, lo = _split(x)
    return _dot(hi, e) + _dot(lo, e)


def _seg_dot_hl(x, e):
    k = e.shape[0]
    return jnp.concatenate([_dot_hl(x[:, j:j + k], e) for j in range(0, x.shape[1], k)], axis=1)


def _seg_dot(x, e):
    k = e.shape[0]
    xb = x.astype(BF16)
    return jnp.concatenate([_dot(xb[:, j:j + k], e) for j in range(0, x.shape[1], k)], axis=1)


def _params(sem):
    return pltpu.CompilerParams(dimension_semantics=sem, vmem_limit_bytes=VMEM_LIMIT)


def _mod_kernel(c_ref, w_ref, b_ref, o_ref):
    c = c_ref[...]
    s = c * jax.nn.sigmoid(c)
    s_hi, s_lo = _split(s)
    w_hi, w_lo = _split(w_ref[...])
    o_ref[...] = _dot(s_hi, w_hi) + _dot(s_hi, w_lo) + _dot(s_lo, w_hi) + b_ref[...]


def _mod_call(c, w_ada, b_ada):
    bsz, d = c.shape
    n = w_ada.shape[1]
    tn = MOD_COLS
    return pl.pallas_call(
        _mod_kernel,
        grid=(n // tn,),
        in_specs=[pl.BlockSpec((bsz, d), lambda j: (0, 0)),
                  pl.BlockSpec((d, tn), lambda j: (0, j)),
                  pl.BlockSpec((1, tn), lambda j: (0, j))],
        out_specs=pl.BlockSpec((bsz, tn), lambda j: (0, j)),
        out_shape=jax.ShapeDtypeStruct((bsz, n), F32),
        compiler_params=_params(("arbitrary",)),
    )(c, w_ada, b_ada.reshape(1, n))


def _head_norm2_max(x):
    best = None
    for h in range(x.shape[1] // KV_LATENT):
        xh = x[:, h * KV_LATENT:(h + 1) * KV_LATENT]
        n2 = jnp.max(jnp.sum(xh * xh, axis=-1, keepdims=True), axis=0, keepdims=True)
        best = n2 if best is None else jnp.maximum(best, n2)
    return best


def _prep_a(pa, rows, blk, gkv_ref, gqk_ref, wuk_ref, wukbd_ref, eb_ref, ex_ref, sel_ref, eye_ref,
            ckr_ref, cvt_ref, qabs_ref, qidx_ref, kidx_ref, widx_ref, qn_ref, kn_ref):
    tm = pa.shape[0]
    q = pa[:, :A_WIDTH]
    cl = pa[:, A_WIDTH:A_WIDTH + KV_LATENT]
    o_qi = A_WIDTH + KV_LATENT
    qi = pa[:, o_qi:o_qi + IDX_HEADS * IDX_DIM]
    o_kw = o_qi + IDX_HEADS * IDX_DIM
    kw = pa[:, o_kw:o_kw + LANES]

    ckv = cl * lax.rsqrt(jnp.mean(cl * cl, axis=-1, keepdims=True) + RMS_EPS) * gkv_ref[...]
    ckv_b = ckv.astype(BF16)
    cvt_ref[0, blk, :KV_LATENT, :] = _dot_nt(eye_ref[...], ckv_b).astype(BF16)
    cvt_ref[0, blk, KV_LATENT:, :] = jnp.ones((ONES_ROWS, tm), BF16)
    kf = _dot(ckv_b, wuk_ref[...])
    ss = _seg_dot(kf * kf, ex_ref[...])
    inv_rms = lax.rsqrt(ss * (1.0 / A_HEAD_DIM) + RMS_EPS)
    ckr_ref[0, rows] = (jnp.concatenate([ckv] * A_HEADS, axis=1) * inv_rms).astype(BF16)
    inv_max = inv_rms[:, :KV_LATENT]
    for h in range(1, A_HEADS):
        inv_max = jnp.maximum(inv_max, inv_rms[:, h * KV_LATENT:(h + 1) * KV_LATENT])
    kn2 = jnp.sum(ckv * ckv, axis=-1, keepdims=True) * (inv_max * inv_max)
    kn_ref[0, blk] = jnp.broadcast_to(jnp.max(jnp.max(kn2, axis=1, keepdims=True), axis=0, keepdims=True),
                                      kn_ref.shape[2:])

    ssq = _seg_dot(q * q, eb_ref[...])
    qh = q * lax.rsqrt(ssq * (1.0 / A_HEAD_DIM) + RMS_EPS) * gqk_ref[...]
    qh_b = qh.astype(BF16)
    qn = None
    for j in range(A_HEADS // 2):
        qabs = _dot(qh_b[:, j * LANES:(j + 1) * LANES], wukbd_ref[j]) * (A_HEAD_DIM ** -0.5 * LOG2E)
        qabs_ref[0, rows, 2 * j * KV_LATENT:2 * (j + 1) * KV_LATENT] = qabs.astype(BF16)
        qn = _head_norm2_max(qabs) if qn is None else jnp.maximum(qn, _head_norm2_max(qabs))
    qn_ref[0, blk] = jnp.broadcast_to(qn, qn_ref.shape[2:])
    for h in range(IDX_HEADS):
        qidx_ref[0, h, rows] = qi[:, h * IDX_DIM:(h + 1) * IDX_DIM].astype(BF16)
    kidx_ref[0, rows] = kw[:, :IDX_DIM].astype(BF16)
    kw_hi, kw_lo = _split(kw)
    w_t = _dot_nt(sel_ref[...], kw_hi) + _dot_nt(sel_ref[...], kw_lo)
    widx_ref[0, blk] = w_t * (IDX_HEADS ** -0.5 * IDX_DIM ** -0.5)


def _prep_b(pb, rows, blk, w0_ref, w2_ref, a0_ref, a2_ref, g2_ref, kk_ref, ka_ref, rk_ref, eb_ref, tri_ref,
            rt_ref, kt_ref, bt_ref, kl_ref, v_ref, g_ref, bv_ref, pc_ref):
    r = pb[:, :B_WIDTH]
    k = pb[:, B_WIDTH:2 * B_WIDTH]
    v = pb[:, 2 * B_WIDTH:3 * B_WIDTH]
    o = 3 * B_WIDTH
    xw = pb[:, o:o + W_LORA]
    xa = pb[:, o + W_LORA:o + W_LORA + A_LORA]
    xg = pb[:, o + W_LORA + A_LORA:o + W_LORA + A_LORA + G_LORA]

    z = w0_ref[...] + _dot(jnp.tanh(xw).astype(BF16), w2_ref[...])
    lw = -EXP_NEG_HALF * jax.nn.sigmoid(z)
    a = jax.nn.sigmoid(a0_ref[...] + _dot(xa.astype(BF16), a2_ref[...]))
    g = _dot(jax.nn.sigmoid(xg).astype(BF16), g2_ref[...])
    kk = k * kk_ref[...]
    kkn = kk * lax.rsqrt(jnp.maximum(_seg_dot(kk * kk, eb_ref[...]), 1e-24))
    kp = k * (1.0 + (a - 1.0) * ka_ref[...])
    bonus = _seg_dot(r * kp * rk_ref[...], eb_ref[...])

    lw_hi, lw_lo = _split(lw)
    cum = _dot(tri_ref[...], lw_hi) + _dot(tri_ref[...], lw_lo)
    e_pos = jnp.exp(cum)
    e_neg = jnp.exp(-cum)
    rt_ref[0, rows] = (r * e_pos).astype(BF16)
    kt_ref[0, rows] = (kkn * jnp.exp(cum - lw)).astype(BF16)
    bt_ref[0, rows] = (kkn * a * e_neg).astype(BF16)
    kl_ref[0, rows] = (kp * e_neg).astype(BF16)
    v_ref[0, rows] = v.astype(BF16)
    g_ref[0, rows] = g
    bv_ref[0, rows] = bonus * v
    for c in range(pb.shape[0] // CHUNK):
        pc_ref[0, blk, c:c + 1, :] = e_pos[(c + 1) * CHUNK - 1:(c + 1) * CHUNK, :]


N_FRONT_IN = 6
N_PREP_A_IN = 8
N_PREP_B_IN = 10
N_PREP_A_OUT = 8


def _front_kernel(*refs):
    x_ref, sh_ref, sc_ref, g_ref, w_ref, mu_ref = refs[:N_FRONT_IN]
    a_in = refs[N_FRONT_IN:N_FRONT_IN + N_PREP_A_IN]
    b_in = refs[N_FRONT_IN + N_PREP_A_IN:N_FRONT_IN + N_PREP_A_IN + N_PREP_B_IN]
    outs = refs[N_FRONT_IN + N_PREP_A_IN + N_PREP_B_IN:-1]
    carry_ref = refs[-1]
    j = pl.program_id(1)

    @pl.when(j == 0)
    def _():
        carry_ref[...] = jnp.zeros_like(carry_ref)

    x = x_ref[0]
    y = x * lax.rsqrt(jnp.mean(x * x, axis=-1, keepdims=True) + RMS_EPS) * g_ref[...]
    h = y * (1.0 + sc_ref[0, 0]) + sh_ref[0, 0]
    p = _dot(h.astype(BF16), w_ref[...])
    pb = p[:, N_A_PAD:]
    tm = pb.shape[0]
    row = lax.broadcasted_iota(jnp.int32, (tm, 1), 0)
    prev = jnp.where(row == 0, carry_ref[...], pltpu.roll(pb, 1, axis=0))
    carry_ref[...] = pb[tm - 1:tm, :]
    pb = pb + mu_ref[...] * (prev - pb)
    for blk in range(tm // TOK_TILE):
        rows = slice(blk * TOK_TILE, (blk + 1) * TOK_TILE)
        _prep_a(p[rows, :N_A_PAD], rows, blk, *a_in, *outs[:N_PREP_A_OUT])
        _prep_b(pb[rows], rows, blk, *b_in, *outs[N_PREP_A_OUT:])


def _front_call(x, mod4, g_mix, w_in_p, mu, a_consts, b_consts):
    bsz, s, d = x.shape
    n = w_in_p.shape[1]
    nb = n - N_A_PAD
    tm = FRONT_TILE
    tt = TOK_TILE
    full = lambda arr: pl.BlockSpec(arr.shape, lambda b, j, nd=arr.ndim: (0,) * nd)
    tok = lambda w: pl.BlockSpec((1, tm, w), lambda b, j: (b, j, 0))
    per_tile = lambda r, c: pl.BlockSpec((1, tm // tt, r, c), lambda b, j: (b, j, 0, 0))
    bf = lambda w: jax.ShapeDtypeStruct((bsz, s, w), BF16)
    ff = lambda w: jax.ShapeDtypeStruct((bsz, s, w), F32)
    nt = s // tt
    out_specs = [tok(A_HEADS * KV_LATENT), per_tile(KV_LATENT + ONES_ROWS, tt), tok(A_HEADS * KV_LATENT),
                 pl.BlockSpec((1, IDX_HEADS, tm, IDX_DIM), lambda b, j: (b, 0, j, 0)),
                 tok(IDX_DIM), per_tile(IDX_HEADS, tt), per_tile(SUBLANES, LANES), per_tile(SUBLANES, LANES)
                 ] + [tok(B_WIDTH)] * 7 + [per_tile(tt // CHUNK, B_WIDTH)]
    out_shape = [bf(A_HEADS * KV_LATENT),
                 jax.ShapeDtypeStruct((bsz, nt, KV_LATENT + ONES_ROWS, tt), BF16),
                 bf(A_HEADS * KV_LATENT),
                 jax.ShapeDtypeStruct((bsz, IDX_HEADS, s, IDX_DIM), BF16),
                 bf(IDX_DIM),
                 jax.ShapeDtypeStruct((bsz, nt, IDX_HEADS, tt), F32),
                 jax.ShapeDtypeStruct((bsz, nt, SUBLANES, LANES), F32),
                 jax.ShapeDtypeStruct((bsz, nt, SUBLANES, LANES), F32),
                 bf(B_WIDTH), bf(B_WIDTH), bf(B_WIDTH), bf(B_WIDTH), bf(B_WIDTH), ff(B_WIDTH), ff(B_WIDTH),
                 jax.ShapeDtypeStruct((bsz, nt, tt // CHUNK, B_WIDTH), F32)]
    assert len(a_consts) == N_PREP_A_IN and len(b_consts) == N_PREP_B_IN
    return pl.pallas_call(
        _front_kernel,
        grid=(bsz, s // tm),
        in_specs=[pl.BlockSpec((1, tm, d), lambda b, j: (b, j, 0)),
                  pl.BlockSpec((1, 1, 1, d), lambda b, j: (b, 0, 0, 0)),
                  pl.BlockSpec((1, 1, 1, d), lambda b, j: (b, 1, 0, 0)),
                  full(g_mix),
                  pl.BlockSpec(w_in_p.shape, lambda b, j: (0, 0), pipeline_mode=pl.Buffered(1)),
                  full(mu)] + [full(t) for t in a_consts] + [full(t) for t in b_consts],
        out_specs=out_specs,
        out_shape=out_shape,
        scratch_shapes=[pltpu.VMEM((1, nb), F32)],
        compiler_params=_params(("arbitrary", "arbitrary")),
    )(x, mod4, mod4, g_mix, w_in_p, mu, *a_consts, *b_consts)


def _colsum8(x):
    y = x.reshape(ADD_CHAINS, K_TILE // (ADD_CHAINS * SUBLANES), SUBLANES, Q_TILE)
    return jnp.sum(jnp.sum(y, axis=1), axis=0)


def _colmin8(x):
    y = x.reshape(ADD_CHAINS, K_TILE // (ADD_CHAINS * SUBLANES), SUBLANES, Q_TILE)
    return jnp.min(jnp.min(y, axis=1), axis=0)


def _colmax8(x):
    y = x.reshape(ADD_CHAINS, K_TILE // (ADD_CHAINS * SUBLANES), SUBLANES, Q_TILE)
    return jnp.max(jnp.max(y, axis=1), axis=0)


def _for_key_tiles(nkc, body, init):
    def quad(j, c):
        return body(4 * j + 3, body(4 * j + 2, body(4 * j + 1, body(4 * j, c))))
    c = lax.fori_loop(0, nkc // 4, quad, init)
    base = (nkc // 4) * 4
    c = lax.cond(nkc % 4 >= 2, lambda c: body(base + 1, body(base, c)), lambda c: c, c)
    return lax.cond(nkc % 2 == 1, lambda c: body(nkc - 1, c), lambda c: c, c)


def _dsa_kernel(topk, qabs_ref, qidx_ref, widx_ref, ckr_ref, cvt_ref, kidx_ref, wuv_ref, lstrict_ref, qn_ref, kn_ref,
                o_ref, score_ref, dist_ref, logit_ref, p_ref, m_ref, acc_ref):
    i = pl.program_id(1)
    nkc = i + 1
    t0 = i * Q_TILE
    krow = lax.broadcasted_iota(jnp.int32, (K_TILE, 1), 0)
    qcol = lax.broadcasted_iota(jnp.int32, (1, Q_TILE), 1)
    limit = ((t0 + qcol) // CHUNK + 1) * CHUNK
    kp = jnp.minimum(limit, topk).astype(F32)
    rel = (qcol - krow).astype(F32)

    def p1(kc, carry):
        rmin, rmax = carry
        k = kidx_ref[0, pl.ds(pl.multiple_of(kc * K_TILE, K_TILE), K_TILE), :]
        acc = jnp.zeros((K_TILE, Q_TILE), F32)
        for h in range(IDX_HEADS):
            s = _dot_nt(k, qidx_ref[0, h])
            acc = acc + widx_ref[0, 0, h:h + 1, :] * jnp.maximum(s, 0.0)
        adm = (kc * K_TILE + krow) < limit
        score_ref[kc] = jnp.where(adm, acc, -jnp.inf)
        rmin = jnp.minimum(rmin, _colmin8(jnp.where(adm, acc, jnp.inf)))
        rmax = jnp.maximum(rmax, _colmax8(jnp.where(adm, acc, -jnp.inf)))
        return rmin, rmax

    rmin, rmax = _for_key_tiles(
        nkc, p1, (jnp.full((SUBLANES, Q_TILE), jnp.inf, F32), jnp.full((SUBLANES, Q_TILE), -jnp.inf, F32)))
    lo = jnp.min(rmin, axis=0, keepdims=True)
    hi = jnp.max(rmax, axis=0, keepdims=True)

    def count(pred):
        def body(kc, acc):
            return acc + _colsum8(jnp.where(pred(score_ref[kc]), 1.0, 0.0))
        return jnp.sum(lax.fori_loop(0, nkc, body, jnp.zeros((SUBLANES, Q_TILE), F32)), axis=0, keepdims=True)

    def probe(c):
        lo, hi, cnt_lo = c
        mid = lo + 0.5 * (hi - lo)
        cnt = count(lambda sc: sc >= mid)
        ge = cnt >= kp
        return jnp.where(ge, mid, lo), jnp.where(ge, hi, mid), jnp.where(ge, cnt, cnt_lo)

    def smallest(pred):
        def body(kc, acc):
            sc = score_ref[kc]
            return jnp.minimum(acc, _colmin8(jnp.where(pred(sc), sc, jnp.inf)))
        return jnp.min(lax.fori_loop(0, nkc, body, jnp.full((SUBLANES, Q_TILE), jnp.inf, F32)), axis=0, keepdims=True)

    def any_true(x):
        return jnp.max(jnp.where(x, 1.0, 0.0)) > 0.0

    def search():
        lo_b, _, cnt_lo = lax.fori_loop(0, SEARCH_PROBES, lambda _, c: probe(c), (lo, hi, limit.astype(F32)))

        def step_up(c):
            it, thr, cnt_gt, cnt_ge = c
            up = cnt_gt >= kp
            thr = jnp.where(up, smallest(lambda sc: sc > thr), thr)
            return it + 1, thr, count(lambda sc: sc > thr), jnp.where(up, cnt_gt, cnt_ge)

        thr = smallest(lambda sc: sc >= lo_b)
        found = lax.while_loop(
            lambda c: jnp.logical_and(c[0] < nkc * K_TILE, any_true(c[2] >= kp)),
            step_up, (jnp.int32(0), thr, count(lambda sc: sc > thr), cnt_lo))
        return found[1], kp - found[2], found[3]

    thr, need, cnt_ge = lax.cond((i + 1) * Q_TILE <= topk, lambda: (lo, kp, kp), search)

    def dist_tile(kc):
        return jnp.abs(rel + (t0 - kc * K_TILE).astype(F32))

    big8 = jnp.full((SUBLANES, Q_TILE), DIST_BIG, F32)

    def sel_plain():
        def body(kc, near):
            d = jnp.where(score_ref[kc] >= thr, dist_tile(kc), DIST_BIG)
            dist_ref[kc] = d
            return jnp.minimum(near, _colmin8(d))
        return jnp.min(lax.fori_loop(0, nkc, body, big8), axis=0, keepdims=True)

    def sel_ties():
        def body(kc, c):
            run, near = c
            sc = score_ref[kc]
            eq = sc == thr
            eq_f = jnp.where(eq, 1.0, 0.0)
            pre = run + _dot(lstrict_ref[...], eq_f.astype(BF16))
            keep = (sc > thr) | (eq & (pre < need))
            d = jnp.where(keep, dist_tile(kc), DIST_BIG)
            dist_ref[kc] = d
            return run + jnp.sum(_colsum8(eq_f), axis=0, keepdims=True), jnp.minimum(near, _colmin8(d))
        _, near = _for_key_tiles(nkc, body, (jnp.zeros((1, Q_TILE), F32), big8))
        return jnp.min(near, axis=0, keepdims=True)

    near = lax.cond(any_true(cnt_ge != kp), sel_ties, sel_plain)

    acc_ref[...] = jnp.zeros(acc_ref.shape, F32)

    bound = jnp.sqrt(jnp.max(qn_ref[0, 0]) * jnp.max(kn_ref[0])) * BOUND_MARGIN

    def att_shifted():
        def body(kc, _):
            d = dist_ref[kc] - near
            for h in range(A_HEADS):
                slope = 2.0 ** (-8.0 * (h + 1) / A_HEADS) * LOG2E
                ck = ckr_ref[0, pl.ds(pl.multiple_of(kc * K_TILE, K_TILE), K_TILE),
                             h * KV_LATENT:(h + 1) * KV_LATENT]
                logit = _dot_nt(ck, qabs_ref[0, :, h * KV_LATENT:(h + 1) * KV_LATENT]) - slope * d
                p_ref[h] = jnp.exp2(logit).astype(BF16)
            cv = cvt_ref[0, kc]
            for h in range(A_HEADS):
                acc_ref[h] = acc_ref[h] + _dot(cv, p_ref[h])
            return 0
        _for_key_tiles(nkc, body, 0)

    def att_online():
        m_ref[...] = jnp.full(m_ref.shape, -jnp.inf, F32)
        _for_key_tiles(nkc, att, 0)

    def att(kc, _):
        dist = dist_ref[kc]
        m_new = []
        for h in range(A_HEADS):
            slope = 2.0 ** (-8.0 * (h + 1) / A_HEADS) * LOG2E
            ck = ckr_ref[0, pl.ds(pl.multiple_of(kc * K_TILE, K_TILE), K_TILE), h * KV_LATENT:(h + 1) * KV_LATENT]
            logit = _dot_nt(ck, qabs_ref[0, :, h * KV_LATENT:(h + 1) * KV_LATENT]) - slope * dist
            logit_ref[h] = logit
            m_new.append(jnp.maximum(m_ref[h], jnp.max(_colmax8(logit), axis=0, keepdims=True)))
        cv = cvt_ref[0, kc]
        for h in range(A_HEADS):
            p = jnp.exp2(logit_ref[h] - m_new[h])
            acc_ref[h] = acc_ref[h] * jnp.exp2(m_ref[h] - m_new[h]) + _dot(cv, p.astype(BF16))
            m_ref[h] = m_new[h]
        return 0

    lax.cond(bound <= EXP_RANGE, att_shifted, att_online)

    for pair in range(A_HEADS // 2):
        o_pair = []
        for hh in range(2):
            a = acc_ref[2 * pair + hh]
            o_t = a[:KV_LATENT] * (1.0 / a[KV_LATENT:KV_LATENT + 1])
            o_pair.append(o_t.T.astype(BF16))
        o_lat = jnp.concatenate(o_pair, axis=1)
        o_ref[0, :, pair * LANES:(pair + 1) * LANES] = _dot(o_lat, wuv_ref[pair]).astype(o_ref.dtype)


def _dsa_call(topk, qabs, qidx, widx, ckr, cvt, kidx, wuv_pair, lstrict, qn, kn):
    bsz, s, _ = qabs.shape
    nq = s // Q_TILE
    nk = s // K_TILE
    qt = lambda w: pl.BlockSpec((1, Q_TILE, w), lambda b, i: (b, i, 0))
    return pl.pallas_call(
        functools.partial(_dsa_kernel, topk),
        grid=(bsz, nq),
        in_specs=[qt(A_HEADS * KV_LATENT),
                  pl.BlockSpec((1, IDX_HEADS, Q_TILE, IDX_DIM), lambda b, i: (b, 0, i, 0)),
                  pl.BlockSpec((1, 1, IDX_HEADS, Q_TILE), lambda b, i: (b, i, 0, 0)),
                  pl.BlockSpec((1, s, A_HEADS * KV_LATENT), lambda b, i: (b, 0, 0)),
                  pl.BlockSpec((1, nk, KV_LATENT + ONES_ROWS, K_TILE), lambda b, i: (b, 0, 0, 0)),
                  pl.BlockSpec((1, s, IDX_DIM), lambda b, i: (b, 0, 0)),
                  pl.BlockSpec((A_HEADS // 2, 2 * KV_LATENT, LANES), lambda b, i: (0, 0, 0)),
                  pl.BlockSpec((K_TILE, K_TILE), lambda b, i: (0, 0)),
                  pl.BlockSpec((1, 1, SUBLANES, LANES), lambda b, i: (b, i, 0, 0)),
                  pl.BlockSpec((1, nk, SUBLANES, LANES), lambda b, i: (b, 0, 0, 0))],
        out_specs=qt(A_WIDTH),
        out_shape=jax.ShapeDtypeStruct((bsz, s, A_WIDTH), BF16),
        scratch_shapes=[pltpu.VMEM((nk, K_TILE, Q_TILE), F32),
                        pltpu.VMEM((nk, K_TILE, Q_TILE), F32),
                        pltpu.VMEM((A_HEADS, K_TILE, Q_TILE), F32),
                        pltpu.VMEM((A_HEADS, K_TILE, Q_TILE), BF16),
                        pltpu.VMEM((A_HEADS, 1, Q_TILE), F32),
                        pltpu.VMEM((A_HEADS, KV_LATENT + ONES_ROWS, Q_TILE), F32)],
        compiler_params=_params(("arbitrary", "arbitrary")),
    )(qabs, qidx, widx, ckr, cvt, kidx, wuv_pair, lstrict, qn, kn)


def _rwkv_block(blk, tm, masks, rt_ref, kt_ref, bt_ref, kl_ref, v_ref, pc_ref):
    strict, incl, eye_s, lane_chunk, row_chunk, blk_diag, diag, head0 = masks
    nch = tm // CHUNK
    npair = B_HEADS // 2
    rows_b = slice(blk * tm, (blk + 1) * tm)
    zero_b = jnp.zeros((), BF16)

    def block_diag(m):
        return jnp.concatenate([jnp.where(lane_chunk == c, m, jnp.zeros((), m.dtype)) for c in range(nch)], axis=0)

    heads = [(p, hh) for p in range(npair) for hh in range(2)]
    rt, kt, bt, kl, v, v_bd = [], [], [], [], [], []
    a_ab, a_ak, m_rb, m_rk = [], [], [], []
    for p in range(npair):
        sl = slice(p * LANES, (p + 1) * LANES)
        rt.append(rt_ref[0, rows_b, sl])
        kt.append(kt_ref[0, rows_b, sl])
        bt.append(bt_ref[0, rows_b, sl])
        kl.append(kl_ref[0, rows_b, sl])
        v.append(v_ref[0, rows_b, sl])
        tc = 2 * CHUNK
        folded = []
        for g in range(nch // 2):
            r2 = slice(g * tc, (g + 1) * tc)
            lhs = jnp.concatenate([jnp.where(head0, kt[p][r2], zero_b), jnp.where(head0, zero_b, kt[p][r2]),
                                   jnp.where(head0, rt[p][r2], zero_b), jnp.where(head0, zero_b, rt[p][r2])], axis=0)
            prod = _dot_nt(lhs, jnp.concatenate([bt[p][r2], kl[p][r2]], axis=0))
            fold = lambda m: m[:CHUNK] + m[CHUNK:]
            folded.append([fold(jnp.where(strict if j < 2 else incl, prod[j * tc:(j + 1) * tc], 0.0))
                           for j in range(4)])
        gather = lambda j, half: jnp.concatenate(
            [f[j][:, half * tc:(half + 1) * tc] for f in folded], axis=1)
        v_bd.append(jnp.concatenate([jnp.where(row_chunk == c, v[p], zero_b) for c in range(nch)], axis=1))
        for hh in range(2):
            a_ab.append(gather(hh, 0))
            a_ak.append(gather(hh, 1).astype(BF16))
            m_rb.append(block_diag(gather(2 + hh, 0).astype(BF16)))
            m_rk.append(gather(2 + hh, 1).astype(BF16))

    mul = lambda xs, ys: [_dot(x, block_diag(y)).astype(BF16) for x, y in zip(xs, ys)]
    factors = [[(eye_s - a).astype(BF16) for a in a_ab]]
    a_pow = [a.astype(BF16) for a in a_ab]
    for _ in range(5):
        a_sq = [_dot(a, block_diag(a)) for a in a_pow]
        a_pow = [a.astype(BF16) for a in a_sq]
        factors.append([(eye_s + a).astype(BF16) for a in a_sq])
    pairs = [mul(factors[k], factors[k + 1]) for k in range(0, 6, 2)]
    t_inv = [block_diag(t) for t in mul(mul(pairs[0], pairs[1]), pairs[2])]

    avm = [_dot(jnp.concatenate([a_ak[i], m_rk[i]], axis=0), v_bd[p]) for i, (p, _) in enumerate(heads)]
    stack = lambda m: jnp.concatenate([m[:, c * LANES:(c + 1) * LANES] for c in range(nch)], axis=0)
    av = [stack(m[:CHUNK]) for m in avm]
    mv = [stack(m[CHUNK:]) for m in avm]
    x = [_dot(t_inv[i], jnp.concatenate([kt[p], av[i].astype(BF16)], axis=1))
         for i, (p, _) in enumerate(heads)]
    y = [_dot(m_rb[i], x[i].astype(BF16)) for i in range(len(heads))]

    head0_2 = jnp.concatenate([head0, head0], axis=1)
    zeros_b = jnp.zeros((CHUNK, LANES), BF16)
    q_b, ol, g_mat, f_mat = [], [], [], []
    for p in range(npair):
        i0, i1 = 2 * p, 2 * p + 1
        sl = slice(p * LANES, (p + 1) * LANES)
        wu_b = (-jnp.where(head0_2, x[i0], x[i1])).astype(BF16)
        yy = jnp.where(head0_2, y[i0], y[i1])
        q_b.append((rt[p].astype(F32) - yy[:, :LANES]).astype(BF16))
        ol.append(jnp.where(head0, mv[i0], mv[i1]) - yy[:, LANES:])
        gp, fp = [], []
        for c in range(nch):
            rows = slice(c * CHUNK, (c + 1) * CHUNK)
            pc = pc_ref[0, blk, c:c + 1, sl]
            bh = (bt[p][rows].astype(F32) * pc).astype(BF16)
            kh = (kl[p][rows].astype(F32) * pc).astype(BF16)
            rhs = jnp.concatenate([wu_b[rows], jnp.concatenate([zeros_b, v[p][rows]], axis=1)], axis=0)
            bw = _dot_tn(jnp.concatenate([bh, kh], axis=0), rhs)
            gp.append((jnp.where(diag, pc, 0.0) + jnp.where(blk_diag, bw[:, :LANES], 0.0)).astype(BF16))
            fp.append(jnp.where(blk_diag, bw[:, LANES:], 0.0))
        g_mat.append(gp)
        f_mat.append(fp)
    return q_b, ol, g_mat, f_mat


def _rwkv_kernel(rt_ref, kt_ref, bt_ref, kl_ref, v_ref, g_ref, bv_ref, pc_ref, lnw_ref, lnb_ref, eb_ref,
                 o_ref, h_ref):
    j = pl.program_id(1)

    @pl.when(j == 0)
    def _():
        h_ref[...] = jnp.zeros_like(h_ref)

    tm = TOK_TILE
    nch = tm // CHUNK
    npair = B_HEADS // 2
    ri = lax.broadcasted_iota(jnp.int32, (2 * CHUNK, 4 * CHUNK), 0)
    ci = lax.broadcasted_iota(jnp.int32, (2 * CHUNK, 4 * CHUNK), 1) % (2 * CHUNK)
    same = (ri // CHUNK) == (ci // CHUNK)
    r2 = lax.broadcasted_iota(jnp.int32, (LANES, LANES), 0)
    c2 = lax.broadcasted_iota(jnp.int32, (LANES, LANES), 1)
    lane = lax.broadcasted_iota(jnp.int32, (1, LANES), 1)
    rs = lax.broadcasted_iota(jnp.int32, (CHUNK, tm), 0)
    cs = lax.broadcasted_iota(jnp.int32, (CHUNK, tm), 1)
    masks = (same & (ri > ci), same & (ri >= ci), jnp.where(cs % CHUNK == rs, 1.0, 0.0),
             lax.broadcasted_iota(jnp.int32, (1, tm), 1) // CHUNK, lax.broadcasted_iota(jnp.int32, (tm, 1), 0) // CHUNK,
             (r2 // B_HEAD_DIM) == (c2 // B_HEAD_DIM), r2 == c2, (lane // B_HEAD_DIM) == 0)

    nblk = rt_ref.shape[1] // tm
    blocks = [_rwkv_block(b, tm, masks, rt_ref, kt_ref, bt_ref, kl_ref, v_ref, pc_ref) for b in range(nblk)]

    h = [h_ref[p] for p in range(npair)]
    for b, (q_b, ol, g_mat, f_mat) in enumerate(blocks):
        o_chunks = [[] for _ in range(npair)]
        for c in range(nch):
            rows = slice(c * CHUNK, (c + 1) * CHUNK)
            for p in range(npair):
                h_b = h[p].astype(BF16)
                o_chunks[p].append(_dot(q_b[p][rows], h_b) + ol[p][rows])
                h[p] = _dot(g_mat[p][c], h_b) + f_mat[p][c]
        out = jnp.concatenate([jnp.concatenate(oc, axis=0) for oc in o_chunks], axis=1)

        rows_b = slice(b * tm, (b + 1) * tm)
        eb = eb_ref[...]
        mean = _seg_dot_hl(out, eb) * (1.0 / B_HEAD_DIM)
        d = out - mean
        var = _seg_dot(d * d, eb) * (1.0 / B_HEAD_DIM)
        y = d * lax.rsqrt(var + GN_EPS) * lnw_ref[...] + lnb_ref[...] + bv_ref[0, rows_b]
        o_ref[0, rows_b] = (y * g_ref[0, rows_b]).astype(o_ref.dtype)
    for p in range(npair):
        h_ref[p] = h[p]


def _rwkv_call(rt, kt, bt, kl, v, g, bv, pc, ln_w, ln_b, eb):
    bsz, s, _ = rt.shape
    tm = RWKV_TILE
    tok = pl.BlockSpec((1, tm, B_WIDTH), lambda b, j: (b, j, 0))
    row = pl.BlockSpec((1, B_WIDTH), lambda b, j: (0, 0))
    return pl.pallas_call(
        _rwkv_kernel,
        grid=(bsz, s // tm),
        in_specs=[tok] * 7 + [pl.BlockSpec((1, tm // TOK_TILE, TOK_TILE // CHUNK, B_WIDTH), lambda b, j: (b, j, 0, 0)),
                              row, row, pl.BlockSpec((SEG_K, SEG_K), lambda b, j: (0, 0))],
        out_specs=tok,
        out_shape=jax.ShapeDtypeStruct((bsz, s, B_WIDTH), BF16),
        scratch_shapes=[pltpu.VMEM((B_HEADS // 2, LANES, LANES), F32)],
        compiler_params=_params(("arbitrary", "arbitrary")),
    )(rt, kt, bt, kl, v, g, bv, pc, ln_w, ln_b, eb)


def _ffn_kernel(x_ref, oa_ref, ob_ref, gt1_ref, sh2_ref, sc2_ref, gt2_ref, gf_ref, woa_ref, wob_ref,
                w1_ref, w2_ref, o_ref):
    for r0 in range(0, x_ref.shape[1], FFN_ROWS):
        rows = slice(r0, r0 + FFN_ROWS)
        mix = _dot(oa_ref[0, rows], woa_ref[...]) + _dot(ob_ref[0, rows], wob_ref[...])
        x1 = x_ref[0, rows] + gt1_ref[0, 0] * mix
        y = x1 * lax.rsqrt(jnp.mean(x1 * x1, axis=-1, keepdims=True) + RMS_EPS) * gf_ref[...]
        h2 = (y * (1.0 + sc2_ref[0, 0]) + sh2_ref[0, 0]).astype(BF16)
        u = jnp.maximum(_dot(h2, w1_ref[...]), 0.0)
        o_ref[0, rows] = x1 + gt2_ref[0, 0] * _dot((u * u).astype(BF16), w2_ref[...])


def _ffn_call(x, oa, ob, mod4, g_ffn, w_out_a, w_out_b, w1, w2):
    bsz, s, d = x.shape
    dff = w1.shape[1]
    tm = FFN_TILE
    tok = lambda w: pl.BlockSpec((1, tm, w), lambda b, j: (b, j, 0))
    modk = lambda k: pl.BlockSpec((1, 1, 1, d), lambda b, j, k=k: (b, k, 0, 0))
    res = lambda shape: pl.BlockSpec(shape, lambda b, j: (0, 0), pipeline_mode=pl.Buffered(1))
    return pl.pallas_call(
        _ffn_kernel,
        grid=(bsz, s // tm),
        in_specs=[tok(d), tok(A_WIDTH), tok(B_WIDTH), modk(2), modk(3), modk(4), modk(5),
                  pl.BlockSpec((1, d), lambda b, j: (0, 0)),
                  res((A_WIDTH, d)), res((B_WIDTH, d)), res((d, dff)), res((dff, d))],
        out_specs=tok(d),
        out_shape=jax.ShapeDtypeStruct((bsz, s, d), F32),
        compiler_params=_params(("arbitrary", "arbitrary")),
    )(x, oa, ob, mod4, mod4, mod4, mod4, g_ffn, w_out_a, w_out_b, w1, w2)


def _block_ones(n, blk, dtype=BF16):
    i = jnp.arange(n)
    return ((i[:, None] // blk) == (i[None, :] // blk)).astype(dtype)


def kernel(x, c, w_ada, b_ada, g_mix, g_ffn, w_in, g_q, g_k, g_kv, w_uk, w_uv, mu_shift, w0, w2, a0, a2, g2,
           k_k, k_a, r_k, ln_w, ln_b, w_out, w_ff1, w_ff2):
    bsz, s, d = x.shape
    depth = w_ada.shape[0]
    assert s % Q_TILE == 0 and s % FRONT_TILE == 0 and s % RWKV_TILE == 0 and s % FFN_TILE == 0
    assert Q_TILE == TOK_TILE and K_TILE == TOK_TILE
    topk = min(TOPK_MAX, s // 4)

    eb = _block_ones(SEG_K, B_HEAD_DIM)
    ex = (jnp.arange(2 * A_HEAD_DIM)[:, None] // A_HEAD_DIM == jnp.arange(2 * KV_LATENT)[None, :] // KV_LATENT
          ).astype(BF16)
    sel = (jnp.arange(LANES)[None, :] == IDX_DIM + jnp.arange(IDX_HEADS)[:, None]).astype(BF16)
    eye_l = jnp.eye(KV_LATENT, dtype=BF16)
    ti = jnp.arange(TOK_TILE)
    tri = (((ti[:, None] // CHUNK) == (ti[None, :] // CHUNK)) & (ti[:, None] >= ti[None, :])).astype(BF16)
    ki = jnp.arange(K_TILE)
    lstrict = (ki[None, :] < ki[:, None]).astype(BF16)

    for l in range(depth):
        w_a = jnp.pad(w_in[l][:, :N_IN_A], ((0, 0), (0, N_A_PAD - N_IN_A)))
        w_in_p = jnp.concatenate([w_a, w_in[l][:, N_IN_A:]], axis=1).astype(BF16)
        wuk_flat = w_uk[l].reshape(KV_LATENT, A_WIDTH).astype(BF16)
        wuk_t = jnp.transpose(w_uk[l], (1, 2, 0)).reshape(A_HEADS // 2, 2, A_HEAD_DIM, KV_LATENT)
        wuk_bd = (jnp.eye(2, dtype=F32)[None, :, None, :, None] * wuk_t[:, :, :, None, :]).reshape(
            A_HEADS // 2, 2 * A_HEAD_DIM, 2 * KV_LATENT).astype(BF16)
        wuv_t = jnp.transpose(w_uv[l], (1, 0, 2)).reshape(A_HEADS // 2, 2, KV_LATENT, A_HEAD_DIM)
        wuv_pair = (jnp.eye(2, dtype=F32)[None, :, None, :, None] * wuv_t[:, :, :, None, :]).reshape(
            A_HEADS // 2, 2 * KV_LATENT, 2 * A_HEAD_DIM).astype(BF16)
        gqk = jnp.tile(g_q[l] * g_k[l], A_HEADS).reshape(1, A_WIDTH)
        r1 = lambda t: t.reshape(1, -1)

        mod = _mod_call(c, w_ada[l], b_ada[l])
        mod4 = mod.reshape(bsz, 6, 1, d)
        a_consts = (r1(g_kv[l]), gqk, wuk_flat, wuk_bd, eb, ex, sel, eye_l)
        b_consts = (r1(w0[l]), w2[l].astype(BF16), r1(a0[l]), a2[l].astype(BF16), g2[l].astype(BF16),
                    r1(k_k[l]), r1(k_a[l]), r1(r_k[l]), eb, tri)
        ckr, cvt, qabs, qidx, kidx, widx, qn, kn, rt, kt, bt, kl, v, g, bv, pc = _front_call(
            x, mod4, r1(g_mix[l]), w_in_p, r1(mu_shift[l]), a_consts, b_consts)
        o_a = _dsa_call(topk, qabs, qidx, widx, ckr, cvt, kidx, wuv_pair, lstrict, qn, kn)
        o_b = _rwkv_call(rt, kt, bt, kl, v, g, bv, pc, r1(ln_w[l]), r1(ln_b[l]), eb)
        x = _ffn_call(x, o_a, o_b, mod4, r1(g_ffn[l]), w_out[l][:A_WIDTH].astype(BF16),
                      w_out[l][A_WIDTH:].astype(BF16), w_ff1[l].astype(BF16), w_ff2[l].astype(BF16))
    return x
```
